```python
import math
import jax, jax.numpy as jnp
from jax import lax
import numpy as np


D_MODEL = 1024
BATCH = 4
SEQ = 4096
DEPTH = 2

HEAD_DIM = 64
BLOCK = 128
EPS = 1e-6
RET_HEADS = 4
RET_QK_DIM = 64
RET_V_DIM = 128
RET_CHUNK = 128
RET_THETA = 10000.0
DIL_HEADS = 8
DIL_PATTERNS = ((128, 1), (512, 4), (2048, 16))
SWA_Q_HEADS = 16
SWA_KV_HEADS = 4
SWA_WINDOW = 128
ROPE_THETA = 500000.0
ROPE_DIMS = HEAD_DIM // 4
D_FF = 4 * D_MODEL
EVEN_SPLITS = (RET_HEADS * RET_QK_DIM, RET_HEADS * RET_QK_DIM, RET_HEADS * RET_V_DIM, RET_HEADS * RET_V_DIM,
               DIL_HEADS * HEAD_DIM, DIL_HEADS * HEAD_DIM, DIL_HEADS * HEAD_DIM)
EVEN_IN = sum(EVEN_SPLITS)
EVEN_MIX = RET_HEADS * RET_V_DIM + DIL_HEADS * HEAD_DIM
SWA_SPLITS = (SWA_Q_HEADS * HEAD_DIM, SWA_KV_HEADS * HEAD_DIM, SWA_KV_HEADS * HEAD_DIM)
SWA_IN = sum(SWA_SPLITS)
SWA_MIX = SWA_Q_HEADS * HEAD_DIM

kernel_name = 'hybrid_retention_dilated_swa_block'


def rmsnorm(x, g):
    xf = x.astype(jnp.float32)
    y = xf * lax.rsqrt(jnp.mean(xf * xf, axis=-1, keepdims=True) + EPS)
    return (y * g.astype(jnp.float32)).astype(x.dtype)


def to_heads(t, n_heads):
    b, s, _ = t.shape
    return t.reshape(b, s, n_heads, -1).transpose(0, 2, 1, 3)


def from_heads(t):
    b, h, s, d = t.shape
    return t.transpose(0, 2, 1, 3).reshape(b, s, h * d)


def rope(x, pos, n_rot, theta):
    half = n_rot // 2
    inv = jnp.power(jnp.float32(theta), -jnp.arange(half, dtype=jnp.float32) * (2.0 / n_rot))
    ang = pos.astype(jnp.float32)[:, None, :, None] * inv
    cos, sin = jnp.cos(ang), jnp.sin(ang)
    xf = x.astype(jnp.float32)
    x1, x2, rest = xf[..., :half], xf[..., half:n_rot], xf[..., n_rot:]
    return jnp.concatenate([x1 * cos - x2 * sin, x2 * cos + x1 * sin, rest], axis=-1)


def banded_attn(q, k, v, max_dist, sinks=None):
    b, hk, g, L, d = q.shape
    bs = math.gcd(L, BLOCK)
    nb = L // bs
    P = max_dist
    kp = jnp.pad(k.astype(jnp.float32), ((0, 0), (0, 0), (P, 0), (0, 0)))
    vp = jnp.pad(v.astype(jnp.float32), ((0, 0), (0, 0), (P, 0), (0, 0)))
    idx = (jnp.arange(nb) * bs)[:, None] + jnp.arange(bs + P)[None, :]
    kb = kp[:, :, idx]
    vb = vp[:, :, idx]
    qb = q.astype(jnp.float32).reshape(b, hk, g, nb, bs, d)
    s = jnp.einsum('bhgnqd,bhnkd->bhgnqk', qb, kb) * (d ** -0.5)
    qpos = jnp.arange(L).reshape(nb, bs)
    kpos = idx - P
    dist = qpos[:, :, None] - kpos[:, None, :]
    valid = (dist >= 0) & (dist <= max_dist) & (kpos[:, None, :] >= 0)
    s = jnp.where(valid, s, -jnp.inf)
    m = jnp.max(s, axis=-1, keepdims=True)
    if sinks is not None:
        sk = sinks.astype(jnp.float32).reshape(1, hk, g, 1, 1, 1)
        m = jnp.maximum(m, sk)
    p = jnp.exp(s - m)
    den = jnp.sum(p, axis=-1, keepdims=True)
    if sinks is not None:
        den = den + jnp.exp(sk - m)
    o = jnp.einsum('bhgnqk,bhnkd->bhgnqd', p, vb) / den
    lse = (m + jnp.log(den))[..., 0]
    return o.reshape(b, hk, g, L, d), lse.reshape(b, hk, g, L)


def dilated_attention(q, k, v):
    b, h, S, d = q.shape
    outs, lses = [], []
    for w, r in DIL_PATTERNS:
        L = S // r
        def split(t):
            return t.reshape(b, h, L, r, d).transpose(0, 1, 3, 2, 4).reshape(b, h * r, L, d)
        o, lse = banded_attn(split(q)[:, :, None], split(k), split(v), w // r)
        outs.append(o[:, :, 0].reshape(b, h, r, L, d).transpose(0, 1, 3, 2, 4).reshape(b, h, S, d))
        lses.append(lse[:, :, 0].reshape(b, h, r, L).transpose(0, 1, 3, 2).reshape(b, h, S))
    wts = jax.nn.softmax(jnp.stack(lses, axis=0), axis=0)
    return jnp.sum(wts[..., None] * jnp.stack(outs, axis=0), axis=0)


def retention(q, k, v):
    b, h, S, dk = q.shape
    dv = v.shape[-1]
    C = math.gcd(S, RET_CHUNK)
    nc = S // C
    log_gamma = jnp.log1p(-jnp.exp2(-5.0 - jnp.arange(h, dtype=jnp.float32)))
    qc = q.reshape(b, h, nc, C, dk)
    kc = k.reshape(b, h, nc, C, dk)
    vc = v.astype(jnp.float32).reshape(b, h, nc, C, dv)
    i = jnp.arange(C, dtype=jnp.float32)
    diff = i[:, None] - i[None, :]
    decay = jnp.where(diff >= 0, jnp.exp(log_gamma[:, None, None] * jnp.maximum(diff, 0.0)), 0.0)
    scores = jnp.einsum('bhncd,bhnkd->bhnck', qc, kc) * decay[None, :, None]
    o_inner = jnp.einsum('bhnck,bhnkv->bhncv', scores, vc)
    xi = jnp.exp(log_gamma[:, None] * (i + 1.0))
    zeta = jnp.exp(log_gamma[:, None] * (C - 1.0 - i))
    chunk_decay = jnp.exp(log_gamma * C)
    kz = kc * zeta[None, :, None, :, None]

    def step(R, inp):
        q_n, kz_n, v_n = inp
        o = jnp.einsum('bhcd,bhdv->bhcv', q_n, R)
        R = R * chunk_decay[None, :, None, None] + jnp.einsum('bhcd,bhcv->bhdv', kz_n, v_n)
        return R, o

    R0 = jnp.zeros((b, h, dk, dv), jnp.float32)
    _, o_cross = lax.scan(step, R0, (qc.transpose(2, 0, 1, 3, 4), kz.transpose(2, 0, 1, 3, 4),
                                     vc.transpose(2, 0, 1, 3, 4)))
    o_cross = o_cross.transpose(1, 2, 0, 3, 4) * xi[None, :, None, :, None]
    return (o_inner + o_cross).reshape(b, h, S, dv)


def even_mixer(h, positions, w_in, w_out, gn_gain, q_gain, k_gain):
    proj = h @ w_in
    cuts = [int(c) for c in np.cumsum(EVEN_SPLITS)[:-1]]
    rq, rk, rv, rg, dq, dk, dv = jnp.split(proj, cuts, axis=-1)
    rq = rope(to_heads(rq, RET_HEADS), positions, RET_QK_DIM, RET_THETA)
    rk = rope(to_heads(rk, RET_HEADS), positions, RET_QK_DIM, RET_THETA) * (RET_QK_DIM ** -0.5)
    ro = retention(rq, rk, to_heads(rv, RET_HEADS))
    mu = jnp.mean(ro, axis=-1, keepdims=True)
    var = jnp.mean(jnp.square(ro - mu), axis=-1, keepdims=True)
    ro = (ro - mu) * lax.rsqrt(var + EPS) * gn_gain.astype(jnp.float32)[None, :, None, :]
    ra = jax.nn.silu(rg.astype(jnp.float32)) * from_heads(ro)
    dq = rope(rmsnorm(to_heads(dq, DIL_HEADS), q_gain), positions, ROPE_DIMS, ROPE_THETA)
    dk = rope(rmsnorm(to_heads(dk, DIL_HEADS), k_gain), positions, ROPE_DIMS, ROPE_THETA)
    da = from_heads(dilated_attention(dq, dk, to_heads(dv, DIL_HEADS).astype(jnp.float32)))
    mixed = jnp.concatenate([ra, da], axis=-1).astype(h.dtype)
    return mixed @ w_out


def swa_mixer(h, positions, w_qkv, b_qkv, w_out, q_gain, k_gain, sinks):
    b, S, _ = h.shape
    g = SWA_Q_HEADS // SWA_KV_HEADS
    proj = h @ w_qkv + b_qkv
    cuts = [int(c) for c in np.cumsum(SWA_SPLITS)[:-1]]
    q, k, v = jnp.split(proj, cuts, axis=-1)
    q = rope(rmsnorm(to_heads(q, SWA_Q_HEADS), q_gain), positions, ROPE_DIMS, ROPE_THETA)
    k = rope(rmsnorm(to_heads(k, SWA_KV_HEADS), k_gain), positions, ROPE_DIMS, ROPE_THETA)
    v = to_heads(v, SWA_KV_HEADS)
    q = q.reshape(b, SWA_KV_HEADS, g, S, HEAD_DIM)
    o, _ = banded_attn(q, k, v, SWA_WINDOW - 1, sinks.reshape(SWA_KV_HEADS, g))
    o = o.reshape(b, SWA_Q_HEADS, S, HEAD_DIM)
    return from_heads(o).astype(h.dtype) @ w_out


def sq_relu_mlp(h, w_up, w_down):
    return jnp.square(jax.nn.relu(h @ w_up)) @ w_down


def setup_inputs(seed: int = 0) -> dict:
    key = jax.random.key(seed)
    ks = jax.random.split(key, 20)
    ne, no = (DEPTH + 1) // 2, DEPTH // 2
    f32 = jnp.float32

    def w(k, shape, fan_in):
        return jax.random.normal(k, shape, f32) * (fan_in ** -0.5)

    def gain(k, shape):
        return 1.0 + 0.05 * jax.random.normal(k, shape, f32)

    return {
        'x': jax.random.normal(ks[0], (BATCH, SEQ, D_MODEL), f32),
        'positions': jnp.broadcast_to(jnp.arange(SEQ, dtype=jnp.int32), (BATCH, SEQ)),
        'norm_mix': gain(ks[1], (DEPTH, D_MODEL)),
        'norm_mlp': gain(ks[2], (DEPTH, D_MODEL)),
        'mlp_w_up': w(ks[3], (DEPTH, D_MODEL, D_FF), D_MODEL),
        'mlp_w_down': w(ks[4], (DEPTH, D_FF, D_MODEL), D_FF),
        'hyb_w_in': w(ks[5], (ne, D_MODEL, EVEN_IN), D_MODEL),
        'hyb_w_out': w(ks[6], (ne, EVEN_MIX, D_MODEL), EVEN_MIX),
        'ret_gn_gain': gain(ks[7], (ne, RET_HEADS, RET_V_DIM)),
        'dil_q_gain': gain(ks[8], (ne, HEAD_DIM)),
        'dil_k_gain': gain(ks[9], (ne, HEAD_DIM)),
        'swa_w_qkv': w(ks[10], (no, D_MODEL, SWA_IN), D_MODEL),
        'swa_b_qkv': 0.02 * jax.random.normal(ks[11], (no, SWA_IN), f32),
        'swa_w_out': w(ks[12], (no, SWA_MIX, D_MODEL), SWA_MIX),
        'swa_q_gain': gain(ks[13], (no, HEAD_DIM)),
        'swa_k_gain': gain(ks[14], (no, HEAD_DIM)),
        'swa_sinks': 0.5 * jax.random.normal(ks[15], (no, SWA_Q_HEADS), f32),
    }


def reference(x, positions, norm_mix, norm_mlp, mlp_w_up, mlp_w_down, hyb_w_in, hyb_w_out,
              ret_gn_gain, dil_q_gain, dil_k_gain, swa_w_qkv, swa_b_qkv, swa_w_out,
              swa_q_gain, swa_k_gain, swa_sinks):
    for layer in range(DEPTH):
        i = layer // 2
        h = rmsnorm(x, norm_mix[layer])
        if layer % 2 == 0:
            x = x + even_mixer(h, positions, hyb_w_in[i], hyb_w_out[i], ret_gn_gain[i],
                               dil_q_gain[i], dil_k_gain[i])
        else:
            x = x + swa_mixer(h, positions, swa_w_qkv[i], swa_b_qkv[i], swa_w_out[i],
                              swa_q_gain[i], swa_k_gain[i], swa_sinks[i])
        h = rmsnorm(x, norm_mlp[layer])
        x = x + sq_relu_mlp(h, mlp_w_up[layer], mlp_w_down[layer])
    return x
```

```python
import functools

import jax
import jax.numpy as jnp
import numpy as np
from jax import lax
from jax.experimental import pallas as pl
from jax.experimental.pallas import tpu as pltpu

F32 = jnp.float32
BF16 = jnp.bfloat16

D_MODEL = 1024
D_FF = 4 * D_MODEL
HEAD_DIM = 64
EPS = 1e-6
RET_HEADS = 4
RET_QK_DIM = 64
RET_V_DIM = 128
RET_CHUNK = 128
RET_THETA = 10000.0
DIL_HEADS = 8
DIL_PATTERNS = ((128, 1), (512, 4), (2048, 16))
SWA_Q_HEADS = 16
SWA_KV_HEADS = 4
SWA_WINDOW = 128
ROPE_THETA = 500000.0
ROPE_DIMS = HEAD_DIM // 4

LANES = 128
QBLK = 128
VMEM_LIMIT_BYTES = 56 * 1024 * 1024
NEG_BIG = -1e30

ROW_TILE = 512
FF_CHUNK = 1024
PRO_ROWS = 256


def _cparams(n_axes):
    return pltpu.CompilerParams(
        dimension_semantics=("arbitrary",) * n_axes,
        vmem_limit_bytes=VMEM_LIMIT_BYTES,
    )


def _resident(shape):
    nd = len(shape)
    return pl.BlockSpec(shape, lambda *_: (0,) * nd, pipeline_mode=pl.Buffered(1))


def _trig_kernel(p_ref, inv_ref, cos_ref, sin_ref):
    ang = p_ref[...] * inv_ref[...]
    cos_ref[...] = jnp.cos(ang)
    sin_ref[...] = jnp.sin(ang)


def _trig(pos_rows, inv_rows):
    rows = pos_rows.shape[0]
    tile = 256
    assert rows % tile == 0
    spec = pl.BlockSpec((tile, LANES), lambda i: (i, 0))
    return pl.pallas_call(
        _trig_kernel,
        grid=(rows // tile,),
        in_specs=[spec, spec],
        out_specs=[spec, spec],
        out_shape=[jax.ShapeDtypeStruct((rows, LANES), F32)] * 2,
        compiler_params=_cparams(1),
        name="trig_tables",
    )(pos_rows, inv_rows)


def _rope_tables(positions):
    b, s = positions.shape
    posf = positions.astype(F32)
    half_r = RET_QK_DIM // 2
    half_p = ROPE_DIMS // 2
    inv_r = jnp.power(jnp.float32(RET_THETA), -jnp.arange(half_r, dtype=F32) * (2.0 / RET_QK_DIM))
    inv_p = jnp.power(jnp.float32(ROPE_THETA), -jnp.arange(half_p, dtype=F32) * (2.0 / ROPE_DIMS))

    def dense(half, inv):
        p = jnp.broadcast_to(posf[..., None], (b, s, half)).reshape(-1, LANES)
        i = jnp.broadcast_to(jnp.tile(inv, LANES // half)[None, :], p.shape)
        return p, i

    p_r, i_r = dense(half_r, inv_r)
    p_p, i_p = dense(half_p, inv_p)
    cos, sin = _trig(jnp.concatenate([p_r, p_p], 0), jnp.concatenate([i_r, i_p], 0))
    n_r = p_r.shape[0]
    cos_r = cos[:n_r].reshape(b, s, half_r)
    sin_r = sin[:n_r].reshape(b, s, half_r)
    cos_p = cos[n_r:].reshape(b, s, half_p)
    sin_p = sin[n_r:].reshape(b, s, half_p)

    def tables(c, sn, half):
        rest = HEAD_DIM - 2 * half
        one = jnp.ones((b, s, rest), F32)
        z_h = jnp.zeros((b, s, half), F32)
        z_r = jnp.zeros((b, s, rest), F32)
        ch = jnp.concatenate([c, c, one], -1)
        sa = jnp.concatenate([z_h, sn, z_r], -1)
        sb = jnp.concatenate([-sn, z_h, z_r], -1)
        two = lambda t: jnp.concatenate([t, t], -1)
        return two(ch), two(sa), two(sb)

    return tables(cos_r, sin_r, half_r), tables(cos_p, sin_p, half_p)


def _rotate(x, c, sa, sb, half):
    return x * c + pltpu.roll(x, half, 1) * sa + pltpu.roll(x, LANES - half, 1) * sb


def _lo_mask(shape):
    return lax.broadcasted_iota(jnp.int32, shape, len(shape) - 1) < HEAD_DIM


def _head_rms(x, lo):
    x2 = x * x
    s_lo = jnp.sum(jnp.where(lo, x2, 0.0), axis=-1, keepdims=True)
    s_hi = jnp.sum(jnp.where(lo, 0.0, x2), axis=-1, keepdims=True)
    return jnp.where(lo, s_lo, s_hi) * (1.0 / HEAD_DIM)


def _norm_proj_kernel(x_ref, g_ref, w_ref, b_ref, o_ref, *, n_chunk):
    x = x_ref[...]
    ms = jnp.mean(x * x, axis=-1, keepdims=True)
    h = (x * lax.rsqrt(ms + EPS) * g_ref[...]).astype(BF16)
    n = o_ref.shape[-1]
    for c in range(0, n, n_chunk):
        acc = jnp.dot(h, w_ref[:, c:c + n_chunk], preferred_element_type=F32)
        o_ref[:, c:c + n_chunk] = (acc + b_ref[:, c:c + n_chunk]).astype(o_ref.dtype)


def _norm_proj(x2d, gain, w_bf16, bias):
    m, d = x2d.shape
    n = w_bf16.shape[1]
    return pl.pallas_call(
        functools.partial(_norm_proj_kernel, n_chunk=512),
        grid=(m // ROW_TILE,),
        in_specs=[
            pl.BlockSpec((ROW_TILE, d), lambda i: (i, 0)),
            _resident((1, d)),
            _resident((d, n)),
            _resident((1, n)),
        ],
        out_specs=pl.BlockSpec((ROW_TILE, n), lambda i: (i, 0)),
        out_shape=jax.ShapeDtypeStruct((m, n), BF16),
        compiler_params=_cparams(1),
        name="norm_proj",
    )(x2d, gain.reshape(1, d), w_bf16, bias.reshape(1, n))


def _out_mlp_kernel(*refs, n_mix):
    a_refs = refs[:n_mix]
    x_ref, wo_ref, g_ref, wup_ref, wdn_ref, o_ref = refs[n_mix:]
    mixed = a_refs[0][...] if n_mix == 1 else jnp.concatenate([a[...] for a in a_refs], axis=-1)
    x1 = x_ref[...] + jnp.dot(mixed, wo_ref[...], preferred_element_type=F32)
    ms = jnp.mean(x1 * x1, axis=-1, keepdims=True)
    h = (x1 * lax.rsqrt(ms + EPS) * g_ref[...]).astype(BF16)
    y = x1
    for c in range(0, D_FF, FF_CHUNK):
        u = jnp.dot(h, wup_ref[:, c:c + FF_CHUNK], preferred_element_type=F32)
        u = jnp.square(jnp.maximum(u, 0.0)).astype(BF16)
        y = y + jnp.dot(u, wdn_ref[c:c + FF_CHUNK, :], preferred_element_type=F32)
    o_ref[...] = y


def _out_mlp(mixed, x2d, w_out, gain, w_up, w_down):
    m, d = x2d.shape
    n_mix = len(mixed)
    in_specs = [pl.BlockSpec((ROW_TILE, a.shape[1]), lambda i: (i, 0)) for a in mixed]
    in_specs.append(pl.BlockSpec((ROW_TILE, d), lambda i: (i, 0)))
    in_specs += [_resident(w_out.shape), _resident((1, d)), _resident(w_up.shape), _resident(w_down.shape)]
    return pl.pallas_call(
        functools.partial(_out_mlp_kernel, n_mix=n_mix),
        grid=(m // ROW_TILE,),
        in_specs=in_specs,
        out_specs=pl.BlockSpec((ROW_TILE, d), lambda i: (i, 0)),
        out_shape=jax.ShapeDtypeStruct((m, d), F32),
        compiler_params=_cparams(1),
        name="out_mlp",
    )(*mixed, x2d, w_out, gain.reshape(1, d), w_up, w_down)


def _retention_kernel(q_ref, k_ref, v_ref, g_ref, c_ref, sa_ref, sb_ref, lg_ref, gn_ref,
                      o_ref, q_s, k_s, kt_s, *, seq):
    n_chunks = seq // RET_CHUNK
    half = RET_QK_DIM // 2
    cs = RET_CHUNK

    def prologue(n, carry):
        r0 = pl.multiple_of(n * cs, cs)
        rows = pl.ds(r0, cs)
        c, sa, sb = c_ref[0, rows, :], sa_ref[0, rows, :], sb_ref[0, rows, :]
        q = _rotate(q_ref[0, rows, :].astype(F32), c, sa, sb, half)
        k = _rotate(k_ref[0, rows, :].astype(F32), c, sa, sb, half) * (RET_QK_DIM ** -0.5)
        q_s[n] = q.astype(BF16)
        k_s[n] = k.astype(BF16)
        kt_s[n] = k.T
        return carry

    lax.fori_loop(0, n_chunks, prologue, 0)

    lo = _lo_mask((cs, LANES))
    row = lax.broadcasted_iota(jnp.int32, (cs, cs), 0).astype(F32)
    col = lax.broadcasted_iota(jnp.int32, (cs, cs), 1).astype(F32)
    diff = row - col
    consts = []
    for hh in range(2):
        lg = lg_ref[0, hh:hh + 1, :]
        decay = jnp.where(diff >= 0, jnp.exp(lg * jnp.maximum(diff, 0.0)), 0.0)
        xi = jnp.exp(lg * (row + 1.0))
        zeta = jnp.exp(lg * (cs - 1.0 - col[0:1, :]))
        cd = jnp.exp(lg * float(cs))
        head = lo if hh == 0 else jnp.logical_not(lo)
        consts.append((decay, xi, zeta, cd, head))

    def body(n, state):
        r0 = pl.multiple_of(n * cs, cs)
        rows = pl.ds(r0, cs)
        qc = q_s[n]
        kc = k_s[n]
        ktc = kt_s[n]
        new_state = []
        for hh in range(2):
            decay, xi, zeta, cd, head = consts[hh]
            cols = slice(hh * RET_V_DIM, (hh + 1) * RET_V_DIM)
            r_prev = state[hh]
            qm = jnp.where(head, qc, jnp.zeros_like(qc))
            vc = v_ref[0, rows, cols]
            sc = lax.dot_general(qm, kc, (((1,), (1,)), ((), ())), preferred_element_type=F32) * decay
            o = jnp.dot(sc.astype(BF16), vc, preferred_element_type=F32)
            o = o + jnp.dot(qm, r_prev.astype(BF16), preferred_element_type=F32) * xi
            kz = (ktc * zeta).astype(BF16)
            new_state.append(r_prev * cd + jnp.dot(kz, vc, preferred_element_type=F32))
            mu = jnp.mean(o, axis=-1, keepdims=True)
            dev = o - mu
            var = jnp.mean(dev * dev, axis=-1, keepdims=True)
            y = dev * lax.rsqrt(var + EPS) * gn_ref[0, hh:hh + 1, :]
            gate = g_ref[0, rows, cols].astype(F32)
            o_ref[0, rows, cols] = (gate * jax.nn.sigmoid(gate) * y).astype(o_ref.dtype)
        return tuple(new_state)

    zero = jnp.zeros((LANES, RET_V_DIM), F32)
    lax.fori_loop(0, n_chunks, body, (zero, zero))


def _retention(proj, tables, log_gamma, gn_gain):
    b, s, _ = proj.shape
    n_pairs = RET_HEADS // 2
    pair_w = 2 * RET_V_DIM
    qk_tiles = RET_HEADS * RET_QK_DIM // LANES
    v_off = 2 * qk_tiles * LANES // pair_w
    g_off = v_off + RET_HEADS * RET_V_DIM // pair_w
    tab = pl.BlockSpec((1, s, LANES), lambda bi, p: (bi, 0, 0), pipeline_mode=pl.Buffered(1))
    lg = jnp.broadcast_to(log_gamma.reshape(n_pairs, 2, 1), (n_pairs, 2, LANES))
    n_chunks = s // RET_CHUNK
    return pl.pallas_call(
        functools.partial(_retention_kernel, seq=s),
        grid=(b, n_pairs),
        in_specs=[
            pl.BlockSpec((1, s, LANES), lambda bi, p: (bi, 0, p)),
            pl.BlockSpec((1, s, LANES), lambda bi, p: (bi, 0, qk_tiles + p)),
            pl.BlockSpec((1, s, pair_w), lambda bi, p: (bi, 0, v_off + p)),
            pl.BlockSpec((1, s, pair_w), lambda bi, p: (bi, 0, g_off + p)),
            tab, tab, tab,
            pl.BlockSpec((1, 2, LANES), lambda bi, p: (p, 0, 0)),
            pl.BlockSpec((1, 2, RET_V_DIM), lambda bi, p: (p, 0, 0)),
        ],
        out_specs=pl.BlockSpec((1, s, pair_w), lambda bi, p: (bi, 0, p)),
        out_shape=jax.ShapeDtypeStruct((b, s, RET_HEADS * RET_V_DIM), BF16),
        scratch_shapes=[
            pltpu.VMEM((n_chunks, RET_CHUNK, LANES), BF16),
            pltpu.VMEM((n_chunks, RET_CHUNK, LANES), BF16),
            pltpu.VMEM((n_chunks, LANES, RET_CHUNK), F32),
        ],
        compiler_params=_cparams(2),
        name="retention",
    )(proj, proj, proj, proj, *tables, lg, gn_gain.reshape(n_pairs, 2, RET_V_DIM))


def _band_bias(lo_off, hi_off):
    a = lax.broadcasted_iota(jnp.int32, (QBLK, 2 * QBLK), 0)
    c = lax.broadcasted_iota(jnp.int32, (QBLK, 2 * QBLK), 1)
    band = (c - a >= lo_off) & (c - a <= hi_off)
    later = jnp.where(band, 0.0, NEG_BIG).astype(F32)
    first = jnp.where(band & (c >= QBLK), 0.0, NEG_BIG).astype(F32)
    return first, later


def _dilated_kernel(q_ref, k_ref, v_ref, c_ref, sa_ref, sb_ref, qg_ref, kg_ref, o_ref,
                    qn_s, kn_s, vn_s, qc_s, kc_s, vc_s, bias_s,
                    o0_s, o1_s, o2_s, l0_s, l1_s, l2_s, *, seq):
    half = ROPE_DIMS // 2
    lo_p = _lo_mask((PRO_ROWS, LANES))

    def prologue(n, carry):
        r0 = pl.multiple_of(n * PRO_ROWS, PRO_ROWS)
        rows = pl.ds(r0, PRO_ROWS)
        c, sa, sb = c_ref[0, rows, :], sa_ref[0, rows, :], sb_ref[0, rows, :]
        q = q_ref[0, rows, :].astype(F32)
        q = q * lax.rsqrt(_head_rms(q, lo_p) + EPS) * qg_ref[...]
        qn_s[rows, :] = _rotate(q, c, sa, sb, half) * (HEAD_DIM ** -0.5)
        k = k_ref[0, rows, :].astype(F32)
        k = k * lax.rsqrt(_head_rms(k, lo_p) + EPS) * kg_ref[...]
        kn_s[rows, :] = _rotate(k, c, sa, sb, half)
        vn_s[rows, :] = v_ref[0, rows, :].astype(F32)
        return carry

    lax.fori_loop(0, seq // PRO_ROWS, prologue, 0)

    first, later = _band_bias(0, QBLK)
    bias_s[0] = first
    bias_s[1] = later
    zero_pad = jnp.zeros((QBLK, LANES), BF16)
    kc_s[0:QBLK, :] = zero_pad
    vc_s[0:QBLK, :] = zero_pad
    lo = _lo_mask((QBLK, LANES))
    heads = (lo, jnp.logical_not(lo))

    outs = (o0_s, o1_s, o2_s)
    lses = (l0_s, l1_s, l2_s)
    for (window, r), on_s, ln_s in zip(DIL_PATTERNS, outs, lses):
        assert window // r == QBLK
        cls_len = seq // r
        n_blk = cls_len // QBLK

        def per_class(j, carry, r=r, n_blk=n_blk, on_s=on_s, ln_s=ln_s):
            def gather(i, c2):
                src = pl.ds(j + r * QBLK * i, QBLK, stride=r)
                d0 = pl.multiple_of(i * QBLK, QBLK)
                qc_s[pl.ds(d0, QBLK), :] = qn_s[src, :].astype(BF16)
                kc_s[pl.ds(d0 + QBLK, QBLK), :] = kn_s[src, :].astype(BF16)
                vc_s[pl.ds(d0 + QBLK, QBLK), :] = vn_s[src, :].astype(BF16)
                return c2

            lax.fori_loop(0, n_blk, gather, 0)

            def block(i, c2):
                d0 = pl.multiple_of(i * QBLK, QBLK)
                qb = qc_s[pl.ds(d0, QBLK), :]
                kw = kc_s[pl.ds(d0, 2 * QBLK), :]
                vw = vc_s[pl.ds(d0, 2 * QBLK), :]
                bias = bias_s[jnp.minimum(i, 1)]
                res = []
                for head in heads:
                    qm = jnp.where(head, qb, jnp.zeros_like(qb))
                    sc = lax.dot_general(qm, kw, (((1,), (1,)), ((), ())),
                                         preferred_element_type=F32) + bias
                    m = jnp.max(sc, axis=-1, keepdims=True)
                    p = jnp.exp(sc - m)
                    den = jnp.sum(p, axis=-1, keepdims=True)
                    o = jnp.dot(p.astype(BF16), vw, preferred_element_type=F32) / den
                    res.append((o, m + jnp.log(den)))
                dst = pl.ds(j + r * QBLK * i, QBLK, stride=r)
                on_s[dst, :] = jnp.where(lo, res[0][0], res[1][0])
                ln_s[dst, :] = jnp.where(lo, res[0][1], res[1][1])
                return c2

            lax.fori_loop(0, n_blk, block, 0)
            return carry

        lax.fori_loop(0, r, per_class, 0)

    def combine(n, carry):
        r0 = pl.multiple_of(n * PRO_ROWS, PRO_ROWS)
        rows = pl.ds(r0, PRO_ROWS)
        ls = [l_s[rows, :] for l_s in lses]
        m = jnp.maximum(jnp.maximum(ls[0], ls[1]), ls[2])
        es = [jnp.exp(l - m) for l in ls]
        num = es[0] * o0_s[rows, :] + es[1] * o1_s[rows, :] + es[2] * o2_s[rows, :]
        o_ref[0, rows, :] = (num / (es[0] + es[1] + es[2])).astype(o_ref.dtype)
        return carry

    lax.fori_loop(0, seq // PRO_ROWS, combine, 0)


def _dilated(proj, tables, q_gain, k_gain, col0):
    b, s, _ = proj.shape
    n_pairs = DIL_HEADS * HEAD_DIM // LANES
    t0 = col0 // LANES
    tab = pl.BlockSpec((1, s, LANES), lambda bi, p: (bi, 0, 0), pipeline_mode=pl.Buffered(1))
    two = lambda g: jnp.concatenate([g, g]).reshape(1, LANES)
    nat = pltpu.VMEM((s, LANES), F32)
    return pl.pallas_call(
        functools.partial(_dilated_kernel, seq=s),
        grid=(b, n_pairs),
        in_specs=[
            pl.BlockSpec((1, s, LANES), lambda bi, p: (bi, 0, t0 + p)),
            pl.BlockSpec((1, s, LANES), lambda bi, p: (bi, 0, t0 + n_pairs + p)),
            pl.BlockSpec((1, s, LANES), lambda bi, p: (bi, 0, t0 + 2 * n_pairs + p)),
            tab, tab, tab,
            _resident((1, LANES)),
            _resident((1, LANES)),
        ],
        out_specs=pl.BlockSpec((1, s, LANES), lambda bi, p: (bi, 0, p)),
        out_shape=jax.ShapeDtypeStruct((b, s, DIL_HEADS * HEAD_DIM), BF16),
        scratch_shapes=[
            nat, nat, nat,
            pltpu.VMEM((s, LANES), BF16),
            pltpu.VMEM((s + QBLK, LANES), BF16),
            pltpu.VMEM((s + QBLK, LANES), BF16),
            pltpu.VMEM((2, QBLK, 2 * QBLK), F32),
            nat, nat, nat, nat, nat, nat,
        ],
        compiler_params=_cparams(2),
        name="dilated_attention",
    )(proj, proj, proj, *tables, two(q_gain), two(k_gain))


def _swa_kernel(q_ref, k_ref, v_ref, c_ref, sa_ref, sb_ref, qg_ref, kg_ref, sink_ref, o_ref,
                q_s, kd_s, vd_s, bias_s, *, seq):
    half = ROPE_DIMS // 2
    group = SWA_Q_HEADS // SWA_KV_HEADS
    lo_p = _lo_mask((PRO_ROWS, LANES))
    kv_in_hi = (pl.program_id(1) % 2) == 1

    def both_halves(x):
        x = jnp.where(kv_in_hi, pltpu.roll(x, HEAD_DIM, 1), x)
        return jnp.where(lo_p, x, pltpu.roll(x, HEAD_DIM, 1))

    def prologue(n, carry):
        r0 = pl.multiple_of(n * PRO_ROWS, PRO_ROWS)
        rows = pl.ds(r0, PRO_ROWS)
        c, sa, sb = c_ref[0, rows, :], sa_ref[0, rows, :], sb_ref[0, rows, :]
        for t in range(group // 2):
            cols = slice(t * LANES, (t + 1) * LANES)
            q = q_ref[0, rows, cols].astype(F32)
            q = q * lax.rsqrt(_head_rms(q, lo_p) + EPS) * qg_ref[...]
            q_s[rows, cols] = (_rotate(q, c, sa, sb, half) * (HEAD_DIM ** -0.5)).astype(BF16)
        k = k_ref[0, rows, :].astype(F32)
        k = k * lax.rsqrt(_head_rms(k, lo_p) + EPS) * kg_ref[...]
        k = _rotate(k, c, sa, sb, half)
        kd_s[pl.ds(r0 + QBLK, PRO_ROWS), :] = both_halves(k).astype(BF16)
        vd_s[pl.ds(r0 + QBLK, PRO_ROWS), :] = both_halves(v_ref[0, rows, :].astype(F32)).astype(BF16)
        return carry

    lax.fori_loop(0, seq // PRO_ROWS, prologue, 0)

    first, later = _band_bias(1, QBLK)
    bias_s[0] = first
    bias_s[1] = later
    zero_pad = jnp.zeros((QBLK, LANES), BF16)
    kd_s[0:QBLK, :] = zero_pad
    vd_s[0:QBLK, :] = zero_pad
    lo = _lo_mask((QBLK, LANES))
    heads = (lo, jnp.logical_not(lo))

    def block(i, carry):
        d0 = pl.multiple_of(i * QBLK, QBLK)
        rows = pl.ds(d0, QBLK)
        kw = kd_s[pl.ds(d0, 2 * QBLK), :]
        vw = vd_s[pl.ds(d0, 2 * QBLK), :]
        bias = bias_s[jnp.minimum(i, 1)]
        stack = []
        for t in range(group // 2):
            qt = q_s[rows, t * LANES:(t + 1) * LANES]
            for head in heads:
                stack.append(jnp.where(head, qt, jnp.zeros_like(qt)))
        qm = jnp.concatenate(stack, axis=0)
        sc = lax.dot_general(qm, kw, (((1,), (1,)), ((), ())), preferred_element_type=F32)
        ps, dens = [], []
        for hh in range(group):
            s_h = sc[hh * QBLK:(hh + 1) * QBLK, :] + bias
            sink = sink_ref[0, hh:hh + 1, 0:1]
            m = jnp.maximum(jnp.max(s_h, axis=-1, keepdims=True), sink)
            p = jnp.exp(s_h - m)
            dens.append(jnp.sum(p, axis=-1, keepdims=True) + jnp.exp(sink - m))
            ps.append(p.astype(BF16))
        o = jnp.dot(jnp.concatenate(ps, axis=0), vw, preferred_element_type=F32)
        o_h = [o[hh * QBLK:(hh + 1) * QBLK, :] / dens[hh] for hh in range(group)]
        for t in range(group // 2):
            o_ref[0, rows, t * LANES:(t + 1) * LANES] = jnp.where(
                lo, o_h[2 * t], o_h[2 * t + 1]).astype(o_ref.dtype)
        return carry

    lax.fori_loop(0, seq // QBLK, block, 0)


def _swa(proj, tables, q_gain, k_gain, sinks):
    b, s, _ = proj.shape
    group = SWA_Q_HEADS // SWA_KV_HEADS
    q_w = group * HEAD_DIM
    k_t0 = SWA_Q_HEADS * HEAD_DIM // LANES
    v_t0 = k_t0 + SWA_KV_HEADS * HEAD_DIM // LANES
    tab = pl.BlockSpec((1, s, LANES), lambda bi, g: (bi, 0, 0), pipeline_mode=pl.Buffered(1))
    two = lambda g: jnp.concatenate([g, g]).reshape(1, LANES)
    sink_rows = jnp.broadcast_to(sinks.reshape(SWA_KV_HEADS, group, 1), (SWA_KV_HEADS, group, LANES))
    return pl.pallas_call(
        functools.partial(_swa_kernel, seq=s),
        grid=(b, SWA_KV_HEADS),
        in_specs=[
            pl.BlockSpec((1, s, q_w), lambda bi, g: (bi, 0, g)),
            pl.BlockSpec((1, s, LANES), lambda bi, g: (bi, 0, k_t0 + g // 2)),
            pl.BlockSpec((1, s, LANES), lambda bi, g: (bi, 0, v_t0 + g // 2)),
            tab, tab, tab,
            _resident((1, LANES)),
            _resident((1, LANES)),
            pl.BlockSpec((1, group, LANES), lambda bi, g: (g, 0, 0)),
        ],
        out_specs=pl.BlockSpec((1, s, q_w), lambda bi, g: (bi, 0, g)),
        out_shape=jax.ShapeDtypeStruct((b, s, SWA_Q_HEADS * HEAD_DIM), BF16),
        scratch_shapes=[
            pltpu.VMEM((s, q_w), BF16),
            pltpu.VMEM((s + QBLK, LANES), BF16),
            pltpu.VMEM((s + QBLK, LANES), BF16),
            pltpu.VMEM((2, QBLK, 2 * QBLK), F32),
        ],
        compiler_params=_cparams(2),
        name="swa_attention",
    )(proj, proj, proj, *tables, two(q_gain), two(k_gain), sink_rows)


def kernel(x, positions, norm_mix, norm_mlp, mlp_w_up, mlp_w_down, hyb_w_in, hyb_w_out, ret_gn_gain, dil_q_gain, dil_k_gain, swa_w_qkv, swa_b_qkv, swa_w_out, swa_q_gain, swa_k_gain, swa_sinks):
    b, s, d = x.shape
    depth = norm_mix.shape[0]
    ret_tab, rope_tab = _rope_tables(positions)
    log_gamma = jnp.log1p(-jnp.exp2(-5.0 - jnp.arange(RET_HEADS, dtype=F32)))
    ret_w = RET_HEADS * RET_V_DIM
    dil_col0 = 2 * RET_HEADS * RET_QK_DIM + 2 * ret_w

    x2d = x.reshape(b * s, d)
    for layer in range(depth):
        i = layer // 2
        if layer % 2 == 0:
            w_in = hyb_w_in[i].astype(BF16)
            proj = _norm_proj(x2d, norm_mix[layer], w_in, jnp.zeros((w_in.shape[1],), F32))
            proj = proj.reshape(b, s, -1)
            ra = _retention(proj, ret_tab, log_gamma, ret_gn_gain[i])
            da = _dilated(proj, rope_tab, dil_q_gain[i], dil_k_gain[i], dil_col0)
            w_out = hyb_w_out[i].astype(BF16)
            mixed = [ra.reshape(b * s, -1), da.reshape(b * s, -1)]
        else:
            proj = _norm_proj(x2d, norm_mix[layer], swa_w_qkv[i].astype(BF16), swa_b_qkv[i])
            proj = proj.reshape(b, s, -1)
            att = _swa(proj, rope_tab, swa_q_gain[i], swa_k_gain[i], swa_sinks[i])
            mixed = [att.reshape(b * s, -1)]
            w_out = swa_w_out[i].astype(BF16)
        x2d = _out_mlp(mixed, x2d, w_out, norm_mlp[layer],
                       mlp_w_up[layer].astype(BF16), mlp_w_down[layer].astype(BF16))
    return x2d.reshape(b, s, d)
```

```python
import functools

import jax
import jax.numpy as jnp
import numpy as np
from jax import lax
from jax.experimental import pallas as pl
from jax.experimental.pallas import tpu as pltpu

F32 = jnp.float32
BF16 = jnp.bfloat16

D_MODEL = 1024
D_FF = 4 * D_MODEL
HEAD_DIM = 64
EPS = 1e-6
RET_HEADS = 4
RET_QK_DIM = 64
RET_V_DIM = 128
RET_CHUNK = 128
RET_THETA = 10000.0
DIL_HEADS = 8
DIL_PATTERNS = ((128, 1), (512, 4), (2048, 16))
SWA_Q_HEADS = 16
SWA_KV_HEADS = 4
SWA_WINDOW = 128
ROPE_THETA = 500000.0
ROPE_DIMS = HEAD_DIM // 4

LANES = 128
QBLK = 128
VMEM_LIMIT_BYTES = 56 * 1024 * 1024
NEG_BIG = -1e30

ROW_TILE = 512
FF_CHUNK = 1024
PRO_ROWS = 256
BLOCK_UNROLL = 4


def _cparams(n_axes):
    return pltpu.CompilerParams(
        dimension_semantics=("arbitrary",) * n_axes,
        vmem_limit_bytes=VMEM_LIMIT_BYTES,
    )


def _resident(shape):
    nd = len(shape)
    return pl.BlockSpec(shape, lambda *_: (0,) * nd, pipeline_mode=pl.Buffered(1))


def _trig_kernel(p_ref, inv_ref, cos_ref, sin_ref):
    ang = p_ref[...] * inv_ref[...]
    cos_ref[...] = jnp.cos(ang)
    sin_ref[...] = jnp.sin(ang)


def _trig(pos_rows, inv_rows):
    rows = pos_rows.shape[0]
    tile = 256
    assert rows % tile == 0
    spec = pl.BlockSpec((tile, LANES), lambda i: (i, 0))
    return pl.pallas_call(
        _trig_kernel,
        grid=(rows // tile,),
        in_specs=[spec, spec],
        out_specs=[spec, spec],
        out_shape=[jax.ShapeDtypeStruct((rows, LANES), F32)] * 2,
        compiler_params=_cparams(1),
        name="trig_tables",
    )(pos_rows, inv_rows)


def _rope_tables(positions):
    b, s = positions.shape
    posf = positions.astype(F32)
    half_r = RET_QK_DIM // 2
    half_p = ROPE_DIMS // 2
    inv_r = jnp.power(jnp.float32(RET_THETA), -jnp.arange(half_r, dtype=F32) * (2.0 / RET_QK_DIM))
    inv_p = jnp.power(jnp.float32(ROPE_THETA), -jnp.arange(half_p, dtype=F32) * (2.0 / ROPE_DIMS))

    def dense(half, inv):
        p = jnp.broadcast_to(posf[..., None], (b, s, half)).reshape(-1, LANES)
        i = jnp.broadcast_to(jnp.tile(inv, LANES // half)[None, :], p.shape)
        return p, i

    p_r, i_r = dense(half_r, inv_r)
    p_p, i_p = dense(half_p, inv_p)
    cos, sin = _trig(jnp.concatenate([p_r, p_p], 0), jnp.concatenate([i_r, i_p], 0))
    n_r = p_r.shape[0]
    cos_r = cos[:n_r].reshape(b, s, half_r)
    sin_r = sin[:n_r].reshape(b, s, half_r)
    cos_p = cos[n_r:].reshape(b, s, half_p)
    sin_p = sin[n_r:].reshape(b, s, half_p)

    def tables(c, sn, half):
        rest = HEAD_DIM - 2 * half
        one = jnp.ones((b, s, rest), F32)
        z_h = jnp.zeros((b, s, half), F32)
        z_r = jnp.zeros((b, s, rest), F32)
        ch = jnp.concatenate([c, c, one], -1)
        sa = jnp.concatenate([z_h, sn, z_r], -1)
        sb = jnp.concatenate([-sn, z_h, z_r], -1)
        two = lambda t: jnp.concatenate([t, t], -1)
        return two(ch), two(sa), two(sb)

    return tables(cos_r, sin_r, half_r), tables(cos_p, sin_p, half_p)


def _rotate(x, c, sa, sb, half):
    return x * c + pltpu.roll(x, half, 1) * sa + pltpu.roll(x, LANES - half, 1) * sb


def _lo_mask(shape):
    return lax.broadcasted_iota(jnp.int32, shape, len(shape) - 1) < HEAD_DIM


def _head_rms(x, lo):
    x2 = x * x
    s_lo = jnp.sum(jnp.where(lo, x2, 0.0), axis=-1, keepdims=True)
    s_hi = jnp.sum(jnp.where(lo, 0.0, x2), axis=-1, keepdims=True)
    return jnp.where(lo, s_lo, s_hi) * (1.0 / HEAD_DIM)


def _norm_proj_kernel(x_ref, g_ref, w_ref, b_ref, o_ref, *, n_chunk):
    x = x_ref[...]
    ms = jnp.mean(x * x, axis=-1, keepdims=True)
    h = (x * lax.rsqrt(ms + EPS) * g_ref[...]).astype(BF16)
    n = o_ref.shape[-1]
    for c in range(0, n, n_chunk):
        acc = jnp.dot(h, w_ref[:, c:c + n_chunk], preferred_element_type=F32)
        o_ref[:, c:c + n_chunk] = (acc + b_ref[:, c:c + n_chunk]).astype(o_ref.dtype)


def _norm_proj(x2d, gain, w_bf16, bias):
    m, d = x2d.shape
    n = w_bf16.shape[1]
    return pl.pallas_call(
        functools.partial(_norm_proj_kernel, n_chunk=512),
        grid=(m // ROW_TILE,),
        in_specs=[
            pl.BlockSpec((ROW_TILE, d), lambda i: (i, 0)),
            _resident((1, d)),
            _resident((d, n)),
            _resident((1, n)),
        ],
        out_specs=pl.BlockSpec((ROW_TILE, n), lambda i: (i, 0)),
        out_shape=jax.ShapeDtypeStruct((m, n), BF16),
        compiler_params=_cparams(1),
        name="norm_proj",
    )(x2d, gain.reshape(1, d), w_bf16, bias.reshape(1, n))


def _out_mlp_kernel(*refs, n_mix):
    a_refs = refs[:n_mix]
    x_ref, wo_ref, g_ref, wup_ref, wdn_ref, o_ref = refs[n_mix:]
    mixed = a_refs[0][...] if n_mix == 1 else jnp.concatenate([a[...] for a in a_refs], axis=-1)
    x1 = x_ref[...] + jnp.dot(mixed, wo_ref[...], preferred_element_type=F32)
    ms = jnp.mean(x1 * x1, axis=-1, keepdims=True)
    h = (x1 * lax.rsqrt(ms + EPS) * g_ref[...]).astype(BF16)
    y = x1
    for c in range(0, D_FF, FF_CHUNK):
        u = jnp.dot(h, wup_ref[:, c:c + FF_CHUNK], preferred_element_type=F32)
        u = jnp.square(jnp.maximum(u, 0.0)).astype(BF16)
        y = y + jnp.dot(u, wdn_ref[c:c + FF_CHUNK, :], preferred_element_type=F32)
    o_ref[...] = y


def _out_mlp(mixed, x2d, w_out, gain, w_up, w_down):
    m, d = x2d.shape
    n_mix = len(mixed)
    in_specs = [pl.BlockSpec((ROW_TILE, a.shape[1]), lambda i: (i, 0)) for a in mixed]
    in_specs.append(pl.BlockSpec((ROW_TILE, d), lambda i: (i, 0)))
    in_specs += [_resident(w_out.shape), _resident((1, d)), _resident(w_up.shape), _resident(w_down.shape)]
    return pl.pallas_call(
        functools.partial(_out_mlp_kernel, n_mix=n_mix),
        grid=(m // ROW_TILE,),
        in_specs=in_specs,
        out_specs=pl.BlockSpec((ROW_TILE, d), lambda i: (i, 0)),
        out_shape=jax.ShapeDtypeStruct((m, d), F32),
        compiler_params=_cparams(1),
        name="out_mlp",
    )(*mixed, x2d, w_out, gain.reshape(1, d), w_up, w_down)


def _retention_kernel(q_ref, k_ref, v_ref, g_ref, c_ref, sa_ref, sb_ref, lg_ref, gn_ref,
                      o_ref, q_s, k_s, kt_s, *, seq):
    n_chunks = seq // RET_CHUNK
    half = RET_QK_DIM // 2
    cs = RET_CHUNK

    def prologue(n, carry):
        r0 = pl.multiple_of(n * cs, cs)
        rows = pl.ds(r0, cs)
        c, sa, sb = c_ref[0, rows, :], sa_ref[0, rows, :], sb_ref[0, rows, :]
        q = _rotate(q_ref[0, rows, :].astype(F32), c, sa, sb, half)
        k = _rotate(k_ref[0, rows, :].astype(F32), c, sa, sb, half) * (RET_QK_DIM ** -0.5)
        q_s[n] = q.astype(BF16)
        k_s[n] = k.astype(BF16)
        kt_s[n] = k.T
        return carry

    lax.fori_loop(0, n_chunks, prologue, 0)

    lo = _lo_mask((cs, LANES))
    row = lax.broadcasted_iota(jnp.int32, (cs, cs), 0).astype(F32)
    col = lax.broadcasted_iota(jnp.int32, (cs, cs), 1).astype(F32)
    diff = row - col
    consts = []
    for hh in range(2):
        lg = lg_ref[0, hh:hh + 1, :]
        decay = jnp.where(diff >= 0, jnp.exp(lg * jnp.maximum(diff, 0.0)), 0.0)
        xi = jnp.exp(lg * (row + 1.0))
        zeta = jnp.exp(lg * (cs - 1.0 - col[0:1, :]))
        cd = jnp.exp(lg * float(cs))
        head = lo if hh == 0 else jnp.logical_not(lo)
        consts.append((decay, xi, zeta, cd, head))

    def body(n, state):
        r0 = pl.multiple_of(n * cs, cs)
        rows = pl.ds(r0, cs)
        qc = q_s[n]
        kc = k_s[n]
        ktc = kt_s[n]
        new_state = []
        for hh in range(2):
            decay, xi, zeta, cd, head = consts[hh]
            cols = slice(hh * RET_V_DIM, (hh + 1) * RET_V_DIM)
            r_prev = state[hh]
            qm = jnp.where(head, qc, jnp.zeros_like(qc))
            vc = v_ref[0, rows, cols]
            sc = lax.dot_general(qm, kc, (((1,), (1,)), ((), ())), preferred_element_type=F32) * decay
            o = jnp.dot(sc.astype(BF16), vc, preferred_element_type=F32)
            o = o + jnp.dot(qm, r_prev.astype(BF16), preferred_element_type=F32) * xi
            kz = (ktc * zeta).astype(BF16)
            new_state.append(r_prev * cd + jnp.dot(kz, vc, preferred_element_type=F32))
            mu = jnp.mean(o, axis=-1, keepdims=True)
            dev = o - mu
            var = jnp.mean(dev * dev, axis=-1, keepdims=True)
            y = dev * lax.rsqrt(var + EPS) * gn_ref[0, hh:hh + 1, :]
            gate = g_ref[0, rows, cols].astype(F32)
            o_ref[0, rows, cols] = (gate * jax.nn.sigmoid(gate) * y).astype(o_ref.dtype)
        return tuple(new_state)

    zero = jnp.zeros((LANES, RET_V_DIM), F32)
    lax.fori_loop(0, n_chunks, body, (zero, zero))


def _retention(proj, tables, log_gamma, gn_gain):
    b, s, _ = proj.shape
    n_pairs = RET_HEADS // 2
    pair_w = 2 * RET_V_DIM
    qk_tiles = RET_HEADS * RET_QK_DIM // LANES
    v_off = 2 * qk_tiles * LANES // pair_w
    g_off = v_off + RET_HEADS * RET_V_DIM // pair_w
    tab = pl.BlockSpec((1, s, LANES), lambda bi, p: (bi, 0, 0), pipeline_mode=pl.Buffered(1))
    lg = jnp.broadcast_to(log_gamma.reshape(n_pairs, 2, 1), (n_pairs, 2, LANES))
    n_chunks = s // RET_CHUNK
    return pl.pallas_call(
        functools.partial(_retention_kernel, seq=s),
        grid=(b, n_pairs),
        in_specs=[
            pl.BlockSpec((1, s, LANES), lambda bi, p: (bi, 0, p)),
            pl.BlockSpec((1, s, LANES), lambda bi, p: (bi, 0, qk_tiles + p)),
            pl.BlockSpec((1, s, pair_w), lambda bi, p: (bi, 0, v_off + p)),
            pl.BlockSpec((1, s, pair_w), lambda bi, p: (bi, 0, g_off + p)),
            tab, tab, tab,
            pl.BlockSpec((1, 2, LANES), lambda bi, p: (p, 0, 0)),
            pl.BlockSpec((1, 2, RET_V_DIM), lambda bi, p: (p, 0, 0)),
        ],
        out_specs=pl.BlockSpec((1, s, pair_w), lambda bi, p: (bi, 0, p)),
        out_shape=jax.ShapeDtypeStruct((b, s, RET_HEADS * RET_V_DIM), BF16),
        scratch_shapes=[
            pltpu.VMEM((n_chunks, RET_CHUNK, LANES), BF16),
            pltpu.VMEM((n_chunks, RET_CHUNK, LANES), BF16),
            pltpu.VMEM((n_chunks, LANES, RET_CHUNK), F32),
        ],
        compiler_params=_cparams(2),
        name="retention",
    )(proj, proj, proj, proj, *tables, lg, gn_gain.reshape(n_pairs, 2, RET_V_DIM))


def _band_bias(lo_off, hi_off):
    a = lax.broadcasted_iota(jnp.int32, (QBLK, 2 * QBLK), 0)
    c = lax.broadcasted_iota(jnp.int32, (QBLK, 2 * QBLK), 1)
    band = (c - a >= lo_off) & (c - a <= hi_off)
    later = jnp.where(band, 0.0, NEG_BIG).astype(F32)
    first = jnp.where(band & (c >= QBLK), 0.0, NEG_BIG).astype(F32)
    return first, later


def _band_bias_t(lo_off, hi_off):
    c = lax.broadcasted_iota(jnp.int32, (2 * QBLK, QBLK), 0)
    a = lax.broadcasted_iota(jnp.int32, (2 * QBLK, QBLK), 1)
    band = (c - a >= lo_off) & (c - a <= hi_off)
    later = jnp.where(band, 0.0, NEG_BIG).astype(F32)
    first = jnp.where(band & (c >= QBLK), 0.0, NEG_BIG).astype(F32)
    return first, later


def _dilated_kernel(q_ref, k_ref, v_ref, c_ref, sa_ref, sb_ref, qg_ref, kg_ref, o_ref,
                    qn_s, kn_s, vn_s, qc_s, kc_s, vc_s, bias_s,
                    o0_s, o1_s, o2_s, l0_s, l1_s, l2_s, *, seq):
    half = ROPE_DIMS // 2
    lo_p = _lo_mask((PRO_ROWS, LANES))

    def prologue(n, carry):
        r0 = pl.multiple_of(n * PRO_ROWS, PRO_ROWS)
        rows = pl.ds(r0, PRO_ROWS)
        c, sa, sb = c_ref[0, rows, :], sa_ref[0, rows, :], sb_ref[0, rows, :]
        q = q_ref[0, rows, :].astype(F32)
        q = q * lax.rsqrt(_head_rms(q, lo_p) + EPS) * qg_ref[...]
        qn_s[rows, :] = _rotate(q, c, sa, sb, half) * (HEAD_DIM ** -0.5)
        k = k_ref[0, rows, :].astype(F32)
        k = k * lax.rsqrt(_head_rms(k, lo_p) + EPS) * kg_ref[...]
        kn_s[rows, :] = _rotate(k, c, sa, sb, half)
        vn_s[rows, :] = v_ref[0, rows, :].astype(F32)
        return carry

    lax.fori_loop(0, seq // PRO_ROWS, prologue, 0)

    first, later = _band_bias_t(0, QBLK)
    bias_s[0] = first
    bias_s[1] = later
    zero_pad = jnp.zeros((QBLK, LANES), BF16)
    lo = _lo_mask((QBLK, LANES))
    heads = (lo, jnp.logical_not(lo))
    ones_cols = jnp.ones((2 * QBLK, LANES), BF16)
    eye = (lax.broadcasted_iota(jnp.int32, (QBLK, QBLK), 0)
           == lax.broadcasted_iota(jnp.int32, (QBLK, QBLK), 1))
    e_row = lax.broadcasted_iota(jnp.int32, (2 * QBLK, LANES), 0) < QBLK
    half_ones = jnp.where(e_row == _lo_mask((2 * QBLK, LANES)), 1.0, 0.0).astype(BF16)

    outs = (o0_s, o1_s, o2_s)
    lses = (l0_s, l1_s, l2_s)
    n_flat = seq // QBLK
    for (window, r), on_s, ln_s in zip(DIL_PATTERNS, outs, lses):
        assert window // r == QBLK
        n_blk = seq // r // QBLK
        blk_shift = n_blk.bit_length() - 1
        assert n_blk == 1 << blk_shift

        def pad(j, c2, n_blk=n_blk):
            z0 = pl.multiple_of(j * (n_blk + 1) * QBLK, QBLK)
            kc_s[pl.ds(z0, QBLK), :] = zero_pad
            vc_s[pl.ds(z0, QBLK), :] = zero_pad
            return c2

        lax.fori_loop(0, r, pad, 0)

        def split(idx, r=r, n_blk=n_blk, blk_shift=blk_shift):
            j = lax.shift_right_logical(idx, blk_shift)
            i = idx & (n_blk - 1)
            return j, i

        def gather(idx, c2, r=r, split=split):
            j, i = split(idx)
            src = pl.ds(j + r * QBLK * i, QBLK, stride=r)
            q0 = pl.multiple_of(idx * QBLK, QBLK)
            k0 = pl.multiple_of((idx + j + 1) * QBLK, QBLK)
            qc_s[pl.ds(q0, QBLK), :] = qn_s[src, :].astype(BF16)
            kc_s[pl.ds(k0, QBLK), :] = kn_s[src, :].astype(BF16)
            vc_s[pl.ds(k0, QBLK), :] = vn_s[src, :].astype(BF16)
            return c2

        lax.fori_loop(0, n_flat, gather, 0)

        def block(idx, c2, r=r, split=split, on_s=on_s, ln_s=ln_s):
            j, i = split(idx)
            q0 = pl.multiple_of(idx * QBLK, QBLK)
            k0 = pl.multiple_of((idx + j) * QBLK, QBLK)
            qb = qc_s[pl.ds(q0, QBLK), :]
            kw = kc_s[pl.ds(k0, 2 * QBLK), :]
            vw = vc_s[pl.ds(k0, 2 * QBLK), :]
            bias = bias_s[jnp.minimum(i, 1)]
            qm = jnp.concatenate([jnp.where(h, qb, jnp.zeros_like(qb)) for h in heads], axis=0)
            sc = lax.dot_general(kw, qm, (((1,), (1,)), ((), ())), preferred_element_type=F32)
            sc = sc + jnp.concatenate([bias, bias], axis=1)
            m = jnp.max(sc, axis=0, keepdims=True)
            mt = m.astype(BF16).astype(F32)
            p = jnp.exp(sc - mt).astype(BF16)
            x = lax.dot_general(p, jnp.concatenate([vw, ones_cols], axis=1),
                                (((0,), (0,)), ((), ())), preferred_element_type=F32)
            num = jnp.where(lo, x[0:QBLK, 0:LANES], x[QBLK:2 * QBLK, 0:LANES])
            den = jnp.where(lo, x[0:QBLK, LANES:2 * LANES], x[QBLK:2 * QBLK, LANES:2 * LANES])
            m_diag = jnp.concatenate(
                [jnp.where(eye, mt[:, 0:QBLK], 0.0), jnp.where(eye, mt[:, QBLK:2 * QBLK], 0.0)],
                axis=1).astype(BF16)
            m_rows = jnp.dot(m_diag, half_ones, preferred_element_type=F32)
            dst = pl.ds(j + r * QBLK * i, QBLK, stride=r)
            on_s[dst, :] = num / den
            ln_s[dst, :] = m_rows + jnp.log(den)
            return c2

        lax.fori_loop(0, n_flat, block, 0, unroll=BLOCK_UNROLL)

    def combine(n, carry):
        r0 = pl.multiple_of(n * PRO_ROWS, PRO_ROWS)
        rows = pl.ds(r0, PRO_ROWS)
        ls = [l_s[rows, :] for l_s in lses]
        m = jnp.maximum(jnp.maximum(ls[0], ls[1]), ls[2])
        es = [jnp.exp(l - m) for l in ls]
        num = es[0] * o0_s[rows, :] + es[1] * o1_s[rows, :] + es[2] * o2_s[rows, :]
        o_ref[0, rows, :] = (num / (es[0] + es[1] + es[2])).astype(o_ref.dtype)
        return carry

    lax.fori_loop(0, seq // PRO_ROWS, combine, 0)


def _dilated(proj, tables, q_gain, k_gain, col0):
    b, s, _ = proj.shape
    n_pairs = DIL_HEADS * HEAD_DIM // LANES
    t0 = col0 // LANES
    tab = pl.BlockSpec((1, s, LANES), lambda bi, p: (bi, 0, 0), pipeline_mode=pl.Buffered(1))
    two = lambda g: jnp.concatenate([g, g]).reshape(1, LANES)
    nat = pltpu.VMEM((s, LANES), F32)
    max_r = max(r for _, r in DIL_PATTERNS)
    return pl.pallas_call(
        functools.partial(_dilated_kernel, seq=s),
        grid=(b, n_pairs),
        in_specs=[
            pl.BlockSpec((1, s, LANES), lambda bi, p: (bi, 0, t0 + p)),
            pl.BlockSpec((1, s, LANES), lambda bi, p: (bi, 0, t0 + n_pairs + p)),
            pl.BlockSpec((1, s, LANES), lambda bi, p: (bi, 0, t0 + 2 * n_pairs + p)),
            tab, tab, tab,
            _resident((1, LANES)),
            _resident((1, LANES)),
        ],
        out_specs=pl.BlockSpec((1, s, LANES), lambda bi, p: (bi, 0, p)),
        out_shape=jax.ShapeDtypeStruct((b, s, DIL_HEADS * HEAD_DIM), BF16),
        scratch_shapes=[
            nat, nat, nat,
            pltpu.VMEM((s, LANES), BF16),
            pltpu.VMEM((s + max_r * QBLK, LANES), BF16),
            pltpu.VMEM((s + max_r * QBLK, LANES), BF16),
            pltpu.VMEM((2, 2 * QBLK, QBLK), F32),
            nat, nat, nat, nat, nat, nat,
        ],
        compiler_params=_cparams(2),
        name="dilated_attention",
    )(proj, proj, proj, *tables, two(q_gain), two(k_gain))


def _swa_kernel(q_ref, k_ref, v_ref, c_ref, sa_ref, sb_ref, qg_ref, kg_ref, sink_ref, o_ref,
                q_s, kd_s, vd_s, bias_s, *, seq):
    half = ROPE_DIMS // 2
    group = SWA_Q_HEADS // SWA_KV_HEADS
    lo_p = _lo_mask((PRO_ROWS, LANES))
    kv_in_hi = (pl.program_id(1) % 2) == 1

    def both_halves(x):
        x = jnp.where(kv_in_hi, pltpu.roll(x, HEAD_DIM, 1), x)
        return jnp.where(lo_p, x, pltpu.roll(x, HEAD_DIM, 1))

    def prologue(n, carry):
        r0 = pl.multiple_of(n * PRO_ROWS, PRO_ROWS)
        rows = pl.ds(r0, PRO_ROWS)
        c, sa, sb = c_ref[0, rows, :], sa_ref[0, rows, :], sb_ref[0, rows, :]
        for t in range(group // 2):
            cols = slice(t * LANES, (t + 1) * LANES)
            q = q_ref[0, rows, cols].astype(F32)
            q = q * lax.rsqrt(_head_rms(q, lo_p) + EPS) * qg_ref[...]
            q_s[rows, cols] = (_rotate(q, c, sa, sb, half) * (HEAD_DIM ** -0.5)).astype(BF16)
        k = k_ref[0, rows, :].astype(F32)
        k = k * lax.rsqrt(_head_rms(k, lo_p) + EPS) * kg_ref[...]
        k = _rotate(k, c, sa, sb, half)
        kd_s[pl.ds(r0 + QBLK, PRO_ROWS), :] = both_halves(k).astype(BF16)
        vd_s[pl.ds(r0 + QBLK, PRO_ROWS), :] = both_halves(v_ref[0, rows, :].astype(F32)).astype(BF16)
        return carry

    lax.fori_loop(0, seq // PRO_ROWS, prologue, 0)

    first, later = _band_bias_t(1, QBLK)
    bias_s[0] = first
    bias_s[1] = later
    zero_pad = jnp.zeros((QBLK, LANES), BF16)
    kd_s[0:QBLK, :] = zero_pad
    vd_s[0:QBLK, :] = zero_pad
    lo = _lo_mask((QBLK, LANES))
    heads = (lo, jnp.logical_not(lo))

    def block(i, carry):
        d0 = pl.multiple_of(i * QBLK, QBLK)
        rows = pl.ds(d0, QBLK)
        kw = kd_s[pl.ds(d0, 2 * QBLK), :]
        vw = vd_s[pl.ds(d0, 2 * QBLK), :]
        bias = bias_s[jnp.minimum(i, 1)]
        stack = []
        for t in range(group // 2):
            qt = q_s[rows, t * LANES:(t + 1) * LANES]
            for head in heads:
                stack.append(jnp.where(head, qt, jnp.zeros_like(qt)))
        qm = jnp.concatenate(stack, axis=0)
        sc = lax.dot_general(kw, qm, (((1,), (1,)), ((), ())), preferred_element_type=F32)
        ps = []
        for hh in range(group):
            s_h = sc[:, hh * QBLK:(hh + 1) * QBLK] + bias
            sink = sink_ref[0, hh:hh + 1, :]
            m = jnp.maximum(jnp.max(s_h, axis=0, keepdims=True), sink)
            p = jnp.exp(s_h - m)
            den = jnp.sum(p, axis=0, keepdims=True) + jnp.exp(sink - m)
            ps.append((p * (1.0 / den)).astype(BF16))
        pt = jnp.concatenate(ps, axis=1)
        o = lax.dot_general(pt, vw, (((0,), (0,)), ((), ())), preferred_element_type=F32)
        for t in range(group // 2):
            o_ref[0, rows, t * LANES:(t + 1) * LANES] = jnp.where(
                lo, o[2 * t * QBLK:(2 * t + 1) * QBLK, :],
                o[(2 * t + 1) * QBLK:(2 * t + 2) * QBLK, :]).astype(o_ref.dtype)
        return carry

    lax.fori_loop(0, seq // QBLK, block, 0, unroll=BLOCK_UNROLL)


def _swa(proj, tables, q_gain, k_gain, sinks):
    b, s, _ = proj.shape
    group = SWA_Q_HEADS // SWA_KV_HEADS
    q_w = group * HEAD_DIM
    k_t0 = SWA_Q_HEADS * HEAD_DIM // LANES
    v_t0 = k_t0 + SWA_KV_HEADS * HEAD_DIM // LANES
    tab = pl.BlockSpec((1, s, LANES), lambda bi, g: (bi, 0, 0), pipeline_mode=pl.Buffered(1))
    two = lambda g: jnp.concatenate([g, g]).reshape(1, LANES)
    sink_rows = jnp.broadcast_to(sinks.reshape(SWA_KV_HEADS, group, 1), (SWA_KV_HEADS, group, LANES))
    return pl.pallas_call(
        functools.partial(_swa_kernel, seq=s),
        grid=(b, SWA_KV_HEADS),
        in_specs=[
            pl.BlockSpec((1, s, q_w), lambda bi, g: (bi, 0, g)),
            pl.BlockSpec((1, s, LANES), lambda bi, g: (bi, 0, k_t0 + g // 2)),
            pl.BlockSpec((1, s, LANES), lambda bi, g: (bi, 0, v_t0 + g // 2)),
            tab, tab, tab,
            _resident((1, LANES)),
            _resident((1, LANES)),
            pl.BlockSpec((1, group, LANES), lambda bi, g: (g, 0, 0)),
        ],
        out_specs=pl.BlockSpec((1, s, q_w), lambda bi, g: (bi, 0, g)),
        out_shape=jax.ShapeDtypeStruct((b, s, SWA_Q_HEADS * HEAD_DIM), BF16),
        scratch_shapes=[
            pltpu.VMEM((s, q_w), BF16),
            pltpu.VMEM((s + QBLK, LANES), BF16),
            pltpu.VMEM((s + QBLK, LANES), BF16),
            pltpu.VMEM((2, 2 * QBLK, QBLK), F32),
        ],
        compiler_params=_cparams(2),
        name="swa_attention",
    )(proj, proj, proj, *tables, two(q_gain), two(k_gain), sink_rows)


def kernel(x, positions, norm_mix, norm_mlp, mlp_w_up, mlp_w_down, hyb_w_in, hyb_w_out, ret_gn_gain, dil_q_gain, dil_k_gain, swa_w_qkv, swa_b_qkv, swa_w_out, swa_q_gain, swa_k_gain, swa_sinks):
    b, s, d = x.shape
    depth = norm_mix.shape[0]
    ret_tab, rope_tab = _rope_tables(positions)
    log_gamma = jnp.log1p(-jnp.exp2(-5.0 - jnp.arange(RET_HEADS, dtype=F32)))
    ret_w = RET_HEADS * RET_V_DIM
    dil_col0 = 2 * RET_HEADS * RET_QK_DIM + 2 * ret_w

    x2d = x.reshape(b * s, d)
    for layer in range(depth):
        i = layer // 2
        if layer % 2 == 0:
            w_in = hyb_w_in[i].astype(BF16)
            proj = _norm_proj(x2d, norm_mix[layer], w_in, jnp.zeros((w_in.shape[1],), F32))
            proj = proj.reshape(b, s, -1)
            ra = _retention(proj, ret_tab, log_gamma, ret_gn_gain[i])
            da = _dilated(proj, rope_tab, dil_q_gain[i], dil_k_gain[i], dil_col0)
            w_out = hyb_w_out[i].astype(BF16)
            mixed = [ra.reshape(b * s, -1), da.reshape(b * s, -1)]
        else:
            proj = _norm_proj(x2d, norm_mix[layer], swa_w_qkv[i].astype(BF16), swa_b_qkv[i])
            proj = proj.reshape(b, s, -1)
            att = _swa(proj, rope_tab, swa_q_gain[i], swa_k_gain[i], swa_sinks[i])
            mixed = [att.reshape(b * s, -1)]
            w_out = swa_w_out[i].astype(BF16)
        x2d = _out_mlp(mixed, x2d, w_out, norm_mlp[layer],
                       mlp_w_up[layer].astype(BF16), mlp_w_down[layer].astype(BF16))
    return x2d.reshape(b, s, d)
```

```python
import functools

import jax
import jax.numpy as jnp
import numpy as np
from jax import lax
from jax.experimental import pallas as pl
from jax.experimental.pallas import tpu as pltpu

F32 = jnp.float32
BF16 = jnp.bfloat16

D_MODEL = 1024
D_FF = 4 * D_MODEL
HEAD_DIM = 64
EPS = 1e-6
RET_HEADS = 4
RET_QK_DIM = 64
RET_V_DIM = 128
RET_CHUNK = 128
RET_THETA = 10000.0
DIL_HEADS = 8
DIL_PATTERNS = ((128, 1), (512, 4), (2048, 16))
SWA_Q_HEADS = 16
SWA_KV_HEADS = 4
SWA_WINDOW = 128
ROPE_THETA = 500000.0
ROPE_DIMS = HEAD_DIM // 4

LANES = 128
QBLK = 128
VMEM_LIMIT_BYTES = 56 * 1024 * 1024
NEG_BIG = -1e30

ROW_TILE = 512
FF_CHUNK = 1024
PRO_ROWS = 256
BLOCK_UNROLL = 4


def _cparams(n_axes):
    return pltpu.CompilerParams(
        dimension_semantics=("arbitrary",) * n_axes,
        vmem_limit_bytes=VMEM_LIMIT_BYTES,
    )


def _resident(shape):
    nd = len(shape)
    return pl.BlockSpec(shape, lambda *_: (0,) * nd, pipeline_mode=pl.Buffered(1))


def _split3(x):
    hi = x.astype(BF16)
    r1 = x - hi.astype(F32)
    mid = r1.astype(BF16)
    lo = (r1 - mid.astype(F32)).astype(BF16)
    return hi, mid, lo


def _trig_kernel(p_ref, inv_ref, ec_ref, es_ref, base_ref, c_ref, s_ref, *, n_pos):
    ang = p_ref[...] * inv_ref[...]
    cos_parts = _split3(jnp.cos(ang))
    sin_parts = _split3(jnp.sin(ang))
    rows = p_ref.shape[0]
    for c in range(n_pos):
        dst = pl.ds(c, rows, stride=n_pos)
        c_ref[dst, :] = base_ref[...] + sum(
            jnp.dot(part, ec_ref[c], preferred_element_type=F32) for part in cos_parts)
        s_ref[dst, :] = sum(
            jnp.dot(part, es_ref[c], preferred_element_type=F32) for part in sin_parts)


def _selectors(half):
    n_pos = LANES // half
    ec = np.zeros((n_pos, LANES, LANES), np.float32)
    es = np.zeros((n_pos, LANES, LANES), np.float32)
    base = np.zeros((1, LANES), np.float32)
    for j in range(LANES):
        d = j % HEAD_DIM
        if d >= 2 * half:
            base[0, j] = 1.0
            continue
        f = d % half
        for c in range(n_pos):
            ec[c, c * half + f, j] = 1.0
            es[c, c * half + f, j] = -1.0 if d < half else 1.0
    return n_pos, jnp.asarray(ec, BF16), jnp.asarray(es, BF16), jnp.asarray(base)


def _rope_table(positions, half, theta, n_rot):
    b, s = positions.shape
    posf = positions.astype(F32)
    inv = jnp.power(jnp.float32(theta), -jnp.arange(half, dtype=F32) * (2.0 / n_rot))
    n_pos, ec, es, base = _selectors(half)
    p = jnp.broadcast_to(posf[..., None], (b, s, half)).reshape(-1, LANES)
    inv_rows = jnp.broadcast_to(jnp.tile(inv, n_pos)[None, :], p.shape)
    rows = p.shape[0]
    tile = min(rows, 2048 // n_pos)
    assert rows % tile == 0
    dense = pl.BlockSpec((tile, LANES), lambda i: (i, 0))
    wide = pl.BlockSpec((tile * n_pos, LANES), lambda i: (i, 0))
    c_tab, s_tab = pl.pallas_call(
        functools.partial(_trig_kernel, n_pos=n_pos),
        grid=(rows // tile,),
        in_specs=[dense, dense, _resident(ec.shape), _resident(es.shape), _resident(base.shape)],
        out_specs=[wide, wide],
        out_shape=[jax.ShapeDtypeStruct((rows * n_pos, LANES), F32)] * 2,
        compiler_params=_cparams(1),
        name="trig_tables",
    )(p, inv_rows, ec, es, base)
    return c_tab.reshape(b, s, LANES), s_tab.reshape(b, s, LANES)


def _rotate(x, c, s, half):
    d = lax.broadcasted_iota(jnp.int32, x.shape, x.ndim - 1) & (HEAD_DIM - 1)
    swapped = jnp.where(d >= half, pltpu.roll(x, half, 1), pltpu.roll(x, LANES - half, 1))
    return x * c + swapped * s


def _lo_mask(shape):
    return lax.broadcasted_iota(jnp.int32, shape, len(shape) - 1) < HEAD_DIM


def _head_ones():
    r = lax.broadcasted_iota(jnp.int32, (LANES, LANES), 0) < HEAD_DIM
    c = lax.broadcasted_iota(jnp.int32, (LANES, LANES), 1) < HEAD_DIM
    return jnp.where(r == c, 1.0, 0.0).astype(BF16)


def _head_rms(x, head_ones):
    x2 = x * x
    hi = x2.astype(BF16)
    lo = (x2 - hi.astype(F32)).astype(BF16)
    ss = (jnp.dot(hi, head_ones, preferred_element_type=F32)
          + jnp.dot(lo, head_ones, preferred_element_type=F32))
    return ss * (1.0 / HEAD_DIM)


def _norm_proj_kernel(x_ref, g_ref, w_ref, b_ref, o_ref, *, n_chunk):
    x = x_ref[...]
    ms = jnp.mean(x * x, axis=-1, keepdims=True)
    h = (x * lax.rsqrt(ms + EPS) * g_ref[...]).astype(BF16)
    n = o_ref.shape[-1]
    for c in range(0, n, n_chunk):
        acc = jnp.dot(h, w_ref[:, c:c + n_chunk], preferred_element_type=F32)
        o_ref[:, c:c + n_chunk] = (acc + b_ref[:, c:c + n_chunk]).astype(o_ref.dtype)


def _norm_proj(x2d, gain, w_bf16, bias):
    m, d = x2d.shape
    n = w_bf16.shape[1]
    return pl.pallas_call(
        functools.partial(_norm_proj_kernel, n_chunk=512),
        grid=(m // ROW_TILE,),
        in_specs=[
            pl.BlockSpec((ROW_TILE, d), lambda i: (i, 0)),
            _resident((1, d)),
            _resident((d, n)),
            _resident((1, n)),
        ],
        out_specs=pl.BlockSpec((ROW_TILE, n), lambda i: (i, 0)),
        out_shape=jax.ShapeDtypeStruct((m, n), BF16),
        compiler_params=_cparams(1),
        name="norm_proj",
    )(x2d, gain.reshape(1, d), w_bf16, bias.reshape(1, n))


def _out_mlp_kernel(*refs, n_mix):
    a_refs = refs[:n_mix]
    x_ref, wo_ref, g_ref, wup_ref, wdn_ref, o_ref = refs[n_mix:]
    mixed = a_refs[0][...] if n_mix == 1 else jnp.concatenate([a[...] for a in a_refs], axis=-1)
    x1 = x_ref[...] + jnp.dot(mixed, wo_ref[...], preferred_element_type=F32)
    ms = jnp.mean(x1 * x1, axis=-1, keepdims=True)
    h = (x1 * lax.rsqrt(ms + EPS) * g_ref[...]).astype(BF16)
    y = x1
    for c in range(0, D_FF, FF_CHUNK):
        u = jnp.dot(h, wup_ref[:, c:c + FF_CHUNK], preferred_element_type=F32)
        u = jnp.square(jnp.maximum(u, 0.0)).astype(BF16)
        y = y + jnp.dot(u, wdn_ref[c:c + FF_CHUNK, :], preferred_element_type=F32)
    o_ref[...] = y


def _out_mlp(mixed, x2d, w_out, gain, w_up, w_down):
    m, d = x2d.shape
    n_mix = len(mixed)
    in_specs = [pl.BlockSpec((ROW_TILE, a.shape[1]), lambda i: (i, 0)) for a in mixed]
    in_specs.append(pl.BlockSpec((ROW_TILE, d), lambda i: (i, 0)))
    in_specs += [_resident(w_out.shape), _resident((1, d)), _resident(w_up.shape), _resident(w_down.shape)]
    return pl.pallas_call(
        functools.partial(_out_mlp_kernel, n_mix=n_mix),
        grid=(m // ROW_TILE,),
        in_specs=in_specs,
        out_specs=pl.BlockSpec((ROW_TILE, d), lambda i: (i, 0)),
        out_shape=jax.ShapeDtypeStruct((m, d), F32),
        compiler_params=_cparams(1),
        name="out_mlp",
    )(*mixed, x2d, w_out, gain.reshape(1, d), w_up, w_down)


def _retention_kernel(q_ref, k_ref, v_ref, g_ref, c_ref, s_ref, lg_ref, gn_ref,
                      o_ref, q_s, k_s, kt_s, *, seq):
    n_chunks = seq // RET_CHUNK
    half = RET_QK_DIM // 2
    cs = RET_CHUNK

    def prologue(n, carry):
        r0 = pl.multiple_of(n * cs, cs)
        rows = pl.ds(r0, cs)
        c, sn = c_ref[0, rows, :], s_ref[0, rows, :]
        q = _rotate(q_ref[0, rows, :].astype(F32), c, sn, half)
        k = _rotate(k_ref[0, rows, :].astype(F32), c, sn, half) * (RET_QK_DIM ** -0.5)
        q_s[n] = q.astype(BF16)
        k_s[n] = k.astype(BF16)
        kt_s[n] = k.T
        return carry

    lax.fori_loop(0, n_chunks, prologue, 0)

    lo = _lo_mask((cs, LANES))
    row = lax.broadcasted_iota(jnp.int32, (cs, cs), 0).astype(F32)
    col = lax.broadcasted_iota(jnp.int32, (cs, cs), 1).astype(F32)
    diff = row - col
    consts = []
    for hh in range(2):
        lg = lg_ref[0, hh:hh + 1, :]
        decay = jnp.where(diff >= 0, jnp.exp(lg * jnp.maximum(diff, 0.0)), 0.0)
        xi = jnp.exp(lg * (row + 1.0))
        zeta = jnp.exp(lg * (cs - 1.0 - col[0:1, :]))
        cd = jnp.exp(lg * float(cs))
        head = lo if hh == 0 else jnp.logical_not(lo)
        consts.append((decay, xi, zeta, cd, head))

    def body(n, state):
        r0 = pl.multiple_of(n * cs, cs)
        rows = pl.ds(r0, cs)
        qc = q_s[n]
        kc = k_s[n]
        ktc = kt_s[n]
        new_state = []
        for hh in range(2):
            decay, xi, zeta, cd, head = consts[hh]
            cols = slice(hh * RET_V_DIM, (hh + 1) * RET_V_DIM)
            r_prev = state[hh]
            qm = jnp.where(head, qc, jnp.zeros_like(qc))
            vc = v_ref[0, rows, cols]
            sc = lax.dot_general(qm, kc, (((1,), (1,)), ((), ())), preferred_element_type=F32) * decay
            o = jnp.dot(sc.astype(BF16), vc, preferred_element_type=F32)
            o = o + jnp.dot(qm, r_prev.astype(BF16), preferred_element_type=F32) * xi
            kz = (ktc * zeta).astype(BF16)
            new_state.append(r_prev * cd + jnp.dot(kz, vc, preferred_element_type=F32))
            mu = jnp.mean(o, axis=-1, keepdims=True)
            dev = o - mu
            var = jnp.mean(dev * dev, axis=-1, keepdims=True)
            y = dev * lax.rsqrt(var + EPS) * gn_ref[0, hh:hh + 1, :]
            gate = g_ref[0, rows, cols].astype(F32)
            o_ref[0, rows, cols] = (gate * jax.nn.sigmoid(gate) * y).astype(o_ref.dtype)
        return tuple(new_state)

    zero = jnp.zeros((LANES, RET_V_DIM), F32)
    lax.fori_loop(0, n_chunks, body, (zero, zero))


def _retention(proj, tables, log_gamma, gn_gain):
    b, s, _ = proj.shape
    n_pairs = RET_HEADS // 2
    pair_w = 2 * RET_V_DIM
    qk_tiles = RET_HEADS * RET_QK_DIM // LANES
    v_off = 2 * qk_tiles * LANES // pair_w
    g_off = v_off + RET_HEADS * RET_V_DIM // pair_w
    tab = pl.BlockSpec((1, s, LANES), lambda bi, p: (bi, 0, 0), pipeline_mode=pl.Buffered(1))
    lg = jnp.broadcast_to(log_gamma.reshape(n_pairs, 2, 1), (n_pairs, 2, LANES))
    n_chunks = s // RET_CHUNK
    return pl.pallas_call(
        functools.partial(_retention_kernel, seq=s),
        grid=(b, n_pairs),
        in_specs=[
            pl.BlockSpec((1, s, LANES), lambda bi, p: (bi, 0, p)),
            pl.BlockSpec((1, s, LANES), lambda bi, p: (bi, 0, qk_tiles + p)),
            pl.BlockSpec((1, s, pair_w), lambda bi, p: (bi, 0, v_off + p)),
            pl.BlockSpec((1, s, pair_w), lambda bi, p: (bi, 0, g_off + p)),
            tab, tab,
            pl.BlockSpec((1, 2, LANES), lambda bi, p: (p, 0, 0)),
            pl.BlockSpec((1, 2, RET_V_DIM), lambda bi, p: (p, 0, 0)),
        ],
        out_specs=pl.BlockSpec((1, s, pair_w), lambda bi, p: (bi, 0, p)),
        out_shape=jax.ShapeDtypeStruct((b, s, RET_HEADS * RET_V_DIM), BF16),
        scratch_shapes=[
            pltpu.VMEM((n_chunks, RET_CHUNK, LANES), BF16),
            pltpu.VMEM((n_chunks, RET_CHUNK, LANES), BF16),
            pltpu.VMEM((n_chunks, LANES, RET_CHUNK), F32),
        ],
        compiler_params=_cparams(2),
        name="retention",
    )(proj, proj, proj, proj, *tables, lg, gn_gain.reshape(n_pairs, 2, RET_V_DIM))


def _band_bias_t(lo_off, hi_off):
    c = lax.broadcasted_iota(jnp.int32, (2 * QBLK, QBLK), 0)
    a = lax.broadcasted_iota(jnp.int32, (2 * QBLK, QBLK), 1)
    band = (c - a >= lo_off) & (c - a <= hi_off)
    later = jnp.where(band, 0.0, NEG_BIG).astype(F32)
    first = jnp.where(band & (c >= QBLK), 0.0, NEG_BIG).astype(F32)
    return first, later


def _dilated_kernel(q_ref, k_ref, v_ref, c_ref, s_ref, qg_ref, kg_ref, o_ref,
                    qn_s, kn_s, vn_s, qc_s, kc_s, vc_s, bias_s,
                    o0_s, o1_s, o2_s, l0_s, l1_s, l2_s, *, seq):
    half = ROPE_DIMS // 2
    head_ones = _head_ones()

    def prologue(n, carry):
        r0 = pl.multiple_of(n * PRO_ROWS, PRO_ROWS)
        rows = pl.ds(r0, PRO_ROWS)
        c, sn = c_ref[0, rows, :], s_ref[0, rows, :]
        q = q_ref[0, rows, :].astype(F32)
        q = q * lax.rsqrt(_head_rms(q, head_ones) + EPS) * qg_ref[...]
        qn_s[rows, :] = _rotate(q, c, sn, half) * (HEAD_DIM ** -0.5)
        k = k_ref[0, rows, :].astype(F32)
        k = k * lax.rsqrt(_head_rms(k, head_ones) + EPS) * kg_ref[...]
        kn_s[rows, :] = _rotate(k, c, sn, half)
        vn_s[rows, :] = v_ref[0, rows, :].astype(F32)
        return carry

    lax.fori_loop(0, seq // PRO_ROWS, prologue, 0)

    first, later = _band_bias_t(0, QBLK)
    bias_s[0] = first
    bias_s[1] = later
    zero_pad = jnp.zeros((QBLK, LANES), BF16)
    lo = _lo_mask((QBLK, LANES))
    heads = (lo, jnp.logical_not(lo))
    ones_cols = jnp.ones((2 * QBLK, LANES), BF16)
    eye = (lax.broadcasted_iota(jnp.int32, (QBLK, QBLK), 0)
           == lax.broadcasted_iota(jnp.int32, (QBLK, QBLK), 1))
    e_row = lax.broadcasted_iota(jnp.int32, (2 * QBLK, LANES), 0) < QBLK
    half_ones = jnp.where(e_row == _lo_mask((2 * QBLK, LANES)), 1.0, 0.0).astype(BF16)

    outs = (o0_s, o1_s, o2_s)
    lses = (l0_s, l1_s, l2_s)
    n_flat = seq // QBLK
    for (window, r), on_s, ln_s in zip(DIL_PATTERNS, outs, lses):
        assert window // r == QBLK
        n_blk = seq // r // QBLK
        blk_shift = n_blk.bit_length() - 1
        assert n_blk == 1 << blk_shift

        def pad(j, c2, n_blk=n_blk):
            z0 = pl.multiple_of(j * (n_blk + 1) * QBLK, QBLK)
            kc_s[pl.ds(z0, QBLK), :] = zero_pad
            vc_s[pl.ds(z0, QBLK), :] = zero_pad
            return c2

        lax.fori_loop(0, r, pad, 0)

        def split(idx, r=r, n_blk=n_blk, blk_shift=blk_shift):
            j = lax.shift_right_logical(idx, blk_shift)
            i = idx & (n_blk - 1)
            return j, i

        def gather(idx, c2, r=r, split=split):
            j, i = split(idx)
            src = pl.ds(j + r * QBLK * i, QBLK, stride=r)
            q0 = pl.multiple_of(idx * QBLK, QBLK)
            k0 = pl.multiple_of((idx + j + 1) * QBLK, QBLK)
            qc_s[pl.ds(q0, QBLK), :] = qn_s[src, :].astype(BF16)
            kc_s[pl.ds(k0, QBLK), :] = kn_s[src, :].astype(BF16)
            vc_s[pl.ds(k0, QBLK), :] = vn_s[src, :].astype(BF16)
            return c2

        lax.fori_loop(0, n_flat, gather, 0)

        def block(idx, c2, r=r, split=split, on_s=on_s, ln_s=ln_s):
            j, i = split(idx)
            q0 = pl.multiple_of(idx * QBLK, QBLK)
            k0 = pl.multiple_of((idx + j) * QBLK, QBLK)
            qb = qc_s[pl.ds(q0, QBLK), :]
            kw = kc_s[pl.ds(k0, 2 * QBLK), :]
            vw = vc_s[pl.ds(k0, 2 * QBLK), :]
            bias = bias_s[jnp.minimum(i, 1)]
            qm = jnp.concatenate([jnp.where(h, qb, jnp.zeros_like(qb)) for h in heads], axis=0)
            sc = lax.dot_general(kw, qm, (((1,), (1,)), ((), ())), preferred_element_type=F32)
            sc = sc + jnp.concatenate([bias, bias], axis=1)
            m = jnp.max(sc, axis=0, keepdims=True)
            mt = m.astype(BF16).astype(F32)
            p = jnp.exp(sc - mt).astype(BF16)
            x = lax.dot_general(p, jnp.concatenate([vw, ones_cols], axis=1),
                                (((0,), (0,)), ((), ())), preferred_element_type=F32)
            num = jnp.where(lo, x[0:QBLK, 0:LANES], x[QBLK:2 * QBLK, 0:LANES])
            den = jnp.where(lo, x[0:QBLK, LANES:2 * LANES], x[QBLK:2 * QBLK, LANES:2 * LANES])
            m_diag = jnp.concatenate(
                [jnp.where(eye, mt[:, 0:QBLK], 0.0), jnp.where(eye, mt[:, QBLK:2 * QBLK], 0.0)],
                axis=1).astype(BF16)
            m_rows = jnp.dot(m_diag, half_ones, preferred_element_type=F32)
            dst = pl.ds(j + r * QBLK * i, QBLK, stride=r)
            on_s[dst, :] = num / den
            ln_s[dst, :] = m_rows + jnp.log(den)
            return c2

        lax.fori_loop(0, n_flat, block, 0, unroll=BLOCK_UNROLL)

    def combine(n, carry):
        r0 = pl.multiple_of(n * PRO_ROWS, PRO_ROWS)
        rows = pl.ds(r0, PRO_ROWS)
        ls = [l_s[rows, :] for l_s in lses]
        m = jnp.maximum(jnp.maximum(ls[0], ls[1]), ls[2])
        es = [jnp.exp(l - m) for l in ls]
        num = es[0] * o0_s[rows, :] + es[1] * o1_s[rows, :] + es[2] * o2_s[rows, :]
        o_ref[0, rows, :] = (num / (es[0] + es[1] + es[2])).astype(o_ref.dtype)
        return carry

    lax.fori_loop(0, seq // PRO_ROWS, combine, 0)


def _dilated(proj, tables, q_gain, k_gain, col0):
    b, s, _ = proj.shape
    n_pairs = DIL_HEADS * HEAD_DIM // LANES
    t0 = col0 // LANES
    tab = pl.BlockSpec((1, s, LANES), lambda bi, p: (bi, 0, 0), pipeline_mode=pl.Buffered(1))
    two = lambda g: jnp.concatenate([g, g]).reshape(1, LANES)
    nat = pltpu.VMEM((s, LANES), F32)
    max_r = max(r for _, r in DIL_PATTERNS)
    return pl.pallas_call(
        functools.partial(_dilated_kernel, seq=s),
        grid=(b, n_pairs),
        in_specs=[
            pl.BlockSpec((1, s, LANES), lambda bi, p: (bi, 0, t0 + p)),
            pl.BlockSpec((1, s, LANES), lambda bi, p: (bi, 0, t0 + n_pairs + p)),
            pl.BlockSpec((1, s, LANES), lambda bi, p: (bi, 0, t0 + 2 * n_pairs + p)),
            tab, tab,
            _resident((1, LANES)),
            _resident((1, LANES)),
        ],
        out_specs=pl.BlockSpec((1, s, LANES), lambda bi, p: (bi, 0, p)),
        out_shape=jax.ShapeDtypeStruct((b, s, DIL_HEADS * HEAD_DIM), BF16),
        scratch_shapes=[
            nat, nat, nat,
            pltpu.VMEM((s, LANES), BF16),
            pltpu.VMEM((s + max_r * QBLK, LANES), BF16),
            pltpu.VMEM((s + max_r * QBLK, LANES), BF16),
            pltpu.VMEM((2, 2 * QBLK, QBLK), F32),
            nat, nat, nat, nat, nat, nat,
        ],
        compiler_params=_cparams(2),
        name="dilated_attention",
    )(proj, proj, proj, *tables, two(q_gain), two(k_gain))


def _swa_kernel(q_ref, k_ref, v_ref, c_ref, s_ref, qg_ref, kg_ref, sink_ref, o_ref,
                q_s, kd_s, vd_s, bias_s, *, seq):
    half = ROPE_DIMS // 2
    group = SWA_Q_HEADS // SWA_KV_HEADS
    lo_p = _lo_mask((PRO_ROWS, LANES))
    head_ones = _head_ones()
    kv_in_hi = (pl.program_id(1) % 2) == 1

    def both_halves(x):
        x = jnp.where(kv_in_hi, pltpu.roll(x, HEAD_DIM, 1), x)
        return jnp.where(lo_p, x, pltpu.roll(x, HEAD_DIM, 1))

    def prologue(n, carry):
        r0 = pl.multiple_of(n * PRO_ROWS, PRO_ROWS)
        rows = pl.ds(r0, PRO_ROWS)
        c, sn = c_ref[0, rows, :], s_ref[0, rows, :]
        for t in range(group // 2):
            cols = slice(t * LANES, (t + 1) * LANES)
            q = q_ref[0, rows, cols].astype(F32)
            q = q * lax.rsqrt(_head_rms(q, head_ones) + EPS) * qg_ref[...]
            q_s[rows, cols] = (_rotate(q, c, sn, half) * (HEAD_DIM ** -0.5)).astype(BF16)
        k = k_ref[0, rows, :].astype(F32)
        k = k * lax.rsqrt(_head_rms(k, head_ones) + EPS) * kg_ref[...]
        k = _rotate(k, c, sn, half)
        kd_s[pl.ds(r0 + QBLK, PRO_ROWS), :] = both_halves(k).astype(BF16)
        vd_s[pl.ds(r0 + QBLK, PRO_ROWS), :] = both_halves(v_ref[0, rows, :].astype(F32)).astype(BF16)
        return carry

    lax.fori_loop(0, seq // PRO_ROWS, prologue, 0)

    first, later = _band_bias_t(1, QBLK)
    bias_s[0] = first
    bias_s[1] = later
    zero_pad = jnp.zeros((QBLK, LANES), BF16)
    kd_s[0:QBLK, :] = zero_pad
    vd_s[0:QBLK, :] = zero_pad
    lo = _lo_mask((QBLK, LANES))
    heads = (lo, jnp.logical_not(lo))

    def block(i, carry):
        d0 = pl.multiple_of(i * QBLK, QBLK)
        rows = pl.ds(d0, QBLK)
        kw = kd_s[pl.ds(d0, 2 * QBLK), :]
        vw = vd_s[pl.ds(d0, 2 * QBLK), :]
        bias = bias_s[jnp.minimum(i, 1)]
        stack = []
        for t in range(group // 2):
            qt = q_s[rows, t * LANES:(t + 1) * LANES]
            for head in heads:
                stack.append(jnp.where(head, qt, jnp.zeros_like(qt)))
        qm = jnp.concatenate(stack, axis=0)
        sc = lax.dot_general(kw, qm, (((1,), (1,)), ((), ())), preferred_element_type=F32)
        ps = []
        for hh in range(group):
            s_h = sc[:, hh * QBLK:(hh + 1) * QBLK] + bias
            sink = sink_ref[0, hh:hh + 1, :]
            m = jnp.maximum(jnp.max(s_h, axis=0, keepdims=True), sink)
            p = jnp.exp(s_h - m)
            den = jnp.sum(p, axis=0, keepdims=True) + jnp.exp(sink - m)
            ps.append((p * (1.0 / den)).astype(BF16))
        pt = jnp.concatenate(ps, axis=1)
        o = lax.dot_general(pt, vw, (((0,), (0,)), ((), ())), preferred_element_type=F32)
        for t in range(group // 2):
            o_ref[0, rows, t * LANES:(t + 1) * LANES] = jnp.where(
                lo, o[2 * t * QBLK:(2 * t + 1) * QBLK, :],
                o[(2 * t + 1) * QBLK:(2 * t + 2) * QBLK, :]).astype(o_ref.dtype)
        return carry

    lax.fori_loop(0, seq // QBLK, block, 0, unroll=BLOCK_UNROLL)


def _swa(proj, tables, q_gain, k_gain, sinks):
    b, s, _ = proj.shape
    group = SWA_Q_HEADS // SWA_KV_HEADS
    q_w = group * HEAD_DIM
    k_t0 = SWA_Q_HEADS * HEAD_DIM // LANES
    v_t0 = k_t0 + SWA_KV_HEADS * HEAD_DIM // LANES
    tab = pl.BlockSpec((1, s, LANES), lambda bi, g: (bi, 0, 0), pipeline_mode=pl.Buffered(1))
    two = lambda g: jnp.concatenate([g, g]).reshape(1, LANES)
    sink_rows = jnp.broadcast_to(sinks.reshape(SWA_KV_HEADS, group, 1), (SWA_KV_HEADS, group, LANES))
    return pl.pallas_call(
        functools.partial(_swa_kernel, seq=s),
        grid=(b, SWA_KV_HEADS),
        in_specs=[
            pl.BlockSpec((1, s, q_w), lambda bi, g: (bi, 0, g)),
            pl.BlockSpec((1, s, LANES), lambda bi, g: (bi, 0, k_t0 + g // 2)),
            pl.BlockSpec((1, s, LANES), lambda bi, g: (bi, 0, v_t0 + g // 2)),
            tab, tab,
            _resident((1, LANES)),
            _resident((1, LANES)),
            pl.BlockSpec((1, group, LANES), lambda bi, g: (g, 0, 0)),
        ],
        out_specs=pl.BlockSpec((1, s, q_w), lambda bi, g: (bi, 0, g)),
        out_shape=jax.ShapeDtypeStruct((b, s, SWA_Q_HEADS * HEAD_DIM), BF16),
        scratch_shapes=[
            pltpu.VMEM((s, q_w), BF16),
            pltpu.VMEM((s + QBLK, LANES), BF16),
            pltpu.VMEM((s + QBLK, LANES), BF16),
            pltpu.VMEM((2, 2 * QBLK, QBLK), F32),
        ],
        compiler_params=_cparams(2),
        name="swa_attention",
    )(proj, proj, proj, *tables, two(q_gain), two(k_gain), sink_rows)


def kernel(x, positions, norm_mix, norm_mlp, mlp_w_up, mlp_w_down, hyb_w_in, hyb_w_out, ret_gn_gain, dil_q_gain, dil_k_gain, swa_w_qkv, swa_b_qkv, swa_w_out, swa_q_gain, swa_k_gain, swa_sinks):
    b, s, d = x.shape
    depth = norm_mix.shape[0]
    ret_tab = _rope_table(positions, RET_QK_DIM // 2, RET_THETA, RET_QK_DIM)
    rope_tab = _rope_table(positions, ROPE_DIMS // 2, ROPE_THETA, ROPE_DIMS)
    log_gamma = jnp.log1p(-jnp.exp2(-5.0 - jnp.arange(RET_HEADS, dtype=F32)))
    ret_w = RET_HEADS * RET_V_DIM
    dil_col0 = 2 * RET_HEADS * RET_QK_DIM + 2 * ret_w

    x2d = x.reshape(b * s, d)
    for layer in range(depth):
        i = layer // 2
        if layer % 2 == 0:
            w_in = hyb_w_in[i].astype(BF16)
            proj = _norm_proj(x2d, norm_mix[layer], w_in, jnp.zeros((w_in.shape[1],), F32))
            proj = proj.reshape(b, s, -1)
            ra = _retention(proj, ret_tab, log_gamma, ret_gn_gain[i])
            da = _dilated(proj, rope_tab, dil_q_gain[i], dil_k_gain[i], dil_col0)
            w_out = hyb_w_out[i].astype(BF16)
            mixed = [ra.reshape(b * s, -1), da.reshape(b * s, -1)]
        else:
            proj = _norm_proj(x2d, norm_mix[layer], swa_w_qkv[i].astype(BF16), swa_b_qkv[i])
            proj = proj.reshape(b, s, -1)
            att = _swa(proj, rope_tab, swa_q_gain[i], swa_k_gain[i], swa_sinks[i])
            mixed = [att.reshape(b * s, -1)]
            w_out = swa_w_out[i].astype(BF16)
        x2d = _out_mlp(mixed, x2d, w_out, norm_mlp[layer],
                       mlp_w_up[layer].astype(BF16), mlp_w_down[layer].astype(BF16))
    return x2d.reshape(b, s, d)
```

```python
import functools

import jax
import jax.numpy as jnp
import numpy as np
from jax import lax
from jax.experimental import pallas as pl
from jax.experimental.pallas import tpu as pltpu

F32 = jnp.float32
BF16 = jnp.bfloat16

D_MODEL = 1024
D_FF = 4 * D_MODEL
HEAD_DIM = 64
EPS = 1e-6
RET_HEADS = 4
RET_QK_DIM = 64
RET_V_DIM = 128
RET_CHUNK = 128
RET_THETA = 10000.0
DIL_HEADS = 8
DIL_PATTERNS = ((128, 1), (512, 4), (2048, 16))
SWA_Q_HEADS = 16
SWA_KV_HEADS = 4
SWA_WINDOW = 128
ROPE_THETA = 500000.0
ROPE_DIMS = HEAD_DIM // 4

LANES = 128
SUBLANES = 8
QBLK = 128
VMEM_LIMIT_BYTES = 56 * 1024 * 1024
NEG_BIG = -1e30

ROW_TILE = 512
FF_CHUNK = 1024
PRO_ROWS = 256
BLOCK_UNROLL = 4
DEN_ROWS = 16


def _cparams(n_axes):
    return pltpu.CompilerParams(
        dimension_semantics=("arbitrary",) * n_axes,
        vmem_limit_bytes=VMEM_LIMIT_BYTES,
    )


def _resident(shape):
    nd = len(shape)
    return pl.BlockSpec(shape, lambda *_: (0,) * nd, pipeline_mode=pl.Buffered(1))


def _split3(x):
    hi = x.astype(BF16)
    r1 = x - hi.astype(F32)
    mid = r1.astype(BF16)
    lo = (r1 - mid.astype(F32)).astype(BF16)
    return hi, mid, lo


def _trig_kernel(p_ref, inv_ref, ec_ref, es_ref, base_ref, c_ref, s_ref, *, n_pos):
    ang = p_ref[...] * inv_ref[...]
    cos_parts = _split3(jnp.cos(ang))
    sin_parts = _split3(jnp.sin(ang))
    rows = p_ref.shape[0]
    for c in range(n_pos):
        dst = pl.ds(c, rows, stride=n_pos)
        c_ref[dst, :] = base_ref[...] + sum(
            jnp.dot(part, ec_ref[c], preferred_element_type=F32) for part in cos_parts)
        s_ref[dst, :] = sum(
            jnp.dot(part, es_ref[c], preferred_element_type=F32) for part in sin_parts)


def _selectors(half):
    n_pos = LANES // half
    ec = np.zeros((n_pos, LANES, LANES), np.float32)
    es = np.zeros((n_pos, LANES, LANES), np.float32)
    base = np.zeros((1, LANES), np.float32)
    for j in range(LANES):
        d = j % HEAD_DIM
        if d >= 2 * half:
            base[0, j] = 1.0
            continue
        f = d % half
        for c in range(n_pos):
            ec[c, c * half + f, j] = 1.0
            es[c, c * half + f, j] = -1.0 if d < half else 1.0
    return n_pos, jnp.asarray(ec, BF16), jnp.asarray(es, BF16), jnp.asarray(base)


def _rope_table(positions, half, theta, n_rot):
    b, s = positions.shape
    posf = positions.astype(F32)
    inv = jnp.power(jnp.float32(theta), -jnp.arange(half, dtype=F32) * (2.0 / n_rot))
    n_pos, ec, es, base = _selectors(half)
    p = jnp.broadcast_to(posf[..., None], (b, s, half)).reshape(-1, LANES)
    inv_rows = jnp.broadcast_to(jnp.tile(inv, n_pos)[None, :], p.shape)
    rows = p.shape[0]
    tile = min(rows, 2048 // n_pos)
    assert rows % tile == 0
    dense = pl.BlockSpec((tile, LANES), lambda i: (i, 0))
    wide = pl.BlockSpec((tile * n_pos, LANES), lambda i: (i, 0))
    c_tab, s_tab = pl.pallas_call(
        functools.partial(_trig_kernel, n_pos=n_pos),
        grid=(rows // tile,),
        in_specs=[dense, dense, _resident(ec.shape), _resident(es.shape), _resident(base.shape)],
        out_specs=[wide, wide],
        out_shape=[jax.ShapeDtypeStruct((rows * n_pos, LANES), F32)] * 2,
        compiler_params=_cparams(1),
        name="trig_tables",
    )(p, inv_rows, ec, es, base)
    return c_tab.reshape(b, s, LANES), s_tab.reshape(b, s, LANES)


def _rotate(x, c, s, half):
    d = lax.broadcasted_iota(jnp.int32, x.shape, x.ndim - 1) & (HEAD_DIM - 1)
    swapped = jnp.where(d >= half, pltpu.roll(x, half, 1), pltpu.roll(x, LANES - half, 1))
    return x * c + swapped * s


def _lo_mask(shape):
    return lax.broadcasted_iota(jnp.int32, shape, len(shape) - 1) < HEAD_DIM


def _head_ones():
    r = lax.broadcasted_iota(jnp.int32, (LANES, LANES), 0) < HEAD_DIM
    c = lax.broadcasted_iota(jnp.int32, (LANES, LANES), 1) < HEAD_DIM
    return jnp.where(r == c, 1.0, 0.0).astype(BF16)


def _head_rms(x, head_ones):
    x2 = x * x
    hi = x2.astype(BF16)
    lo = (x2 - hi.astype(F32)).astype(BF16)
    ss = (jnp.dot(hi, head_ones, preferred_element_type=F32)
          + jnp.dot(lo, head_ones, preferred_element_type=F32))
    return ss * (1.0 / HEAD_DIM)


def _norm_proj_kernel(x_ref, g_ref, w_ref, b_ref, o_ref, *, n_chunk):
    x = x_ref[...]
    ms = jnp.mean(x * x, axis=-1, keepdims=True)
    h = (x * lax.rsqrt(ms + EPS) * g_ref[...]).astype(BF16)
    n = o_ref.shape[-1]
    for c in range(0, n, n_chunk):
        acc = jnp.dot(h, w_ref[:, c:c + n_chunk], preferred_element_type=F32)
        o_ref[:, c:c + n_chunk] = (acc + b_ref[:, c:c + n_chunk]).astype(o_ref.dtype)


def _norm_proj(x2d, gain, w_bf16, bias):
    m, d = x2d.shape
    n = w_bf16.shape[1]
    return pl.pallas_call(
        functools.partial(_norm_proj_kernel, n_chunk=512),
        grid=(m // ROW_TILE,),
        in_specs=[
            pl.BlockSpec((ROW_TILE, d), lambda i: (i, 0)),
            _resident((1, d)),
            _resident((d, n)),
            _resident((1, n)),
        ],
        out_specs=pl.BlockSpec((ROW_TILE, n), lambda i: (i, 0)),
        out_shape=jax.ShapeDtypeStruct((m, n), BF16),
        compiler_params=_cparams(1),
        name="norm_proj",
    )(x2d, gain.reshape(1, d), w_bf16, bias.reshape(1, n))


def _out_mlp_kernel(*refs, n_mix):
    a_refs = refs[:n_mix]
    x_ref, wo_ref, g_ref, wup_ref, wdn_ref, o_ref = refs[n_mix:]
    mixed = a_refs[0][...] if n_mix == 1 else jnp.concatenate([a[...] for a in a_refs], axis=-1)
    x1 = x_ref[...] + jnp.dot(mixed, wo_ref[...], preferred_element_type=F32)
    ms = jnp.mean(x1 * x1, axis=-1, keepdims=True)
    h = (x1 * lax.rsqrt(ms + EPS) * g_ref[...]).astype(BF16)
    y = x1
    for c in range(0, D_FF, FF_CHUNK):
        u = jnp.dot(h, wup_ref[:, c:c + FF_CHUNK], preferred_element_type=F32)
        u = jnp.square(jnp.maximum(u, 0.0)).astype(BF16)
        y = y + jnp.dot(u, wdn_ref[c:c + FF_CHUNK, :], preferred_element_type=F32)
    o_ref[...] = y


def _out_mlp(mixed, x2d, w_out, gain, w_up, w_down):
    m, d = x2d.shape
    n_mix = len(mixed)
    in_specs = [pl.BlockSpec((ROW_TILE, a.shape[1]), lambda i: (i, 0)) for a in mixed]
    in_specs.append(pl.BlockSpec((ROW_TILE, d), lambda i: (i, 0)))
    in_specs += [_resident(w_out.shape), _resident((1, d)), _resident(w_up.shape), _resident(w_down.shape)]
    return pl.pallas_call(
        functools.partial(_out_mlp_kernel, n_mix=n_mix),
        grid=(m // ROW_TILE,),
        in_specs=in_specs,
        out_specs=pl.BlockSpec((ROW_TILE, d), lambda i: (i, 0)),
        out_shape=jax.ShapeDtypeStruct((m, d), F32),
        compiler_params=_cparams(1),
        name="out_mlp",
    )(*mixed, x2d, w_out, gain.reshape(1, d), w_up, w_down)


def _retention_kernel(q_ref, k_ref, v_ref, g_ref, c_ref, s_ref, lg_ref, gn_ref,
                      o_ref, q_s, k_s, kt_s, *, seq):
    n_chunks = seq // RET_CHUNK
    half = RET_QK_DIM // 2
    cs = RET_CHUNK

    def prologue(n, carry):
        r0 = pl.multiple_of(n * cs, cs)
        rows = pl.ds(r0, cs)
        c, sn = c_ref[0, rows, :], s_ref[0, rows, :]
        q = _rotate(q_ref[0, rows, :].astype(F32), c, sn, half)
        k = _rotate(k_ref[0, rows, :].astype(F32), c, sn, half) * (RET_QK_DIM ** -0.5)
        q_s[n] = q.astype(BF16)
        k_s[n] = k.astype(BF16)
        kt_s[n] = k.T
        return carry

    lax.fori_loop(0, n_chunks, prologue, 0, unroll=2)

    lo = _lo_mask((cs, LANES))
    row = lax.broadcasted_iota(jnp.int32, (cs, cs), 0).astype(F32)
    col = lax.broadcasted_iota(jnp.int32, (cs, cs), 1).astype(F32)
    diff = row - col
    consts = []
    for hh in range(2):
        lg = lg_ref[0, hh:hh + 1, :]
        decay = jnp.where(diff >= 0, jnp.exp(lg * jnp.maximum(diff, 0.0)), 0.0)
        xi = jnp.exp(lg * (row + 1.0))
        zeta = jnp.exp(lg * (cs - 1.0 - col[0:1, :]))
        cd = jnp.exp(lg * float(cs))
        head = lo if hh == 0 else jnp.logical_not(lo)
        consts.append((decay, xi, zeta, cd, head))

    def body(n, state):
        r0 = pl.multiple_of(n * cs, cs)
        rows = pl.ds(r0, cs)
        qc = q_s[n]
        kc = k_s[n]
        ktc = kt_s[n]
        new_state = []
        for hh in range(2):
            decay, xi, zeta, cd, head = consts[hh]
            cols = slice(hh * RET_V_DIM, (hh + 1) * RET_V_DIM)
            r_prev = state[hh]
            qm = jnp.where(head, qc, jnp.zeros_like(qc))
            vc = v_ref[0, rows, cols]
            sc = lax.dot_general(qm, kc, (((1,), (1,)), ((), ())), preferred_element_type=F32) * decay
            o = jnp.dot(sc.astype(BF16), vc, preferred_element_type=F32)
            o = o + jnp.dot(qm, r_prev.astype(BF16), preferred_element_type=F32) * xi
            kz = (ktc * zeta).astype(BF16)
            new_state.append(r_prev * cd + jnp.dot(kz, vc, preferred_element_type=F32))
            mu = jnp.mean(o, axis=-1, keepdims=True)
            dev = o - mu
            var = jnp.mean(dev * dev, axis=-1, keepdims=True)
            y = dev * lax.rsqrt(var + EPS) * gn_ref[0, hh:hh + 1, :]
            gate = g_ref[0, rows, cols].astype(F32)
            o_ref[0, rows, cols] = (gate * jax.nn.sigmoid(gate) * y).astype(o_ref.dtype)
        return tuple(new_state)

    zero = jnp.zeros((LANES, RET_V_DIM), F32)
    lax.fori_loop(0, n_chunks, body, (zero, zero), unroll=4)


def _retention(proj, tables, log_gamma, gn_gain):
    b, s, _ = proj.shape
    n_pairs = RET_HEADS // 2
    pair_w = 2 * RET_V_DIM
    qk_tiles = RET_HEADS * RET_QK_DIM // LANES
    v_off = 2 * qk_tiles * LANES // pair_w
    g_off = v_off + RET_HEADS * RET_V_DIM // pair_w
    tab = pl.BlockSpec((1, s, LANES), lambda bi, p: (bi, 0, 0), pipeline_mode=pl.Buffered(1))
    lg = jnp.broadcast_to(log_gamma.reshape(n_pairs, 2, 1), (n_pairs, 2, LANES))
    n_chunks = s // RET_CHUNK
    return pl.pallas_call(
        functools.partial(_retention_kernel, seq=s),
        grid=(b, n_pairs),
        in_specs=[
            pl.BlockSpec((1, s, LANES), lambda bi, p: (bi, 0, p)),
            pl.BlockSpec((1, s, LANES), lambda bi, p: (bi, 0, qk_tiles + p)),
            pl.BlockSpec((1, s, pair_w), lambda bi, p: (bi, 0, v_off + p)),
            pl.BlockSpec((1, s, pair_w), lambda bi, p: (bi, 0, g_off + p)),
            tab, tab,
            pl.BlockSpec((1, 2, LANES), lambda bi, p: (p, 0, 0)),
            pl.BlockSpec((1, 2, RET_V_DIM), lambda bi, p: (p, 0, 0)),
        ],
        out_specs=pl.BlockSpec((1, s, pair_w), lambda bi, p: (bi, 0, p)),
        out_shape=jax.ShapeDtypeStruct((b, s, RET_HEADS * RET_V_DIM), BF16),
        scratch_shapes=[
            pltpu.VMEM((n_chunks, RET_CHUNK, LANES), BF16),
            pltpu.VMEM((n_chunks, RET_CHUNK, LANES), BF16),
            pltpu.VMEM((n_chunks, LANES, RET_CHUNK), F32),
        ],
        compiler_params=_cparams(2),
        name="retention",
    )(proj, proj, proj, proj, *tables, lg, gn_gain.reshape(n_pairs, 2, RET_V_DIM))


def _band_bias_t(lo_off, hi_off):
    c = lax.broadcasted_iota(jnp.int32, (2 * QBLK, QBLK), 0)
    a = lax.broadcasted_iota(jnp.int32, (2 * QBLK, QBLK), 1)
    band = (c - a >= lo_off) & (c - a <= hi_off)
    later = jnp.where(band, 0.0, NEG_BIG).astype(F32)
    first = jnp.where(band & (c >= QBLK), 0.0, NEG_BIG).astype(F32)
    return first, later


def _dilated_kernel(q_ref, k_ref, v_ref, c_ref, s_ref, qg_ref, kg_ref, o_ref,
                    qn_s, kn_s, vn_s, qt_s, kc_s, vt_s, bias_s, p_s, m_s,
                    o0_s, o1_s, o2_s, l0_s, l1_s, l2_s, *, seq):
    half = ROPE_DIMS // 2
    head_ones = _head_ones()

    def prologue(n, carry):
        r0 = pl.multiple_of(n * PRO_ROWS, PRO_ROWS)
        rows = pl.ds(r0, PRO_ROWS)
        c, sn = c_ref[0, rows, :], s_ref[0, rows, :]
        q = q_ref[0, rows, :].astype(F32)
        q = q * lax.rsqrt(_head_rms(q, head_ones) + EPS) * qg_ref[...]
        qn_s[rows, :] = _rotate(q, c, sn, half) * (HEAD_DIM ** -0.5)
        k = k_ref[0, rows, :].astype(F32)
        k = k * lax.rsqrt(_head_rms(k, head_ones) + EPS) * kg_ref[...]
        kn_s[rows, :] = _rotate(k, c, sn, half)
        vn_s[rows, :] = v_ref[0, rows, :].astype(F32)
        return carry

    lax.fori_loop(0, seq // PRO_ROWS, prologue, 0)

    first, later = _band_bias_t(0, QBLK)
    bias_s[0] = jnp.concatenate([first, first], axis=1)
    bias_s[1] = jnp.concatenate([later, later], axis=1)
    zero_pad = jnp.zeros((QBLK, LANES), BF16)
    top = lax.broadcasted_iota(jnp.int32, (LANES, QBLK), 0) < HEAD_DIM
    ones_rows = jnp.ones((DEN_ROWS, 2 * QBLK), BF16)

    outs = (o0_s, o1_s, o2_s)
    lses = (l0_s, l1_s, l2_s)
    n_flat = seq // QBLK
    for (window, r), on_s, ln_s in zip(DIL_PATTERNS, outs, lses):
        assert window // r == QBLK
        n_blk = seq // r // QBLK
        blk_shift = n_blk.bit_length() - 1
        assert n_blk == 1 << blk_shift

        def pad(j, c2, n_blk=n_blk):
            z = j * (n_blk + 1)
            kc_s[pl.ds(pl.multiple_of(z * QBLK, QBLK), QBLK), :] = zero_pad
            vt_s[z] = zero_pad
            return c2

        lax.fori_loop(0, r, pad, 0)

        def split(idx, r=r, n_blk=n_blk, blk_shift=blk_shift):
            j = lax.shift_right_logical(idx, blk_shift)
            i = idx & (n_blk - 1)
            return j, i

        def gather(idx, c2, r=r, split=split):
            j, i = split(idx)
            src = pl.ds(j + r * QBLK * i, QBLK, stride=r)
            k0 = pl.multiple_of((idx + j + 1) * QBLK, QBLK)
            qt_s[idx] = qn_s[src, :].T.astype(BF16)
            kc_s[pl.ds(k0, QBLK), :] = kn_s[src, :].astype(BF16)
            vt_s[idx + j + 1] = vn_s[src, :].T.astype(BF16)
            return c2

        lax.fori_loop(0, n_flat, gather, 0, unroll=2)

        def scores(idx, c2, split=split):
            j, i = split(idx)
            k0 = pl.multiple_of((idx + j) * QBLK, QBLK)
            qt = qt_s[idx]
            kw = kc_s[pl.ds(k0, 2 * QBLK), :]
            zero = jnp.zeros_like(qt)
            rhs = jnp.concatenate([jnp.where(top, qt, zero), jnp.where(top, zero, qt)], axis=1)
            sc = jnp.dot(kw, rhs, preferred_element_type=F32) + bias_s[jnp.minimum(i, 1)]
            m = jnp.max(sc, axis=0, keepdims=True)
            p_s[idx] = jnp.exp(sc - m).astype(BF16)
            m_s[idx] = jnp.broadcast_to(m, (SUBLANES, 2 * QBLK))
            return c2

        lax.fori_loop(0, n_flat, scores, 0, unroll=8)

        def block(idx, c2, r=r, split=split, on_s=on_s, ln_s=ln_s):
            j, i = split(idx)
            p = p_s[idx]
            m = m_s[idx][0:1, :]
            lhs = jnp.concatenate(
                [jnp.concatenate([vt_s[idx + j], vt_s[idx + j + 1]], axis=1), ones_rows], axis=0)
            ot = jnp.dot(lhs, p, preferred_element_type=F32)
            den = ot[LANES:LANES + 1, :]
            inv = 1.0 / den
            lse = m + jnp.log(den)
            o_t = jnp.concatenate(
                [ot[0:HEAD_DIM, 0:QBLK] * inv[:, 0:QBLK],
                 ot[HEAD_DIM:LANES, QBLK:2 * QBLK] * inv[:, QBLK:2 * QBLK]], axis=0)
            l_t = jnp.concatenate(
                [jnp.broadcast_to(lse[:, 0:QBLK], (HEAD_DIM, QBLK)),
                 jnp.broadcast_to(lse[:, QBLK:2 * QBLK], (HEAD_DIM, QBLK))], axis=0)
            dst = pl.ds(j + r * QBLK * i, QBLK, stride=r)
            on_s[dst, :] = o_t.T
            ln_s[dst, :] = l_t.T
            return c2

        lax.fori_loop(0, n_flat, block, 0, unroll=8)

    def combine(n, carry):
        r0 = pl.multiple_of(n * PRO_ROWS, PRO_ROWS)
        rows = pl.ds(r0, PRO_ROWS)
        ls = [l_s[rows, :] for l_s in lses]
        m = jnp.maximum(jnp.maximum(ls[0], ls[1]), ls[2])
        es = [jnp.exp(l - m) for l in ls]
        num = es[0] * o0_s[rows, :] + es[1] * o1_s[rows, :] + es[2] * o2_s[rows, :]
        o_ref[0, rows, :] = (num / (es[0] + es[1] + es[2])).astype(o_ref.dtype)
        return carry

    lax.fori_loop(0, seq // PRO_ROWS, combine, 0)


def _dilated(proj, tables, q_gain, k_gain, col0):
    b, s, _ = proj.shape
    n_pairs = DIL_HEADS * HEAD_DIM // LANES
    t0 = col0 // LANES
    tab = pl.BlockSpec((1, s, LANES), lambda bi, p: (bi, 0, 0), pipeline_mode=pl.Buffered(1))
    two = lambda g: jnp.concatenate([g, g]).reshape(1, LANES)
    nat = pltpu.VMEM((s, LANES), F32)
    max_r = max(r for _, r in DIL_PATTERNS)
    return pl.pallas_call(
        functools.partial(_dilated_kernel, seq=s),
        grid=(b, n_pairs),
        in_specs=[
            pl.BlockSpec((1, s, LANES), lambda bi, p: (bi, 0, t0 + p)),
            pl.BlockSpec((1, s, LANES), lambda bi, p: (bi, 0, t0 + n_pairs + p)),
            pl.BlockSpec((1, s, LANES), lambda bi, p: (bi, 0, t0 + 2 * n_pairs + p)),
            tab, tab,
            _resident((1, LANES)),
            _resident((1, LANES)),
        ],
        out_specs=pl.BlockSpec((1, s, LANES), lambda bi, p: (bi, 0, p)),
        out_shape=jax.ShapeDtypeStruct((b, s, DIL_HEADS * HEAD_DIM), BF16),
        scratch_shapes=[
            nat, nat, nat,
            pltpu.VMEM((s // QBLK, LANES, QBLK), BF16),
            pltpu.VMEM((s + max_r * QBLK, LANES), BF16),
            pltpu.VMEM((s // QBLK + max_r, LANES, QBLK), BF16),
            pltpu.VMEM((2, 2 * QBLK, 2 * QBLK), F32),
            pltpu.VMEM((s // QBLK, 2 * QBLK, 2 * QBLK), BF16),
            pltpu.VMEM((s // QBLK, SUBLANES, 2 * QBLK), F32),
            nat, nat, nat, nat, nat, nat,
        ],
        compiler_params=_cparams(2),
        name="dilated_attention",
    )(proj, proj, proj, *tables, two(q_gain), two(k_gain))


def _swa_kernel(q_ref, k_ref, v_ref, c_ref, s_ref, qg_ref, kg_ref, sink_ref, o_ref,
                q_s, kd_s, vt_s, bias_s, p_s, m_s, *, seq):
    half = ROPE_DIMS // 2
    group = SWA_Q_HEADS // SWA_KV_HEADS
    lo_p = _lo_mask((PRO_ROWS, LANES))
    head_ones = _head_ones()
    kv_in_hi = (pl.program_id(1) % 2) == 1
    blk_per_step = PRO_ROWS // QBLK

    def both_halves(x):
        x = jnp.where(kv_in_hi, pltpu.roll(x, HEAD_DIM, 1), x)
        return jnp.where(lo_p, x, pltpu.roll(x, HEAD_DIM, 1))

    def prologue(n, carry):
        r0 = pl.multiple_of(n * PRO_ROWS, PRO_ROWS)
        rows = pl.ds(r0, PRO_ROWS)
        c, sn = c_ref[0, rows, :], s_ref[0, rows, :]
        for t in range(group // 2):
            cols = slice(t * LANES, (t + 1) * LANES)
            q = q_ref[0, rows, cols].astype(F32)
            q = q * lax.rsqrt(_head_rms(q, head_ones) + EPS) * qg_ref[...]
            q_s[rows, cols] = (_rotate(q, c, sn, half) * (HEAD_DIM ** -0.5)).astype(BF16)
        k = k_ref[0, rows, :].astype(F32)
        k = k * lax.rsqrt(_head_rms(k, head_ones) + EPS) * kg_ref[...]
        k = _rotate(k, c, sn, half)
        kd_s[pl.ds(r0 + QBLK, PRO_ROWS), :] = both_halves(k).astype(BF16)
        v_t = v_ref[0, rows, :].astype(F32).T
        v_t = jnp.where(kv_in_hi, v_t[HEAD_DIM:LANES, :], v_t[0:HEAD_DIM, :]).astype(BF16)
        for u in range(blk_per_step):
            vt_s[n * blk_per_step + u + 1] = v_t[:, u * QBLK:(u + 1) * QBLK]
        return carry

    lax.fori_loop(0, seq // PRO_ROWS, prologue, 0)

    first, later = _band_bias_t(1, QBLK)
    bias_s[0] = jnp.concatenate([first] * group, axis=1)
    bias_s[1] = jnp.concatenate([later] * group, axis=1)
    kd_s[0:QBLK, :] = jnp.zeros((QBLK, LANES), BF16)
    vt_s[0] = jnp.zeros((HEAD_DIM, QBLK), BF16)
    lo = _lo_mask((QBLK, LANES))
    heads = (lo, jnp.logical_not(lo))
    sink = jnp.concatenate([sink_ref[0, hh:hh + 1, :] for hh in range(group)], axis=1)
    ones_rows = jnp.ones((DEN_ROWS, 2 * QBLK), BF16)

    def scores(i, carry):
        d0 = pl.multiple_of(i * QBLK, QBLK)
        rows = pl.ds(d0, QBLK)
        kw = kd_s[pl.ds(d0, 2 * QBLK), :]
        stack = []
        for t in range(group // 2):
            qt = q_s[rows, t * LANES:(t + 1) * LANES]
            for head in heads:
                stack.append(jnp.where(head, qt, jnp.zeros_like(qt)))
        qm = jnp.concatenate(stack, axis=0)
        sc = lax.dot_general(kw, qm, (((1,), (1,)), ((), ())), preferred_element_type=F32)
        sc = sc + bias_s[jnp.minimum(i, 1)]
        m = jnp.maximum(jnp.max(sc, axis=0, keepdims=True), sink)
        p_s[i] = jnp.exp(sc - m).astype(BF16)
        m_s[i] = jnp.broadcast_to(m, (SUBLANES, group * QBLK))
        return carry

    lax.fori_loop(0, seq // QBLK, scores, 0, unroll=BLOCK_UNROLL)

    def values(i, carry):
        rows = pl.ds(pl.multiple_of(i * QBLK, QBLK), QBLK)
        m = m_s[i][0:1, :]
        lhs = jnp.concatenate(
            [jnp.concatenate([vt_s[i], vt_s[i + 1]], axis=1), ones_rows], axis=0)
        ot = jnp.dot(lhs, p_s[i], preferred_element_type=F32)
        inv = 1.0 / (ot[HEAD_DIM:HEAD_DIM + 1, :] + jnp.exp(sink - m))
        on = ot[0:HEAD_DIM, :] * inv
        for t in range(group // 2):
            pair_t = jnp.concatenate(
                [on[:, 2 * t * QBLK:(2 * t + 1) * QBLK], on[:, (2 * t + 1) * QBLK:(2 * t + 2) * QBLK]],
                axis=0)
            o_ref[0, rows, t * LANES:(t + 1) * LANES] = pair_t.T.astype(o_ref.dtype)
        return carry

    lax.fori_loop(0, seq // QBLK, values, 0, unroll=BLOCK_UNROLL)


def _swa(proj, tables, q_gain, k_gain, sinks):
    b, s, _ = proj.shape
    group = SWA_Q_HEADS // SWA_KV_HEADS
    q_w = group * HEAD_DIM
    k_t0 = SWA_Q_HEADS * HEAD_DIM // LANES
    v_t0 = k_t0 + SWA_KV_HEADS * HEAD_DIM // LANES
    tab = pl.BlockSpec((1, s, LANES), lambda bi, g: (bi, 0, 0), pipeline_mode=pl.Buffered(1))
    two = lambda g: jnp.concatenate([g, g]).reshape(1, LANES)
    sink_rows = jnp.broadcast_to(sinks.reshape(SWA_KV_HEADS, group, 1), (SWA_KV_HEADS, group, LANES))
    return pl.pallas_call(
        functools.partial(_swa_kernel, seq=s),
        grid=(b, SWA_KV_HEADS),
        in_specs=[
            pl.BlockSpec((1, s, q_w), lambda bi, g: (bi, 0, g)),
            pl.BlockSpec((1, s, LANES), lambda bi, g: (bi, 0, k_t0 + g // 2)),
            pl.BlockSpec((1, s, LANES), lambda bi, g: (bi, 0, v_t0 + g // 2)),
            tab, tab,
            _resident((1, LANES)),
            _resident((1, LANES)),
            pl.BlockSpec((1, group, LANES), lambda bi, g: (g, 0, 0)),
        ],
        out_specs=pl.BlockSpec((1, s, q_w), lambda bi, g: (bi, 0, g)),
        out_shape=jax.ShapeDtypeStruct((b, s, SWA_Q_HEADS * HEAD_DIM), BF16),
        scratch_shapes=[
            pltpu.VMEM((s, q_w), BF16),
            pltpu.VMEM((s + QBLK, LANES), BF16),
            pltpu.VMEM((s // QBLK + 1, HEAD_DIM, QBLK), BF16),
            pltpu.VMEM((2, 2 * QBLK, group * QBLK), F32),
            pltpu.VMEM((s // QBLK, 2 * QBLK, group * QBLK), BF16),
            pltpu.VMEM((s // QBLK, SUBLANES, group * QBLK), F32),
        ],
        compiler_params=_cparams(2),
        name="swa_attention",
    )(proj, proj, proj, *tables, two(q_gain), two(k_gain), sink_rows)


def kernel(x, positions, norm_mix, norm_mlp, mlp_w_up, mlp_w_down, hyb_w_in, hyb_w_out, ret_gn_gain, dil_q_gain, dil_k_gain, swa_w_qkv, swa_b_qkv, swa_w_out, swa_q_gain, swa_k_gain, swa_sinks):
    b, s, d = x.shape
    depth = norm_mix.shape[0]
    ret_tab = _rope_table(positions, RET_QK_DIM // 2, RET_THETA, RET_QK_DIM)
    rope_tab = _rope_table(positions, ROPE_DIMS // 2, ROPE_THETA, ROPE_DIMS)
    log_gamma = jnp.log1p(-jnp.exp2(-5.0 - jnp.arange(RET_HEADS, dtype=F32)))
    ret_w = RET_HEADS * RET_V_DIM
    dil_col0 = 2 * RET_HEADS * RET_QK_DIM + 2 * ret_w

    x2d = x.reshape(b * s, d)
    for layer in range(depth):
        i = layer // 2
        if layer % 2 == 0:
            w_in = hyb_w_in[i].astype(BF16)
            proj = _norm_proj(x2d, norm_mix[layer], w_in, jnp.zeros((w_in.shape[1],), F32))
            proj = proj.reshape(b, s, -1)
            ra = _retention(proj, ret_tab, log_gamma, ret_gn_gain[i])
            da = _dilated(proj, rope_tab, dil_q_gain[i], dil_k_gain[i], dil_col0)
            w_out = hyb_w_out[i].astype(BF16)
            mixed = [ra.reshape(b * s, -1), da.reshape(b * s, -1)]
        else:
            proj = _norm_proj(x2d, norm_mix[layer], swa_w_qkv[i].astype(BF16), swa_b_qkv[i])
            proj = proj.reshape(b, s, -1)
            att = _swa(proj, rope_tab, swa_q_gain[i], swa_k_gain[i], swa_sinks[i])
            mixed = [att.reshape(b * s, -1)]
            w_out = swa_w_out[i].astype(BF16)
        x2d = _out_mlp(mixed, x2d, w_out, norm_mlp[layer],
                       mlp_w_up[layer].astype(BF16), mlp_w_down[layer].astype(BF16))
    return x2d.reshape(b, s, d)
```

```python
import functools

import jax
import jax.numpy as jnp
import numpy as np
from jax import lax
from jax.experimental import pallas as pl
from jax.experimental.pallas import tpu as pltpu

F32 = jnp.float32
BF16 = jnp.bfloat16

D_MODEL = 1024
D_FF = 4 * D_MODEL
HEAD_DIM = 64
EPS = 1e-6
RET_HEADS = 4
RET_QK_DIM = 64
RET_V_DIM = 128
RET_CHUNK = 128
RET_THETA = 10000.0
DIL_HEADS = 8
DIL_PATTERNS = ((128, 1), (512, 4), (2048, 16))
SWA_Q_HEADS = 16
SWA_KV_HEADS = 4
SWA_WINDOW = 128
ROPE_THETA = 500000.0
ROPE_DIMS = HEAD_DIM // 4

LANES = 128
SUBLANES = 8
QBLK = 128
VMEM_LIMIT_BYTES = 56 * 1024 * 1024
NEG_BIG = -1e30

ROW_TILE = 512
FF_CHUNK = 1024
PRO_ROWS = 128
BLOCK_UNROLL = 4
DEN_ROWS = 16


def _cparams(n_axes):
    return pltpu.CompilerParams(
        dimension_semantics=("arbitrary",) * n_axes,
        vmem_limit_bytes=VMEM_LIMIT_BYTES,
    )


def _resident(shape):
    nd = len(shape)
    return pl.BlockSpec(shape, lambda *_: (0,) * nd, pipeline_mode=pl.Buffered(1))


def _split3(x):
    hi = x.astype(BF16)
    r1 = x - hi.astype(F32)
    mid = r1.astype(BF16)
    lo = (r1 - mid.astype(F32)).astype(BF16)
    return hi, mid, lo


def _trig_kernel(p_ref, inv_ref, ec_ref, es_ref, base_ref, c_ref, s_ref, *, n_pos):
    ang = p_ref[...] * inv_ref[...]
    cos_parts = _split3(jnp.cos(ang))
    sin_parts = _split3(jnp.sin(ang))
    rows = p_ref.shape[0]
    for c in range(n_pos):
        dst = pl.ds(c, rows, stride=n_pos)
        c_ref[dst, :] = base_ref[...] + sum(
            jnp.dot(part, ec_ref[c], preferred_element_type=F32) for part in cos_parts)
        s_ref[dst, :] = sum(
            jnp.dot(part, es_ref[c], preferred_element_type=F32) for part in sin_parts)


def _selectors(half):
    n_pos = LANES // half
    ec = np.zeros((n_pos, LANES, LANES), np.float32)
    es = np.zeros((n_pos, LANES, LANES), np.float32)
    base = np.zeros((1, LANES), np.float32)
    for j in range(LANES):
        d = j % HEAD_DIM
        if d >= 2 * half:
            base[0, j] = 1.0
            continue
        f = d % half
        for c in range(n_pos):
            ec[c, c * half + f, j] = 1.0
            es[c, c * half + f, j] = -1.0 if d < half else 1.0
    return n_pos, jnp.asarray(ec, BF16), jnp.asarray(es, BF16), jnp.asarray(base)


def _rope_table(positions, half, theta, n_rot):
    b, s = positions.shape
    posf = positions.astype(F32)
    inv = jnp.power(jnp.float32(theta), -jnp.arange(half, dtype=F32) * (2.0 / n_rot))
    n_pos, ec, es, base = _selectors(half)
    p = jnp.broadcast_to(posf[..., None], (b, s, half)).reshape(-1, LANES)
    inv_rows = jnp.broadcast_to(jnp.tile(inv, n_pos)[None, :], p.shape)
    rows = p.shape[0]
    tile = min(rows, 2048 // n_pos)
    assert rows % tile == 0
    dense = pl.BlockSpec((tile, LANES), lambda i: (i, 0))
    wide = pl.BlockSpec((tile * n_pos, LANES), lambda i: (i, 0))
    c_tab, s_tab = pl.pallas_call(
        functools.partial(_trig_kernel, n_pos=n_pos),
        grid=(rows // tile,),
        in_specs=[dense, dense, _resident(ec.shape), _resident(es.shape), _resident(base.shape)],
        out_specs=[wide, wide],
        out_shape=[jax.ShapeDtypeStruct((rows * n_pos, LANES), F32)] * 2,
        compiler_params=_cparams(1),
        name="trig_tables",
    )(p, inv_rows, ec, es, base)
    return c_tab.reshape(b, s, LANES), s_tab.reshape(b, s, LANES)


def _swap_matrix(half):
    src = lax.broadcasted_iota(jnp.int32, (LANES, LANES), 0)
    dst = lax.broadcasted_iota(jnp.int32, (LANES, LANES), 1)
    d = dst & (HEAD_DIM - 1)
    want = jnp.where(d < half, dst + half, jnp.where(d < 2 * half, dst - half, -1))
    return jnp.where(src == want, 1.0, 0.0).astype(BF16)


def _rotate(x, c, s, swap):
    swapped = jnp.dot(x.astype(BF16), swap, preferred_element_type=F32)
    return x * c + swapped * s


def _lo_mask(shape):
    return lax.broadcasted_iota(jnp.int32, shape, len(shape) - 1) < HEAD_DIM


def _head_ones():
    r = lax.broadcasted_iota(jnp.int32, (LANES, LANES), 0) < HEAD_DIM
    c = lax.broadcasted_iota(jnp.int32, (LANES, LANES), 1) < HEAD_DIM
    return jnp.where(r == c, 1.0, 0.0).astype(BF16)


def _head_rinv(x, head_ones):
    ss = jnp.dot((x * x).astype(BF16), head_ones, preferred_element_type=F32)
    return lax.rsqrt(ss * (1.0 / HEAD_DIM) + EPS)


def _norm_proj_kernel(x_ref, g_ref, w_ref, b_ref, o_ref, *, n_chunk):
    x = x_ref[...]
    ms = jnp.mean(x * x, axis=-1, keepdims=True)
    h = (x * lax.rsqrt(ms + EPS) * g_ref[...]).astype(BF16)
    n = o_ref.shape[-1]
    for c in range(0, n, n_chunk):
        acc = jnp.dot(h, w_ref[:, c:c + n_chunk], preferred_element_type=F32)
        o_ref[:, c:c + n_chunk] = (acc + b_ref[:, c:c + n_chunk]).astype(o_ref.dtype)


def _norm_proj(x2d, gain, w_bf16, bias):
    m, d = x2d.shape
    n = w_bf16.shape[1]
    return pl.pallas_call(
        functools.partial(_norm_proj_kernel, n_chunk=512),
        grid=(m // ROW_TILE,),
        in_specs=[
            pl.BlockSpec((ROW_TILE, d), lambda i: (i, 0)),
            _resident((1, d)),
            _resident((d, n)),
            _resident((1, n)),
        ],
        out_specs=pl.BlockSpec((ROW_TILE, n), lambda i: (i, 0)),
        out_shape=jax.ShapeDtypeStruct((m, n), BF16),
        compiler_params=_cparams(1),
        name="norm_proj",
    )(x2d, gain.reshape(1, d), w_bf16, bias.reshape(1, n))


def _out_mlp_kernel(*refs, n_mix):
    a_refs = refs[:n_mix]
    x_ref, wo_ref, g_ref, wup_ref, wdn_ref, o_ref = refs[n_mix:]
    mixed = a_refs[0][...] if n_mix == 1 else jnp.concatenate([a[...] for a in a_refs], axis=-1)
    x1 = x_ref[...] + jnp.dot(mixed, wo_ref[...], preferred_element_type=F32)
    ms = jnp.mean(x1 * x1, axis=-1, keepdims=True)
    h = (x1 * lax.rsqrt(ms + EPS) * g_ref[...]).astype(BF16)
    y = x1
    for c in range(0, D_FF, FF_CHUNK):
        u = jnp.dot(h, wup_ref[:, c:c + FF_CHUNK], preferred_element_type=F32)
        u = jnp.square(jnp.maximum(u, 0.0)).astype(BF16)
        y = y + jnp.dot(u, wdn_ref[c:c + FF_CHUNK, :], preferred_element_type=F32)
    o_ref[...] = y


def _out_mlp(mixed, x2d, w_out, gain, w_up, w_down):
    m, d = x2d.shape
    n_mix = len(mixed)
    in_specs = [pl.BlockSpec((ROW_TILE, a.shape[1]), lambda i: (i, 0)) for a in mixed]
    in_specs.append(pl.BlockSpec((ROW_TILE, d), lambda i: (i, 0)))
    in_specs += [_resident(w_out.shape), _resident((1, d)), _resident(w_up.shape), _resident(w_down.shape)]
    return pl.pallas_call(
        functools.partial(_out_mlp_kernel, n_mix=n_mix),
        grid=(m // ROW_TILE,),
        in_specs=in_specs,
        out_specs=pl.BlockSpec((ROW_TILE, d), lambda i: (i, 0)),
        out_shape=jax.ShapeDtypeStruct((m, d), F32),
        compiler_params=_cparams(1),
        name="out_mlp",
    )(*mixed, x2d, w_out, gain.reshape(1, d), w_up, w_down)


def _retention_kernel(q_ref, k_ref, v_ref, g_ref, c_ref, s_ref, lg_ref, gn_ref,
                      o_ref, q_s, k_s, kt_s, *, seq):
    n_chunks = seq // RET_CHUNK
    swap = _swap_matrix(RET_QK_DIM // 2)
    cs = RET_CHUNK

    def prologue(n, carry):
        r0 = pl.multiple_of(n * cs, cs)
        rows = pl.ds(r0, cs)
        c, sn = c_ref[0, rows, :], s_ref[0, rows, :]
        q = _rotate(q_ref[0, rows, :].astype(F32), c, sn, swap)
        k = _rotate(k_ref[0, rows, :].astype(F32), c, sn, swap) * (RET_QK_DIM ** -0.5)
        q_s[n] = q.astype(BF16)
        k_s[n] = k.astype(BF16)
        kt_s[n] = k.T
        return carry

    lax.fori_loop(0, n_chunks, prologue, 0, unroll=4)

    lo = _lo_mask((cs, LANES))
    row = lax.broadcasted_iota(jnp.int32, (cs, cs), 0).astype(F32)
    col = lax.broadcasted_iota(jnp.int32, (cs, cs), 1).astype(F32)
    diff = row - col
    consts = []
    for hh in range(2):
        lg = lg_ref[0, hh:hh + 1, :]
        decay = jnp.where(diff >= 0, jnp.exp(lg * jnp.maximum(diff, 0.0)), 0.0)
        xi = jnp.exp(lg * (row + 1.0))
        zeta = jnp.exp(lg * (cs - 1.0 - col[0:1, :]))
        cd = jnp.exp(lg * float(cs))
        head = lo if hh == 0 else jnp.logical_not(lo)
        consts.append((decay, xi, zeta, cd, head))

    def body(n, state):
        r0 = pl.multiple_of(n * cs, cs)
        rows = pl.ds(r0, cs)
        qc = q_s[n]
        kc = k_s[n]
        ktc = kt_s[n]
        new_state = []
        for hh in range(2):
            decay, xi, zeta, cd, head = consts[hh]
            cols = slice(hh * RET_V_DIM, (hh + 1) * RET_V_DIM)
            r_prev = state[hh]
            qm = jnp.where(head, qc, jnp.zeros_like(qc))
            vc = v_ref[0, rows, cols]
            sc = lax.dot_general(qm, kc, (((1,), (1,)), ((), ())), preferred_element_type=F32) * decay
            o = jnp.dot(sc.astype(BF16), vc, preferred_element_type=F32)
            o = o + jnp.dot(qm, r_prev.astype(BF16), preferred_element_type=F32) * xi
            kz = (ktc * zeta).astype(BF16)
            new_state.append(r_prev * cd + jnp.dot(kz, vc, preferred_element_type=F32))
            mu = jnp.mean(o, axis=-1, keepdims=True)
            dev = o - mu
            var = jnp.mean(dev * dev, axis=-1, keepdims=True)
            y = dev * lax.rsqrt(var + EPS) * gn_ref[0, hh:hh + 1, :]
            gate = g_ref[0, rows, cols].astype(F32)
            o_ref[0, rows, cols] = (gate * jax.nn.sigmoid(gate) * y).astype(o_ref.dtype)
        return tuple(new_state)

    zero = jnp.zeros((LANES, RET_V_DIM), F32)
    lax.fori_loop(0, n_chunks, body, (zero, zero), unroll=4)


def _retention(proj, tables, log_gamma, gn_gain):
    b, s, _ = proj.shape
    n_pairs = RET_HEADS // 2
    pair_w = 2 * RET_V_DIM
    qk_tiles = RET_HEADS * RET_QK_DIM // LANES
    v_off = 2 * qk_tiles * LANES // pair_w
    g_off = v_off + RET_HEADS * RET_V_DIM // pair_w
    tab = pl.BlockSpec((1, s, LANES), lambda bi, p: (bi, 0, 0), pipeline_mode=pl.Buffered(1))
    lg = jnp.broadcast_to(log_gamma.reshape(n_pairs, 2, 1), (n_pairs, 2, LANES))
    n_chunks = s // RET_CHUNK
    return pl.pallas_call(
        functools.partial(_retention_kernel, seq=s),
        grid=(b, n_pairs),
        in_specs=[
            pl.BlockSpec((1, s, LANES), lambda bi, p: (bi, 0, p)),
            pl.BlockSpec((1, s, LANES), lambda bi, p: (bi, 0, qk_tiles + p)),
            pl.BlockSpec((1, s, pair_w), lambda bi, p: (bi, 0, v_off + p)),
            pl.BlockSpec((1, s, pair_w), lambda bi, p: (bi, 0, g_off + p)),
            tab, tab,
            pl.BlockSpec((1, 2, LANES), lambda bi, p: (p, 0, 0)),
            pl.BlockSpec((1, 2, RET_V_DIM), lambda bi, p: (p, 0, 0)),
        ],
        out_specs=pl.BlockSpec((1, s, pair_w), lambda bi, p: (bi, 0, p)),
        out_shape=jax.ShapeDtypeStruct((b, s, RET_HEADS * RET_V_DIM), BF16),
        scratch_shapes=[
            pltpu.VMEM((n_chunks, RET_CHUNK, LANES), BF16),
            pltpu.VMEM((n_chunks, RET_CHUNK, LANES), BF16),
            pltpu.VMEM((n_chunks, LANES, RET_CHUNK), F32),
        ],
        compiler_params=_cparams(2),
        name="retention",
    )(proj, proj, proj, proj, *tables, lg, gn_gain.reshape(n_pairs, 2, RET_V_DIM))


def _band_bias_t(lo_off, hi_off):
    c = lax.broadcasted_iota(jnp.int32, (2 * QBLK, QBLK), 0)
    a = lax.broadcasted_iota(jnp.int32, (2 * QBLK, QBLK), 1)
    band = (c - a >= lo_off) & (c - a <= hi_off)
    later = jnp.where(band, 0.0, NEG_BIG).astype(F32)
    first = jnp.where(band & (c >= QBLK), 0.0, NEG_BIG).astype(F32)
    return first, later


def _dilated_kernel(q_ref, k_ref, v_ref, c_ref, s_ref, qg_ref, kg_ref, o_ref,
                    qn_s, kn_s, vn_s, qt_s, kc_s, vt_s, bias_s, p_s, m_s,
                    o0_s, o1_s, o2_s, l0_s, l1_s, l2_s, *, seq):
    swap = _swap_matrix(ROPE_DIMS // 2)
    head_ones = _head_ones()
    rq_s, rk_s = o0_s, l0_s

    def norms(n, carry):
        rows = pl.ds(pl.multiple_of(n * PRO_ROWS, PRO_ROWS), PRO_ROWS)
        rq_s[rows, :] = _head_rinv(q_ref[0, rows, :].astype(F32), head_ones)
        rk_s[rows, :] = _head_rinv(k_ref[0, rows, :].astype(F32), head_ones)
        return carry

    lax.fori_loop(0, seq // PRO_ROWS, norms, 0, unroll=4)

    def prologue(n, carry):
        rows = pl.ds(pl.multiple_of(n * PRO_ROWS, PRO_ROWS), PRO_ROWS)
        c, sn = c_ref[0, rows, :], s_ref[0, rows, :]
        q = q_ref[0, rows, :].astype(F32) * rq_s[rows, :] * qg_ref[...]
        qn_s[rows, :] = _rotate(q, c, sn, swap) * (HEAD_DIM ** -0.5)
        k = k_ref[0, rows, :].astype(F32) * rk_s[rows, :] * kg_ref[...]
        kn_s[rows, :] = _rotate(k, c, sn, swap)
        vn_s[rows, :] = v_ref[0, rows, :].astype(F32)
        return carry

    lax.fori_loop(0, seq // PRO_ROWS, prologue, 0, unroll=4)

    first, later = _band_bias_t(0, QBLK)
    bias_s[0] = jnp.concatenate([first, first], axis=1)
    bias_s[1] = jnp.concatenate([later, later], axis=1)
    zero_pad = jnp.zeros((QBLK, LANES), BF16)
    top = lax.broadcasted_iota(jnp.int32, (LANES, QBLK), 0) < HEAD_DIM
    ones_rows = jnp.ones((DEN_ROWS, 2 * QBLK), BF16)

    outs = (o0_s, o1_s, o2_s)
    lses = (l0_s, l1_s, l2_s)
    n_flat = seq // QBLK
    for (window, r), on_s, ln_s in zip(DIL_PATTERNS, outs, lses):
        assert window // r == QBLK
        n_blk = seq // r // QBLK
        blk_shift = n_blk.bit_length() - 1
        assert n_blk == 1 << blk_shift

        def pad(j, c2, n_blk=n_blk):
            z = j * (n_blk + 1)
            kc_s[pl.ds(pl.multiple_of(z * QBLK, QBLK), QBLK), :] = zero_pad
            vt_s[z] = zero_pad
            return c2

        lax.fori_loop(0, r, pad, 0)

        def split(idx, r=r, n_blk=n_blk, blk_shift=blk_shift):
            j = lax.shift_right_logical(idx, blk_shift)
            i = idx & (n_blk - 1)
            return j, i

        def gather(idx, c2, r=r, split=split):
            j, i = split(idx)
            src = pl.ds(j + r * QBLK * i, QBLK, stride=r)
            k0 = pl.multiple_of((idx + j + 1) * QBLK, QBLK)
            qt_s[idx] = qn_s[src, :].T.astype(BF16)
            kc_s[pl.ds(k0, QBLK), :] = kn_s[src, :].astype(BF16)
            vt_s[idx + j + 1] = vn_s[src, :].T.astype(BF16)
            return c2

        lax.fori_loop(0, n_flat, gather, 0, unroll=2)

        def scores(idx, c2, split=split):
            j, i = split(idx)
            k0 = pl.multiple_of((idx + j) * QBLK, QBLK)
            qt = qt_s[idx]
            kw = kc_s[pl.ds(k0, 2 * QBLK), :]
            zero = jnp.zeros_like(qt)
            rhs = jnp.concatenate([jnp.where(top, qt, zero), jnp.where(top, zero, qt)], axis=1)
            sc = jnp.dot(kw, rhs, preferred_element_type=F32) + bias_s[jnp.minimum(i, 1)]
            m = jnp.max(sc, axis=0, keepdims=True)
            p_s[idx] = jnp.exp(sc - m).astype(BF16)
            m_s[idx] = jnp.broadcast_to(m, (SUBLANES, 2 * QBLK))
            return c2

        lax.fori_loop(0, n_flat, scores, 0, unroll=8)

        def block(idx, c2, r=r, split=split, on_s=on_s, ln_s=ln_s):
            j, i = split(idx)
            p = p_s[idx]
            m = m_s[idx][0:1, :]
            lhs = jnp.concatenate(
                [jnp.concatenate([vt_s[idx + j], vt_s[idx + j + 1]], axis=1), ones_rows], axis=0)
            ot = jnp.dot(lhs, p, preferred_element_type=F32)
            den = ot[LANES:LANES + 1, :]
            inv = 1.0 / den
            lse = m + jnp.log(den)
            o_t = jnp.concatenate(
                [ot[0:HEAD_DIM, 0:QBLK] * inv[:, 0:QBLK],
                 ot[HEAD_DIM:LANES, QBLK:2 * QBLK] * inv[:, QBLK:2 * QBLK]], axis=0)
            l_t = jnp.concatenate(
                [jnp.broadcast_to(lse[:, 0:QBLK], (HEAD_DIM, QBLK)),
                 jnp.broadcast_to(lse[:, QBLK:2 * QBLK], (HEAD_DIM, QBLK))], axis=0)
            dst = pl.ds(j + r * QBLK * i, QBLK, stride=r)
            on_s[dst, :] = o_t.T
            ln_s[dst, :] = l_t.T
            return c2

        lax.fori_loop(0, n_flat, block, 0, unroll=8)

    def combine(n, carry):
        r0 = pl.multiple_of(n * PRO_ROWS, PRO_ROWS)
        rows = pl.ds(r0, PRO_ROWS)
        ls = [l_s[rows, :] for l_s in lses]
        m = jnp.maximum(jnp.maximum(ls[0], ls[1]), ls[2])
        es = [jnp.exp(l - m) for l in ls]
        num = es[0] * o0_s[rows, :] + es[1] * o1_s[rows, :] + es[2] * o2_s[rows, :]
        o_ref[0, rows, :] = (num / (es[0] + es[1] + es[2])).astype(o_ref.dtype)
        return carry

    lax.fori_loop(0, seq // PRO_ROWS, combine, 0)


def _dilated(proj, tables, q_gain, k_gain, col0):
    b, s, _ = proj.shape
    n_pairs = DIL_HEADS * HEAD_DIM // LANES
    t0 = col0 // LANES
    tab = pl.BlockSpec((1, s, LANES), lambda bi, p: (bi, 0, 0), pipeline_mode=pl.Buffered(1))
    two = lambda g: jnp.concatenate([g, g]).reshape(1, LANES)
    nat = pltpu.VMEM((s, LANES), F32)
    max_r = max(r for _, r in DIL_PATTERNS)
    return pl.pallas_call(
        functools.partial(_dilated_kernel, seq=s),
        grid=(b, n_pairs),
        in_specs=[
            pl.BlockSpec((1, s, LANES), lambda bi, p: (bi, 0, t0 + p)),
            pl.BlockSpec((1, s, LANES), lambda bi, p: (bi, 0, t0 + n_pairs + p)),
            pl.BlockSpec((1, s, LANES), lambda bi, p: (bi, 0, t0 + 2 * n_pairs + p)),
            tab, tab,
            _resident((1, LANES)),
            _resident((1, LANES)),
        ],
        out_specs=pl.BlockSpec((1, s, LANES), lambda bi, p: (bi, 0, p)),
        out_shape=jax.ShapeDtypeStruct((b, s, DIL_HEADS * HEAD_DIM), BF16),
        scratch_shapes=[
            nat, nat, nat,
            pltpu.VMEM((s // QBLK, LANES, QBLK), BF16),
            pltpu.VMEM((s + max_r * QBLK, LANES), BF16),
            pltpu.VMEM((s // QBLK + max_r, LANES, QBLK), BF16),
            pltpu.VMEM((2, 2 * QBLK, 2 * QBLK), F32),
            pltpu.VMEM((s // QBLK, 2 * QBLK, 2 * QBLK), BF16),
            pltpu.VMEM((s // QBLK, SUBLANES, 2 * QBLK), F32),
            nat, nat, nat, nat, nat, nat,
        ],
        compiler_params=_cparams(2),
        name="dilated_attention",
    )(proj, proj, proj, *tables, two(q_gain), two(k_gain))


def _swa_kernel(q_ref, k_ref, v_ref, c_ref, s_ref, qg_ref, kg_ref, sink_ref, o_ref,
                q_s, kd_s, vt_s, bias_s, p_s, m_s, rq_s, rk_s, *, seq):
    group = SWA_Q_HEADS // SWA_KV_HEADS
    swap = _swap_matrix(ROPE_DIMS // 2)
    head_ones = _head_ones()
    kv_in_hi = (pl.program_id(1) % 2) == 1
    blk_per_step = PRO_ROWS // QBLK
    src = lax.broadcasted_iota(jnp.int32, (LANES, LANES), 0)
    dst = lax.broadcasted_iota(jnp.int32, (LANES, LANES), 1) & (HEAD_DIM - 1)
    dup = jnp.where(src == dst + jnp.where(kv_in_hi, HEAD_DIM, 0), 1.0, 0.0).astype(BF16)

    def norms(n, carry):
        rows = pl.ds(pl.multiple_of(n * PRO_ROWS, PRO_ROWS), PRO_ROWS)
        for t in range(group // 2):
            cols = slice(t * LANES, (t + 1) * LANES)
            rq_s[rows, cols] = _head_rinv(q_ref[0, rows, cols].astype(F32), head_ones)
        rk_s[rows, :] = _head_rinv(k_ref[0, rows, :].astype(F32), head_ones)
        return carry

    lax.fori_loop(0, seq // PRO_ROWS, norms, 0, unroll=4)

    def prologue(n, carry):
        r0 = pl.multiple_of(n * PRO_ROWS, PRO_ROWS)
        rows = pl.ds(r0, PRO_ROWS)
        c, sn = c_ref[0, rows, :], s_ref[0, rows, :]
        for t in range(group // 2):
            cols = slice(t * LANES, (t + 1) * LANES)
            q = q_ref[0, rows, cols].astype(F32) * rq_s[rows, cols] * qg_ref[...]
            q_s[rows, cols] = (_rotate(q, c, sn, swap) * (HEAD_DIM ** -0.5)).astype(BF16)
        k = k_ref[0, rows, :].astype(F32) * rk_s[rows, :] * kg_ref[...]
        k = _rotate(k, c, sn, swap).astype(BF16)
        kd_s[pl.ds(r0 + QBLK, PRO_ROWS), :] = jnp.dot(k, dup, preferred_element_type=F32).astype(BF16)
        v_t = v_ref[0, rows, :].astype(F32).T
        v_t = jnp.where(kv_in_hi, v_t[HEAD_DIM:LANES, :], v_t[0:HEAD_DIM, :]).astype(BF16)
        for u in range(blk_per_step):
            vt_s[n * blk_per_step + u + 1] = v_t[:, u * QBLK:(u + 1) * QBLK]
        return carry

    lax.fori_loop(0, seq // PRO_ROWS, prologue, 0, unroll=4)

    first, later = _band_bias_t(1, QBLK)
    bias_s[0] = jnp.concatenate([first] * group, axis=1)
    bias_s[1] = jnp.concatenate([later] * group, axis=1)
    kd_s[0:QBLK, :] = jnp.zeros((QBLK, LANES), BF16)
    vt_s[0] = jnp.zeros((HEAD_DIM, QBLK), BF16)
    lo = _lo_mask((QBLK, LANES))
    heads = (lo, jnp.logical_not(lo))
    sink = jnp.concatenate([sink_ref[0, hh:hh + 1, :] for hh in range(group)], axis=1)
    ones_rows = jnp.ones((DEN_ROWS, 2 * QBLK), BF16)

    def scores(i, carry):
        d0 = pl.multiple_of(i * QBLK, QBLK)
        rows = pl.ds(d0, QBLK)
        kw = kd_s[pl.ds(d0, 2 * QBLK), :]
        stack = []
        for t in range(group // 2):
            qt = q_s[rows, t * LANES:(t + 1) * LANES]
            for head in heads:
                stack.append(jnp.where(head, qt, jnp.zeros_like(qt)))
        qm = jnp.concatenate(stack, axis=0)
        sc = lax.dot_general(kw, qm, (((1,), (1,)), ((), ())), preferred_element_type=F32)
        sc = sc + bias_s[jnp.minimum(i, 1)]
        m = jnp.maximum(jnp.max(sc, axis=0, keepdims=True), sink)
        p_s[i] = jnp.exp(sc - m).astype(BF16)
        m_s[i] = jnp.broadcast_to(m, (SUBLANES, group * QBLK))
        return carry

    lax.fori_loop(0, seq // QBLK, scores, 0, unroll=BLOCK_UNROLL)

    def values(i, carry):
        rows = pl.ds(pl.multiple_of(i * QBLK, QBLK), QBLK)
        m = m_s[i][0:1, :]
        lhs = jnp.concatenate(
            [jnp.concatenate([vt_s[i], vt_s[i + 1]], axis=1), ones_rows], axis=0)
        ot = jnp.dot(lhs, p_s[i], preferred_element_type=F32)
        inv = 1.0 / (ot[HEAD_DIM:HEAD_DIM + 1, :] + jnp.exp(sink - m))
        on = ot[0:HEAD_DIM, :] * inv
        for t in range(group // 2):
            pair_t = jnp.concatenate(
                [on[:, 2 * t * QBLK:(2 * t + 1) * QBLK], on[:, (2 * t + 1) * QBLK:(2 * t + 2) * QBLK]],
                axis=0)
            o_ref[0, rows, t * LANES:(t + 1) * LANES] = pair_t.T.astype(o_ref.dtype)
        return carry

    lax.fori_loop(0, seq // QBLK, values, 0, unroll=BLOCK_UNROLL)


def _swa(proj, tables, q_gain, k_gain, sinks):
    b, s, _ = proj.shape
    group = SWA_Q_HEADS // SWA_KV_HEADS
    q_w = group * HEAD_DIM
    k_t0 = SWA_Q_HEADS * HEAD_DIM // LANES
    v_t0 = k_t0 + SWA_KV_HEADS * HEAD_DIM // LANES
    tab = pl.BlockSpec((1, s, LANES), lambda bi, g: (bi, 0, 0), pipeline_mode=pl.Buffered(1))
    two = lambda g: jnp.concatenate([g, g]).reshape(1, LANES)
    sink_rows = jnp.broadcast_to(sinks.reshape(SWA_KV_HEADS, group, 1), (SWA_KV_HEADS, group, LANES))
    return pl.pallas_call(
        functools.partial(_swa_kernel, seq=s),
        grid=(b, SWA_KV_HEADS),
        in_specs=[
            pl.BlockSpec((1, s, q_w), lambda bi, g: (bi, 0, g)),
            pl.BlockSpec((1, s, LANES), lambda bi, g: (bi, 0, k_t0 + g // 2)),
            pl.BlockSpec((1, s, LANES), lambda bi, g: (bi, 0, v_t0 + g // 2)),
            tab, tab,
            _resident((1, LANES)),
            _resident((1, LANES)),
            pl.BlockSpec((1, group, LANES), lambda bi, g: (g, 0, 0)),
        ],
        out_specs=pl.BlockSpec((1, s, q_w), lambda bi, g: (bi, 0, g)),
        out_shape=jax.ShapeDtypeStruct((b, s, SWA_Q_HEADS * HEAD_DIM), BF16),
        scratch_shapes=[
            pltpu.VMEM((s, q_w), BF16),
            pltpu.VMEM((s + QBLK, LANES), BF16),
            pltpu.VMEM((s // QBLK + 1, HEAD_DIM, QBLK), BF16),
            pltpu.VMEM((2, 2 * QBLK, group * QBLK), F32),
            pltpu.VMEM((s // QBLK, 2 * QBLK, group * QBLK), BF16),
            pltpu.VMEM((s // QBLK, SUBLANES, group * QBLK), F32),
            pltpu.VMEM((s, q_w), F32),
            pltpu.VMEM((s, LANES), F32),
        ],
        compiler_params=_cparams(2),
        name="swa_attention",
    )(proj, proj, proj, *tables, two(q_gain), two(k_gain), sink_rows)


def kernel(x, positions, norm_mix, norm_mlp, mlp_w_up, mlp_w_down, hyb_w_in, hyb_w_out, ret_gn_gain, dil_q_gain, dil_k_gain, swa_w_qkv, swa_b_qkv, swa_w_out, swa_q_gain, swa_k_gain, swa_sinks):
    b, s, d = x.shape
    depth = norm_mix.shape[0]
    ret_tab = _rope_table(positions, RET_QK_DIM // 2, RET_THETA, RET_QK_DIM)
    rope_tab = _rope_table(positions, ROPE_DIMS // 2, ROPE_THETA, ROPE_DIMS)
    log_gamma = jnp.log1p(-jnp.exp2(-5.0 - jnp.arange(RET_HEADS, dtype=F32)))
    ret_w = RET_HEADS * RET_V_DIM
    dil_col0 = 2 * RET_HEADS * RET_QK_DIM + 2 * ret_w

    x2d = x.reshape(b * s, d)
    for layer in range(depth):
        i = layer // 2
        if layer % 2 == 0:
            w_in = hyb_w_in[i].astype(BF16)
            proj = _norm_proj(x2d, norm_mix[layer], w_in, jnp.zeros((w_in.shape[1],), F32))
            proj = proj.reshape(b, s, -1)
            ra = _retention(proj, ret_tab, log_gamma, ret_gn_gain[i])
            da = _dilated(proj, rope_tab, dil_q_gain[i], dil_k_gain[i], dil_col0)
            w_out = hyb_w_out[i].astype(BF16)
            mixed = [ra.reshape(b * s, -1), da.reshape(b * s, -1)]
        else:
            proj = _norm_proj(x2d, norm_mix[layer], swa_w_qkv[i].astype(BF16), swa_b_qkv[i])
            proj = proj.reshape(b, s, -1)
            att = _swa(proj, rope_tab, swa_q_gain[i], swa_k_gain[i], swa_sinks[i])
            mixed = [att.reshape(b * s, -1)]
            w_out = swa_w_out[i].astype(BF16)
        x2d = _out_mlp(mixed, x2d, w_out, norm_mlp[layer],
                       mlp_w_up[layer].astype(BF16), mlp_w_down[layer].astype(BF16))
    return x2d.reshape(b, s, d)
```

```python
import functools

import jax
import jax.numpy as jnp
import numpy as np
from jax import lax
from jax.experimental import pallas as pl
from jax.experimental.pallas import tpu as pltpu

F32 = jnp.float32
BF16 = jnp.bfloat16

D_MODEL = 1024
D_FF = 4 * D_MODEL
HEAD_DIM = 64
EPS = 1e-6
RET_HEADS = 4
RET_QK_DIM = 64
RET_V_DIM = 128
RET_CHUNK = 128
RET_THETA = 10000.0
DIL_HEADS = 8
DIL_PATTERNS = ((128, 1), (512, 4), (2048, 16))
SWA_Q_HEADS = 16
SWA_KV_HEADS = 4
SWA_WINDOW = 128
ROPE_THETA = 500000.0
ROPE_DIMS = HEAD_DIM // 4

LANES = 128
SUBLANES = 8
QBLK = 128
VMEM_LIMIT_BYTES = 56 * 1024 * 1024
NEG_BIG = -1e30
LOG2_E = 1.4426950408889634
Q_SCALE = HEAD_DIM ** -0.5 * LOG2_E

ROW_TILE = 512
FF_CHUNK = 1024
PRO_ROWS = 128
BLOCK_UNROLL = 4
DEN_ROWS = 16
COARSE_R, FINE_R = DIL_PATTERNS[1][1], DIL_PATTERNS[2][1]


def _cparams(n_axes):
    return pltpu.CompilerParams(
        dimension_semantics=("arbitrary",) * n_axes,
        vmem_limit_bytes=VMEM_LIMIT_BYTES,
    )


def _resident(shape):
    nd = len(shape)
    return pl.BlockSpec(shape, lambda *_: (0,) * nd, pipeline_mode=pl.Buffered(1))


def _split3(x):
    hi = x.astype(BF16)
    r1 = x - hi.astype(F32)
    mid = r1.astype(BF16)
    lo = (r1 - mid.astype(F32)).astype(BF16)
    return hi, mid, lo


def _trig_kernel(p_ref, inv_ref, ec_ref, es_ref, base_ref, c_ref, s_ref, *, n_pos):
    ang = p_ref[...] * inv_ref[...]
    cos_parts = _split3(jnp.cos(ang))
    sin_parts = _split3(jnp.sin(ang))
    rows = p_ref.shape[0]
    for c in range(n_pos):
        dst = pl.ds(c, rows, stride=n_pos)
        c_ref[dst, :] = base_ref[...] + sum(
            jnp.dot(part, ec_ref[c], preferred_element_type=F32) for part in cos_parts)
        s_ref[dst, :] = sum(
            jnp.dot(part, es_ref[c], preferred_element_type=F32) for part in sin_parts)


def _selectors(half):
    n_pos = LANES // half
    ec = np.zeros((n_pos, LANES, LANES), np.float32)
    es = np.zeros((n_pos, LANES, LANES), np.float32)
    base = np.zeros((1, LANES), np.float32)
    for j in range(LANES):
        d = j % HEAD_DIM
        if d >= 2 * half:
            base[0, j] = 1.0
            continue
        f = d % half
        for c in range(n_pos):
            ec[c, c * half + f, j] = 1.0
            es[c, c * half + f, j] = -1.0 if d < half else 1.0
    return n_pos, jnp.asarray(ec, BF16), jnp.asarray(es, BF16), jnp.asarray(base)


def _rope_table(positions, half, theta, n_rot):
    b, s = positions.shape
    posf = positions.astype(F32)
    inv = jnp.power(jnp.float32(theta), -jnp.arange(half, dtype=F32) * (2.0 / n_rot))
    n_pos, ec, es, base = _selectors(half)
    p = jnp.broadcast_to(posf[..., None], (b, s, half)).reshape(-1, LANES)
    inv_rows = jnp.broadcast_to(jnp.tile(inv, n_pos)[None, :], p.shape)
    rows = p.shape[0]
    tile = min(rows, 2048 // n_pos)
    assert rows % tile == 0
    dense = pl.BlockSpec((tile, LANES), lambda i: (i, 0))
    wide = pl.BlockSpec((tile * n_pos, LANES), lambda i: (i, 0))
    c_tab, s_tab = pl.pallas_call(
        functools.partial(_trig_kernel, n_pos=n_pos),
        grid=(rows // tile,),
        in_specs=[dense, dense, _resident(ec.shape), _resident(es.shape), _resident(base.shape)],
        out_specs=[wide, wide],
        out_shape=[jax.ShapeDtypeStruct((rows * n_pos, LANES), F32)] * 2,
        compiler_params=_cparams(1),
        name="trig_tables",
    )(p, inv_rows, ec, es, base)
    return c_tab.reshape(b, s, LANES), s_tab.reshape(b, s, LANES)


def _swap_matrix(half):
    src = lax.broadcasted_iota(jnp.int32, (LANES, LANES), 0)
    dst = lax.broadcasted_iota(jnp.int32, (LANES, LANES), 1)
    d = dst & (HEAD_DIM - 1)
    want = jnp.where(d < half, dst + half, jnp.where(d < 2 * half, dst - half, -1))
    return jnp.where(src == want, 1.0, 0.0).astype(BF16)


def _rotate(x, c, s, swap):
    swapped = jnp.dot(x.astype(BF16), swap, preferred_element_type=F32)
    return x * c + swapped * s


def _lo_mask(shape):
    return lax.broadcasted_iota(jnp.int32, shape, len(shape) - 1) < HEAD_DIM


def _head_ones():
    r = lax.broadcasted_iota(jnp.int32, (LANES, LANES), 0) < HEAD_DIM
    c = lax.broadcasted_iota(jnp.int32, (LANES, LANES), 1) < HEAD_DIM
    return jnp.where(r == c, 1.0, 0.0).astype(BF16)


def _head_rinv(x, head_ones):
    ss = jnp.dot((x * x).astype(BF16), head_ones, preferred_element_type=F32)
    return lax.rsqrt(ss * (1.0 / HEAD_DIM) + EPS)


def _norm_proj_kernel(x_ref, g_ref, w_ref, b_ref, o_ref, *, n_chunk):
    x = x_ref[...]
    ms = jnp.mean(x * x, axis=-1, keepdims=True)
    h = (x * lax.rsqrt(ms + EPS) * g_ref[...]).astype(BF16)
    n = o_ref.shape[-1]
    for c in range(0, n, n_chunk):
        acc = jnp.dot(h, w_ref[:, c:c + n_chunk], preferred_element_type=F32)
        o_ref[:, c:c + n_chunk] = (acc + b_ref[:, c:c + n_chunk]).astype(o_ref.dtype)


def _norm_proj(x2d, gain, w_bf16, bias):
    m, d = x2d.shape
    n = w_bf16.shape[1]
    return pl.pallas_call(
        functools.partial(_norm_proj_kernel, n_chunk=512),
        grid=(m // ROW_TILE,),
        in_specs=[
            pl.BlockSpec((ROW_TILE, d), lambda i: (i, 0)),
            _resident((1, d)),
            _resident((d, n)),
            _resident((1, n)),
        ],
        out_specs=pl.BlockSpec((ROW_TILE, n), lambda i: (i, 0)),
        out_shape=jax.ShapeDtypeStruct((m, n), BF16),
        compiler_params=_cparams(1),
        name="norm_proj",
    )(x2d, gain.reshape(1, d), w_bf16, bias.reshape(1, n))


def _out_mlp_kernel(*refs, n_mix):
    a_refs = refs[:n_mix]
    x_ref, wo_ref, g_ref, wup_ref, wdn_ref, o_ref = refs[n_mix:]
    mixed = a_refs[0][...] if n_mix == 1 else jnp.concatenate([a[...] for a in a_refs], axis=-1)
    x1 = x_ref[...] + jnp.dot(mixed, wo_ref[...], preferred_element_type=F32)
    ms = jnp.mean(x1 * x1, axis=-1, keepdims=True)
    h = (x1 * lax.rsqrt(ms + EPS) * g_ref[...]).astype(BF16)
    y = x1
    for c in range(0, D_FF, FF_CHUNK):
        u = jnp.dot(h, wup_ref[:, c:c + FF_CHUNK], preferred_element_type=F32)
        u = jnp.square(jnp.maximum(u, 0.0)).astype(BF16)
        y = y + jnp.dot(u, wdn_ref[c:c + FF_CHUNK, :], preferred_element_type=F32)
    o_ref[...] = y


def _out_mlp(mixed, x2d, w_out, gain, w_up, w_down):
    m, d = x2d.shape
    n_mix = len(mixed)
    in_specs = [pl.BlockSpec((ROW_TILE, a.shape[1]), lambda i: (i, 0)) for a in mixed]
    in_specs.append(pl.BlockSpec((ROW_TILE, d), lambda i: (i, 0)))
    in_specs += [_resident(w_out.shape), _resident((1, d)), _resident(w_up.shape), _resident(w_down.shape)]
    return pl.pallas_call(
        functools.partial(_out_mlp_kernel, n_mix=n_mix),
        grid=(m // ROW_TILE,),
        in_specs=in_specs,
        out_specs=pl.BlockSpec((ROW_TILE, d), lambda i: (i, 0)),
        out_shape=jax.ShapeDtypeStruct((m, d), F32),
        compiler_params=_cparams(1),
        name="out_mlp",
    )(*mixed, x2d, w_out, gain.reshape(1, d), w_up, w_down)


def _retention_kernel(q_ref, k_ref, v_ref, g_ref, c_ref, s_ref, lg_ref, gn_ref,
                      o_ref, q_s, k_s, kt_s, *, seq):
    n_chunks = seq // RET_CHUNK
    swap = _swap_matrix(RET_QK_DIM // 2)
    cs = RET_CHUNK

    def prologue(n, carry):
        r0 = pl.multiple_of(n * cs, cs)
        rows = pl.ds(r0, cs)
        c, sn = c_ref[0, rows, :], s_ref[0, rows, :]
        q = _rotate(q_ref[0, rows, :].astype(F32), c, sn, swap)
        k = _rotate(k_ref[0, rows, :].astype(F32), c, sn, swap) * (RET_QK_DIM ** -0.5)
        q_s[n] = q.astype(BF16)
        k_s[n] = k.astype(BF16)
        kt_s[n] = k.T
        return carry

    lax.fori_loop(0, n_chunks, prologue, 0, unroll=4)

    lo = _lo_mask((cs, LANES))
    row = lax.broadcasted_iota(jnp.int32, (cs, cs), 0).astype(F32)
    col = lax.broadcasted_iota(jnp.int32, (cs, cs), 1).astype(F32)
    diff = row - col
    consts = []
    for hh in range(2):
        lg = lg_ref[0, hh:hh + 1, :]
        decay = jnp.where(diff >= 0, jnp.exp(lg * jnp.maximum(diff, 0.0)), 0.0)
        xi = jnp.exp(lg * (row + 1.0))
        zeta = jnp.exp(lg * (cs - 1.0 - col[0:1, :]))
        cd = jnp.exp(lg * float(cs))
        head = lo if hh == 0 else jnp.logical_not(lo)
        consts.append((decay, xi, zeta, cd, head))

    def body(n, state):
        r0 = pl.multiple_of(n * cs, cs)
        rows = pl.ds(r0, cs)
        qc = q_s[n]
        kc = k_s[n]
        ktc = kt_s[n]
        new_state = []
        for hh in range(2):
            decay, xi, zeta, cd, head = consts[hh]
            cols = slice(hh * RET_V_DIM, (hh + 1) * RET_V_DIM)
            r_prev = state[hh]
            qm = jnp.where(head, qc, jnp.zeros_like(qc))
            vc = v_ref[0, rows, cols]
            sc = lax.dot_general(qm, kc, (((1,), (1,)), ((), ())), preferred_element_type=F32) * decay
            o = jnp.dot(sc.astype(BF16), vc, preferred_element_type=F32)
            o = o + jnp.dot(qm, r_prev.astype(BF16), preferred_element_type=F32) * xi
            kz = (ktc * zeta).astype(BF16)
            new_state.append(r_prev * cd + jnp.dot(kz, vc, preferred_element_type=F32))
            mu = jnp.mean(o, axis=-1, keepdims=True)
            dev = o - mu
            var = jnp.mean(dev * dev, axis=-1, keepdims=True)
            y = dev * lax.rsqrt(var + EPS) * gn_ref[0, hh:hh + 1, :]
            gate = g_ref[0, rows, cols].astype(F32)
            o_ref[0, rows, cols] = (gate * jax.nn.sigmoid(gate) * y).astype(o_ref.dtype)
        return tuple(new_state)

    zero = jnp.zeros((LANES, RET_V_DIM), F32)
    lax.fori_loop(0, n_chunks, body, (zero, zero), unroll=4)


def _retention(proj, tables, log_gamma, gn_gain):
    b, s, _ = proj.shape
    n_pairs = RET_HEADS // 2
    pair_w = 2 * RET_V_DIM
    qk_tiles = RET_HEADS * RET_QK_DIM // LANES
    v_off = 2 * qk_tiles * LANES // pair_w
    g_off = v_off + RET_HEADS * RET_V_DIM // pair_w
    tab = pl.BlockSpec((1, s, LANES), lambda bi, p: (bi, 0, 0), pipeline_mode=pl.Buffered(1))
    lg = jnp.broadcast_to(log_gamma.reshape(n_pairs, 2, 1), (n_pairs, 2, LANES))
    n_chunks = s // RET_CHUNK
    return pl.pallas_call(
        functools.partial(_retention_kernel, seq=s),
        grid=(b, n_pairs),
        in_specs=[
            pl.BlockSpec((1, s, LANES), lambda bi, p: (bi, 0, p)),
            pl.BlockSpec((1, s, LANES), lambda bi, p: (bi, 0, qk_tiles + p)),
            pl.BlockSpec((1, s, pair_w), lambda bi, p: (bi, 0, v_off + p)),
            pl.BlockSpec((1, s, pair_w), lambda bi, p: (bi, 0, g_off + p)),
            tab, tab,
            pl.BlockSpec((1, 2, LANES), lambda bi, p: (p, 0, 0)),
            pl.BlockSpec((1, 2, RET_V_DIM), lambda bi, p: (p, 0, 0)),
        ],
        out_specs=pl.BlockSpec((1, s, pair_w), lambda bi, p: (bi, 0, p)),
        out_shape=jax.ShapeDtypeStruct((b, s, RET_HEADS * RET_V_DIM), BF16),
        scratch_shapes=[
            pltpu.VMEM((n_chunks, RET_CHUNK, LANES), BF16),
            pltpu.VMEM((n_chunks, RET_CHUNK, LANES), BF16),
            pltpu.VMEM((n_chunks, LANES, RET_CHUNK), F32),
        ],
        compiler_params=_cparams(2),
        name="retention",
    )(proj, proj, proj, proj, *tables, lg, gn_gain.reshape(n_pairs, 2, RET_V_DIM))


def _band_bias_t(lo_off, hi_off):
    c = lax.broadcasted_iota(jnp.int32, (2 * QBLK, QBLK), 0)
    a = lax.broadcasted_iota(jnp.int32, (2 * QBLK, QBLK), 1)
    band = (c - a >= lo_off) & (c - a <= hi_off)
    later = jnp.where(band, 0.0, NEG_BIG).astype(F32)
    first = jnp.where(band & (c >= QBLK), 0.0, NEG_BIG).astype(F32)
    return first, later


def _dilated_kernel(q_ref, k_ref, v_ref, c_ref, s_ref, qg_ref, kg_ref, o_ref,
                    qn_s, kn_s, vn_s, q4_s, k4_s, v4_s, qt_s, kc_s, vt_s, bias_s, p_s, m_s,
                    o0_s, o1_s, o2_s, l0_s, l1_s, l2_s, *, seq):
    swap = _swap_matrix(ROPE_DIMS // 2)
    head_ones = _head_ones()
    rq_s, rk_s = o0_s, l0_s

    def norms(n, carry):
        rows = pl.ds(pl.multiple_of(n * PRO_ROWS, PRO_ROWS), PRO_ROWS)
        rq_s[rows, :] = _head_rinv(q_ref[0, rows, :].astype(F32), head_ones)
        rk_s[rows, :] = _head_rinv(k_ref[0, rows, :].astype(F32), head_ones)
        return carry

    lax.fori_loop(0, seq // PRO_ROWS, norms, 0, unroll=4)

    def prologue(n, carry):
        rows = pl.ds(pl.multiple_of(n * PRO_ROWS, PRO_ROWS), PRO_ROWS)
        c, sn = c_ref[0, rows, :], s_ref[0, rows, :]
        q = q_ref[0, rows, :].astype(F32) * rq_s[rows, :] * qg_ref[...]
        qn_s[rows, :] = _rotate(q, c, sn, swap) * Q_SCALE
        k = k_ref[0, rows, :].astype(F32) * rk_s[rows, :] * kg_ref[...]
        kn_s[rows, :] = _rotate(k, c, sn, swap)
        vn_s[rows, :] = v_ref[0, rows, :].astype(F32)
        return carry

    lax.fori_loop(0, seq // PRO_ROWS, prologue, 0, unroll=4)

    first, later = _band_bias_t(0, QBLK)
    bias_s[0] = jnp.concatenate([first, first], axis=1)
    bias_s[1] = jnp.concatenate([later, later], axis=1)
    zero_pad = jnp.zeros((QBLK, LANES), BF16)
    top = lax.broadcasted_iota(jnp.int32, (LANES, QBLK), 0) < HEAD_DIM
    ones_rows = jnp.ones((DEN_ROWS, 2 * QBLK), BF16)

    outs = (o0_s, o1_s, o2_s)
    lses = (l0_s, l1_s, l2_s)
    n_flat = seq // QBLK
    for (window, r), on_s, ln_s in zip(DIL_PATTERNS, outs, lses):
        assert window // r == QBLK
        n_blk = seq // r // QBLK
        blk_shift = n_blk.bit_length() - 1
        assert n_blk == 1 << blk_shift

        def pad(j, c2, n_blk=n_blk):
            z = j * (n_blk + 1)
            kc_s[pl.ds(pl.multiple_of(z * QBLK, QBLK), QBLK), :] = zero_pad
            vt_s[z] = zero_pad
            return c2

        lax.fori_loop(0, r, pad, 0)

        def split(idx, r=r, n_blk=n_blk, blk_shift=blk_shift):
            j = lax.shift_right_logical(idx, blk_shift)
            i = idx & (n_blk - 1)
            return j, i

        def gather(idx, c2, r=r, split=split):
            j, i = split(idx)
            if r == FINE_R:
                base = ((j & (COARSE_R - 1)) * (seq // COARSE_R)
                        + lax.shift_right_logical(j, COARSE_R.bit_length() - 1))
                src = pl.ds(base + (QBLK * r // COARSE_R) * i, QBLK, stride=r // COARSE_R)
                q, k, v = q4_s[src, :], k4_s[src, :], v4_s[src, :]
            else:
                src = pl.ds(j + r * QBLK * i, QBLK, stride=r)
                q, k, v = qn_s[src, :], kn_s[src, :], vn_s[src, :]
            if r == COARSE_R:
                dense = pl.ds(pl.multiple_of(idx * QBLK, QBLK), QBLK)
                q4_s[dense, :] = q
                k4_s[dense, :] = k
                v4_s[dense, :] = v
            k0 = pl.multiple_of((idx + j + 1) * QBLK, QBLK)
            qt_s[idx] = q.T.astype(BF16)
            kc_s[pl.ds(k0, QBLK), :] = k.astype(BF16)
            vt_s[idx + j + 1] = v.T.astype(BF16)
            return c2

        lax.fori_loop(0, n_flat, gather, 0, unroll=8)

        def scores(idx, c2, split=split):
            j, i = split(idx)
            k0 = pl.multiple_of((idx + j) * QBLK, QBLK)
            qt = qt_s[idx]
            kw = kc_s[pl.ds(k0, 2 * QBLK), :]
            zero = jnp.zeros_like(qt)
            rhs = jnp.concatenate([jnp.where(top, qt, zero), jnp.where(top, zero, qt)], axis=1)
            sc = jnp.dot(kw, rhs, preferred_element_type=F32) + bias_s[jnp.minimum(i, 1)]
            m = jnp.max(sc, axis=0, keepdims=True)
            p_s[idx] = jnp.exp2(sc - m).astype(BF16)
            m_s[idx] = jnp.broadcast_to(m, (SUBLANES, 2 * QBLK))
            return c2

        lax.fori_loop(0, n_flat, scores, 0, unroll=16)

        def block(idx, c2, r=r, split=split, on_s=on_s, ln_s=ln_s):
            j, i = split(idx)
            p = p_s[idx]
            m = m_s[idx][0:1, :]
            lhs = jnp.concatenate(
                [jnp.concatenate([vt_s[idx + j], vt_s[idx + j + 1]], axis=1), ones_rows], axis=0)
            ot = jnp.dot(lhs, p, preferred_element_type=F32)
            den = ot[LANES:LANES + 1, :]
            inv = 1.0 / den
            lse = m + jnp.log2(den)
            o_t = jnp.concatenate(
                [ot[0:HEAD_DIM, 0:QBLK] * inv[:, 0:QBLK],
                 ot[HEAD_DIM:LANES, QBLK:2 * QBLK] * inv[:, QBLK:2 * QBLK]], axis=0)
            l_t = jnp.concatenate(
                [jnp.broadcast_to(lse[:, 0:QBLK], (HEAD_DIM, QBLK)),
                 jnp.broadcast_to(lse[:, QBLK:2 * QBLK], (HEAD_DIM, QBLK))], axis=0)
            dst = pl.ds(j + r * QBLK * i, QBLK, stride=r)
            on_s[dst, :] = o_t.T
            ln_s[dst, :] = l_t.T
            return c2

        lax.fori_loop(0, n_flat, block, 0, unroll=8)

    def combine(n, carry):
        r0 = pl.multiple_of(n * PRO_ROWS, PRO_ROWS)
        rows = pl.ds(r0, PRO_ROWS)
        ls = [l_s[rows, :] for l_s in lses]
        m = jnp.maximum(jnp.maximum(ls[0], ls[1]), ls[2])
        es = [jnp.exp2(l - m) for l in ls]
        num = es[0] * o0_s[rows, :] + es[1] * o1_s[rows, :] + es[2] * o2_s[rows, :]
        o_ref[0, rows, :] = (num / (es[0] + es[1] + es[2])).astype(o_ref.dtype)
        return carry

    lax.fori_loop(0, seq // PRO_ROWS, combine, 0)


def _dilated(proj, tables, q_gain, k_gain, col0):
    b, s, _ = proj.shape
    n_pairs = DIL_HEADS * HEAD_DIM // LANES
    t0 = col0 // LANES
    tab = pl.BlockSpec((1, s, LANES), lambda bi, p: (bi, 0, 0), pipeline_mode=pl.Buffered(1))
    two = lambda g: jnp.concatenate([g, g]).reshape(1, LANES)
    nat = pltpu.VMEM((s, LANES), F32)
    max_r = max(r for _, r in DIL_PATTERNS)
    return pl.pallas_call(
        functools.partial(_dilated_kernel, seq=s),
        grid=(b, n_pairs),
        in_specs=[
            pl.BlockSpec((1, s, LANES), lambda bi, p: (bi, 0, t0 + p)),
            pl.BlockSpec((1, s, LANES), lambda bi, p: (bi, 0, t0 + n_pairs + p)),
            pl.BlockSpec((1, s, LANES), lambda bi, p: (bi, 0, t0 + 2 * n_pairs + p)),
            tab, tab,
            _resident((1, LANES)),
            _resident((1, LANES)),
        ],
        out_specs=pl.BlockSpec((1, s, LANES), lambda bi, p: (bi, 0, p)),
        out_shape=jax.ShapeDtypeStruct((b, s, DIL_HEADS * HEAD_DIM), BF16),
        scratch_shapes=[
            nat, nat, nat, nat, nat, nat,
            pltpu.VMEM((s // QBLK, LANES, QBLK), BF16),
            pltpu.VMEM((s + max_r * QBLK, LANES), BF16),
            pltpu.VMEM((s // QBLK + max_r, LANES, QBLK), BF16),
            pltpu.VMEM((2, 2 * QBLK, 2 * QBLK), F32),
            pltpu.VMEM((s // QBLK, 2 * QBLK, 2 * QBLK), BF16),
            pltpu.VMEM((s // QBLK, SUBLANES, 2 * QBLK), F32),
            nat, nat, nat, nat, nat, nat,
        ],
        compiler_params=_cparams(2),
        name="dilated_attention",
    )(proj, proj, proj, *tables, two(q_gain), two(k_gain))


def _swa_kernel(q_ref, k_ref, v_ref, c_ref, s_ref, qg_ref, kg_ref, sink_ref, o_ref,
                q_s, kd_s, vt_s, bias_s, p_s, m_s, rq_s, rk_s, *, seq):
    group = SWA_Q_HEADS // SWA_KV_HEADS
    swap = _swap_matrix(ROPE_DIMS // 2)
    head_ones = _head_ones()
    kv_in_hi = (pl.program_id(1) % 2) == 1
    blk_per_step = PRO_ROWS // QBLK
    src = lax.broadcasted_iota(jnp.int32, (LANES, LANES), 0)
    dst = lax.broadcasted_iota(jnp.int32, (LANES, LANES), 1) & (HEAD_DIM - 1)
    dup = jnp.where(src == dst + jnp.where(kv_in_hi, HEAD_DIM, 0), 1.0, 0.0).astype(BF16)

    def norms(n, carry):
        rows = pl.ds(pl.multiple_of(n * PRO_ROWS, PRO_ROWS), PRO_ROWS)
        for t in range(group // 2):
            cols = slice(t * LANES, (t + 1) * LANES)
            rq_s[rows, cols] = _head_rinv(q_ref[0, rows, cols].astype(F32), head_ones)
        rk_s[rows, :] = _head_rinv(k_ref[0, rows, :].astype(F32), head_ones)
        return carry

    lax.fori_loop(0, seq // PRO_ROWS, norms, 0, unroll=4)

    def prologue(n, carry):
        r0 = pl.multiple_of(n * PRO_ROWS, PRO_ROWS)
        rows = pl.ds(r0, PRO_ROWS)
        c, sn = c_ref[0, rows, :], s_ref[0, rows, :]
        for t in range(group // 2):
            cols = slice(t * LANES, (t + 1) * LANES)
            q = q_ref[0, rows, cols].astype(F32) * rq_s[rows, cols] * qg_ref[...]
            q_s[rows, cols] = (_rotate(q, c, sn, swap) * Q_SCALE).astype(BF16)
        k = k_ref[0, rows, :].astype(F32) * rk_s[rows, :] * kg_ref[...]
        k = _rotate(k, c, sn, swap).astype(BF16)
        kd_s[pl.ds(r0 + QBLK, PRO_ROWS), :] = jnp.dot(k, dup, preferred_element_type=F32).astype(BF16)
        v_t = v_ref[0, rows, :].astype(F32).T
        v_t = jnp.where(kv_in_hi, v_t[HEAD_DIM:LANES, :], v_t[0:HEAD_DIM, :]).astype(BF16)
        for u in range(blk_per_step):
            vt_s[n * blk_per_step + u + 1] = v_t[:, u * QBLK:(u + 1) * QBLK]
        return carry

    lax.fori_loop(0, seq // PRO_ROWS, prologue, 0, unroll=4)

    first, later = _band_bias_t(1, QBLK)
    bias_s[0] = jnp.concatenate([first] * group, axis=1)
    bias_s[1] = jnp.concatenate([later] * group, axis=1)
    kd_s[0:QBLK, :] = jnp.zeros((QBLK, LANES), BF16)
    vt_s[0] = jnp.zeros((HEAD_DIM, QBLK), BF16)
    lo = _lo_mask((QBLK, LANES))
    heads = (lo, jnp.logical_not(lo))
    sink = jnp.concatenate([sink_ref[0, hh:hh + 1, :] for hh in range(group)], axis=1) * LOG2_E
    ones_rows = jnp.ones((DEN_ROWS, 2 * QBLK), BF16)

    def scores(i, carry):
        d0 = pl.multiple_of(i * QBLK, QBLK)
        rows = pl.ds(d0, QBLK)
        kw = kd_s[pl.ds(d0, 2 * QBLK), :]
        stack = []
        for t in range(group // 2):
            qt = q_s[rows, t * LANES:(t + 1) * LANES]
            for head in heads:
                stack.append(jnp.where(head, qt, jnp.zeros_like(qt)))
        qm = jnp.concatenate(stack, axis=0)
        sc = lax.dot_general(kw, qm, (((1,), (1,)), ((), ())), preferred_element_type=F32)
        sc = sc + bias_s[jnp.minimum(i, 1)]
        m = jnp.maximum(jnp.max(sc, axis=0, keepdims=True), sink)
        p_s[i] = jnp.exp2(sc - m).astype(BF16)
        m_s[i] = jnp.broadcast_to(m, (SUBLANES, group * QBLK))
        return carry

    lax.fori_loop(0, seq // QBLK, scores, 0, unroll=BLOCK_UNROLL)

    def values(i, carry):
        rows = pl.ds(pl.multiple_of(i * QBLK, QBLK), QBLK)
        m = m_s[i][0:1, :]
        lhs = jnp.concatenate(
            [jnp.concatenate([vt_s[i], vt_s[i + 1]], axis=1), ones_rows], axis=0)
        ot = jnp.dot(lhs, p_s[i], preferred_element_type=F32)
        inv = 1.0 / (ot[HEAD_DIM:HEAD_DIM + 1, :] + jnp.exp2(sink - m))
        on = ot[0:HEAD_DIM, :] * inv
        for t in range(group // 2):
            pair_t = jnp.concatenate(
                [on[:, 2 * t * QBLK:(2 * t + 1) * QBLK], on[:, (2 * t + 1) * QBLK:(2 * t + 2) * QBLK]],
                axis=0)
            o_ref[0, rows, t * LANES:(t + 1) * LANES] = pair_t.T.astype(o_ref.dtype)
        return carry

    lax.fori_loop(0, seq // QBLK, values, 0, unroll=BLOCK_UNROLL)


def _swa(proj, tables, q_gain, k_gain, sinks):
    b, s, _ = proj.shape
    group = SWA_Q_HEADS // SWA_KV_HEADS
    q_w = group * HEAD_DIM
    k_t0 = SWA_Q_HEADS * HEAD_DIM // LANES
    v_t0 = k_t0 + SWA_KV_HEADS * HEAD_DIM // LANES
    tab = pl.BlockSpec((1, s, LANES), lambda bi, g: (bi, 0, 0), pipeline_mode=pl.Buffered(1))
    two = lambda g: jnp.concatenate([g, g]).reshape(1, LANES)
    sink_rows = jnp.broadcast_to(sinks.reshape(SWA_KV_HEADS, group, 1), (SWA_KV_HEADS, group, LANES))
    return pl.pallas_call(
        functools.partial(_swa_kernel, seq=s),
        grid=(b, SWA_KV_HEADS),
        in_specs=[
            pl.BlockSpec((1, s, q_w), lambda bi, g: (bi, 0, g)),
            pl.BlockSpec((1, s, LANES), lambda bi, g: (bi, 0, k_t0 + g // 2)),
            pl.BlockSpec((1, s, LANES), lambda bi, g: (bi, 0, v_t0 + g // 2)),
            tab, tab,
            _resident((1, LANES)),
            _resident((1, LANES)),
            pl.BlockSpec((1, group, LANES), lambda bi, g: (g, 0, 0)),
        ],
        out_specs=pl.BlockSpec((1, s, q_w), lambda bi, g: (bi, 0, g)),
        out_shape=jax.ShapeDtypeStruct((b, s, SWA_Q_HEADS * HEAD_DIM), BF16),
        scratch_shapes=[
            pltpu.VMEM((s, q_w), BF16),
            pltpu.VMEM((s + QBLK, LANES), BF16),
            pltpu.VMEM((s // QBLK + 1, HEAD_DIM, QBLK), BF16),
            pltpu.VMEM((2, 2 * QBLK, group * QBLK), F32),
            pltpu.VMEM((s // QBLK, 2 * QBLK, group * QBLK), BF16),
            pltpu.VMEM((s // QBLK, SUBLANES, group * QBLK), F32),
            pltpu.VMEM((s, q_w), F32),
            pltpu.VMEM((s, LANES), F32),
        ],
        compiler_params=_cparams(2),
        name="swa_attention",
    )(proj, proj, proj, *tables, two(q_gain), two(k_gain), sink_rows)


def kernel(x, positions, norm_mix, norm_mlp, mlp_w_up, mlp_w_down, hyb_w_in, hyb_w_out, ret_gn_gain, dil_q_gain, dil_k_gain, swa_w_qkv, swa_b_qkv, swa_w_out, swa_q_gain, swa_k_gain, swa_sinks):
    b, s, d = x.shape
    depth = norm_mix.shape[0]
    ret_tab = _rope_table(positions, RET_QK_DIM // 2, RET_THETA, RET_QK_DIM)
    rope_tab = _rope_table(positions, ROPE_DIMS // 2, ROPE_THETA, ROPE_DIMS)
    log_gamma = jnp.log1p(-jnp.exp2(-5.0 - jnp.arange(RET_HEADS, dtype=F32)))
    ret_w = RET_HEADS * RET_V_DIM
    dil_col0 = 2 * RET_HEADS * RET_QK_DIM + 2 * ret_w

    x2d = x.reshape(b * s, d)
    for layer in range(depth):
        i = layer // 2
        if layer % 2 == 0:
            w_in = hyb_w_in[i].astype(BF16)
            proj = _norm_proj(x2d, norm_mix[layer], w_in, jnp.zeros((w_in.shape[1],), F32))
            proj = proj.reshape(b, s, -1)
            ra = _retention(proj, ret_tab, log_gamma, ret_gn_gain[i])
            da = _dilated(proj, rope_tab, dil_q_gain[i], dil_k_gain[i], dil_col0)
            w_out = hyb_w_out[i].astype(BF16)
            mixed = [ra.reshape(b * s, -1), da.reshape(b * s, -1)]
        else:
            proj = _norm_proj(x2d, norm_mix[layer], swa_w_qkv[i].astype(BF16), swa_b_qkv[i])
            proj = proj.reshape(b, s, -1)
            att = _swa(proj, rope_tab, swa_q_gain[i], swa_k_gain[i], swa_sinks[i])
            mixed = [att.reshape(b * s, -1)]
            w_out = swa_w_out[i].astype(BF16)
        x2d = _out_mlp(mixed, x2d, w_out, norm_mlp[layer],
                       mlp_w_up[layer].astype(BF16), mlp_w_down[layer].astype(BF16))
    return x2d.reshape(b, s, d)
```

```python
import functools

import jax
import jax.numpy as jnp
import numpy as np
from jax import lax
from jax.experimental import pallas as pl
from jax.experimental.pallas import tpu as pltpu

F32 = jnp.float32
BF16 = jnp.bfloat16

D_MODEL = 1024
D_FF = 4 * D_MODEL
HEAD_DIM = 64
EPS = 1e-6
RET_HEADS = 4
RET_QK_DIM = 64
RET_V_DIM = 128
RET_CHUNK = 128
RET_THETA = 10000.0
DIL_HEADS = 8
DIL_PATTERNS = ((128, 1), (512, 4), (2048, 16))
SWA_Q_HEADS = 16
SWA_KV_HEADS = 4
SWA_WINDOW = 128
ROPE_THETA = 500000.0
ROPE_DIMS = HEAD_DIM // 4

LANES = 128
SUBLANES = 8
QBLK = 128
VMEM_LIMIT_BYTES = 56 * 1024 * 1024
NEG_BIG = -1e30
LOG2_E = 1.4426950408889634
Q_SCALE = HEAD_DIM ** -0.5 * LOG2_E

ROW_TILE = 512
FF_CHUNK = 1024
PRO_ROWS = 128
BLOCK_UNROLL = 4
DEN_ROWS = 16
COARSE_R, FINE_R = DIL_PATTERNS[1][1], DIL_PATTERNS[2][1]


def _cparams(n_axes):
    return pltpu.CompilerParams(
        dimension_semantics=("arbitrary",) * n_axes,
        vmem_limit_bytes=VMEM_LIMIT_BYTES,
    )


def _resident(shape):
    nd = len(shape)
    return pl.BlockSpec(shape, lambda *_: (0,) * nd, pipeline_mode=pl.Buffered(1))


def _split3(x):
    hi = x.astype(BF16)
    r1 = x - hi.astype(F32)
    mid = r1.astype(BF16)
    lo = (r1 - mid.astype(F32)).astype(BF16)
    return hi, mid, lo


def _trig_kernel(p_ref, inv_ref, ec_ref, es_ref, base_ref, c_ref, s_ref, *, n_pos):
    ang = p_ref[...] * inv_ref[...]
    cos_parts = _split3(jnp.cos(ang))
    sin_parts = _split3(jnp.sin(ang))
    rows = p_ref.shape[0]
    for c in range(n_pos):
        dst = pl.ds(c, rows, stride=n_pos)
        c_ref[dst, :] = base_ref[...] + sum(
            jnp.dot(part, ec_ref[c], preferred_element_type=F32) for part in cos_parts)
        s_ref[dst, :] = sum(
            jnp.dot(part, es_ref[c], preferred_element_type=F32) for part in sin_parts)


def _selectors(half):
    n_pos = LANES // half
    ec = np.zeros((n_pos, LANES, LANES), np.float32)
    es = np.zeros((n_pos, LANES, LANES), np.float32)
    base = np.zeros((1, LANES), np.float32)
    for j in range(LANES):
        d = j % HEAD_DIM
        if d >= 2 * half:
            base[0, j] = 1.0
            continue
        f = d % half
        for c in range(n_pos):
            ec[c, c * half + f, j] = 1.0
            es[c, c * half + f, j] = -1.0 if d < half else 1.0
    return n_pos, jnp.asarray(ec, BF16), jnp.asarray(es, BF16), jnp.asarray(base)


def _rope_table(positions, half, theta, n_rot):
    b, s = positions.shape
    posf = positions.astype(F32)
    inv = jnp.power(jnp.float32(theta), -jnp.arange(half, dtype=F32) * (2.0 / n_rot))
    n_pos, ec, es, base = _selectors(half)
    p = jnp.broadcast_to(posf[..., None], (b, s, half)).reshape(-1, LANES)
    inv_rows = jnp.broadcast_to(jnp.tile(inv, n_pos)[None, :], p.shape)
    rows = p.shape[0]
    tile = min(rows, 2048 // n_pos)
    assert rows % tile == 0
    dense = pl.BlockSpec((tile, LANES), lambda i: (i, 0))
    wide = pl.BlockSpec((tile * n_pos, LANES), lambda i: (i, 0))
    c_tab, s_tab = pl.pallas_call(
        functools.partial(_trig_kernel, n_pos=n_pos),
        grid=(rows // tile,),
        in_specs=[dense, dense, _resident(ec.shape), _resident(es.shape), _resident(base.shape)],
        out_specs=[wide, wide],
        out_shape=[jax.ShapeDtypeStruct((rows * n_pos, LANES), F32)] * 2,
        compiler_params=_cparams(1),
        name="trig_tables",
    )(p, inv_rows, ec, es, base)
    return c_tab.reshape(b, s, LANES), s_tab.reshape(b, s, LANES)


def _swap_matrix(half):
    src = lax.broadcasted_iota(jnp.int32, (LANES, LANES), 0)
    dst = lax.broadcasted_iota(jnp.int32, (LANES, LANES), 1)
    d = dst & (HEAD_DIM - 1)
    want = jnp.where(d < half, dst + half, jnp.where(d < 2 * half, dst - half, -1))
    return jnp.where(src == want, 1.0, 0.0).astype(BF16)


def _rotate(x, c, s, swap):
    swapped = jnp.dot(x.astype(BF16), swap, preferred_element_type=F32)
    return x * c + swapped * s


def _lo_mask(shape):
    return lax.broadcasted_iota(jnp.int32, shape, len(shape) - 1) < HEAD_DIM


def _head_ones():
    r = lax.broadcasted_iota(jnp.int32, (LANES, LANES), 0) < HEAD_DIM
    c = lax.broadcasted_iota(jnp.int32, (LANES, LANES), 1) < HEAD_DIM
    return jnp.where(r == c, 1.0, 0.0).astype(BF16)


def _head_rinv(x, head_ones):
    ss = jnp.dot((x * x).astype(BF16), head_ones, preferred_element_type=F32)
    return lax.rsqrt(ss * (1.0 / HEAD_DIM) + EPS)


def _norm_proj_kernel(x_ref, g_ref, w_ref, b_ref, o_ref, *, n_chunk):
    x = x_ref[...]
    ms = jnp.mean(x * x, axis=-1, keepdims=True)
    h = (x * lax.rsqrt(ms + EPS) * g_ref[...]).astype(BF16)
    n = o_ref.shape[-1]
    for c in range(0, n, n_chunk):
        acc = jnp.dot(h, w_ref[:, c:c + n_chunk], preferred_element_type=F32)
        o_ref[:, c:c + n_chunk] = (acc + b_ref[:, c:c + n_chunk]).astype(o_ref.dtype)


def _norm_proj(x2d, gain, w_bf16, bias):
    m, d = x2d.shape
    n = w_bf16.shape[1]
    return pl.pallas_call(
        functools.partial(_norm_proj_kernel, n_chunk=512),
        grid=(m // ROW_TILE,),
        in_specs=[
            pl.BlockSpec((ROW_TILE, d), lambda i: (i, 0)),
            _resident((1, d)),
            _resident((d, n)),
            _resident((1, n)),
        ],
        out_specs=pl.BlockSpec((ROW_TILE, n), lambda i: (i, 0)),
        out_shape=jax.ShapeDtypeStruct((m, n), BF16),
        compiler_params=_cparams(1),
        name="norm_proj",
    )(x2d, gain.reshape(1, d), w_bf16, bias.reshape(1, n))


def _out_mlp_kernel(*refs, n_mix):
    a_refs = refs[:n_mix]
    x_ref, wo_ref, g_ref, wup_ref, wdn_ref, o_ref = refs[n_mix:]
    mixed = a_refs[0][...] if n_mix == 1 else jnp.concatenate([a[...] for a in a_refs], axis=-1)
    x1 = x_ref[...] + jnp.dot(mixed, wo_ref[...], preferred_element_type=F32)
    ms = jnp.mean(x1 * x1, axis=-1, keepdims=True)
    h = (x1 * lax.rsqrt(ms + EPS) * g_ref[...]).astype(BF16)
    y = x1
    for c in range(0, D_FF, FF_CHUNK):
        u = jnp.dot(h, wup_ref[:, c:c + FF_CHUNK], preferred_element_type=F32)
        u = jnp.square(jnp.maximum(u, 0.0)).astype(BF16)
        y = y + jnp.dot(u, wdn_ref[c:c + FF_CHUNK, :], preferred_element_type=F32)
    o_ref[...] = y


def _out_mlp(mixed, x2d, w_out, gain, w_up, w_down):
    m, d = x2d.shape
    n_mix = len(mixed)
    in_specs = [pl.BlockSpec((ROW_TILE, a.shape[1]), lambda i: (i, 0)) for a in mixed]
    in_specs.append(pl.BlockSpec((ROW_TILE, d), lambda i: (i, 0)))
    in_specs += [_resident(w_out.shape), _resident((1, d)), _resident(w_up.shape), _resident(w_down.shape)]
    return pl.pallas_call(
        functools.partial(_out_mlp_kernel, n_mix=n_mix),
        grid=(m // ROW_TILE,),
        in_specs=in_specs,
        out_specs=pl.BlockSpec((ROW_TILE, d), lambda i: (i, 0)),
        out_shape=jax.ShapeDtypeStruct((m, d), F32),
        compiler_params=_cparams(1),
        name="out_mlp",
    )(*mixed, x2d, w_out, gain.reshape(1, d), w_up, w_down)


def _retention_kernel(q_ref, k_ref, v_ref, g_ref, c_ref, s_ref, lg_ref, gn_ref,
                      o_ref, q_s, k_s, kt_s, o_s, kv_s, *, seq):
    n_chunks = seq // RET_CHUNK
    swap = _swap_matrix(RET_QK_DIM // 2)
    cs = RET_CHUNK

    def prologue(n, carry):
        r0 = pl.multiple_of(n * cs, cs)
        rows = pl.ds(r0, cs)
        c, sn = c_ref[0, rows, :], s_ref[0, rows, :]
        q = _rotate(q_ref[0, rows, :].astype(F32), c, sn, swap)
        k = _rotate(k_ref[0, rows, :].astype(F32), c, sn, swap) * (RET_QK_DIM ** -0.5)
        q_s[n] = q.astype(BF16)
        k_s[n] = k.astype(BF16)
        kt_s[n] = k.T
        return carry

    lax.fori_loop(0, n_chunks, prologue, 0, unroll=4)

    lo = _lo_mask((cs, LANES))
    row = lax.broadcasted_iota(jnp.int32, (cs, cs), 0).astype(F32)
    col = lax.broadcasted_iota(jnp.int32, (cs, cs), 1).astype(F32)
    diff = row - col
    consts = []
    for hh in range(2):
        lg = lg_ref[0, hh:hh + 1, :]
        decay = jnp.where(diff >= 0, jnp.exp(lg * jnp.maximum(diff, 0.0)), 0.0)
        xi = jnp.exp(lg * (row + 1.0))
        zeta = jnp.exp(lg * (cs - 1.0 - col[0:1, :]))
        cd = jnp.exp(lg * float(cs))
        head = lo if hh == 0 else jnp.logical_not(lo)
        consts.append((decay, xi, zeta, cd, head))

    def inner(n, carry):
        rows = pl.ds(pl.multiple_of(n * cs, cs), cs)
        qc, kc, ktc = q_s[n], k_s[n], kt_s[n]
        for hh in range(2):
            decay, xi, zeta, cd, head = consts[hh]
            cols = slice(hh * RET_V_DIM, (hh + 1) * RET_V_DIM)
            qm = jnp.where(head, qc, jnp.zeros_like(qc))
            vc = v_ref[0, rows, cols]
            sc = lax.dot_general(qm, kc, (((1,), (1,)), ((), ())), preferred_element_type=F32) * decay
            o_s[rows, cols] = jnp.dot(sc.astype(BF16), vc, preferred_element_type=F32)
            kz = (ktc * zeta).astype(BF16)
            kv_s[n, hh] = jnp.dot(kz, vc, preferred_element_type=F32)
        return carry

    lax.fori_loop(0, n_chunks, inner, 0, unroll=4)

    def cross(n, state):
        rows = pl.ds(pl.multiple_of(n * cs, cs), cs)
        qc = q_s[n]
        new_state = []
        for hh in range(2):
            decay, xi, zeta, cd, head = consts[hh]
            cols = slice(hh * RET_V_DIM, (hh + 1) * RET_V_DIM)
            r_prev = state[hh]
            qm = jnp.where(head, qc, jnp.zeros_like(qc))
            o_s[rows, cols] += jnp.dot(qm, r_prev.astype(BF16), preferred_element_type=F32) * xi
            new_state.append(r_prev * cd + kv_s[n, hh])
        return tuple(new_state)

    zero = jnp.zeros((LANES, RET_V_DIM), F32)
    lax.fori_loop(0, n_chunks, cross, (zero, zero), unroll=4)

    def finish(n, carry):
        rows = pl.ds(pl.multiple_of(n * cs, cs), cs)
        for hh in range(2):
            cols = slice(hh * RET_V_DIM, (hh + 1) * RET_V_DIM)
            o = o_s[rows, cols]
            mu = jnp.mean(o, axis=-1, keepdims=True)
            dev = o - mu
            var = jnp.mean(dev * dev, axis=-1, keepdims=True)
            y = dev * lax.rsqrt(var + EPS) * gn_ref[0, hh:hh + 1, :]
            gate = g_ref[0, rows, cols].astype(F32)
            o_ref[0, rows, cols] = (gate * jax.nn.sigmoid(gate) * y).astype(o_ref.dtype)
        return carry

    lax.fori_loop(0, n_chunks, finish, 0, unroll=4)


def _retention(proj, tables, log_gamma, gn_gain):
    b, s, _ = proj.shape
    n_pairs = RET_HEADS // 2
    pair_w = 2 * RET_V_DIM
    qk_tiles = RET_HEADS * RET_QK_DIM // LANES
    v_off = 2 * qk_tiles * LANES // pair_w
    g_off = v_off + RET_HEADS * RET_V_DIM // pair_w
    tab = pl.BlockSpec((1, s, LANES), lambda bi, p: (bi, 0, 0), pipeline_mode=pl.Buffered(1))
    lg = jnp.broadcast_to(log_gamma.reshape(n_pairs, 2, 1), (n_pairs, 2, LANES))
    n_chunks = s // RET_CHUNK
    return pl.pallas_call(
        functools.partial(_retention_kernel, seq=s),
        grid=(b, n_pairs),
        in_specs=[
            pl.BlockSpec((1, s, LANES), lambda bi, p: (bi, 0, p)),
            pl.BlockSpec((1, s, LANES), lambda bi, p: (bi, 0, qk_tiles + p)),
            pl.BlockSpec((1, s, pair_w), lambda bi, p: (bi, 0, v_off + p)),
            pl.BlockSpec((1, s, pair_w), lambda bi, p: (bi, 0, g_off + p)),
            tab, tab,
            pl.BlockSpec((1, 2, LANES), lambda bi, p: (p, 0, 0)),
            pl.BlockSpec((1, 2, RET_V_DIM), lambda bi, p: (p, 0, 0)),
        ],
        out_specs=pl.BlockSpec((1, s, pair_w), lambda bi, p: (bi, 0, p)),
        out_shape=jax.ShapeDtypeStruct((b, s, RET_HEADS * RET_V_DIM), BF16),
        scratch_shapes=[
            pltpu.VMEM((n_chunks, RET_CHUNK, LANES), BF16),
            pltpu.VMEM((n_chunks, RET_CHUNK, LANES), BF16),
            pltpu.VMEM((n_chunks, LANES, RET_CHUNK), F32),
            pltpu.VMEM((s, pair_w), F32),
            pltpu.VMEM((n_chunks, 2, LANES, RET_V_DIM), F32),
        ],
        compiler_params=_cparams(2),
        name="retention",
    )(proj, proj, proj, proj, *tables, lg, gn_gain.reshape(n_pairs, 2, RET_V_DIM))


def _band_bias_t(lo_off, hi_off):
    c = lax.broadcasted_iota(jnp.int32, (2 * QBLK, QBLK), 0)
    a = lax.broadcasted_iota(jnp.int32, (2 * QBLK, QBLK), 1)
    band = (c - a >= lo_off) & (c - a <= hi_off)
    later = jnp.where(band, 0.0, NEG_BIG).astype(F32)
    first = jnp.where(band & (c >= QBLK), 0.0, NEG_BIG).astype(F32)
    return first, later


def _dilated_kernel(q_ref, k_ref, v_ref, c_ref, s_ref, qg_ref, kg_ref, o_ref,
                    qn_s, kn_s, vn_s, q4_s, k4_s, v4_s, qt_s, kc_s, vt_s, bias_s, p_s, m_s,
                    o0_s, o1_s, o2_s, l0_s, l1_s, l2_s, *, seq):
    swap = _swap_matrix(ROPE_DIMS // 2)
    head_ones = _head_ones()
    rq_s, rk_s = o0_s, l0_s

    def norms(n, carry):
        rows = pl.ds(pl.multiple_of(n * PRO_ROWS, PRO_ROWS), PRO_ROWS)
        rq_s[rows, :] = _head_rinv(q_ref[0, rows, :].astype(F32), head_ones)
        rk_s[rows, :] = _head_rinv(k_ref[0, rows, :].astype(F32), head_ones)
        return carry

    lax.fori_loop(0, seq // PRO_ROWS, norms, 0, unroll=4)

    def prologue(n, carry):
        rows = pl.ds(pl.multiple_of(n * PRO_ROWS, PRO_ROWS), PRO_ROWS)
        c, sn = c_ref[0, rows, :], s_ref[0, rows, :]
        q = q_ref[0, rows, :].astype(F32) * rq_s[rows, :] * qg_ref[...]
        qn_s[rows, :] = _rotate(q, c, sn, swap) * Q_SCALE
        k = k_ref[0, rows, :].astype(F32) * rk_s[rows, :] * kg_ref[...]
        kn_s[rows, :] = _rotate(k, c, sn, swap)
        vn_s[rows, :] = v_ref[0, rows, :].astype(F32)
        return carry

    lax.fori_loop(0, seq // PRO_ROWS, prologue, 0, unroll=4)

    first, later = _band_bias_t(0, QBLK)
    bias_s[0] = jnp.concatenate([first, first], axis=1)
    bias_s[1] = jnp.concatenate([later, later], axis=1)
    zero_pad = jnp.zeros((QBLK, LANES), BF16)
    top = lax.broadcasted_iota(jnp.int32, (LANES, QBLK), 0) < HEAD_DIM
    ones_rows = jnp.ones((DEN_ROWS, 2 * QBLK), BF16)

    outs = (o0_s, o1_s, o2_s)
    lses = (l0_s, l1_s, l2_s)
    n_flat = seq // QBLK
    for (window, r), on_s, ln_s in zip(DIL_PATTERNS, outs, lses):
        assert window // r == QBLK
        n_blk = seq // r // QBLK
        blk_shift = n_blk.bit_length() - 1
        assert n_blk == 1 << blk_shift

        def pad(j, c2, n_blk=n_blk):
            z = j * (n_blk + 1)
            kc_s[pl.ds(pl.multiple_of(z * QBLK, QBLK), QBLK), :] = zero_pad
            vt_s[z] = zero_pad
            return c2

        lax.fori_loop(0, r, pad, 0)

        def split(idx, r=r, n_blk=n_blk, blk_shift=blk_shift):
            j = lax.shift_right_logical(idx, blk_shift)
            i = idx & (n_blk - 1)
            return j, i

        def gather(idx, c2, r=r, split=split):
            j, i = split(idx)
            if r == FINE_R:
                base = ((j & (COARSE_R - 1)) * (seq // COARSE_R)
                        + lax.shift_right_logical(j, COARSE_R.bit_length() - 1))
                src = pl.ds(base + (QBLK * r // COARSE_R) * i, QBLK, stride=r // COARSE_R)
                q, k, v = q4_s[src, :], k4_s[src, :], v4_s[src, :]
            else:
                src = pl.ds(j + r * QBLK * i, QBLK, stride=r)
                q, k, v = qn_s[src, :], kn_s[src, :], vn_s[src, :]
            if r == COARSE_R:
                dense = pl.ds(pl.multiple_of(idx * QBLK, QBLK), QBLK)
                q4_s[dense, :] = q
                k4_s[dense, :] = k
                v4_s[dense, :] = v
            k0 = pl.multiple_of((idx + j + 1) * QBLK, QBLK)
            qt_s[idx] = q.T.astype(BF16)
            kc_s[pl.ds(k0, QBLK), :] = k.astype(BF16)
            vt_s[idx + j + 1] = v.T.astype(BF16)
            return c2

        lax.fori_loop(0, n_flat, gather, 0, unroll=8)

        def scores(idx, c2, split=split):
            j, i = split(idx)
            k0 = pl.multiple_of((idx + j) * QBLK, QBLK)
            qt = qt_s[idx]
            kw = kc_s[pl.ds(k0, 2 * QBLK), :]
            zero = jnp.zeros_like(qt)
            rhs = jnp.concatenate([jnp.where(top, qt, zero), jnp.where(top, zero, qt)], axis=1)
            sc = jnp.dot(kw, rhs, preferred_element_type=F32) + bias_s[jnp.minimum(i, 1)]
            m = jnp.max(sc, axis=0, keepdims=True)
            p_s[idx] = jnp.exp2(sc - m).astype(BF16)
            m_s[idx] = jnp.broadcast_to(m, (SUBLANES, 2 * QBLK))
            return c2

        lax.fori_loop(0, n_flat, scores, 0, unroll=16)

        def block(idx, c2, r=r, split=split, on_s=on_s, ln_s=ln_s):
            j, i = split(idx)
            p = p_s[idx]
            m = m_s[idx][0:1, :]
            lhs = jnp.concatenate(
                [jnp.concatenate([vt_s[idx + j], vt_s[idx + j + 1]], axis=1), ones_rows], axis=0)
            ot = jnp.dot(lhs, p, preferred_element_type=F32)
            den = ot[LANES:LANES + 1, :]
            inv = 1.0 / den
            lse = m + jnp.log2(den)
            o_t = jnp.concatenate(
                [ot[0:HEAD_DIM, 0:QBLK] * inv[:, 0:QBLK],
                 ot[HEAD_DIM:LANES, QBLK:2 * QBLK] * inv[:, QBLK:2 * QBLK]], axis=0)
            l_t = jnp.concatenate(
                [jnp.broadcast_to(lse[:, 0:QBLK], (HEAD_DIM, QBLK)),
                 jnp.broadcast_to(lse[:, QBLK:2 * QBLK], (HEAD_DIM, QBLK))], axis=0)
            dst = pl.ds(j + r * QBLK * i, QBLK, stride=r)
            on_s[dst, :] = o_t.T
            ln_s[dst, :] = l_t.T
            return c2

        lax.fori_loop(0, n_flat, block, 0, unroll=8)

    def combine(n, carry):
        r0 = pl.multiple_of(n * PRO_ROWS, PRO_ROWS)
        rows = pl.ds(r0, PRO_ROWS)
        ls = [l_s[rows, :] for l_s in lses]
        m = jnp.maximum(jnp.maximum(ls[0], ls[1]), ls[2])
        es = [jnp.exp2(l - m) for l in ls]
        num = es[0] * o0_s[rows, :] + es[1] * o1_s[rows, :] + es[2] * o2_s[rows, :]
        o_ref[0, rows, :] = (num / (es[0] + es[1] + es[2])).astype(o_ref.dtype)
        return carry

    lax.fori_loop(0, seq // PRO_ROWS, combine, 0)


def _dilated(proj, tables, q_gain, k_gain, col0):
    b, s, _ = proj.shape
    n_pairs = DIL_HEADS * HEAD_DIM // LANES
    t0 = col0 // LANES
    tab = pl.BlockSpec((1, s, LANES), lambda bi, p: (bi, 0, 0), pipeline_mode=pl.Buffered(1))
    two = lambda g: jnp.concatenate([g, g]).reshape(1, LANES)
    nat = pltpu.VMEM((s, LANES), F32)
    max_r = max(r for _, r in DIL_PATTERNS)
    return pl.pallas_call(
        functools.partial(_dilated_kernel, seq=s),
        grid=(b, n_pairs),
        in_specs=[
            pl.BlockSpec((1, s, LANES), lambda bi, p: (bi, 0, t0 + p)),
            pl.BlockSpec((1, s, LANES), lambda bi, p: (bi, 0, t0 + n_pairs + p)),
            pl.BlockSpec((1, s, LANES), lambda bi, p: (bi, 0, t0 + 2 * n_pairs + p)),
            tab, tab,
            _resident((1, LANES)),
            _resident((1, LANES)),
        ],
        out_specs=pl.BlockSpec((1, s, LANES), lambda bi, p: (bi, 0, p)),
        out_shape=jax.ShapeDtypeStruct((b, s, DIL_HEADS * HEAD_DIM), BF16),
        scratch_shapes=[
            nat, nat, nat, nat, nat, nat,
            pltpu.VMEM((s // QBLK, LANES, QBLK), BF16),
            pltpu.VMEM((s + max_r * QBLK, LANES), BF16),
            pltpu.VMEM((s // QBLK + max_r, LANES, QBLK), BF16),
            pltpu.VMEM((2, 2 * QBLK, 2 * QBLK), F32),
            pltpu.VMEM((s // QBLK, 2 * QBLK, 2 * QBLK), BF16),
            pltpu.VMEM((s // QBLK, SUBLANES, 2 * QBLK), F32),
            nat, nat, nat, nat, nat, nat,
        ],
        compiler_params=_cparams(2),
        name="dilated_attention",
    )(proj, proj, proj, *tables, two(q_gain), two(k_gain))


def _swa_kernel(q_ref, k_ref, v_ref, c_ref, s_ref, qg_ref, kg_ref, sink_ref, o_ref,
                q_s, kd_s, vt_s, bias_s, p_s, m_s, rq_s, rk_s, *, seq):
    group = SWA_Q_HEADS // SWA_KV_HEADS
    swap = _swap_matrix(ROPE_DIMS // 2)
    head_ones = _head_ones()
    kv_in_hi = (pl.program_id(1) % 2) == 1
    blk_per_step = PRO_ROWS // QBLK
    src = lax.broadcasted_iota(jnp.int32, (LANES, LANES), 0)
    dst = lax.broadcasted_iota(jnp.int32, (LANES, LANES), 1) & (HEAD_DIM - 1)
    dup = jnp.where(src == dst + jnp.where(kv_in_hi, HEAD_DIM, 0), 1.0, 0.0).astype(BF16)

    def norms(n, carry):
        rows = pl.ds(pl.multiple_of(n * PRO_ROWS, PRO_ROWS), PRO_ROWS)
        for t in range(group // 2):
            cols = slice(t * LANES, (t + 1) * LANES)
            rq_s[rows, cols] = _head_rinv(q_ref[0, rows, cols].astype(F32), head_ones)
        rk_s[rows, :] = _head_rinv(k_ref[0, rows, :].astype(F32), head_ones)
        return carry

    lax.fori_loop(0, seq // PRO_ROWS, norms, 0, unroll=4)

    def prologue(n, carry):
        r0 = pl.multiple_of(n * PRO_ROWS, PRO_ROWS)
        rows = pl.ds(r0, PRO_ROWS)
        c, sn = c_ref[0, rows, :], s_ref[0, rows, :]
        for t in range(group // 2):
            cols = slice(t * LANES, (t + 1) * LANES)
            q = q_ref[0, rows, cols].astype(F32) * rq_s[rows, cols] * qg_ref[...]
            q_s[rows, cols] = (_rotate(q, c, sn, swap) * Q_SCALE).astype(BF16)
        k = k_ref[0, rows, :].astype(F32) * rk_s[rows, :] * kg_ref[...]
        k = _rotate(k, c, sn, swap).astype(BF16)
        kd_s[pl.ds(r0 + QBLK, PRO_ROWS), :] = jnp.dot(k, dup, preferred_element_type=F32).astype(BF16)
        v_t = v_ref[0, rows, :].astype(F32).T
        v_t = jnp.where(kv_in_hi, v_t[HEAD_DIM:LANES, :], v_t[0:HEAD_DIM, :]).astype(BF16)
        for u in range(blk_per_step):
            vt_s[n * blk_per_step + u + 1] = v_t[:, u * QBLK:(u + 1) * QBLK]
        return carry

    lax.fori_loop(0, seq // PRO_ROWS, prologue, 0, unroll=4)

    first, later = _band_bias_t(1, QBLK)
    bias_s[0] = jnp.concatenate([first] * group, axis=1)
    bias_s[1] = jnp.concatenate([later] * group, axis=1)
    kd_s[0:QBLK, :] = jnp.zeros((QBLK, LANES), BF16)
    vt_s[0] = jnp.zeros((HEAD_DIM, QBLK), BF16)
    lo = _lo_mask((QBLK, LANES))
    heads = (lo, jnp.logical_not(lo))
    sink = jnp.concatenate([sink_ref[0, hh:hh + 1, :] for hh in range(group)], axis=1) * LOG2_E
    ones_rows = jnp.ones((DEN_ROWS, 2 * QBLK), BF16)

    def scores(i, carry):
        d0 = pl.multiple_of(i * QBLK, QBLK)
        rows = pl.ds(d0, QBLK)
        kw = kd_s[pl.ds(d0, 2 * QBLK), :]
        stack = []
        for t in range(group // 2):
            qt = q_s[rows, t * LANES:(t + 1) * LANES]
            for head in heads:
                stack.append(jnp.where(head, qt, jnp.zeros_like(qt)))
        qm = jnp.concatenate(stack, axis=0)
        sc = lax.dot_general(kw, qm, (((1,), (1,)), ((), ())), preferred_element_type=F32)
        sc = sc + bias_s[jnp.minimum(i, 1)]
        m = jnp.maximum(jnp.max(sc, axis=0, keepdims=True), sink)
        p_s[i] = jnp.exp2(sc - m).astype(BF16)
        m_s[i] = jnp.broadcast_to(m, (SUBLANES, group * QBLK))
        return carry

    lax.fori_loop(0, seq // QBLK, scores, 0, unroll=BLOCK_UNROLL)

    def values(i, carry):
        rows = pl.ds(pl.multiple_of(i * QBLK, QBLK), QBLK)
        m = m_s[i][0:1, :]
        lhs = jnp.concatenate(
            [jnp.concatenate([vt_s[i], vt_s[i + 1]], axis=1), ones_rows], axis=0)
        ot = jnp.dot(lhs, p_s[i], preferred_element_type=F32)
        inv = 1.0 / (ot[HEAD_DIM:HEAD_DIM + 1, :] + jnp.exp2(sink - m))
        on = ot[0:HEAD_DIM, :] * inv
        for t in range(group // 2):
            pair_t = jnp.concatenate(
                [on[:, 2 * t * QBLK:(2 * t + 1) * QBLK], on[:, (2 * t + 1) * QBLK:(2 * t + 2) * QBLK]],
                axis=0)
            o_ref[0, rows, t * LANES:(t + 1) * LANES] = pair_t.T.astype(o_ref.dtype)
        return carry

    lax.fori_loop(0, seq // QBLK, values, 0, unroll=BLOCK_UNROLL)


def _swa(proj, tables, q_gain, k_gain, sinks):
    b, s, _ = proj.shape
    group = SWA_Q_HEADS // SWA_KV_HEADS
    q_w = group * HEAD_DIM
    k_t0 = SWA_Q_HEADS * HEAD_DIM // LANES
    v_t0 = k_t0 + SWA_KV_HEADS * HEAD_DIM // LANES
    tab = pl.BlockSpec((1, s, LANES), lambda bi, g: (bi, 0, 0), pipeline_mode=pl.Buffered(1))
    two = lambda g: jnp.concatenate([g, g]).reshape(1, LANES)
    sink_rows = jnp.broadcast_to(sinks.reshape(SWA_KV_HEADS, group, 1), (SWA_KV_HEADS, group, LANES))
    return pl.pallas_call(
        functools.partial(_swa_kernel, seq=s),
        grid=(b, SWA_KV_HEADS),
        in_specs=[
            pl.BlockSpec((1, s, q_w), lambda bi, g: (bi, 0, g)),
            pl.BlockSpec((1, s, LANES), lambda bi, g: (bi, 0, k_t0 + g // 2)),
            pl.BlockSpec((1, s, LANES), lambda bi, g: (bi, 0, v_t0 + g // 2)),
            tab, tab,
            _resident((1, LANES)),
            _resident((1, LANES)),
            pl.BlockSpec((1, group, LANES), lambda bi, g: (g, 0, 0)),
        ],
        out_specs=pl.BlockSpec((1, s, q_w), lambda bi, g: (bi, 0, g)),
        out_shape=jax.ShapeDtypeStruct((b, s, SWA_Q_HEADS * HEAD_DIM), BF16),
        scratch_shapes=[
            pltpu.VMEM((s, q_w), BF16),
            pltpu.VMEM((s + QBLK, LANES), BF16),
            pltpu.VMEM((s // QBLK + 1, HEAD_DIM, QBLK), BF16),
            pltpu.VMEM((2, 2 * QBLK, group * QBLK), F32),
            pltpu.VMEM((s // QBLK, 2 * QBLK, group * QBLK), BF16),
            pltpu.VMEM((s // QBLK, SUBLANES, group * QBLK), F32),
            pltpu.VMEM((s, q_w), F32),
            pltpu.VMEM((s, LANES), F32),
        ],
        compiler_params=_cparams(2),
        name="swa_attention",
    )(proj, proj, proj, *tables, two(q_gain), two(k_gain), sink_rows)


def kernel(x, positions, norm_mix, norm_mlp, mlp_w_up, mlp_w_down, hyb_w_in, hyb_w_out, ret_gn_gain, dil_q_gain, dil_k_gain, swa_w_qkv, swa_b_qkv, swa_w_out, swa_q_gain, swa_k_gain, swa_sinks):
    b, s, d = x.shape
    depth = norm_mix.shape[0]
    ret_tab = _rope_table(positions, RET_QK_DIM // 2, RET_THETA, RET_QK_DIM)
    rope_tab = _rope_table(positions, ROPE_DIMS // 2, ROPE_THETA, ROPE_DIMS)
    log_gamma = jnp.log1p(-jnp.exp2(-5.0 - jnp.arange(RET_HEADS, dtype=F32)))
    ret_w = RET_HEADS * RET_V_DIM
    dil_col0 = 2 * RET_HEADS * RET_QK_DIM + 2 * ret_w

    x2d = x.reshape(b * s, d)
    for layer in range(depth):
        i = layer // 2
        if layer % 2 == 0:
            w_in = hyb_w_in[i].astype(BF16)
            proj = _norm_proj(x2d, norm_mix[layer], w_in, jnp.zeros((w_in.shape[1],), F32))
            proj = proj.reshape(b, s, -1)
            ra = _retention(proj, ret_tab, log_gamma, ret_gn_gain[i])
            da = _dilated(proj, rope_tab, dil_q_gain[i], dil_k_gain[i], dil_col0)
            w_out = hyb_w_out[i].astype(BF16)
            mixed = [ra.reshape(b * s, -1), da.reshape(b * s, -1)]
        else:
            proj = _norm_proj(x2d, norm_mix[layer], swa_w_qkv[i].astype(BF16), swa_b_qkv[i])
            proj = proj.reshape(b, s, -1)
            att = _swa(proj, rope_tab, swa_q_gain[i], swa_k_gain[i], swa_sinks[i])
            mixed = [att.reshape(b * s, -1)]
            w_out = swa_w_out[i].astype(BF16)
        x2d = _out_mlp(mixed, x2d, w_out, norm_mlp[layer],
                       mlp_w_up[layer].astype(BF16), mlp_w_down[layer].astype(BF16))
    return x2d.reshape(b, s, d)
```

```python
import functools

import jax
import jax.numpy as jnp
import numpy as np
from jax import lax
from jax.experimental import pallas as pl
from jax.experimental.pallas import tpu as pltpu

F32 = jnp.float32
BF16 = jnp.bfloat16

D_MODEL = 1024
D_FF = 4 * D_MODEL
HEAD_DIM = 64
EPS = 1e-6
RET_HEADS = 4
RET_QK_DIM = 64
RET_V_DIM = 128
RET_CHUNK = 128
RET_THETA = 10000.0
DIL_HEADS = 8
DIL_PATTERNS = ((128, 1), (512, 4), (2048, 16))
SWA_Q_HEADS = 16
SWA_KV_HEADS = 4
SWA_WINDOW = 128
ROPE_THETA = 500000.0
ROPE_DIMS = HEAD_DIM // 4

LANES = 128
SUBLANES = 8
QBLK = 128
VMEM_LIMIT_BYTES = 56 * 1024 * 1024
NEG_BIG = -1e30
LOG2_E = 1.4426950408889634
Q_SCALE = HEAD_DIM ** -0.5 * LOG2_E

ROW_TILE = 512
FF_CHUNK = 1024
PRO_ROWS = 128
BLOCK_UNROLL = 8
DEN_ROWS = 16
COARSE_R, FINE_R = DIL_PATTERNS[1][1], DIL_PATTERNS[2][1]


def _cparams(n_axes):
    return pltpu.CompilerParams(
        dimension_semantics=("arbitrary",) * n_axes,
        vmem_limit_bytes=VMEM_LIMIT_BYTES,
    )


def _resident(shape):
    nd = len(shape)
    return pl.BlockSpec(shape, lambda *_: (0,) * nd, pipeline_mode=pl.Buffered(1))


def _split3(x):
    hi = x.astype(BF16)
    r1 = x - hi.astype(F32)
    mid = r1.astype(BF16)
    lo = (r1 - mid.astype(F32)).astype(BF16)
    return hi, mid, lo


def _trig_kernel(p_ref, inv_ref, ec_ref, es_ref, base_ref, c_ref, s_ref, *, n_pos):
    ang = p_ref[...] * inv_ref[...]
    cos_parts = _split3(jnp.cos(ang))
    sin_parts = _split3(jnp.sin(ang))
    rows = p_ref.shape[0]
    for c in range(n_pos):
        dst = pl.ds(c, rows, stride=n_pos)
        c_ref[dst, :] = base_ref[...] + sum(
            jnp.dot(part, ec_ref[c], preferred_element_type=F32) for part in cos_parts)
        s_ref[dst, :] = sum(
            jnp.dot(part, es_ref[c], preferred_element_type=F32) for part in sin_parts)


def _selectors(half):
    n_pos = LANES // half
    ec = np.zeros((n_pos, LANES, LANES), np.float32)
    es = np.zeros((n_pos, LANES, LANES), np.float32)
    base = np.zeros((1, LANES), np.float32)
    for j in range(LANES):
        d = j % HEAD_DIM
        if d >= 2 * half:
            base[0, j] = 1.0
            continue
        f = d % half
        for c in range(n_pos):
            ec[c, c * half + f, j] = 1.0
            es[c, c * half + f, j] = -1.0 if d < half else 1.0
    return n_pos, jnp.asarray(ec, BF16), jnp.asarray(es, BF16), jnp.asarray(base)


def _rope_table(positions, half, theta, n_rot):
    b, s = positions.shape
    posf = positions.astype(F32)
    inv = jnp.power(jnp.float32(theta), -jnp.arange(half, dtype=F32) * (2.0 / n_rot))
    n_pos, ec, es, base = _selectors(half)
    p = jnp.broadcast_to(posf[..., None], (b, s, half)).reshape(-1, LANES)
    inv_rows = jnp.broadcast_to(jnp.tile(inv, n_pos)[None, :], p.shape)
    rows = p.shape[0]
    tile = min(rows, 2048 // n_pos)
    assert rows % tile == 0
    dense = pl.BlockSpec((tile, LANES), lambda i: (i, 0))
    wide = pl.BlockSpec((tile * n_pos, LANES), lambda i: (i, 0))
    c_tab, s_tab = pl.pallas_call(
        functools.partial(_trig_kernel, n_pos=n_pos),
        grid=(rows // tile,),
        in_specs=[dense, dense, _resident(ec.shape), _resident(es.shape), _resident(base.shape)],
        out_specs=[wide, wide],
        out_shape=[jax.ShapeDtypeStruct((rows * n_pos, LANES), F32)] * 2,
        compiler_params=_cparams(1),
        name="trig_tables",
    )(p, inv_rows, ec, es, base)
    return c_tab.reshape(b, s, LANES), s_tab.reshape(b, s, LANES)


def _swap_matrix(half):
    src = lax.broadcasted_iota(jnp.int32, (LANES, LANES), 0)
    dst = lax.broadcasted_iota(jnp.int32, (LANES, LANES), 1)
    d = dst & (HEAD_DIM - 1)
    want = jnp.where(d < half, dst + half, jnp.where(d < 2 * half, dst - half, -1))
    return jnp.where(src == want, 1.0, 0.0).astype(BF16)


def _rotate(x, c, s, swap):
    swapped = jnp.dot(x.astype(BF16), swap, preferred_element_type=F32)
    return x * c + swapped * s


def _lo_mask(shape):
    return lax.broadcasted_iota(jnp.int32, shape, len(shape) - 1) < HEAD_DIM


def _head_ones():
    r = lax.broadcasted_iota(jnp.int32, (LANES, LANES), 0) < HEAD_DIM
    c = lax.broadcasted_iota(jnp.int32, (LANES, LANES), 1) < HEAD_DIM
    return jnp.where(r == c, 1.0, 0.0).astype(BF16)


def _head_rinv(x, head_ones):
    ss = jnp.dot((x * x).astype(BF16), head_ones, preferred_element_type=F32)
    return lax.rsqrt(ss * (1.0 / HEAD_DIM) + EPS)


def _norm_proj_kernel(x_ref, g_ref, w_ref, b_ref, o_ref, *, n_chunk):
    x = x_ref[...]
    ms = jnp.mean(x * x, axis=-1, keepdims=True)
    h = (x * lax.rsqrt(ms + EPS) * g_ref[...]).astype(BF16)
    n = o_ref.shape[-1]
    for c in range(0, n, n_chunk):
        acc = jnp.dot(h, w_ref[:, c:c + n_chunk], preferred_element_type=F32)
        o_ref[:, c:c + n_chunk] = (acc + b_ref[:, c:c + n_chunk]).astype(o_ref.dtype)


def _norm_proj(x2d, gain, w_bf16, bias):
    m, d = x2d.shape
    n = w_bf16.shape[1]
    return pl.pallas_call(
        functools.partial(_norm_proj_kernel, n_chunk=512),
        grid=(m // ROW_TILE,),
        in_specs=[
            pl.BlockSpec((ROW_TILE, d), lambda i: (i, 0)),
            _resident((1, d)),
            _resident((d, n)),
            _resident((1, n)),
        ],
        out_specs=pl.BlockSpec((ROW_TILE, n), lambda i: (i, 0)),
        out_shape=jax.ShapeDtypeStruct((m, n), BF16),
        compiler_params=_cparams(1),
        name="norm_proj",
    )(x2d, gain.reshape(1, d), w_bf16, bias.reshape(1, n))


def _out_mlp_kernel(*refs, n_mix):
    a_refs = refs[:n_mix]
    x_ref, wo_ref, g_ref, wup_ref, wdn_ref, o_ref = refs[n_mix:]
    mixed = a_refs[0][...] if n_mix == 1 else jnp.concatenate([a[...] for a in a_refs], axis=-1)
    x1 = x_ref[...] + jnp.dot(mixed, wo_ref[...], preferred_element_type=F32)
    ms = jnp.mean(x1 * x1, axis=-1, keepdims=True)
    h = (x1 * lax.rsqrt(ms + EPS) * g_ref[...]).astype(BF16)
    y = x1
    for c in range(0, D_FF, FF_CHUNK):
        u = jnp.dot(h, wup_ref[:, c:c + FF_CHUNK], preferred_element_type=F32)
        u = jnp.square(jnp.maximum(u, 0.0)).astype(BF16)
        y = y + jnp.dot(u, wdn_ref[c:c + FF_CHUNK, :], preferred_element_type=F32)
    o_ref[...] = y


def _out_mlp(mixed, x2d, w_out, gain, w_up, w_down):
    m, d = x2d.shape
    n_mix = len(mixed)
    in_specs = [pl.BlockSpec((ROW_TILE, a.shape[1]), lambda i: (i, 0)) for a in mixed]
    in_specs.append(pl.BlockSpec((ROW_TILE, d), lambda i: (i, 0)))
    in_specs += [_resident(w_out.shape), _resident((1, d)), _resident(w_up.shape), _resident(w_down.shape)]
    return pl.pallas_call(
        functools.partial(_out_mlp_kernel, n_mix=n_mix),
        grid=(m // ROW_TILE,),
        in_specs=in_specs,
        out_specs=pl.BlockSpec((ROW_TILE, d), lambda i: (i, 0)),
        out_shape=jax.ShapeDtypeStruct((m, d), F32),
        compiler_params=_cparams(1),
        name="out_mlp",
    )(*mixed, x2d, w_out, gain.reshape(1, d), w_up, w_down)


def _retention_kernel(q_ref, k_ref, v_ref, g_ref, c_ref, s_ref, lg_ref, gn_ref,
                      o_ref, q_s, k_s, kt_s, o_s, kv_s, *, seq):
    n_chunks = seq // RET_CHUNK
    swap = _swap_matrix(RET_QK_DIM // 2)
    cs = RET_CHUNK

    def prologue(n, carry):
        r0 = pl.multiple_of(n * cs, cs)
        rows = pl.ds(r0, cs)
        c, sn = c_ref[0, rows, :], s_ref[0, rows, :]
        q = _rotate(q_ref[0, rows, :].astype(F32), c, sn, swap)
        k = _rotate(k_ref[0, rows, :].astype(F32), c, sn, swap) * (RET_QK_DIM ** -0.5)
        q_s[n] = q.astype(BF16)
        k_s[n] = k.astype(BF16)
        kt_s[n] = k.T
        return carry

    lax.fori_loop(0, n_chunks, prologue, 0, unroll=8)

    lo = _lo_mask((cs, LANES))
    row = lax.broadcasted_iota(jnp.int32, (cs, cs), 0).astype(F32)
    col = lax.broadcasted_iota(jnp.int32, (cs, cs), 1).astype(F32)
    diff = row - col
    consts = []
    for hh in range(2):
        lg = lg_ref[0, hh:hh + 1, :]
        decay = jnp.where(diff >= 0, jnp.exp(lg * jnp.maximum(diff, 0.0)), 0.0)
        xi = jnp.exp(lg * (row + 1.0))
        zeta = jnp.exp(lg * (cs - 1.0 - col[0:1, :]))
        cd = jnp.exp(lg * float(cs))
        head = lo if hh == 0 else jnp.logical_not(lo)
        consts.append((decay, xi, zeta, cd, head))

    def inner(n, carry):
        rows = pl.ds(pl.multiple_of(n * cs, cs), cs)
        qc, kc, ktc = q_s[n], k_s[n], kt_s[n]
        for hh in range(2):
            decay, xi, zeta, cd, head = consts[hh]
            cols = slice(hh * RET_V_DIM, (hh + 1) * RET_V_DIM)
            qm = jnp.where(head, qc, jnp.zeros_like(qc))
            vc = v_ref[0, rows, cols]
            sc = lax.dot_general(qm, kc, (((1,), (1,)), ((), ())), preferred_element_type=F32) * decay
            o_s[rows, cols] = jnp.dot(sc.astype(BF16), vc, preferred_element_type=F32)
            kz = (ktc * zeta).astype(BF16)
            kv_s[n, hh] = jnp.dot(kz, vc, preferred_element_type=F32)
        return carry

    lax.fori_loop(0, n_chunks, inner, 0, unroll=16)

    def cross(n, state):
        rows = pl.ds(pl.multiple_of(n * cs, cs), cs)
        qc = q_s[n]
        new_state = []
        for hh in range(2):
            decay, xi, zeta, cd, head = consts[hh]
            cols = slice(hh * RET_V_DIM, (hh + 1) * RET_V_DIM)
            r_prev = state[hh]
            qm = jnp.where(head, qc, jnp.zeros_like(qc))
            o_s[rows, cols] += jnp.dot(qm, r_prev.astype(BF16), preferred_element_type=F32) * xi
            new_state.append(r_prev * cd + kv_s[n, hh])
        return tuple(new_state)

    zero = jnp.zeros((LANES, RET_V_DIM), F32)
    lax.fori_loop(0, n_chunks, cross, (zero, zero), unroll=8)

    def finish(n, carry):
        rows = pl.ds(pl.multiple_of(n * cs, cs), cs)
        for hh in range(2):
            cols = slice(hh * RET_V_DIM, (hh + 1) * RET_V_DIM)
            o = o_s[rows, cols]
            mu = jnp.mean(o, axis=-1, keepdims=True)
            dev = o - mu
            var = jnp.mean(dev * dev, axis=-1, keepdims=True)
            y = dev * lax.rsqrt(var + EPS) * gn_ref[0, hh:hh + 1, :]
            gate = g_ref[0, rows, cols].astype(F32)
            o_ref[0, rows, cols] = (gate * jax.nn.sigmoid(gate) * y).astype(o_ref.dtype)
        return carry

    lax.fori_loop(0, n_chunks, finish, 0, unroll=2)


def _retention(proj, tables, log_gamma, gn_gain):
    b, s, _ = proj.shape
    n_pairs = RET_HEADS // 2
    pair_w = 2 * RET_V_DIM
    qk_tiles = RET_HEADS * RET_QK_DIM // LANES
    v_off = 2 * qk_tiles * LANES // pair_w
    g_off = v_off + RET_HEADS * RET_V_DIM // pair_w
    tab = pl.BlockSpec((1, s, LANES), lambda bi, p: (bi, 0, 0), pipeline_mode=pl.Buffered(1))
    lg = jnp.broadcast_to(log_gamma.reshape(n_pairs, 2, 1), (n_pairs, 2, LANES))
    n_chunks = s // RET_CHUNK
    return pl.pallas_call(
        functools.partial(_retention_kernel, seq=s),
        grid=(b, n_pairs),
        in_specs=[
            pl.BlockSpec((1, s, LANES), lambda bi, p: (bi, 0, p)),
            pl.BlockSpec((1, s, LANES), lambda bi, p: (bi, 0, qk_tiles + p)),
            pl.BlockSpec((1, s, pair_w), lambda bi, p: (bi, 0, v_off + p)),
            pl.BlockSpec((1, s, pair_w), lambda bi, p: (bi, 0, g_off + p)),
            tab, tab,
            pl.BlockSpec((1, 2, LANES), lambda bi, p: (p, 0, 0)),
            pl.BlockSpec((1, 2, RET_V_DIM), lambda bi, p: (p, 0, 0)),
        ],
        out_specs=pl.BlockSpec((1, s, pair_w), lambda bi, p: (bi, 0, p)),
        out_shape=jax.ShapeDtypeStruct((b, s, RET_HEADS * RET_V_DIM), BF16),
        scratch_shapes=[
            pltpu.VMEM((n_chunks, RET_CHUNK, LANES), BF16),
            pltpu.VMEM((n_chunks, RET_CHUNK, LANES), BF16),
            pltpu.VMEM((n_chunks, LANES, RET_CHUNK), F32),
            pltpu.VMEM((s, pair_w), F32),
            pltpu.VMEM((n_chunks, 2, LANES, RET_V_DIM), F32),
        ],
        compiler_params=_cparams(2),
        name="retention",
    )(proj, proj, proj, proj, *tables, lg, gn_gain.reshape(n_pairs, 2, RET_V_DIM))


def _band_bias_t(lo_off, hi_off):
    c = lax.broadcasted_iota(jnp.int32, (2 * QBLK, QBLK), 0)
    a = lax.broadcasted_iota(jnp.int32, (2 * QBLK, QBLK), 1)
    band = (c - a >= lo_off) & (c - a <= hi_off)
    later = jnp.where(band, 0.0, NEG_BIG).astype(F32)
    first = jnp.where(band & (c >= QBLK), 0.0, NEG_BIG).astype(F32)
    return first, later


def _dilated_kernel(q_ref, k_ref, v_ref, c_ref, s_ref, qg_ref, kg_ref, o_ref,
                    qn_s, kn_s, vn_s, q4_s, k4_s, v4_s, qt_s, kc_s, vt_s, bias_s, p_s, m_s,
                    o0_s, o1_s, o2_s, l0_s, l1_s, l2_s, *, seq):
    swap = _swap_matrix(ROPE_DIMS // 2)
    head_ones = _head_ones()
    rq_s, rk_s = o0_s, l0_s

    def norms(n, carry):
        rows = pl.ds(pl.multiple_of(n * PRO_ROWS, PRO_ROWS), PRO_ROWS)
        rq_s[rows, :] = _head_rinv(q_ref[0, rows, :].astype(F32), head_ones)
        rk_s[rows, :] = _head_rinv(k_ref[0, rows, :].astype(F32), head_ones)
        return carry

    lax.fori_loop(0, seq // PRO_ROWS, norms, 0, unroll=16)

    def prologue(n, carry):
        rows = pl.ds(pl.multiple_of(n * PRO_ROWS, PRO_ROWS), PRO_ROWS)
        c, sn = c_ref[0, rows, :], s_ref[0, rows, :]
        q = q_ref[0, rows, :].astype(F32) * rq_s[rows, :] * qg_ref[...]
        qn_s[rows, :] = _rotate(q, c, sn, swap) * Q_SCALE
        k = k_ref[0, rows, :].astype(F32) * rk_s[rows, :] * kg_ref[...]
        kn_s[rows, :] = _rotate(k, c, sn, swap)
        vn_s[rows, :] = v_ref[0, rows, :].astype(F32)
        return carry

    lax.fori_loop(0, seq // PRO_ROWS, prologue, 0, unroll=8)

    first, later = _band_bias_t(0, QBLK)
    bias_s[0] = jnp.concatenate([first, first], axis=1)
    bias_s[1] = jnp.concatenate([later, later], axis=1)
    zero_pad = jnp.zeros((QBLK, LANES), BF16)
    top = lax.broadcasted_iota(jnp.int32, (LANES, QBLK), 0) < HEAD_DIM
    ones_rows = jnp.ones((DEN_ROWS, 2 * QBLK), BF16)

    outs = (o0_s, o1_s, o2_s)
    lses = (l0_s, l1_s, l2_s)
    n_flat = seq // QBLK
    for (window, r), on_s, ln_s in zip(DIL_PATTERNS, outs, lses):
        assert window // r == QBLK
        n_blk = seq // r // QBLK
        blk_shift = n_blk.bit_length() - 1
        assert n_blk == 1 << blk_shift

        def pad(j, c2, n_blk=n_blk):
            z = j * (n_blk + 1)
            kc_s[pl.ds(pl.multiple_of(z * QBLK, QBLK), QBLK), :] = zero_pad
            vt_s[z] = zero_pad
            return c2

        lax.fori_loop(0, r, pad, 0)

        def split(idx, r=r, n_blk=n_blk, blk_shift=blk_shift):
            j = lax.shift_right_logical(idx, blk_shift)
            i = idx & (n_blk - 1)
            return j, i

        def gather(idx, c2, r=r, split=split):
            j, i = split(idx)
            if r == FINE_R:
                base = ((j & (COARSE_R - 1)) * (seq // COARSE_R)
                        + lax.shift_right_logical(j, COARSE_R.bit_length() - 1))
                src = pl.ds(base + (QBLK * r // COARSE_R) * i, QBLK, stride=r // COARSE_R)
                q, k, v = q4_s[src, :], k4_s[src, :], v4_s[src, :]
            else:
                src = pl.ds(j + r * QBLK * i, QBLK, stride=r)
                q, k, v = qn_s[src, :], kn_s[src, :], vn_s[src, :]
            if r == COARSE_R:
                dense = pl.ds(pl.multiple_of(idx * QBLK, QBLK), QBLK)
                q4_s[dense, :] = q
                k4_s[dense, :] = k
                v4_s[dense, :] = v
            k0 = pl.multiple_of((idx + j + 1) * QBLK, QBLK)
            qt_s[idx] = q.T.astype(BF16)
            kc_s[pl.ds(k0, QBLK), :] = k.astype(BF16)
            vt_s[idx + j + 1] = v.T.astype(BF16)
            return c2

        lax.fori_loop(0, n_flat, gather, 0, unroll=8)

        def scores(idx, c2, split=split):
            j, i = split(idx)
            k0 = pl.multiple_of((idx + j) * QBLK, QBLK)
            qt = qt_s[idx]
            kw = kc_s[pl.ds(k0, 2 * QBLK), :]
            zero = jnp.zeros_like(qt)
            rhs = jnp.concatenate([jnp.where(top, qt, zero), jnp.where(top, zero, qt)], axis=1)
            sc = jnp.dot(kw, rhs, preferred_element_type=F32) + bias_s[jnp.minimum(i, 1)]
            m = jnp.max(sc, axis=0, keepdims=True)
            p_s[idx] = jnp.exp2(sc - m).astype(BF16)
            m_s[idx] = jnp.broadcast_to(m, (SUBLANES, 2 * QBLK))
            return c2

        lax.fori_loop(0, n_flat, scores, 0, unroll=16)

        def block(idx, c2, r=r, split=split, on_s=on_s, ln_s=ln_s):
            j, i = split(idx)
            p = p_s[idx]
            m = m_s[idx][0:1, :]
            lhs = jnp.concatenate(
                [jnp.concatenate([vt_s[idx + j], vt_s[idx + j + 1]], axis=1), ones_rows], axis=0)
            ot = jnp.dot(lhs, p, preferred_element_type=F32)
            den = ot[LANES:LANES + 1, :]
            inv = 1.0 / den
            lse = m + jnp.log2(den)
            o_t = jnp.concatenate(
                [ot[0:HEAD_DIM, 0:QBLK] * inv[:, 0:QBLK],
                 ot[HEAD_DIM:LANES, QBLK:2 * QBLK] * inv[:, QBLK:2 * QBLK]], axis=0)
            l_t = jnp.concatenate(
                [jnp.broadcast_to(lse[:, 0:QBLK], (HEAD_DIM, QBLK)),
                 jnp.broadcast_to(lse[:, QBLK:2 * QBLK], (HEAD_DIM, QBLK))], axis=0)
            dst = pl.ds(j + r * QBLK * i, QBLK, stride=r)
            on_s[dst, :] = o_t.T
            ln_s[dst, :] = l_t.T
            return c2

        lax.fori_loop(0, n_flat, block, 0, unroll=16)

    def combine(n, carry):
        r0 = pl.multiple_of(n * PRO_ROWS, PRO_ROWS)
        rows = pl.ds(r0, PRO_ROWS)
        ls = [l_s[rows, :] for l_s in lses]
        m = jnp.maximum(jnp.maximum(ls[0], ls[1]), ls[2])
        es = [jnp.exp2(l - m) for l in ls]
        num = es[0] * o0_s[rows, :] + es[1] * o1_s[rows, :] + es[2] * o2_s[rows, :]
        o_ref[0, rows, :] = (num / (es[0] + es[1] + es[2])).astype(o_ref.dtype)
        return carry

    lax.fori_loop(0, seq // PRO_ROWS, combine, 0, unroll=4)


def _dilated(proj, tables, q_gain, k_gain, col0):
    b, s, _ = proj.shape
    n_pairs = DIL_HEADS * HEAD_DIM // LANES
    t0 = col0 // LANES
    tab = pl.BlockSpec((1, s, LANES), lambda bi, p: (bi, 0, 0), pipeline_mode=pl.Buffered(1))
    two = lambda g: jnp.concatenate([g, g]).reshape(1, LANES)
    nat = pltpu.VMEM((s, LANES), F32)
    max_r = max(r for _, r in DIL_PATTERNS)
    return pl.pallas_call(
        functools.partial(_dilated_kernel, seq=s),
        grid=(b, n_pairs),
        in_specs=[
            pl.BlockSpec((1, s, LANES), lambda bi, p: (bi, 0, t0 + p)),
            pl.BlockSpec((1, s, LANES), lambda bi, p: (bi, 0, t0 + n_pairs + p)),
            pl.BlockSpec((1, s, LANES), lambda bi, p: (bi, 0, t0 + 2 * n_pairs + p)),
            tab, tab,
            _resident((1, LANES)),
            _resident((1, LANES)),
        ],
        out_specs=pl.BlockSpec((1, s, LANES), lambda bi, p: (bi, 0, p)),
        out_shape=jax.ShapeDtypeStruct((b, s, DIL_HEADS * HEAD_DIM), BF16),
        scratch_shapes=[
            nat, nat, nat, nat, nat, nat,
            pltpu.VMEM((s // QBLK, LANES, QBLK), BF16),
            pltpu.VMEM((s + max_r * QBLK, LANES), BF16),
            pltpu.VMEM((s // QBLK + max_r, LANES, QBLK), BF16),
            pltpu.VMEM((2, 2 * QBLK, 2 * QBLK), F32),
            pltpu.VMEM((s // QBLK, 2 * QBLK, 2 * QBLK), BF16),
            pltpu.VMEM((s // QBLK, SUBLANES, 2 * QBLK), F32),
            nat, nat, nat, nat, nat, nat,
        ],
        compiler_params=_cparams(2),
        name="dilated_attention",
    )(proj, proj, proj, *tables, two(q_gain), two(k_gain))


def _swa_kernel(q_ref, k_ref, v_ref, c_ref, s_ref, qg_ref, kg_ref, sink_ref, o_ref,
                q_s, kd_s, vt_s, bias_s, p_s, m_s, rq_s, rk_s, *, seq):
    group = SWA_Q_HEADS // SWA_KV_HEADS
    swap = _swap_matrix(ROPE_DIMS // 2)
    head_ones = _head_ones()
    kv_in_hi = (pl.program_id(1) % 2) == 1
    blk_per_step = PRO_ROWS // QBLK
    src = lax.broadcasted_iota(jnp.int32, (LANES, LANES), 0)
    dst = lax.broadcasted_iota(jnp.int32, (LANES, LANES), 1) & (HEAD_DIM - 1)
    dup = jnp.where(src == dst + jnp.where(kv_in_hi, HEAD_DIM, 0), 1.0, 0.0).astype(BF16)

    def norms(n, carry):
        rows = pl.ds(pl.multiple_of(n * PRO_ROWS, PRO_ROWS), PRO_ROWS)
        for t in range(group // 2):
            cols = slice(t * LANES, (t + 1) * LANES)
            rq_s[rows, cols] = _head_rinv(q_ref[0, rows, cols].astype(F32), head_ones)
        rk_s[rows, :] = _head_rinv(k_ref[0, rows, :].astype(F32), head_ones)
        return carry

    lax.fori_loop(0, seq // PRO_ROWS, norms, 0, unroll=16)

    def prologue(n, carry):
        r0 = pl.multiple_of(n * PRO_ROWS, PRO_ROWS)
        rows = pl.ds(r0, PRO_ROWS)
        c, sn = c_ref[0, rows, :], s_ref[0, rows, :]
        for t in range(group // 2):
            cols = slice(t * LANES, (t + 1) * LANES)
            q = q_ref[0, rows, cols].astype(F32) * rq_s[rows, cols] * qg_ref[...]
            q_s[rows, cols] = (_rotate(q, c, sn, swap) * Q_SCALE).astype(BF16)
        k = k_ref[0, rows, :].astype(F32) * rk_s[rows, :] * kg_ref[...]
        k = _rotate(k, c, sn, swap).astype(BF16)
        kd_s[pl.ds(r0 + QBLK, PRO_ROWS), :] = jnp.dot(k, dup, preferred_element_type=F32).astype(BF16)
        v_t = v_ref[0, rows, :].astype(F32).T
        v_t = jnp.where(kv_in_hi, v_t[HEAD_DIM:LANES, :], v_t[0:HEAD_DIM, :]).astype(BF16)
        for u in range(blk_per_step):
            vt_s[n * blk_per_step + u + 1] = v_t[:, u * QBLK:(u + 1) * QBLK]
        return carry

    lax.fori_loop(0, seq // PRO_ROWS, prologue, 0, unroll=8)

    first, later = _band_bias_t(1, QBLK)
    bias_s[0] = jnp.concatenate([first] * group, axis=1)
    bias_s[1] = jnp.concatenate([later] * group, axis=1)
    kd_s[0:QBLK, :] = jnp.zeros((QBLK, LANES), BF16)
    vt_s[0] = jnp.zeros((HEAD_DIM, QBLK), BF16)
    lo = _lo_mask((QBLK, LANES))
    heads = (lo, jnp.logical_not(lo))
    sink = jnp.concatenate([sink_ref[0, hh:hh + 1, :] for hh in range(group)], axis=1) * LOG2_E
    ones_rows = jnp.ones((DEN_ROWS, 2 * QBLK), BF16)

    def scores(i, carry):
        d0 = pl.multiple_of(i * QBLK, QBLK)
        rows = pl.ds(d0, QBLK)
        kw = kd_s[pl.ds(d0, 2 * QBLK), :]
        stack = []
        for t in range(group // 2):
            qt = q_s[rows, t * LANES:(t + 1) * LANES]
            for head in heads:
                stack.append(jnp.where(head, qt, jnp.zeros_like(qt)))
        qm = jnp.concatenate(stack, axis=0)
        sc = lax.dot_general(kw, qm, (((1,), (1,)), ((), ())), preferred_element_type=F32)
        sc = sc + bias_s[jnp.minimum(i, 1)]
        m = jnp.maximum(jnp.max(sc, axis=0, keepdims=True), sink)
        p_s[i] = jnp.exp2(sc - m).astype(BF16)
        m_s[i] = jnp.broadcast_to(m, (SUBLANES, group * QBLK))
        return carry

    lax.fori_loop(0, seq // QBLK, scores, 0, unroll=BLOCK_UNROLL)

    def values(i, carry):
        rows = pl.ds(pl.multiple_of(i * QBLK, QBLK), QBLK)
        m = m_s[i][0:1, :]
        lhs = jnp.concatenate(
            [jnp.concatenate([vt_s[i], vt_s[i + 1]], axis=1), ones_rows], axis=0)
        ot = jnp.dot(lhs, p_s[i], preferred_element_type=F32)
        inv = 1.0 / (ot[HEAD_DIM:HEAD_DIM + 1, :] + jnp.exp2(sink - m))
        on = ot[0:HEAD_DIM, :] * inv
        for t in range(group // 2):
            pair_t = jnp.concatenate(
                [on[:, 2 * t * QBLK:(2 * t + 1) * QBLK], on[:, (2 * t + 1) * QBLK:(2 * t + 2) * QBLK]],
                axis=0)
            o_ref[0, rows, t * LANES:(t + 1) * LANES] = pair_t.T.astype(o_ref.dtype)
        return carry

    lax.fori_loop(0, seq // QBLK, values, 0, unroll=16)


def _swa(proj, tables, q_gain, k_gain, sinks):
    b, s, _ = proj.shape
    group = SWA_Q_HEADS // SWA_KV_HEADS
    q_w = group * HEAD_DIM
    k_t0 = SWA_Q_HEADS * HEAD_DIM // LANES
    v_t0 = k_t0 + SWA_KV_HEADS * HEAD_DIM // LANES
    tab = pl.BlockSpec((1, s, LANES), lambda bi, g: (bi, 0, 0), pipeline_mode=pl.Buffered(1))
    two = lambda g: jnp.concatenate([g, g]).reshape(1, LANES)
    sink_rows = jnp.broadcast_to(sinks.reshape(SWA_KV_HEADS, group, 1), (SWA_KV_HEADS, group, LANES))
    return pl.pallas_call(
        functools.partial(_swa_kernel, seq=s),
        grid=(b, SWA_KV_HEADS),
        in_specs=[
            pl.BlockSpec((1, s, q_w), lambda bi, g: (bi, 0, g)),
            pl.BlockSpec((1, s, LANES), lambda bi, g: (bi, 0, k_t0 + g // 2)),
            pl.BlockSpec((1, s, LANES), lambda bi, g: (bi, 0, v_t0 + g // 2)),
            tab, tab,
            _resident((1, LANES)),
            _resident((1, LANES)),
            pl.BlockSpec((1, group, LANES), lambda bi, g: (g, 0, 0)),
        ],
        out_specs=pl.BlockSpec((1, s, q_w), lambda bi, g: (bi, 0, g)),
        out_shape=jax.ShapeDtypeStruct((b, s, SWA_Q_HEADS * HEAD_DIM), BF16),
        scratch_shapes=[
            pltpu.VMEM((s, q_w), BF16),
            pltpu.VMEM((s + QBLK, LANES), BF16),
            pltpu.VMEM((s // QBLK + 1, HEAD_DIM, QBLK), BF16),
            pltpu.VMEM((2, 2 * QBLK, group * QBLK), F32),
            pltpu.VMEM((s // QBLK, 2 * QBLK, group * QBLK), BF16),
            pltpu.VMEM((s // QBLK, SUBLANES, group * QBLK), F32),
            pltpu.VMEM((s, q_w), F32),
            pltpu.VMEM((s, LANES), F32),
        ],
        compiler_params=_cparams(2),
        name="swa_attention",
    )(proj, proj, proj, *tables, two(q_gain), two(k_gain), sink_rows)


def kernel(x, positions, norm_mix, norm_mlp, mlp_w_up, mlp_w_down, hyb_w_in, hyb_w_out, ret_gn_gain, dil_q_gain, dil_k_gain, swa_w_qkv, swa_b_qkv, swa_w_out, swa_q_gain, swa_k_gain, swa_sinks):
    b, s, d = x.shape
    depth = norm_mix.shape[0]
    ret_tab = _rope_table(positions, RET_QK_DIM // 2, RET_THETA, RET_QK_DIM)
    rope_tab = _rope_table(positions, ROPE_DIMS // 2, ROPE_THETA, ROPE_DIMS)
    log_gamma = jnp.log1p(-jnp.exp2(-5.0 - jnp.arange(RET_HEADS, dtype=F32)))
    ret_w = RET_HEADS * RET_V_DIM
    dil_col0 = 2 * RET_HEADS * RET_QK_DIM + 2 * ret_w

    x2d = x.reshape(b * s, d)
    for layer in range(depth):
        i = layer // 2
        if layer % 2 == 0:
            w_in = hyb_w_in[i].astype(BF16)
            proj = _norm_proj(x2d, norm_mix[layer], w_in, jnp.zeros((w_in.shape[1],), F32))
            proj = proj.reshape(b, s, -1)
            ra = _retention(proj, ret_tab, log_gamma, ret_gn_gain[i])
            da = _dilated(proj, rope_tab, dil_q_gain[i], dil_k_gain[i], dil_col0)
            w_out = hyb_w_out[i].astype(BF16)
            mixed = [ra.reshape(b * s, -1), da.reshape(b * s, -1)]
        else:
            proj = _norm_proj(x2d, norm_mix[layer], swa_w_qkv[i].astype(BF16), swa_b_qkv[i])
            proj = proj.reshape(b, s, -1)
            att = _swa(proj, rope_tab, swa_q_gain[i], swa_k_gain[i], swa_sinks[i])
            mixed = [att.reshape(b * s, -1)]
            w_out = swa_w_out[i].astype(BF16)
        x2d = _out_mlp(mixed, x2d, w_out, norm_mlp[layer],
                       mlp_w_up[layer].astype(BF16), mlp_w_down[layer].astype(BF16))
    return x2d.reshape(b, s, d)
```

```python
import functools

import jax
import jax.numpy as jnp
import numpy as np
from jax import lax
from jax.experimental import pallas as pl
from jax.experimental.pallas import tpu as pltpu

F32 = jnp.float32
BF16 = jnp.bfloat16

D_MODEL = 1024
D_FF = 4 * D_MODEL
HEAD_DIM = 64
EPS = 1e-6
RET_HEADS = 4
RET_QK_DIM = 64
RET_V_DIM = 128
RET_CHUNK = 128
RET_THETA = 10000.0
DIL_HEADS = 8
DIL_PATTERNS = ((128, 1), (512, 4), (2048, 16))
SWA_Q_HEADS = 16
SWA_KV_HEADS = 4
SWA_WINDOW = 128
ROPE_THETA = 500000.0
ROPE_DIMS = HEAD_DIM // 4

LANES = 128
SUBLANES = 8
QBLK = 128
VMEM_LIMIT_BYTES = 56 * 1024 * 1024
NEG_BIG = -1e30
LOG2_E = 1.4426950408889634
Q_SCALE = HEAD_DIM ** -0.5 * LOG2_E

ROW_TILE = 512
FF_CHUNK = 1024
PRO_ROWS = 128
BLOCK_UNROLL = 16
DEN_ROWS = 16
COARSE_R, FINE_R = DIL_PATTERNS[1][1], DIL_PATTERNS[2][1]


def _cparams(n_axes):
    return pltpu.CompilerParams(
        dimension_semantics=("arbitrary",) * n_axes,
        vmem_limit_bytes=VMEM_LIMIT_BYTES,
    )


def _resident(shape):
    nd = len(shape)
    return pl.BlockSpec(shape, lambda *_: (0,) * nd, pipeline_mode=pl.Buffered(1))


def _split3(x):
    hi = x.astype(BF16)
    r1 = x - hi.astype(F32)
    mid = r1.astype(BF16)
    lo = (r1 - mid.astype(F32)).astype(BF16)
    return hi, mid, lo


def _trig_kernel(p_ref, inv_ref, ec_ref, es_ref, base_ref, c_ref, s_ref, *, n_pos):
    ang = p_ref[...] * inv_ref[...]
    cos_parts = _split3(jnp.cos(ang))
    sin_parts = _split3(jnp.sin(ang))
    rows = p_ref.shape[0]
    for c in range(n_pos):
        dst = pl.ds(c, rows, stride=n_pos)
        c_ref[dst, :] = base_ref[...] + sum(
            jnp.dot(part, ec_ref[c], preferred_element_type=F32) for part in cos_parts)
        s_ref[dst, :] = sum(
            jnp.dot(part, es_ref[c], preferred_element_type=F32) for part in sin_parts)


def _selectors(half):
    n_pos = LANES // half
    ec = np.zeros((n_pos, LANES, LANES), np.float32)
    es = np.zeros((n_pos, LANES, LANES), np.float32)
    base = np.zeros((1, LANES), np.float32)
    for j in range(LANES):
        d = j % HEAD_DIM
        if d >= 2 * half:
            base[0, j] = 1.0
            continue
        f = d % half
        for c in range(n_pos):
            ec[c, c * half + f, j] = 1.0
            es[c, c * half + f, j] = -1.0 if d < half else 1.0
    return n_pos, jnp.asarray(ec, BF16), jnp.asarray(es, BF16), jnp.asarray(base)


def _rope_table(positions, half, theta, n_rot):
    b, s = positions.shape
    posf = positions.astype(F32)
    inv = jnp.power(jnp.float32(theta), -jnp.arange(half, dtype=F32) * (2.0 / n_rot))
    n_pos, ec, es, base = _selectors(half)
    p = jnp.broadcast_to(posf[..., None], (b, s, half)).reshape(-1, LANES)
    inv_rows = jnp.broadcast_to(jnp.tile(inv, n_pos)[None, :], p.shape)
    rows = p.shape[0]
    tile = min(rows, 2048 // n_pos)
    assert rows % tile == 0
    dense = pl.BlockSpec((tile, LANES), lambda i: (i, 0))
    wide = pl.BlockSpec((tile * n_pos, LANES), lambda i: (i, 0))
    c_tab, s_tab = pl.pallas_call(
        functools.partial(_trig_kernel, n_pos=n_pos),
        grid=(rows // tile,),
        in_specs=[dense, dense, _resident(ec.shape), _resident(es.shape), _resident(base.shape)],
        out_specs=[wide, wide],
        out_shape=[jax.ShapeDtypeStruct((rows * n_pos, LANES), F32)] * 2,
        compiler_params=_cparams(1),
        name="trig_tables",
    )(p, inv_rows, ec, es, base)
    return c_tab.reshape(b, s, LANES), s_tab.reshape(b, s, LANES)


def _swap_matrix(half):
    src = lax.broadcasted_iota(jnp.int32, (LANES, LANES), 0)
    dst = lax.broadcasted_iota(jnp.int32, (LANES, LANES), 1)
    d = dst & (HEAD_DIM - 1)
    want = jnp.where(d < half, dst + half, jnp.where(d < 2 * half, dst - half, -1))
    return jnp.where(src == want, 1.0, 0.0).astype(BF16)


def _rotate(x, c, s, swap):
    swapped = jnp.dot(x.astype(BF16), swap, preferred_element_type=F32)
    return x * c + swapped * s


def _lo_mask(shape):
    return lax.broadcasted_iota(jnp.int32, shape, len(shape) - 1) < HEAD_DIM


def _head_ones():
    r = lax.broadcasted_iota(jnp.int32, (LANES, LANES), 0) < HEAD_DIM
    c = lax.broadcasted_iota(jnp.int32, (LANES, LANES), 1) < HEAD_DIM
    return jnp.where(r == c, 1.0, 0.0).astype(BF16)


def _head_rinv(x, head_ones):
    ss = jnp.dot((x * x).astype(BF16), head_ones, preferred_element_type=F32)
    return lax.rsqrt(ss * (1.0 / HEAD_DIM) + EPS)


def _norm_proj_kernel(x_ref, g_ref, w_ref, b_ref, o_ref, *, n_chunk):
    x = x_ref[...]
    ms = jnp.mean(x * x, axis=-1, keepdims=True)
    h = (x * lax.rsqrt(ms + EPS) * g_ref[...]).astype(BF16)
    n = o_ref.shape[-1]
    for c in range(0, n, n_chunk):
        acc = jnp.dot(h, w_ref[:, c:c + n_chunk], preferred_element_type=F32)
        o_ref[:, c:c + n_chunk] = (acc + b_ref[:, c:c + n_chunk]).astype(o_ref.dtype)


def _norm_proj(x2d, gain, w_bf16, bias):
    m, d = x2d.shape
    n = w_bf16.shape[1]
    return pl.pallas_call(
        functools.partial(_norm_proj_kernel, n_chunk=512),
        grid=(m // ROW_TILE,),
        in_specs=[
            pl.BlockSpec((ROW_TILE, d), lambda i: (i, 0)),
            _resident((1, d)),
            _resident((d, n)),
            _resident((1, n)),
        ],
        out_specs=pl.BlockSpec((ROW_TILE, n), lambda i: (i, 0)),
        out_shape=jax.ShapeDtypeStruct((m, n), BF16),
        compiler_params=_cparams(1),
        name="norm_proj",
    )(x2d, gain.reshape(1, d), w_bf16, bias.reshape(1, n))


def _out_mlp_kernel(*refs, n_mix):
    a_refs = refs[:n_mix]
    x_ref, wo_ref, g_ref, wup_ref, wdn_ref, o_ref = refs[n_mix:]
    mixed = a_refs[0][...] if n_mix == 1 else jnp.concatenate([a[...] for a in a_refs], axis=-1)
    x1 = x_ref[...] + jnp.dot(mixed, wo_ref[...], preferred_element_type=F32)
    ms = jnp.mean(x1 * x1, axis=-1, keepdims=True)
    h = (x1 * lax.rsqrt(ms + EPS) * g_ref[...]).astype(BF16)
    y = x1
    for c in range(0, D_FF, FF_CHUNK):
        u = jnp.dot(h, wup_ref[:, c:c + FF_CHUNK], preferred_element_type=F32)
        u = jnp.square(jnp.maximum(u, 0.0)).astype(BF16)
        y = y + jnp.dot(u, wdn_ref[c:c + FF_CHUNK, :], preferred_element_type=F32)
    o_ref[...] = y


def _out_mlp(mixed, x2d, w_out, gain, w_up, w_down):
    m, d = x2d.shape
    n_mix = len(mixed)
    in_specs = [pl.BlockSpec((ROW_TILE, a.shape[1]), lambda i: (i, 0)) for a in mixed]
    in_specs.append(pl.BlockSpec((ROW_TILE, d), lambda i: (i, 0)))
    in_specs += [_resident(w_out.shape), _resident((1, d)), _resident(w_up.shape), _resident(w_down.shape)]
    return pl.pallas_call(
        functools.partial(_out_mlp_kernel, n_mix=n_mix),
        grid=(m // ROW_TILE,),
        in_specs=in_specs,
        out_specs=pl.BlockSpec((ROW_TILE, d), lambda i: (i, 0)),
        out_shape=jax.ShapeDtypeStruct((m, d), F32),
        compiler_params=_cparams(1),
        name="out_mlp",
    )(*mixed, x2d, w_out, gain.reshape(1, d), w_up, w_down)


def _retention_kernel(q_ref, k_ref, v_ref, g_ref, c_ref, s_ref, lg_ref, gn_ref,
                      o_ref, q_s, k_s, kt_s, o_s, kv_s, *, seq):
    n_chunks = seq // RET_CHUNK
    swap = _swap_matrix(RET_QK_DIM // 2)
    cs = RET_CHUNK

    def prologue(n, carry):
        r0 = pl.multiple_of(n * cs, cs)
        rows = pl.ds(r0, cs)
        c, sn = c_ref[0, rows, :], s_ref[0, rows, :]
        q = _rotate(q_ref[0, rows, :].astype(F32), c, sn, swap)
        k = _rotate(k_ref[0, rows, :].astype(F32), c, sn, swap) * (RET_QK_DIM ** -0.5)
        q_s[n] = q.astype(BF16)
        k_s[n] = k.astype(BF16)
        kt_s[n] = k.T
        return carry

    lax.fori_loop(0, n_chunks, prologue, 0, unroll=16)

    lo = _lo_mask((cs, LANES))
    row = lax.broadcasted_iota(jnp.int32, (cs, cs), 0).astype(F32)
    col = lax.broadcasted_iota(jnp.int32, (cs, cs), 1).astype(F32)
    diff = row - col
    consts = []
    for hh in range(2):
        lg = lg_ref[0, hh:hh + 1, :]
        decay = jnp.where(diff >= 0, jnp.exp(lg * jnp.maximum(diff, 0.0)), 0.0)
        xi = jnp.exp(lg * (row + 1.0))
        zeta = jnp.exp(lg * (cs - 1.0 - col[0:1, :]))
        cd = jnp.exp(lg * float(cs))
        head = lo if hh == 0 else jnp.logical_not(lo)
        consts.append((decay, xi, zeta, cd, head))

    def inner(n, carry):
        rows = pl.ds(pl.multiple_of(n * cs, cs), cs)
        qc, kc, ktc = q_s[n], k_s[n], kt_s[n]
        for hh in range(2):
            decay, xi, zeta, cd, head = consts[hh]
            cols = slice(hh * RET_V_DIM, (hh + 1) * RET_V_DIM)
            qm = jnp.where(head, qc, jnp.zeros_like(qc))
            vc = v_ref[0, rows, cols]
            sc = lax.dot_general(qm, kc, (((1,), (1,)), ((), ())), preferred_element_type=F32) * decay
            o_s[rows, cols] = jnp.dot(sc.astype(BF16), vc, preferred_element_type=F32)
            kz = (ktc * zeta).astype(BF16)
            kv_s[n, hh] = jnp.dot(kz, vc, preferred_element_type=F32)
        return carry

    lax.fori_loop(0, n_chunks, inner, 0, unroll=16)

    def cross(n, state):
        rows = pl.ds(pl.multiple_of(n * cs, cs), cs)
        qc = q_s[n]
        new_state = []
        for hh in range(2):
            decay, xi, zeta, cd, head = consts[hh]
            cols = slice(hh * RET_V_DIM, (hh + 1) * RET_V_DIM)
            r_prev = state[hh]
            qm = jnp.where(head, qc, jnp.zeros_like(qc))
            o_s[rows, cols] += jnp.dot(qm, r_prev.astype(BF16), preferred_element_type=F32) * xi
            new_state.append(r_prev * cd + kv_s[n, hh])
        return tuple(new_state)

    zero = jnp.zeros((LANES, RET_V_DIM), F32)
    lax.fori_loop(0, n_chunks, cross, (zero, zero), unroll=16)

    def finish(n, carry):
        rows = pl.ds(pl.multiple_of(n * cs, cs), cs)
        for hh in range(2):
            cols = slice(hh * RET_V_DIM, (hh + 1) * RET_V_DIM)
            o = o_s[rows, cols]
            mu = jnp.mean(o, axis=-1, keepdims=True)
            dev = o - mu
            var = jnp.mean(dev * dev, axis=-1, keepdims=True)
            y = dev * lax.rsqrt(var + EPS) * gn_ref[0, hh:hh + 1, :]
            gate = g_ref[0, rows, cols].astype(F32)
            o_ref[0, rows, cols] = (gate * jax.nn.sigmoid(gate) * y).astype(o_ref.dtype)
        return carry

    lax.fori_loop(0, n_chunks, finish, 0, unroll=2)


def _retention(proj, tables, log_gamma, gn_gain):
    b, s, _ = proj.shape
    n_pairs = RET_HEADS // 2
    pair_w = 2 * RET_V_DIM
    qk_tiles = RET_HEADS * RET_QK_DIM // LANES
    v_off = 2 * qk_tiles * LANES // pair_w
    g_off = v_off + RET_HEADS * RET_V_DIM // pair_w
    tab = pl.BlockSpec((1, s, LANES), lambda bi, p: (bi, 0, 0), pipeline_mode=pl.Buffered(1))
    lg = jnp.broadcast_to(log_gamma.reshape(n_pairs, 2, 1), (n_pairs, 2, LANES))
    n_chunks = s // RET_CHUNK
    return pl.pallas_call(
        functools.partial(_retention_kernel, seq=s),
        grid=(b, n_pairs),
        in_specs=[
            pl.BlockSpec((1, s, LANES), lambda bi, p: (bi, 0, p)),
            pl.BlockSpec((1, s, LANES), lambda bi, p: (bi, 0, qk_tiles + p)),
            pl.BlockSpec((1, s, pair_w), lambda bi, p: (bi, 0, v_off + p)),
            pl.BlockSpec((1, s, pair_w), lambda bi, p: (bi, 0, g_off + p)),
            tab, tab,
            pl.BlockSpec((1, 2, LANES), lambda bi, p: (p, 0, 0)),
            pl.BlockSpec((1, 2, RET_V_DIM), lambda bi, p: (p, 0, 0)),
        ],
        out_specs=pl.BlockSpec((1, s, pair_w), lambda bi, p: (bi, 0, p)),
        out_shape=jax.ShapeDtypeStruct((b, s, RET_HEADS * RET_V_DIM), BF16),
        scratch_shapes=[
            pltpu.VMEM((n_chunks, RET_CHUNK, LANES), BF16),
            pltpu.VMEM((n_chunks, RET_CHUNK, LANES), BF16),
            pltpu.VMEM((n_chunks, LANES, RET_CHUNK), F32),
            pltpu.VMEM((s, pair_w), F32),
            pltpu.VMEM((n_chunks, 2, LANES, RET_V_DIM), F32),
        ],
        compiler_params=_cparams(2),
        name="retention",
    )(proj, proj, proj, proj, *tables, lg, gn_gain.reshape(n_pairs, 2, RET_V_DIM))


def _band_bias_t(lo_off, hi_off):
    c = lax.broadcasted_iota(jnp.int32, (2 * QBLK, QBLK), 0)
    a = lax.broadcasted_iota(jnp.int32, (2 * QBLK, QBLK), 1)
    band = (c - a >= lo_off) & (c - a <= hi_off)
    later = jnp.where(band, 0.0, NEG_BIG).astype(F32)
    first = jnp.where(band & (c >= QBLK), 0.0, NEG_BIG).astype(F32)
    return first, later


def _dilated_kernel(q_ref, k_ref, v_ref, c_ref, s_ref, qg_ref, kg_ref, o_ref,
                    qn_s, kn_s, vn_s, q4_s, k4_s, v4_s, qt_s, kc_s, vt_s, bias_s, p_s, m_s,
                    o0_s, o1_s, o2_s, l0_s, l1_s, l2_s, *, seq):
    swap = _swap_matrix(ROPE_DIMS // 2)
    head_ones = _head_ones()
    rq_s, rk_s = o0_s, l0_s

    def norms(n, carry):
        rows = pl.ds(pl.multiple_of(n * PRO_ROWS, PRO_ROWS), PRO_ROWS)
        rq_s[rows, :] = _head_rinv(q_ref[0, rows, :].astype(F32), head_ones)
        rk_s[rows, :] = _head_rinv(k_ref[0, rows, :].astype(F32), head_ones)
        return carry

    lax.fori_loop(0, seq // PRO_ROWS, norms, 0, unroll=16)

    def prologue(n, carry):
        rows = pl.ds(pl.multiple_of(n * PRO_ROWS, PRO_ROWS), PRO_ROWS)
        c, sn = c_ref[0, rows, :], s_ref[0, rows, :]
        q = q_ref[0, rows, :].astype(F32) * rq_s[rows, :] * qg_ref[...]
        qn_s[rows, :] = _rotate(q, c, sn, swap) * Q_SCALE
        k = k_ref[0, rows, :].astype(F32) * rk_s[rows, :] * kg_ref[...]
        kn_s[rows, :] = _rotate(k, c, sn, swap)
        vn_s[rows, :] = v_ref[0, rows, :].astype(F32)
        return carry

    lax.fori_loop(0, seq // PRO_ROWS, prologue, 0, unroll=16)

    first, later = _band_bias_t(0, QBLK)
    bias_s[0] = jnp.concatenate([first, first], axis=1)
    bias_s[1] = jnp.concatenate([later, later], axis=1)
    zero_pad = jnp.zeros((QBLK, LANES), BF16)
    top = lax.broadcasted_iota(jnp.int32, (LANES, QBLK), 0) < HEAD_DIM
    ones_rows = jnp.ones((DEN_ROWS, 2 * QBLK), BF16)

    outs = (o0_s, o1_s, o2_s)
    lses = (l0_s, l1_s, l2_s)
    n_flat = seq // QBLK
    for (window, r), on_s, ln_s in zip(DIL_PATTERNS, outs, lses):
        assert window // r == QBLK
        n_blk = seq // r // QBLK
        blk_shift = n_blk.bit_length() - 1
        assert n_blk == 1 << blk_shift

        def pad(j, c2, n_blk=n_blk):
            z = j * (n_blk + 1)
            kc_s[pl.ds(pl.multiple_of(z * QBLK, QBLK), QBLK), :] = zero_pad
            vt_s[z] = zero_pad
            return c2

        lax.fori_loop(0, r, pad, 0)

        def split(idx, r=r, n_blk=n_blk, blk_shift=blk_shift):
            j = lax.shift_right_logical(idx, blk_shift)
            i = idx & (n_blk - 1)
            return j, i

        def gather(idx, c2, r=r, split=split):
            j, i = split(idx)
            if r == FINE_R:
                base = ((j & (COARSE_R - 1)) * (seq // COARSE_R)
                        + lax.shift_right_logical(j, COARSE_R.bit_length() - 1))
                src = pl.ds(base + (QBLK * r // COARSE_R) * i, QBLK, stride=r // COARSE_R)
                q, k, v = q4_s[src, :], k4_s[src, :], v4_s[src, :]
            else:
                src = pl.ds(j + r * QBLK * i, QBLK, stride=r)
                q, k, v = qn_s[src, :], kn_s[src, :], vn_s[src, :]
            if r == COARSE_R:
                dense = pl.ds(pl.multiple_of(idx * QBLK, QBLK), QBLK)
                q4_s[dense, :] = q
                k4_s[dense, :] = k
                v4_s[dense, :] = v
            k0 = pl.multiple_of((idx + j + 1) * QBLK, QBLK)
            qt_s[idx] = q.T.astype(BF16)
            kc_s[pl.ds(k0, QBLK), :] = k.astype(BF16)
            vt_s[idx + j + 1] = v.T.astype(BF16)
            return c2

        lax.fori_loop(0, n_flat, gather, 0, unroll=16)

        def scores(idx, c2, split=split):
            j, i = split(idx)
            k0 = pl.multiple_of((idx + j) * QBLK, QBLK)
            qt = qt_s[idx]
            kw = kc_s[pl.ds(k0, 2 * QBLK), :]
            zero = jnp.zeros_like(qt)
            rhs = jnp.concatenate([jnp.where(top, qt, zero), jnp.where(top, zero, qt)], axis=1)
            sc = jnp.dot(kw, rhs, preferred_element_type=F32) + bias_s[jnp.minimum(i, 1)]
            m = jnp.max(sc, axis=0, keepdims=True)
            p_s[idx] = jnp.exp2(sc - m).astype(BF16)
            m_s[idx] = jnp.broadcast_to(m, (SUBLANES, 2 * QBLK))
            return c2

        lax.fori_loop(0, n_flat, scores, 0, unroll=16)

        def block(idx, c2, r=r, split=split, on_s=on_s, ln_s=ln_s):
            j, i = split(idx)
            p = p_s[idx]
            m = m_s[idx][0:1, :]
            lhs = jnp.concatenate(
                [jnp.concatenate([vt_s[idx + j], vt_s[idx + j + 1]], axis=1), ones_rows], axis=0)
            ot = jnp.dot(lhs, p, preferred_element_type=F32)
            den = ot[LANES:LANES + 1, :]
            inv = 1.0 / den
            lse = m + jnp.log2(den)
            o_t = jnp.concatenate(
                [ot[0:HEAD_DIM, 0:QBLK] * inv[:, 0:QBLK],
                 ot[HEAD_DIM:LANES, QBLK:2 * QBLK] * inv[:, QBLK:2 * QBLK]], axis=0)
            l_t = jnp.concatenate(
                [jnp.broadcast_to(lse[:, 0:QBLK], (HEAD_DIM, QBLK)),
                 jnp.broadcast_to(lse[:, QBLK:2 * QBLK], (HEAD_DIM, QBLK))], axis=0)
            dst = pl.ds(j + r * QBLK * i, QBLK, stride=r)
            on_s[dst, :] = o_t.T
            ln_s[dst, :] = l_t.T
            return c2

        lax.fori_loop(0, n_flat, block, 0, unroll=16)

    def combine(n, carry):
        r0 = pl.multiple_of(n * PRO_ROWS, PRO_ROWS)
        rows = pl.ds(r0, PRO_ROWS)
        ls = [l_s[rows, :] for l_s in lses]
        m = jnp.maximum(jnp.maximum(ls[0], ls[1]), ls[2])
        es = [jnp.exp2(l - m) for l in ls]
        num = es[0] * o0_s[rows, :] + es[1] * o1_s[rows, :] + es[2] * o2_s[rows, :]
        o_ref[0, rows, :] = (num / (es[0] + es[1] + es[2])).astype(o_ref.dtype)
        return carry

    lax.fori_loop(0, seq // PRO_ROWS, combine, 0, unroll=8)


def _dilated(proj, tables, q_gain, k_gain, col0):
    b, s, _ = proj.shape
    n_pairs = DIL_HEADS * HEAD_DIM // LANES
    t0 = col0 // LANES
    tab = pl.BlockSpec((1, s, LANES), lambda bi, p: (bi, 0, 0), pipeline_mode=pl.Buffered(1))
    two = lambda g: jnp.concatenate([g, g]).reshape(1, LANES)
    nat = pltpu.VMEM((s, LANES), F32)
    max_r = max(r for _, r in DIL_PATTERNS)
    return pl.pallas_call(
        functools.partial(_dilated_kernel, seq=s),
        grid=(b, n_pairs),
        in_specs=[
            pl.BlockSpec((1, s, LANES), lambda bi, p: (bi, 0, t0 + p)),
            pl.BlockSpec((1, s, LANES), lambda bi, p: (bi, 0, t0 + n_pairs + p)),
            pl.BlockSpec((1, s, LANES), lambda bi, p: (bi, 0, t0 + 2 * n_pairs + p)),
            tab, tab,
            _resident((1, LANES)),
            _resident((1, LANES)),
        ],
        out_specs=pl.BlockSpec((1, s, LANES), lambda bi, p: (bi, 0, p)),
        out_shape=jax.ShapeDtypeStruct((b, s, DIL_HEADS * HEAD_DIM), BF16),
        scratch_shapes=[
            nat, nat, nat, nat, nat, nat,
            pltpu.VMEM((s // QBLK, LANES, QBLK), BF16),
            pltpu.VMEM((s + max_r * QBLK, LANES), BF16),
            pltpu.VMEM((s // QBLK + max_r, LANES, QBLK), BF16),
            pltpu.VMEM((2, 2 * QBLK, 2 * QBLK), F32),
            pltpu.VMEM((s // QBLK, 2 * QBLK, 2 * QBLK), BF16),
            pltpu.VMEM((s // QBLK, SUBLANES, 2 * QBLK), F32),
            nat, nat, nat, nat, nat, nat,
        ],
        compiler_params=_cparams(2),
        name="dilated_attention",
    )(proj, proj, proj, *tables, two(q_gain), two(k_gain))


def _swa_kernel(q_ref, k_ref, v_ref, c_ref, s_ref, qg_ref, kg_ref, sink_ref, o_ref,
                q_s, kd_s, vt_s, bias_s, p_s, m_s, rq_s, rk_s, *, seq):
    group = SWA_Q_HEADS // SWA_KV_HEADS
    swap = _swap_matrix(ROPE_DIMS // 2)
    head_ones = _head_ones()
    kv_in_hi = (pl.program_id(1) % 2) == 1
    blk_per_step = PRO_ROWS // QBLK
    src = lax.broadcasted_iota(jnp.int32, (LANES, LANES), 0)
    dst = lax.broadcasted_iota(jnp.int32, (LANES, LANES), 1) & (HEAD_DIM - 1)
    dup = jnp.where(src == dst + jnp.where(kv_in_hi, HEAD_DIM, 0), 1.0, 0.0).astype(BF16)

    def norms(n, carry):
        rows = pl.ds(pl.multiple_of(n * PRO_ROWS, PRO_ROWS), PRO_ROWS)
        for t in range(group // 2):
            cols = slice(t * LANES, (t + 1) * LANES)
            rq_s[rows, cols] = _head_rinv(q_ref[0, rows, cols].astype(F32), head_ones)
        rk_s[rows, :] = _head_rinv(k_ref[0, rows, :].astype(F32), head_ones)
        return carry

    lax.fori_loop(0, seq // PRO_ROWS, norms, 0, unroll=16)

    def prologue(n, carry):
        r0 = pl.multiple_of(n * PRO_ROWS, PRO_ROWS)
        rows = pl.ds(r0, PRO_ROWS)
        c, sn = c_ref[0, rows, :], s_ref[0, rows, :]
        for t in range(group // 2):
            cols = slice(t * LANES, (t + 1) * LANES)
            q = q_ref[0, rows, cols].astype(F32) * rq_s[rows, cols] * qg_ref[...]
            q_s[rows, cols] = (_rotate(q, c, sn, swap) * Q_SCALE).astype(BF16)
        k = k_ref[0, rows, :].astype(F32) * rk_s[rows, :] * kg_ref[...]
        k = _rotate(k, c, sn, swap).astype(BF16)
        kd_s[pl.ds(r0 + QBLK, PRO_ROWS), :] = jnp.dot(k, dup, preferred_element_type=F32).astype(BF16)
        v_t = v_ref[0, rows, :].astype(F32).T
        v_t = jnp.where(kv_in_hi, v_t[HEAD_DIM:LANES, :], v_t[0:HEAD_DIM, :]).astype(BF16)
        for u in range(blk_per_step):
            vt_s[n * blk_per_step + u + 1] = v_t[:, u * QBLK:(u + 1) * QBLK]
        return carry

    lax.fori_loop(0, seq // PRO_ROWS, prologue, 0, unroll=16)

    first, later = _band_bias_t(1, QBLK)
    bias_s[0] = jnp.concatenate([first] * group, axis=1)
    bias_s[1] = jnp.concatenate([later] * group, axis=1)
    kd_s[0:QBLK, :] = jnp.zeros((QBLK, LANES), BF16)
    vt_s[0] = jnp.zeros((HEAD_DIM, QBLK), BF16)
    lo = _lo_mask((QBLK, LANES))
    heads = (lo, jnp.logical_not(lo))
    sink = jnp.concatenate([sink_ref[0, hh:hh + 1, :] for hh in range(group)], axis=1) * LOG2_E
    ones_rows = jnp.ones((DEN_ROWS, 2 * QBLK), BF16)

    def scores(i, carry):
        d0 = pl.multiple_of(i * QBLK, QBLK)
        rows = pl.ds(d0, QBLK)
        kw = kd_s[pl.ds(d0, 2 * QBLK), :]
        stack = []
        for t in range(group // 2):
            qt = q_s[rows, t * LANES:(t + 1) * LANES]
            for head in heads:
                stack.append(jnp.where(head, qt, jnp.zeros_like(qt)))
        qm = jnp.concatenate(stack, axis=0)
        sc = lax.dot_general(kw, qm, (((1,), (1,)), ((), ())), preferred_element_type=F32)
        sc = sc + bias_s[jnp.minimum(i, 1)]
        m = jnp.maximum(jnp.max(sc, axis=0, keepdims=True), sink)
        p_s[i] = jnp.exp2(sc - m).astype(BF16)
        m_s[i] = jnp.broadcast_to(m, (SUBLANES, group * QBLK))
        return carry

    lax.fori_loop(0, seq // QBLK, scores, 0, unroll=BLOCK_UNROLL)

    def values(i, carry):
        rows = pl.ds(pl.multiple_of(i * QBLK, QBLK), QBLK)
        m = m_s[i][0:1, :]
        lhs = jnp.concatenate(
            [jnp.concatenate([vt_s[i], vt_s[i + 1]], axis=1), ones_rows], axis=0)
        ot = jnp.dot(lhs, p_s[i], preferred_element_type=F32)
        inv = 1.0 / (ot[HEAD_DIM:HEAD_DIM + 1, :] + jnp.exp2(sink - m))
        on = ot[0:HEAD_DIM, :] * inv
        for t in range(group // 2):
            pair_t = jnp.concatenate(
                [on[:, 2 * t * QBLK:(2 * t + 1) * QBLK], on[:, (2 * t + 1) * QBLK:(2 * t + 2) * QBLK]],
                axis=0)
            o_ref[0, rows, t * LANES:(t + 1) * LANES] = pair_t.T.astype(o_ref.dtype)
        return carry

    lax.fori_loop(0, seq // QBLK, values, 0, unroll=16)


def _swa(proj, tables, q_gain, k_gain, sinks):
    b, s, _ = proj.shape
    group = SWA_Q_HEADS // SWA_KV_HEADS
    q_w = group * HEAD_DIM
    k_t0 = SWA_Q_HEADS * HEAD_DIM // LANES
    v_t0 = k_t0 + SWA_KV_HEADS * HEAD_DIM // LANES
    tab = pl.BlockSpec((1, s, LANES), lambda bi, g: (bi, 0, 0), pipeline_mode=pl.Buffered(1))
    two = lambda g: jnp.concatenate([g, g]).reshape(1, LANES)
    sink_rows = jnp.broadcast_to(sinks.reshape(SWA_KV_HEADS, group, 1), (SWA_KV_HEADS, group, LANES))
    return pl.pallas_call(
        functools.partial(_swa_kernel, seq=s),
        grid=(b, SWA_KV_HEADS),
        in_specs=[
            pl.BlockSpec((1, s, q_w), lambda bi, g: (bi, 0, g)),
            pl.BlockSpec((1, s, LANES), lambda bi, g: (bi, 0, k_t0 + g // 2)),
            pl.BlockSpec((1, s, LANES), lambda bi, g: (bi, 0, v_t0 + g // 2)),
            tab, tab,
            _resident((1, LANES)),
            _resident((1, LANES)),
            pl.BlockSpec((1, group, LANES), lambda bi, g: (g, 0, 0)),
        ],
        out_specs=pl.BlockSpec((1, s, q_w), lambda bi, g: (bi, 0, g)),
        out_shape=jax.ShapeDtypeStruct((b, s, SWA_Q_HEADS * HEAD_DIM), BF16),
        scratch_shapes=[
            pltpu.VMEM((s, q_w), BF16),
            pltpu.VMEM((s + QBLK, LANES), BF16),
            pltpu.VMEM((s // QBLK + 1, HEAD_DIM, QBLK), BF16),
            pltpu.VMEM((2, 2 * QBLK, group * QBLK), F32),
            pltpu.VMEM((s // QBLK, 2 * QBLK, group * QBLK), BF16),
            pltpu.VMEM((s // QBLK, SUBLANES, group * QBLK), F32),
            pltpu.VMEM((s, q_w), F32),
            pltpu.VMEM((s, LANES), F32),
        ],
        compiler_params=_cparams(2),
        name="swa_attention",
    )(proj, proj, proj, *tables, two(q_gain), two(k_gain), sink_rows)


def kernel(x, positions, norm_mix, norm_mlp, mlp_w_up, mlp_w_down, hyb_w_in, hyb_w_out, ret_gn_gain, dil_q_gain, dil_k_gain, swa_w_qkv, swa_b_qkv, swa_w_out, swa_q_gain, swa_k_gain, swa_sinks):
    b, s, d = x.shape
    depth = norm_mix.shape[0]
    ret_tab = _rope_table(positions, RET_QK_DIM // 2, RET_THETA, RET_QK_DIM)
    rope_tab = _rope_table(positions, ROPE_DIMS // 2, ROPE_THETA, ROPE_DIMS)
    log_gamma = jnp.log1p(-jnp.exp2(-5.0 - jnp.arange(RET_HEADS, dtype=F32)))
    ret_w = RET_HEADS * RET_V_DIM
    dil_col0 = 2 * RET_HEADS * RET_QK_DIM + 2 * ret_w

    x2d = x.reshape(b * s, d)
    for layer in range(depth):
        i = layer // 2
        if layer % 2 == 0:
            w_in = hyb_w_in[i].astype(BF16)
            proj = _norm_proj(x2d, norm_mix[layer], w_in, jnp.zeros((w_in.shape[1],), F32))
            proj = proj.reshape(b, s, -1)
            ra = _retention(proj, ret_tab, log_gamma, ret_gn_gain[i])
            da = _dilated(proj, rope_tab, dil_q_gain[i], dil_k_gain[i], dil_col0)
            w_out = hyb_w_out[i].astype(BF16)
            mixed = [ra.reshape(b * s, -1), da.reshape(b * s, -1)]
        else:
            proj = _norm_proj(x2d, norm_mix[layer], swa_w_qkv[i].astype(BF16), swa_b_qkv[i])
            proj = proj.reshape(b, s, -1)
            att = _swa(proj, rope_tab, swa_q_gain[i], swa_k_gain[i], swa_sinks[i])
            mixed = [att.reshape(b * s, -1)]
            w_out = swa_w_out[i].astype(BF16)
        x2d = _out_mlp(mixed, x2d, w_out, norm_mlp[layer],
                       mlp_w_up[layer].astype(BF16), mlp_w_down[layer].astype(BF16))
    return x2d.reshape(b, s, d)
```

```python
import functools

import jax
import jax.numpy as jnp
import numpy as np
from jax import lax
from jax.experimental import pallas as pl
from jax.experimental.pallas import tpu as pltpu

F32 = jnp.float32
BF16 = jnp.bfloat16

D_MODEL = 1024
D_FF = 4 * D_MODEL
HEAD_DIM = 64
EPS = 1e-6
RET_HEADS = 4
RET_QK_DIM = 64
RET_V_DIM = 128
RET_CHUNK = 128
RET_THETA = 10000.0
DIL_HEADS = 8
DIL_PATTERNS = ((128, 1), (512, 4), (2048, 16))
SWA_Q_HEADS = 16
SWA_KV_HEADS = 4
SWA_WINDOW = 128
ROPE_THETA = 500000.0
ROPE_DIMS = HEAD_DIM // 4

LANES = 128
SUBLANES = 8
QBLK = 128
VMEM_LIMIT_BYTES = 56 * 1024 * 1024
NEG_BIG = -1e30
LOG2_E = 1.4426950408889634
Q_SCALE = HEAD_DIM ** -0.5 * LOG2_E

ROW_TILE = 512
MLP_ROW_TILE = 512
FF_CHUNK = 1024
PRO_ROWS = 128
BLOCK_UNROLL = 16
DEN_ROWS = 16
COARSE_R, FINE_R = DIL_PATTERNS[1][1], DIL_PATTERNS[2][1]


def _cparams(n_axes):
    return pltpu.CompilerParams(
        dimension_semantics=("arbitrary",) * n_axes,
        vmem_limit_bytes=VMEM_LIMIT_BYTES,
    )


def _resident(shape):
    nd = len(shape)
    return pl.BlockSpec(shape, lambda *_: (0,) * nd, pipeline_mode=pl.Buffered(1))


def _split3(x):
    hi = x.astype(BF16)
    r1 = x - hi.astype(F32)
    mid = r1.astype(BF16)
    lo = (r1 - mid.astype(F32)).astype(BF16)
    return hi, mid, lo


def _trig_kernel(p_ref, inv_ref, ec_ref, es_ref, base_ref, c_ref, s_ref, *, n_pos):
    ang = p_ref[...] * inv_ref[...]
    cos_parts = _split3(jnp.cos(ang))
    sin_parts = _split3(jnp.sin(ang))
    rows = p_ref.shape[0]
    for c in range(n_pos):
        dst = pl.ds(c, rows, stride=n_pos)
        c_ref[dst, :] = base_ref[...] + sum(
            jnp.dot(part, ec_ref[c], preferred_element_type=F32) for part in cos_parts)
        s_ref[dst, :] = sum(
            jnp.dot(part, es_ref[c], preferred_element_type=F32) for part in sin_parts)


def _selectors(half):
    n_pos = LANES // half
    ec = np.zeros((n_pos, LANES, LANES), np.float32)
    es = np.zeros((n_pos, LANES, LANES), np.float32)
    base = np.zeros((1, LANES), np.float32)
    for j in range(LANES):
        d = j % HEAD_DIM
        if d >= 2 * half:
            base[0, j] = 1.0
            continue
        f = d % half
        for c in range(n_pos):
            ec[c, c * half + f, j] = 1.0
            es[c, c * half + f, j] = -1.0 if d < half else 1.0
    return n_pos, jnp.asarray(ec, BF16), jnp.asarray(es, BF16), jnp.asarray(base)


def _rope_table(positions, half, theta, n_rot):
    b, s = positions.shape
    posf = positions.astype(F32)
    inv = jnp.power(jnp.float32(theta), -jnp.arange(half, dtype=F32) * (2.0 / n_rot))
    n_pos, ec, es, base = _selectors(half)
    p = jnp.broadcast_to(posf[..., None], (b, s, half)).reshape(-1, LANES)
    inv_rows = jnp.broadcast_to(jnp.tile(inv, n_pos)[None, :], p.shape)
    rows = p.shape[0]
    tile = min(rows, 2048 // n_pos)
    assert rows % tile == 0
    dense = pl.BlockSpec((tile, LANES), lambda i: (i, 0))
    wide = pl.BlockSpec((tile * n_pos, LANES), lambda i: (i, 0))
    c_tab, s_tab = pl.pallas_call(
        functools.partial(_trig_kernel, n_pos=n_pos),
        grid=(rows // tile,),
        in_specs=[dense, dense, _resident(ec.shape), _resident(es.shape), _resident(base.shape)],
        out_specs=[wide, wide],
        out_shape=[jax.ShapeDtypeStruct((rows * n_pos, LANES), F32)] * 2,
        compiler_params=_cparams(1),
        name="trig_tables",
    )(p, inv_rows, ec, es, base)
    return c_tab.reshape(b, s, LANES), s_tab.reshape(b, s, LANES)


def _swap_matrix(half):
    src = lax.broadcasted_iota(jnp.int32, (LANES, LANES), 0)
    dst = lax.broadcasted_iota(jnp.int32, (LANES, LANES), 1)
    d = dst & (HEAD_DIM - 1)
    want = jnp.where(d < half, dst + half, jnp.where(d < 2 * half, dst - half, -1))
    return jnp.where(src == want, 1.0, 0.0).astype(BF16)


def _rotate(x, c, s, swap):
    swapped = jnp.dot(x.astype(BF16), swap, preferred_element_type=F32)
    return x * c + swapped * s


def _lo_mask(shape):
    return lax.broadcasted_iota(jnp.int32, shape, len(shape) - 1) < HEAD_DIM


def _head_ones():
    r = lax.broadcasted_iota(jnp.int32, (LANES, LANES), 0) < HEAD_DIM
    c = lax.broadcasted_iota(jnp.int32, (LANES, LANES), 1) < HEAD_DIM
    return jnp.where(r == c, 1.0, 0.0).astype(BF16)


def _head_rinv(x, head_ones):
    ss = jnp.dot((x * x).astype(BF16), head_ones, preferred_element_type=F32)
    return lax.rsqrt(ss * (1.0 / HEAD_DIM) + EPS)


def _norm_proj_kernel(x_ref, g_ref, w_ref, b_ref, o_ref, *, n_chunk):
    x = x_ref[...]
    ms = jnp.mean(x * x, axis=-1, keepdims=True)
    h = (x * lax.rsqrt(ms + EPS) * g_ref[...]).astype(BF16)
    n = o_ref.shape[-1]
    for c in range(0, n, n_chunk):
        acc = jnp.dot(h, w_ref[:, c:c + n_chunk].astype(BF16), preferred_element_type=F32)
        o_ref[:, c:c + n_chunk] = (acc + b_ref[:, c:c + n_chunk]).astype(o_ref.dtype)


def _norm_proj(x2d, gain, w, bias):
    m, d = x2d.shape
    n = w.shape[1]
    return pl.pallas_call(
        functools.partial(_norm_proj_kernel, n_chunk=512),
        grid=(m // ROW_TILE,),
        in_specs=[
            pl.BlockSpec((ROW_TILE, d), lambda i: (i, 0)),
            _resident((1, d)),
            _resident((d, n)),
            _resident((1, n)),
        ],
        out_specs=pl.BlockSpec((ROW_TILE, n), lambda i: (i, 0)),
        out_shape=jax.ShapeDtypeStruct((m, n), BF16),
        compiler_params=_cparams(1),
        name="norm_proj",
    )(x2d, gain.reshape(1, d), w, bias.reshape(1, n))


def _out_mlp_kernel(*refs, n_mix):
    a_refs = refs[:n_mix]
    x_ref, wo_ref, g_ref, wup_ref, wdn_ref, o_ref = refs[n_mix:]
    mixed = a_refs[0][...] if n_mix == 1 else jnp.concatenate([a[...] for a in a_refs], axis=-1)
    x1 = x_ref[...] + jnp.dot(mixed, wo_ref[...].astype(BF16), preferred_element_type=F32)
    ms = jnp.mean(x1 * x1, axis=-1, keepdims=True)
    h = (x1 * lax.rsqrt(ms + EPS) * g_ref[...]).astype(BF16)
    y = x1
    for c in range(0, D_FF, FF_CHUNK):
        u = jnp.dot(h, wup_ref[:, c:c + FF_CHUNK].astype(BF16), preferred_element_type=F32)
        u = jnp.square(jnp.maximum(u, 0.0)).astype(BF16)
        y = y + jnp.dot(u, wdn_ref[c:c + FF_CHUNK, :].astype(BF16), preferred_element_type=F32)
    o_ref[...] = y


def _out_mlp(mixed, x2d, w_out, gain, w_up, w_down):
    m, d = x2d.shape
    n_mix = len(mixed)
    in_specs = [pl.BlockSpec((MLP_ROW_TILE, a.shape[1]), lambda i: (i, 0)) for a in mixed]
    in_specs.append(pl.BlockSpec((MLP_ROW_TILE, d), lambda i: (i, 0)))
    in_specs += [_resident(w_out.shape), _resident((1, d)), _resident(w_up.shape), _resident(w_down.shape)]
    return pl.pallas_call(
        functools.partial(_out_mlp_kernel, n_mix=n_mix),
        grid=(m // MLP_ROW_TILE,),
        in_specs=in_specs,
        out_specs=pl.BlockSpec((MLP_ROW_TILE, d), lambda i: (i, 0)),
        out_shape=jax.ShapeDtypeStruct((m, d), F32),
        compiler_params=_cparams(1),
        name="out_mlp",
    )(*mixed, x2d, w_out, gain.reshape(1, d), w_up, w_down)


def _retention_kernel(q_ref, k_ref, v_ref, g_ref, c_ref, s_ref, lg_ref, gn_ref,
                      o_ref, q_s, k_s, kt_s, o_s, kv_s, *, seq):
    n_chunks = seq // RET_CHUNK
    swap = _swap_matrix(RET_QK_DIM // 2)
    cs = RET_CHUNK

    def prologue(n, carry):
        r0 = pl.multiple_of(n * cs, cs)
        rows = pl.ds(r0, cs)
        c, sn = c_ref[0, rows, :], s_ref[0, rows, :]
        q = _rotate(q_ref[0, rows, :].astype(F32), c, sn, swap)
        k = _rotate(k_ref[0, rows, :].astype(F32), c, sn, swap) * (RET_QK_DIM ** -0.5)
        q_s[n] = q.astype(BF16)
        k_s[n] = k.astype(BF16)
        kt_s[n] = k.T
        return carry

    lax.fori_loop(0, n_chunks, prologue, 0, unroll=16)

    lo = _lo_mask((cs, LANES))
    row = lax.broadcasted_iota(jnp.int32, (cs, cs), 0).astype(F32)
    col = lax.broadcasted_iota(jnp.int32, (cs, cs), 1).astype(F32)
    diff = row - col
    consts = []
    for hh in range(2):
        lg = lg_ref[0, hh:hh + 1, :]
        decay = jnp.where(diff >= 0, jnp.exp(lg * jnp.maximum(diff, 0.0)), 0.0)
        xi = jnp.exp(lg * (row + 1.0))
        zeta = jnp.exp(lg * (cs - 1.0 - col[0:1, :]))
        cd = jnp.exp(lg * float(cs))
        head = lo if hh == 0 else jnp.logical_not(lo)
        consts.append((decay, xi, zeta, cd, head))

    def inner(n, carry):
        rows = pl.ds(pl.multiple_of(n * cs, cs), cs)
        qc, kc, ktc = q_s[n], k_s[n], kt_s[n]
        for hh in range(2):
            decay, xi, zeta, cd, head = consts[hh]
            cols = slice(hh * RET_V_DIM, (hh + 1) * RET_V_DIM)
            qm = jnp.where(head, qc, jnp.zeros_like(qc))
            vc = v_ref[0, rows, cols]
            sc = lax.dot_general(qm, kc, (((1,), (1,)), ((), ())), preferred_element_type=F32) * decay
            o_s[rows, cols] = jnp.dot(sc.astype(BF16), vc, preferred_element_type=F32)
            kz = (ktc * zeta).astype(BF16)
            kv_s[n, hh] = jnp.dot(kz, vc, preferred_element_type=F32)
        return carry

    lax.fori_loop(0, n_chunks, inner, 0, unroll=16)

    def cross(n, state):
        rows = pl.ds(pl.multiple_of(n * cs, cs), cs)
        qc = q_s[n]
        new_state = []
        for hh in range(2):
            decay, xi, zeta, cd, head = consts[hh]
            cols = slice(hh * RET_V_DIM, (hh + 1) * RET_V_DIM)
            r_prev = state[hh]
            qm = jnp.where(head, qc, jnp.zeros_like(qc))
            o_s[rows, cols] += jnp.dot(qm, r_prev.astype(BF16), preferred_element_type=F32) * xi
            new_state.append(r_prev * cd + kv_s[n, hh])
        return tuple(new_state)

    zero = jnp.zeros((LANES, RET_V_DIM), F32)
    lax.fori_loop(0, n_chunks, cross, (zero, zero), unroll=16)

    def finish(n, carry):
        rows = pl.ds(pl.multiple_of(n * cs, cs), cs)
        for hh in range(2):
            cols = slice(hh * RET_V_DIM, (hh + 1) * RET_V_DIM)
            o = o_s[rows, cols]
            mu = jnp.mean(o, axis=-1, keepdims=True)
            dev = o - mu
            var = jnp.mean(dev * dev, axis=-1, keepdims=True)
            y = dev * lax.rsqrt(var + EPS) * gn_ref[0, hh:hh + 1, :]
            gate = g_ref[0, rows, cols].astype(F32)
            o_ref[0, rows, cols] = (gate * jax.nn.sigmoid(gate) * y).astype(o_ref.dtype)
        return carry

    lax.fori_loop(0, n_chunks, finish, 0, unroll=2)


def _retention(proj, tables, log_gamma, gn_gain):
    b, s, _ = proj.shape
    n_pairs = RET_HEADS // 2
    pair_w = 2 * RET_V_DIM
    qk_tiles = RET_HEADS * RET_QK_DIM // LANES
    v_off = 2 * qk_tiles * LANES // pair_w
    g_off = v_off + RET_HEADS * RET_V_DIM // pair_w
    tab = pl.BlockSpec((1, s, LANES), lambda bi, p: (bi, 0, 0), pipeline_mode=pl.Buffered(1))
    lg = jnp.broadcast_to(log_gamma.reshape(n_pairs, 2, 1), (n_pairs, 2, LANES))
    n_chunks = s // RET_CHUNK
    return pl.pallas_call(
        functools.partial(_retention_kernel, seq=s),
        grid=(b, n_pairs),
        in_specs=[
            pl.BlockSpec((1, s, LANES), lambda bi, p: (bi, 0, p)),
            pl.BlockSpec((1, s, LANES), lambda bi, p: (bi, 0, qk_tiles + p)),
            pl.BlockSpec((1, s, pair_w), lambda bi, p: (bi, 0, v_off + p)),
            pl.BlockSpec((1, s, pair_w), lambda bi, p: (bi, 0, g_off + p)),
            tab, tab,
            pl.BlockSpec((1, 2, LANES), lambda bi, p: (p, 0, 0)),
            pl.BlockSpec((1, 2, RET_V_DIM), lambda bi, p: (p, 0, 0)),
        ],
        out_specs=pl.BlockSpec((1, s, pair_w), lambda bi, p: (bi, 0, p)),
        out_shape=jax.ShapeDtypeStruct((b, s, RET_HEADS * RET_V_DIM), BF16),
        scratch_shapes=[
            pltpu.VMEM((n_chunks, RET_CHUNK, LANES), BF16),
            pltpu.VMEM((n_chunks, RET_CHUNK, LANES), BF16),
            pltpu.VMEM((n_chunks, LANES, RET_CHUNK), F32),
            pltpu.VMEM((s, pair_w), F32),
            pltpu.VMEM((n_chunks, 2, LANES, RET_V_DIM), F32),
        ],
        compiler_params=_cparams(2),
        name="retention",
    )(proj, proj, proj, proj, *tables, lg, gn_gain.reshape(n_pairs, 2, RET_V_DIM))


def _band_bias_t(lo_off, hi_off):
    c = lax.broadcasted_iota(jnp.int32, (2 * QBLK, QBLK), 0)
    a = lax.broadcasted_iota(jnp.int32, (2 * QBLK, QBLK), 1)
    band = (c - a >= lo_off) & (c - a <= hi_off)
    later = jnp.where(band, 0.0, NEG_BIG).astype(F32)
    first = jnp.where(band & (c >= QBLK), 0.0, NEG_BIG).astype(F32)
    return first, later


def _dilated_kernel(q_ref, k_ref, v_ref, c_ref, s_ref, qg_ref, kg_ref, o_ref,
                    qn_s, kn_s, vn_s, q4_s, k4_s, v4_s, qt_s, kc_s, vt_s, bias_s, p_s, m_s,
                    o0_s, o1_s, o2_s, l0_s, l1_s, l2_s, *, seq):
    swap = _swap_matrix(ROPE_DIMS // 2)
    head_ones = _head_ones()
    rq_s, rk_s = o0_s, l0_s

    def norms(n, carry):
        rows = pl.ds(pl.multiple_of(n * PRO_ROWS, PRO_ROWS), PRO_ROWS)
        rq_s[rows, :] = _head_rinv(q_ref[0, rows, :].astype(F32), head_ones)
        rk_s[rows, :] = _head_rinv(k_ref[0, rows, :].astype(F32), head_ones)
        return carry

    lax.fori_loop(0, seq // PRO_ROWS, norms, 0, unroll=16)

    def prologue(n, carry):
        rows = pl.ds(pl.multiple_of(n * PRO_ROWS, PRO_ROWS), PRO_ROWS)
        c, sn = c_ref[0, rows, :], s_ref[0, rows, :]
        q = q_ref[0, rows, :].astype(F32) * rq_s[rows, :] * qg_ref[...]
        qn_s[rows, :] = _rotate(q, c, sn, swap) * Q_SCALE
        k = k_ref[0, rows, :].astype(F32) * rk_s[rows, :] * kg_ref[...]
        kn_s[rows, :] = _rotate(k, c, sn, swap)
        vn_s[rows, :] = v_ref[0, rows, :].astype(F32)
        return carry

    lax.fori_loop(0, seq // PRO_ROWS, prologue, 0, unroll=16)

    first, later = _band_bias_t(0, QBLK)
    bias_s[0] = jnp.concatenate([first, first], axis=1)
    bias_s[1] = jnp.concatenate([later, later], axis=1)
    zero_pad = jnp.zeros((QBLK, LANES), BF16)
    top = lax.broadcasted_iota(jnp.int32, (LANES, QBLK), 0) < HEAD_DIM
    ones_rows = jnp.ones((DEN_ROWS, 2 * QBLK), BF16)

    outs = (o0_s, o1_s, o2_s)
    lses = (l0_s, l1_s, l2_s)
    n_flat = seq // QBLK
    for (window, r), on_s, ln_s in zip(DIL_PATTERNS, outs, lses):
        assert window // r == QBLK
        n_blk = seq // r // QBLK
        blk_shift = n_blk.bit_length() - 1
        assert n_blk == 1 << blk_shift

        def pad(j, c2, n_blk=n_blk):
            z = j * (n_blk + 1)
            kc_s[pl.ds(pl.multiple_of(z * QBLK, QBLK), QBLK), :] = zero_pad
            vt_s[z] = zero_pad
            return c2

        lax.fori_loop(0, r, pad, 0)

        def split(idx, r=r, n_blk=n_blk, blk_shift=blk_shift):
            j = lax.shift_right_logical(idx, blk_shift)
            i = idx & (n_blk - 1)
            return j, i

        def gather(idx, c2, r=r, split=split):
            j, i = split(idx)
            if r == FINE_R:
                base = ((j & (COARSE_R - 1)) * (seq // COARSE_R)
                        + lax.shift_right_logical(j, COARSE_R.bit_length() - 1))
                src = pl.ds(base + (QBLK * r // COARSE_R) * i, QBLK, stride=r // COARSE_R)
                q, k, v = q4_s[src, :], k4_s[src, :], v4_s[src, :]
            else:
                src = pl.ds(j + r * QBLK * i, QBLK, stride=r)
                q, k, v = qn_s[src, :], kn_s[src, :], vn_s[src, :]
            if r == COARSE_R:
                dense = pl.ds(pl.multiple_of(idx * QBLK, QBLK), QBLK)
                q4_s[dense, :] = q
                k4_s[dense, :] = k
                v4_s[dense, :] = v
            k0 = pl.multiple_of((idx + j + 1) * QBLK, QBLK)
            qt_s[idx] = q.T.astype(BF16)
            kc_s[pl.ds(k0, QBLK), :] = k.astype(BF16)
            vt_s[idx + j + 1] = v.T.astype(BF16)
            return c2

        lax.fori_loop(0, n_flat, gather, 0, unroll=16)

        def scores(idx, c2, split=split):
            j, i = split(idx)
            k0 = pl.multiple_of((idx + j) * QBLK, QBLK)
            qt = qt_s[idx]
            kw = kc_s[pl.ds(k0, 2 * QBLK), :]
            zero = jnp.zeros_like(qt)
            rhs = jnp.concatenate([jnp.where(top, qt, zero), jnp.where(top, zero, qt)], axis=1)
            sc = jnp.dot(kw, rhs, preferred_element_type=F32) + bias_s[jnp.minimum(i, 1)]
            m = jnp.max(sc, axis=0, keepdims=True)
            p_s[idx] = jnp.exp2(sc - m).astype(BF16)
            m_s[idx] = jnp.broadcast_to(m, (SUBLANES, 2 * QBLK))
            return c2

        lax.fori_loop(0, n_flat, scores, 0, unroll=16)

        def block(idx, c2, r=r, split=split, on_s=on_s, ln_s=ln_s):
            j, i = split(idx)
            p = p_s[idx]
            m = m_s[idx][0:1, :]
            lhs = jnp.concatenate(
                [jnp.concatenate([vt_s[idx + j], vt_s[idx + j + 1]], axis=1), ones_rows], axis=0)
            ot = jnp.dot(lhs, p, preferred_element_type=F32)
            den = ot[LANES:LANES + 1, :]
            inv = 1.0 / den
            lse = m + jnp.log2(den)
            o_t = jnp.concatenate(
                [ot[0:HEAD_DIM, 0:QBLK] * inv[:, 0:QBLK],
                 ot[HEAD_DIM:LANES, QBLK:2 * QBLK] * inv[:, QBLK:2 * QBLK]], axis=0)
            l_t = jnp.concatenate(
                [jnp.broadcast_to(lse[:, 0:QBLK], (HEAD_DIM, QBLK)),
                 jnp.broadcast_to(lse[:, QBLK:2 * QBLK], (HEAD_DIM, QBLK))], axis=0)
            dst = pl.ds(j + r * QBLK * i, QBLK, stride=r)
            on_s[dst, :] = o_t.T
            ln_s[dst, :] = l_t.T
            return c2

        lax.fori_loop(0, n_flat, block, 0, unroll=16)

    def combine(n, carry):
        r0 = pl.multiple_of(n * PRO_ROWS, PRO_ROWS)
        rows = pl.ds(r0, PRO_ROWS)
        ls = [l_s[rows, :] for l_s in lses]
        m = jnp.maximum(jnp.maximum(ls[0], ls[1]), ls[2])
        es = [jnp.exp2(l - m) for l in ls]
        num = es[0] * o0_s[rows, :] + es[1] * o1_s[rows, :] + es[2] * o2_s[rows, :]
        o_ref[0, rows, :] = (num / (es[0] + es[1] + es[2])).astype(o_ref.dtype)
        return carry

    lax.fori_loop(0, seq // PRO_ROWS, combine, 0, unroll=8)


def _dilated(proj, tables, q_gain, k_gain, col0):
    b, s, _ = proj.shape
    n_pairs = DIL_HEADS * HEAD_DIM // LANES
    t0 = col0 // LANES
    tab = pl.BlockSpec((1, s, LANES), lambda bi, p: (bi, 0, 0), pipeline_mode=pl.Buffered(1))
    two = lambda g: jnp.concatenate([g, g]).reshape(1, LANES)
    nat = pltpu.VMEM((s, LANES), F32)
    max_r = max(r for _, r in DIL_PATTERNS)
    return pl.pallas_call(
        functools.partial(_dilated_kernel, seq=s),
        grid=(b, n_pairs),
        in_specs=[
            pl.BlockSpec((1, s, LANES), lambda bi, p: (bi, 0, t0 + p)),
            pl.BlockSpec((1, s, LANES), lambda bi, p: (bi, 0, t0 + n_pairs + p)),
            pl.BlockSpec((1, s, LANES), lambda bi, p: (bi, 0, t0 + 2 * n_pairs + p)),
            tab, tab,
            _resident((1, LANES)),
            _resident((1, LANES)),
        ],
        out_specs=pl.BlockSpec((1, s, LANES), lambda bi, p: (bi, 0, p)),
        out_shape=jax.ShapeDtypeStruct((b, s, DIL_HEADS * HEAD_DIM), BF16),
        scratch_shapes=[
            nat, nat, nat, nat, nat, nat,
            pltpu.VMEM((s // QBLK, LANES, QBLK), BF16),
            pltpu.VMEM((s + max_r * QBLK, LANES), BF16),
            pltpu.VMEM((s // QBLK + max_r, LANES, QBLK), BF16),
            pltpu.VMEM((2, 2 * QBLK, 2 * QBLK), F32),
            pltpu.VMEM((s // QBLK, 2 * QBLK, 2 * QBLK), BF16),
            pltpu.VMEM((s // QBLK, SUBLANES, 2 * QBLK), F32),
            nat, nat, nat, nat, nat, nat,
        ],
        compiler_params=_cparams(2),
        name="dilated_attention",
    )(proj, proj, proj, *tables, two(q_gain), two(k_gain))


def _swa_kernel(q_ref, k_ref, v_ref, c_ref, s_ref, qg_ref, kg_ref, sink_ref, o_ref,
                q_s, kd_s, vt_s, bias_s, p_s, m_s, rq_s, rk_s, *, seq):
    group = SWA_Q_HEADS // SWA_KV_HEADS
    swap = _swap_matrix(ROPE_DIMS // 2)
    head_ones = _head_ones()
    kv_in_hi = (pl.program_id(1) % 2) == 1
    blk_per_step = PRO_ROWS // QBLK
    src = lax.broadcasted_iota(jnp.int32, (LANES, LANES), 0)
    dst = lax.broadcasted_iota(jnp.int32, (LANES, LANES), 1) & (HEAD_DIM - 1)
    dup = jnp.where(src == dst + jnp.where(kv_in_hi, HEAD_DIM, 0), 1.0, 0.0).astype(BF16)

    def norms(n, carry):
        rows = pl.ds(pl.multiple_of(n * PRO_ROWS, PRO_ROWS), PRO_ROWS)
        for t in range(group // 2):
            cols = slice(t * LANES, (t + 1) * LANES)
            rq_s[rows, cols] = _head_rinv(q_ref[0, rows, cols].astype(F32), head_ones)
        rk_s[rows, :] = _head_rinv(k_ref[0, rows, :].astype(F32), head_ones)
        return carry

    lax.fori_loop(0, seq // PRO_ROWS, norms, 0, unroll=16)

    def prologue(n, carry):
        r0 = pl.multiple_of(n * PRO_ROWS, PRO_ROWS)
        rows = pl.ds(r0, PRO_ROWS)
        c, sn = c_ref[0, rows, :], s_ref[0, rows, :]
        for t in range(group // 2):
            cols = slice(t * LANES, (t + 1) * LANES)
            q = q_ref[0, rows, cols].astype(F32) * rq_s[rows, cols] * qg_ref[...]
            q_s[rows, cols] = (_rotate(q, c, sn, swap) * Q_SCALE).astype(BF16)
        k = k_ref[0, rows, :].astype(F32) * rk_s[rows, :] * kg_ref[...]
        k = _rotate(k, c, sn, swap).astype(BF16)
        kd_s[pl.ds(r0 + QBLK, PRO_ROWS), :] = jnp.dot(k, dup, preferred_element_type=F32).astype(BF16)
        v_t = v_ref[0, rows, :].astype(F32).T
        v_t = jnp.where(kv_in_hi, v_t[HEAD_DIM:LANES, :], v_t[0:HEAD_DIM, :]).astype(BF16)
        for u in range(blk_per_step):
            vt_s[n * blk_per_step + u + 1] = v_t[:, u * QBLK:(u + 1) * QBLK]
        return carry

    lax.fori_loop(0, seq // PRO_ROWS, prologue, 0, unroll=16)

    first, later = _band_bias_t(1, QBLK)
    bias_s[0] = jnp.concatenate([first] * group, axis=1)
    bias_s[1] = jnp.concatenate([later] * group, axis=1)
    kd_s[0:QBLK, :] = jnp.zeros((QBLK, LANES), BF16)
    vt_s[0] = jnp.zeros((HEAD_DIM, QBLK), BF16)
    lo = _lo_mask((QBLK, LANES))
    heads = (lo, jnp.logical_not(lo))
    sink = jnp.concatenate([sink_ref[0, hh:hh + 1, :] for hh in range(group)], axis=1) * LOG2_E
    ones_rows = jnp.ones((DEN_ROWS, 2 * QBLK), BF16)

    def scores(i, carry):
        d0 = pl.multiple_of(i * QBLK, QBLK)
        rows = pl.ds(d0, QBLK)
        kw = kd_s[pl.ds(d0, 2 * QBLK), :]
        stack = []
        for t in range(group // 2):
            qt = q_s[rows, t * LANES:(t + 1) * LANES]
            for head in heads:
                stack.append(jnp.where(head, qt, jnp.zeros_like(qt)))
        qm = jnp.concatenate(stack, axis=0)
        sc = lax.dot_general(kw, qm, (((1,), (1,)), ((), ())), preferred_element_type=F32)
        sc = sc + bias_s[jnp.minimum(i, 1)]
        m = jnp.maximum(jnp.max(sc, axis=0, keepdims=True), sink)
        p_s[i] = jnp.exp2(sc - m).astype(BF16)
        m_s[i] = jnp.broadcast_to(m, (SUBLANES, group * QBLK))
        return carry

    lax.fori_loop(0, seq // QBLK, scores, 0, unroll=BLOCK_UNROLL)

    def values(i, carry):
        rows = pl.ds(pl.multiple_of(i * QBLK, QBLK), QBLK)
        m = m_s[i][0:1, :]
        lhs = jnp.concatenate(
            [jnp.concatenate([vt_s[i], vt_s[i + 1]], axis=1), ones_rows], axis=0)
        ot = jnp.dot(lhs, p_s[i], preferred_element_type=F32)
        inv = 1.0 / (ot[HEAD_DIM:HEAD_DIM + 1, :] + jnp.exp2(sink - m))
        on = ot[0:HEAD_DIM, :] * inv
        for t in range(group // 2):
            pair_t = jnp.concatenate(
                [on[:, 2 * t * QBLK:(2 * t + 1) * QBLK], on[:, (2 * t + 1) * QBLK:(2 * t + 2) * QBLK]],
                axis=0)
            o_ref[0, rows, t * LANES:(t + 1) * LANES] = pair_t.T.astype(o_ref.dtype)
        return carry

    lax.fori_loop(0, seq // QBLK, values, 0, unroll=16)


def _swa(proj, tables, q_gain, k_gain, sinks):
    b, s, _ = proj.shape
    group = SWA_Q_HEADS // SWA_KV_HEADS
    q_w = group * HEAD_DIM
    k_t0 = SWA_Q_HEADS * HEAD_DIM // LANES
    v_t0 = k_t0 + SWA_KV_HEADS * HEAD_DIM // LANES
    tab = pl.BlockSpec((1, s, LANES), lambda bi, g: (bi, 0, 0), pipeline_mode=pl.Buffered(1))
    two = lambda g: jnp.concatenate([g, g]).reshape(1, LANES)
    sink_rows = jnp.broadcast_to(sinks.reshape(SWA_KV_HEADS, group, 1), (SWA_KV_HEADS, group, LANES))
    return pl.pallas_call(
        functools.partial(_swa_kernel, seq=s),
        grid=(b, SWA_KV_HEADS),
        in_specs=[
            pl.BlockSpec((1, s, q_w), lambda bi, g: (bi, 0, g)),
            pl.BlockSpec((1, s, LANES), lambda bi, g: (bi, 0, k_t0 + g // 2)),
            pl.BlockSpec((1, s, LANES), lambda bi, g: (bi, 0, v_t0 + g // 2)),
            tab, tab,
            _resident((1, LANES)),
            _resident((1, LANES)),
            pl.BlockSpec((1, group, LANES), lambda bi, g: (g, 0, 0)),
        ],
        out_specs=pl.BlockSpec((1, s, q_w), lambda bi, g: (bi, 0, g)),
        out_shape=jax.ShapeDtypeStruct((b, s, SWA_Q_HEADS * HEAD_DIM), BF16),
        scratch_shapes=[
            pltpu.VMEM((s, q_w), BF16),
            pltpu.VMEM((s + QBLK, LANES), BF16),
            pltpu.VMEM((s // QBLK + 1, HEAD_DIM, QBLK), BF16),
            pltpu.VMEM((2, 2 * QBLK, group * QBLK), F32),
            pltpu.VMEM((s // QBLK, 2 * QBLK, group * QBLK), BF16),
            pltpu.VMEM((s // QBLK, SUBLANES, group * QBLK), F32),
            pltpu.VMEM((s, q_w), F32),
            pltpu.VMEM((s, LANES), F32),
        ],
        compiler_params=_cparams(2),
        name="swa_attention",
    )(proj, proj, proj, *tables, two(q_gain), two(k_gain), sink_rows)


def kernel(x, positions, norm_mix, norm_mlp, mlp_w_up, mlp_w_down, hyb_w_in, hyb_w_out, ret_gn_gain, dil_q_gain, dil_k_gain, swa_w_qkv, swa_b_qkv, swa_w_out, swa_q_gain, swa_k_gain, swa_sinks):
    b, s, d = x.shape
    depth = norm_mix.shape[0]
    ret_tab = _rope_table(positions, RET_QK_DIM // 2, RET_THETA, RET_QK_DIM)
    rope_tab = _rope_table(positions, ROPE_DIMS // 2, ROPE_THETA, ROPE_DIMS)
    log_gamma = jnp.log1p(-jnp.exp2(-5.0 - jnp.arange(RET_HEADS, dtype=F32)))
    ret_w = RET_HEADS * RET_V_DIM
    dil_col0 = 2 * RET_HEADS * RET_QK_DIM + 2 * ret_w

    x2d = x.reshape(b * s, d)
    for layer in range(depth):
        i = layer // 2
        if layer % 2 == 0:
            w_in = hyb_w_in[i]
            proj = _norm_proj(x2d, norm_mix[layer], w_in, jnp.zeros((w_in.shape[1],), F32))
            proj = proj.reshape(b, s, -1)
            ra = _retention(proj, ret_tab, log_gamma, ret_gn_gain[i])
            da = _dilated(proj, rope_tab, dil_q_gain[i], dil_k_gain[i], dil_col0)
            w_out = hyb_w_out[i]
            mixed = [ra.reshape(b * s, -1), da.reshape(b * s, -1)]
        else:
            proj = _norm_proj(x2d, norm_mix[layer], swa_w_qkv[i], swa_b_qkv[i])
            proj = proj.reshape(b, s, -1)
            att = _swa(proj, rope_tab, swa_q_gain[i], swa_k_gain[i], swa_sinks[i])
            mixed = [att.reshape(b * s, -1)]
            w_out = swa_w_out[i]
        x2d = _out_mlp(mixed, x2d, w_out, norm_mlp[layer],
                       mlp_w_up[layer], mlp_w_down[layer])
    return x2d.reshape(b, s, d)
```

```python
import functools

import jax
import jax.numpy as jnp
import numpy as np
from jax import lax
from jax.experimental import pallas as pl
from jax.experimental.pallas import tpu as pltpu

F32 = jnp.float32
BF16 = jnp.bfloat16

D_MODEL = 1024
D_FF = 4 * D_MODEL
HEAD_DIM = 64
EPS = 1e-6
RET_HEADS = 4
RET_QK_DIM = 64
RET_V_DIM = 128
RET_CHUNK = 128
RET_THETA = 10000.0
DIL_HEADS = 8
DIL_PATTERNS = ((128, 1), (512, 4), (2048, 16))
SWA_Q_HEADS = 16
SWA_KV_HEADS = 4
SWA_WINDOW = 128
ROPE_THETA = 500000.0
ROPE_DIMS = HEAD_DIM // 4

LANES = 128
SUBLANES = 8
QBLK = 128
VMEM_LIMIT_BYTES = 56 * 1024 * 1024
NEG_BIG = -1e30
LOG2_E = 1.4426950408889634
Q_SCALE = HEAD_DIM ** -0.5 * LOG2_E

ROW_TILE = 512
MLP_ROW_TILE = 512
FF_CHUNK = 1024
PRO_ROWS = 128
BLOCK_UNROLL = 16
DEN_ROWS = 16
COARSE_R, FINE_R = DIL_PATTERNS[1][1], DIL_PATTERNS[2][1]


def _cparams(n_axes):
    return pltpu.CompilerParams(
        dimension_semantics=("arbitrary",) * n_axes,
        vmem_limit_bytes=VMEM_LIMIT_BYTES,
    )


def _resident(shape):
    nd = len(shape)
    return pl.BlockSpec(shape, lambda *_: (0,) * nd, pipeline_mode=pl.Buffered(1))


def _resident_layer(stack_shape, layer):
    return pl.BlockSpec((None,) + tuple(stack_shape[1:]), lambda *_: (layer, 0, 0),
                        pipeline_mode=pl.Buffered(1))


def _split3(x):
    hi = x.astype(BF16)
    r1 = x - hi.astype(F32)
    mid = r1.astype(BF16)
    lo = (r1 - mid.astype(F32)).astype(BF16)
    return hi, mid, lo


def _trig_kernel(p_ref, inv_ref, ec_ref, es_ref, base_ref, c_ref, s_ref, *, n_pos):
    ang = p_ref[...] * inv_ref[...]
    cos_parts = _split3(jnp.cos(ang))
    sin_parts = _split3(jnp.sin(ang))
    rows = p_ref.shape[0]
    for c in range(n_pos):
        dst = pl.ds(c, rows, stride=n_pos)
        c_ref[dst, :] = base_ref[...] + sum(
            jnp.dot(part, ec_ref[c], preferred_element_type=F32) for part in cos_parts)
        s_ref[dst, :] = sum(
            jnp.dot(part, es_ref[c], preferred_element_type=F32) for part in sin_parts)


def _selectors(half):
    n_pos = LANES // half
    ec = np.zeros((n_pos, LANES, LANES), np.float32)
    es = np.zeros((n_pos, LANES, LANES), np.float32)
    base = np.zeros((1, LANES), np.float32)
    for j in range(LANES):
        d = j % HEAD_DIM
        if d >= 2 * half:
            base[0, j] = 1.0
            continue
        f = d % half
        for c in range(n_pos):
            ec[c, c * half + f, j] = 1.0
            es[c, c * half + f, j] = -1.0 if d < half else 1.0
    return n_pos, jnp.asarray(ec, BF16), jnp.asarray(es, BF16), jnp.asarray(base)


def _rope_table(positions, half, theta, n_rot):
    b, s = positions.shape
    posf = positions.astype(F32)
    inv = jnp.power(jnp.float32(theta), -jnp.arange(half, dtype=F32) * (2.0 / n_rot))
    n_pos, ec, es, base = _selectors(half)
    rows = s // n_pos
    p = jnp.broadcast_to(posf[..., None], (b, s, half)).reshape(b, rows, LANES)
    inv_row = jnp.tile(inv, n_pos)[None, :]
    tile = min(rows, 2048 // n_pos)
    assert rows % tile == 0
    dense = pl.BlockSpec((None, tile, LANES), lambda bi, i: (bi, i, 0))
    wide = pl.BlockSpec((None, tile * n_pos, LANES), lambda bi, i: (bi, i, 0))
    c_tab, s_tab = pl.pallas_call(
        functools.partial(_trig_kernel, n_pos=n_pos),
        grid=(b, rows // tile),
        in_specs=[dense, _resident(inv_row.shape), _resident(ec.shape), _resident(es.shape),
                  _resident(base.shape)],
        out_specs=[wide, wide],
        out_shape=[jax.ShapeDtypeStruct((b, s, LANES), F32)] * 2,
        compiler_params=_cparams(2),
        name="trig_tables",
    )(p, inv_row, ec, es, base)
    return c_tab, s_tab


def _swap_matrix(half):
    src = lax.broadcasted_iota(jnp.int32, (LANES, LANES), 0)
    dst = lax.broadcasted_iota(jnp.int32, (LANES, LANES), 1)
    d = dst & (HEAD_DIM - 1)
    want = jnp.where(d < half, dst + half, jnp.where(d < 2 * half, dst - half, -1))
    return jnp.where(src == want, 1.0, 0.0).astype(BF16)


def _rotate(x, c, s, swap):
    swapped = jnp.dot(x.astype(BF16), swap, preferred_element_type=F32)
    return x * c + swapped * s


def _lo_mask(shape):
    return lax.broadcasted_iota(jnp.int32, shape, len(shape) - 1) < HEAD_DIM


def _head_ones():
    r = lax.broadcasted_iota(jnp.int32, (LANES, LANES), 0) < HEAD_DIM
    c = lax.broadcasted_iota(jnp.int32, (LANES, LANES), 1) < HEAD_DIM
    return jnp.where(r == c, 1.0, 0.0).astype(BF16)


def _head_rinv(x, head_ones):
    ss = jnp.dot((x * x).astype(BF16), head_ones, preferred_element_type=F32)
    return lax.rsqrt(ss * (1.0 / HEAD_DIM) + EPS)


def _norm_proj_kernel(x_ref, g_ref, w_ref, b_ref, o_ref, *, n_chunk):
    x = x_ref[...]
    ms = jnp.mean(x * x, axis=-1, keepdims=True)
    h = (x * lax.rsqrt(ms + EPS) * g_ref[...]).astype(BF16)
    n = o_ref.shape[-1]
    for c in range(0, n, n_chunk):
        acc = jnp.dot(h, w_ref[:, c:c + n_chunk].astype(BF16), preferred_element_type=F32)
        o_ref[:, c:c + n_chunk] = (acc + b_ref[:, c:c + n_chunk]).astype(o_ref.dtype)


def _norm_proj(x2d, gain, w_stack, layer, bias):
    m, d = x2d.shape
    n = w_stack.shape[2]
    return pl.pallas_call(
        functools.partial(_norm_proj_kernel, n_chunk=512),
        grid=(m // ROW_TILE,),
        in_specs=[
            pl.BlockSpec((ROW_TILE, d), lambda i: (i, 0)),
            _resident((1, d)),
            _resident_layer(w_stack.shape, layer),
            _resident((1, n)),
        ],
        out_specs=pl.BlockSpec((ROW_TILE, n), lambda i: (i, 0)),
        out_shape=jax.ShapeDtypeStruct((m, n), BF16),
        compiler_params=_cparams(1),
        name="norm_proj",
    )(x2d, gain.reshape(1, d), w_stack, bias.reshape(1, n))


def _out_mlp_kernel(*refs, n_mix):
    a_refs = refs[:n_mix]
    x_ref, wo_ref, g_ref, wup_ref, wdn_ref, o_ref = refs[n_mix:]
    mixed = a_refs[0][...] if n_mix == 1 else jnp.concatenate([a[...] for a in a_refs], axis=-1)
    x1 = x_ref[...] + jnp.dot(mixed, wo_ref[...].astype(BF16), preferred_element_type=F32)
    ms = jnp.mean(x1 * x1, axis=-1, keepdims=True)
    h = (x1 * lax.rsqrt(ms + EPS) * g_ref[...]).astype(BF16)
    y = x1
    for c in range(0, D_FF, FF_CHUNK):
        u = jnp.dot(h, wup_ref[:, c:c + FF_CHUNK].astype(BF16), preferred_element_type=F32)
        u = jnp.square(jnp.maximum(u, 0.0)).astype(BF16)
        y = y + jnp.dot(u, wdn_ref[c:c + FF_CHUNK, :].astype(BF16), preferred_element_type=F32)
    o_ref[...] = y


def _out_mlp(mixed, x2d, w_out, out_layer, gain, w_up, w_down, mlp_layer):
    m, d = x2d.shape
    n_mix = len(mixed)
    in_specs = [pl.BlockSpec((MLP_ROW_TILE, a.shape[1]), lambda i: (i, 0)) for a in mixed]
    in_specs.append(pl.BlockSpec((MLP_ROW_TILE, d), lambda i: (i, 0)))
    in_specs += [_resident_layer(w_out.shape, out_layer), _resident((1, d)),
                 _resident_layer(w_up.shape, mlp_layer), _resident_layer(w_down.shape, mlp_layer)]
    return pl.pallas_call(
        functools.partial(_out_mlp_kernel, n_mix=n_mix),
        grid=(m // MLP_ROW_TILE,),
        in_specs=in_specs,
        out_specs=pl.BlockSpec((MLP_ROW_TILE, d), lambda i: (i, 0)),
        out_shape=jax.ShapeDtypeStruct((m, d), F32),
        compiler_params=_cparams(1),
        name="out_mlp",
    )(*mixed, x2d, w_out, gain.reshape(1, d), w_up, w_down)


def _retention_kernel(q_ref, k_ref, v_ref, g_ref, c_ref, s_ref, lg_ref, gn_ref,
                      o_ref, q_s, k_s, kt_s, o_s, kv_s, *, seq):
    n_chunks = seq // RET_CHUNK
    swap = _swap_matrix(RET_QK_DIM // 2)
    cs = RET_CHUNK

    def prologue(n, carry):
        r0 = pl.multiple_of(n * cs, cs)
        rows = pl.ds(r0, cs)
        c, sn = c_ref[0, rows, :], s_ref[0, rows, :]
        q = _rotate(q_ref[0, rows, :].astype(F32), c, sn, swap)
        k = _rotate(k_ref[0, rows, :].astype(F32), c, sn, swap) * (RET_QK_DIM ** -0.5)
        q_s[n] = q.astype(BF16)
        k_s[n] = k.astype(BF16)
        kt_s[n] = k.T
        return carry

    lax.fori_loop(0, n_chunks, prologue, 0, unroll=16)

    lo = _lo_mask((cs, LANES))
    row = lax.broadcasted_iota(jnp.int32, (cs, cs), 0).astype(F32)
    col = lax.broadcasted_iota(jnp.int32, (cs, cs), 1).astype(F32)
    diff = row - col
    consts = []
    for hh in range(2):
        lg = lg_ref[0, hh:hh + 1, :]
        decay = jnp.where(diff >= 0, jnp.exp(lg * jnp.maximum(diff, 0.0)), 0.0)
        xi = jnp.exp(lg * (row + 1.0))
        zeta = jnp.exp(lg * (cs - 1.0 - col[0:1, :]))
        cd = jnp.exp(lg * float(cs))
        head = lo if hh == 0 else jnp.logical_not(lo)
        consts.append((decay, xi, zeta, cd, head))

    def inner(n, carry):
        rows = pl.ds(pl.multiple_of(n * cs, cs), cs)
        qc, kc, ktc = q_s[n], k_s[n], kt_s[n]
        for hh in range(2):
            decay, xi, zeta, cd, head = consts[hh]
            cols = slice(hh * RET_V_DIM, (hh + 1) * RET_V_DIM)
            qm = jnp.where(head, qc, jnp.zeros_like(qc))
            vc = v_ref[0, rows, cols]
            sc = lax.dot_general(qm, kc, (((1,), (1,)), ((), ())), preferred_element_type=F32) * decay
            o_s[rows, cols] = jnp.dot(sc.astype(BF16), vc, preferred_element_type=F32)
            kz = (ktc * zeta).astype(BF16)
            kv_s[n, hh] = jnp.dot(kz, vc, preferred_element_type=F32)
        return carry

    lax.fori_loop(0, n_chunks, inner, 0, unroll=16)

    def cross(n, state):
        rows = pl.ds(pl.multiple_of(n * cs, cs), cs)
        qc = q_s[n]
        new_state = []
        for hh in range(2):
            decay, xi, zeta, cd, head = consts[hh]
            cols = slice(hh * RET_V_DIM, (hh + 1) * RET_V_DIM)
            r_prev = state[hh]
            qm = jnp.where(head, qc, jnp.zeros_like(qc))
            o_s[rows, cols] += jnp.dot(qm, r_prev.astype(BF16), preferred_element_type=F32) * xi
            new_state.append(r_prev * cd + kv_s[n, hh])
        return tuple(new_state)

    zero = jnp.zeros((LANES, RET_V_DIM), F32)
    lax.fori_loop(0, n_chunks, cross, (zero, zero), unroll=16)

    def finish(n, carry):
        rows = pl.ds(pl.multiple_of(n * cs, cs), cs)
        for hh in range(2):
            cols = slice(hh * RET_V_DIM, (hh + 1) * RET_V_DIM)
            o = o_s[rows, cols]
            mu = jnp.mean(o, axis=-1, keepdims=True)
            dev = o - mu
            var = jnp.mean(dev * dev, axis=-1, keepdims=True)
            y = dev * lax.rsqrt(var + EPS) * gn_ref[0, hh:hh + 1, :]
            gate = g_ref[0, rows, cols].astype(F32)
            o_ref[0, rows, cols] = (gate * jax.nn.sigmoid(gate) * y).astype(o_ref.dtype)
        return carry

    lax.fori_loop(0, n_chunks, finish, 0, unroll=2)


def _retention(proj, tables, log_gamma, gn_gain):
    b, s, _ = proj.shape
    n_pairs = RET_HEADS // 2
    pair_w = 2 * RET_V_DIM
    qk_tiles = RET_HEADS * RET_QK_DIM // LANES
    v_off = 2 * qk_tiles * LANES // pair_w
    g_off = v_off + RET_HEADS * RET_V_DIM // pair_w
    tab = pl.BlockSpec((1, s, LANES), lambda bi, p: (bi, 0, 0), pipeline_mode=pl.Buffered(1))
    lg = jnp.broadcast_to(log_gamma.reshape(n_pairs, 2, 1), (n_pairs, 2, LANES))
    n_chunks = s // RET_CHUNK
    return pl.pallas_call(
        functools.partial(_retention_kernel, seq=s),
        grid=(b, n_pairs),
        in_specs=[
            pl.BlockSpec((1, s, LANES), lambda bi, p: (bi, 0, p)),
            pl.BlockSpec((1, s, LANES), lambda bi, p: (bi, 0, qk_tiles + p)),
            pl.BlockSpec((1, s, pair_w), lambda bi, p: (bi, 0, v_off + p)),
            pl.BlockSpec((1, s, pair_w), lambda bi, p: (bi, 0, g_off + p)),
            tab, tab,
            pl.BlockSpec((1, 2, LANES), lambda bi, p: (p, 0, 0)),
            pl.BlockSpec((1, 2, RET_V_DIM), lambda bi, p: (p, 0, 0)),
        ],
        out_specs=pl.BlockSpec((1, s, pair_w), lambda bi, p: (bi, 0, p)),
        out_shape=jax.ShapeDtypeStruct((b, s, RET_HEADS * RET_V_DIM), BF16),
        scratch_shapes=[
            pltpu.VMEM((n_chunks, RET_CHUNK, LANES), BF16),
            pltpu.VMEM((n_chunks, RET_CHUNK, LANES), BF16),
            pltpu.VMEM((n_chunks, LANES, RET_CHUNK), F32),
            pltpu.VMEM((s, pair_w), F32),
            pltpu.VMEM((n_chunks, 2, LANES, RET_V_DIM), F32),
        ],
        compiler_params=_cparams(2),
        name="retention",
    )(proj, proj, proj, proj, *tables, lg, gn_gain.reshape(n_pairs, 2, RET_V_DIM))


def _band_bias_t(lo_off, hi_off):
    c = lax.broadcasted_iota(jnp.int32, (2 * QBLK, QBLK), 0)
    a = lax.broadcasted_iota(jnp.int32, (2 * QBLK, QBLK), 1)
    band = (c - a >= lo_off) & (c - a <= hi_off)
    later = jnp.where(band, 0.0, NEG_BIG).astype(F32)
    first = jnp.where(band & (c >= QBLK), 0.0, NEG_BIG).astype(F32)
    return first, later


def _dilated_kernel(q_ref, k_ref, v_ref, c_ref, s_ref, qg_ref, kg_ref, o_ref,
                    qn_s, kn_s, vn_s, q4_s, k4_s, v4_s, qt_s, kc_s, vt_s, bias_s, p_s, m_s,
                    o0_s, o1_s, o2_s, l0_s, l1_s, l2_s, *, seq):
    swap = _swap_matrix(ROPE_DIMS // 2)
    head_ones = _head_ones()
    rq_s, rk_s = o0_s, l0_s

    def norms(n, carry):
        rows = pl.ds(pl.multiple_of(n * PRO_ROWS, PRO_ROWS), PRO_ROWS)
        rq_s[rows, :] = _head_rinv(q_ref[0, rows, :].astype(F32), head_ones)
        rk_s[rows, :] = _head_rinv(k_ref[0, rows, :].astype(F32), head_ones)
        return carry

    lax.fori_loop(0, seq // PRO_ROWS, norms, 0, unroll=16)

    def prologue(n, carry):
        rows = pl.ds(pl.multiple_of(n * PRO_ROWS, PRO_ROWS), PRO_ROWS)
        c, sn = c_ref[0, rows, :], s_ref[0, rows, :]
        q = q_ref[0, rows, :].astype(F32) * rq_s[rows, :] * qg_ref[...]
        qn_s[rows, :] = _rotate(q, c, sn, swap) * Q_SCALE
        k = k_ref[0, rows, :].astype(F32) * rk_s[rows, :] * kg_ref[...]
        kn_s[rows, :] = _rotate(k, c, sn, swap)
        vn_s[rows, :] = v_ref[0, rows, :].astype(F32)
        return carry

    lax.fori_loop(0, seq // PRO_ROWS, prologue, 0, unroll=16)

    first, later = _band_bias_t(0, QBLK)
    bias_s[0] = jnp.concatenate([first, first], axis=1)
    bias_s[1] = jnp.concatenate([later, later], axis=1)
    zero_pad = jnp.zeros((QBLK, LANES), BF16)
    top = lax.broadcasted_iota(jnp.int32, (LANES, QBLK), 0) < HEAD_DIM
    ones_rows = jnp.ones((DEN_ROWS, 2 * QBLK), BF16)

    outs = (o0_s, o1_s, o2_s)
    lses = (l0_s, l1_s, l2_s)
    n_flat = seq // QBLK
    for (window, r), on_s, ln_s in zip(DIL_PATTERNS, outs, lses):
        assert window // r == QBLK
        n_blk = seq // r // QBLK
        blk_shift = n_blk.bit_length() - 1
        assert n_blk == 1 << blk_shift

        def pad(j, c2, n_blk=n_blk):
            z = j * (n_blk + 1)
            kc_s[pl.ds(pl.multiple_of(z * QBLK, QBLK), QBLK), :] = zero_pad
            vt_s[z] = zero_pad
            return c2

        lax.fori_loop(0, r, pad, 0)

        def split(idx, r=r, n_blk=n_blk, blk_shift=blk_shift):
            j = lax.shift_right_logical(idx, blk_shift)
            i = idx & (n_blk - 1)
            return j, i

        def gather(idx, c2, r=r, split=split):
            j, i = split(idx)
            if r == FINE_R:
                base = ((j & (COARSE_R - 1)) * (seq // COARSE_R)
                        + lax.shift_right_logical(j, COARSE_R.bit_length() - 1))
                src = pl.ds(base + (QBLK * r // COARSE_R) * i, QBLK, stride=r // COARSE_R)
                q, k, v = q4_s[src, :], k4_s[src, :], v4_s[src, :]
            else:
                src = pl.ds(j + r * QBLK * i, QBLK, stride=r)
                q, k, v = qn_s[src, :], kn_s[src, :], vn_s[src, :]
            if r == COARSE_R:
                dense = pl.ds(pl.multiple_of(idx * QBLK, QBLK), QBLK)
                q4_s[dense, :] = q
                k4_s[dense, :] = k
                v4_s[dense, :] = v
            k0 = pl.multiple_of((idx + j + 1) * QBLK, QBLK)
            qt_s[idx] = q.T.astype(BF16)
            kc_s[pl.ds(k0, QBLK), :] = k.astype(BF16)
            vt_s[idx + j + 1] = v.T.astype(BF16)
            return c2

        lax.fori_loop(0, n_flat, gather, 0, unroll=16)

        def scores(idx, c2, split=split):
            j, i = split(idx)
            k0 = pl.multiple_of((idx + j) * QBLK, QBLK)
            qt = qt_s[idx]
            kw = kc_s[pl.ds(k0, 2 * QBLK), :]
            zero = jnp.zeros_like(qt)
            rhs = jnp.concatenate([jnp.where(top, qt, zero), jnp.where(top, zero, qt)], axis=1)
            sc = jnp.dot(kw, rhs, preferred_element_type=F32) + bias_s[jnp.minimum(i, 1)]
            m = jnp.max(sc, axis=0, keepdims=True)
            p_s[idx] = jnp.exp2(sc - m).astype(BF16)
            m_s[idx] = jnp.broadcast_to(m, (SUBLANES, 2 * QBLK))
            return c2

        lax.fori_loop(0, n_flat, scores, 0, unroll=16)

        def block(idx, c2, r=r, split=split, on_s=on_s, ln_s=ln_s):
            j, i = split(idx)
            p = p_s[idx]
            m = m_s[idx][0:1, :]
            lhs = jnp.concatenate(
                [jnp.concatenate([vt_s[idx + j], vt_s[idx + j + 1]], axis=1), ones_rows], axis=0)
            ot = jnp.dot(lhs, p, preferred_element_type=F32)
            den = ot[LANES:LANES + 1, :]
            inv = 1.0 / den
            lse = m + jnp.log2(den)
            o_t = jnp.concatenate(
                [ot[0:HEAD_DIM, 0:QBLK] * inv[:, 0:QBLK],
                 ot[HEAD_DIM:LANES, QBLK:2 * QBLK] * inv[:, QBLK:2 * QBLK]], axis=0)
            l_t = jnp.concatenate(
                [jnp.broadcast_to(lse[:, 0:QBLK], (HEAD_DIM, QBLK)),
                 jnp.broadcast_to(lse[:, QBLK:2 * QBLK], (HEAD_DIM, QBLK))], axis=0)
            dst = pl.ds(j + r * QBLK * i, QBLK, stride=r)
            on_s[dst, :] = o_t.T
            ln_s[dst, :] = l_t.T
            return c2

        lax.fori_loop(0, n_flat, block, 0, unroll=16)

    def combine(n, carry):
        r0 = pl.multiple_of(n * PRO_ROWS, PRO_ROWS)
        rows = pl.ds(r0, PRO_ROWS)
        ls = [l_s[rows, :] for l_s in lses]
        m = jnp.maximum(jnp.maximum(ls[0], ls[1]), ls[2])
        es = [jnp.exp2(l - m) for l in ls]
        num = es[0] * o0_s[rows, :] + es[1] * o1_s[rows, :] + es[2] * o2_s[rows, :]
        o_ref[0, rows, :] = (num / (es[0] + es[1] + es[2])).astype(o_ref.dtype)
        return carry

    lax.fori_loop(0, seq // PRO_ROWS, combine, 0, unroll=8)


def _dilated(proj, tables, q_gain, k_gain, col0):
    b, s, _ = proj.shape
    n_pairs = DIL_HEADS * HEAD_DIM // LANES
    t0 = col0 // LANES
    tab = pl.BlockSpec((1, s, LANES), lambda bi, p: (bi, 0, 0), pipeline_mode=pl.Buffered(1))
    two = lambda g: jnp.concatenate([g, g]).reshape(1, LANES)
    nat = pltpu.VMEM((s, LANES), F32)
    max_r = max(r for _, r in DIL_PATTERNS)
    return pl.pallas_call(
        functools.partial(_dilated_kernel, seq=s),
        grid=(b, n_pairs),
        in_specs=[
            pl.BlockSpec((1, s, LANES), lambda bi, p: (bi, 0, t0 + p)),
            pl.BlockSpec((1, s, LANES), lambda bi, p: (bi, 0, t0 + n_pairs + p)),
            pl.BlockSpec((1, s, LANES), lambda bi, p: (bi, 0, t0 + 2 * n_pairs + p)),
            tab, tab,
            _resident((1, LANES)),
            _resident((1, LANES)),
        ],
        out_specs=pl.BlockSpec((1, s, LANES), lambda bi, p: (bi, 0, p)),
        out_shape=jax.ShapeDtypeStruct((b, s, DIL_HEADS * HEAD_DIM), BF16),
        scratch_shapes=[
            nat, nat, nat, nat, nat, nat,
            pltpu.VMEM((s // QBLK, LANES, QBLK), BF16),
            pltpu.VMEM((s + max_r * QBLK, LANES), BF16),
            pltpu.VMEM((s // QBLK + max_r, LANES, QBLK), BF16),
            pltpu.VMEM((2, 2 * QBLK, 2 * QBLK), F32),
            pltpu.VMEM((s // QBLK, 2 * QBLK, 2 * QBLK), BF16),
            pltpu.VMEM((s // QBLK, SUBLANES, 2 * QBLK), F32),
            nat, nat, nat, nat, nat, nat,
        ],
        compiler_params=_cparams(2),
        name="dilated_attention",
    )(proj, proj, proj, *tables, two(q_gain), two(k_gain))


def _swa_kernel(q_ref, k_ref, v_ref, c_ref, s_ref, qg_ref, kg_ref, sink_ref, o_ref,
                q_s, kd_s, vt_s, bias_s, p_s, m_s, rq_s, rk_s, *, seq):
    group = SWA_Q_HEADS // SWA_KV_HEADS
    swap = _swap_matrix(ROPE_DIMS // 2)
    head_ones = _head_ones()
    kv_in_hi = (pl.program_id(1) % 2) == 1
    blk_per_step = PRO_ROWS // QBLK
    src = lax.broadcasted_iota(jnp.int32, (LANES, LANES), 0)
    dst = lax.broadcasted_iota(jnp.int32, (LANES, LANES), 1) & (HEAD_DIM - 1)
    dup = jnp.where(src == dst + jnp.where(kv_in_hi, HEAD_DIM, 0), 1.0, 0.0).astype(BF16)

    def norms(n, carry):
        rows = pl.ds(pl.multiple_of(n * PRO_ROWS, PRO_ROWS), PRO_ROWS)
        for t in range(group // 2):
            cols = slice(t * LANES, (t + 1) * LANES)
            rq_s[rows, cols] = _head_rinv(q_ref[0, rows, cols].astype(F32), head_ones)
        rk_s[rows, :] = _head_rinv(k_ref[0, rows, :].astype(F32), head_ones)
        return carry

    lax.fori_loop(0, seq // PRO_ROWS, norms, 0, unroll=16)

    def prologue(n, carry):
        r0 = pl.multiple_of(n * PRO_ROWS, PRO_ROWS)
        rows = pl.ds(r0, PRO_ROWS)
        c, sn = c_ref[0, rows, :], s_ref[0, rows, :]
        for t in range(group // 2):
            cols = slice(t * LANES, (t + 1) * LANES)
            q = q_ref[0, rows, cols].astype(F32) * rq_s[rows, cols] * qg_ref[...]
            q_s[rows, cols] = (_rotate(q, c, sn, swap) * Q_SCALE).astype(BF16)
        k = k_ref[0, rows, :].astype(F32) * rk_s[rows, :] * kg_ref[...]
        k = _rotate(k, c, sn, swap).astype(BF16)
        kd_s[pl.ds(r0 + QBLK, PRO_ROWS), :] = jnp.dot(k, dup, preferred_element_type=F32).astype(BF16)
        v_t = v_ref[0, rows, :].astype(F32).T
        v_t = jnp.where(kv_in_hi, v_t[HEAD_DIM:LANES, :], v_t[0:HEAD_DIM, :]).astype(BF16)
        for u in range(blk_per_step):
            vt_s[n * blk_per_step + u + 1] = v_t[:, u * QBLK:(u + 1) * QBLK]
        return carry

    lax.fori_loop(0, seq // PRO_ROWS, prologue, 0, unroll=16)

    first, later = _band_bias_t(1, QBLK)
    bias_s[0] = jnp.concatenate([first] * group, axis=1)
    bias_s[1] = jnp.concatenate([later] * group, axis=1)
    kd_s[0:QBLK, :] = jnp.zeros((QBLK, LANES), BF16)
    vt_s[0] = jnp.zeros((HEAD_DIM, QBLK), BF16)
    lo = _lo_mask((QBLK, LANES))
    heads = (lo, jnp.logical_not(lo))
    sink = jnp.concatenate([sink_ref[0, hh:hh + 1, :] for hh in range(group)], axis=1) * LOG2_E
    ones_rows = jnp.ones((DEN_ROWS, 2 * QBLK), BF16)

    def scores(i, carry):
        d0 = pl.multiple_of(i * QBLK, QBLK)
        rows = pl.ds(d0, QBLK)
        kw = kd_s[pl.ds(d0, 2 * QBLK), :]
        stack = []
        for t in range(group // 2):
            qt = q_s[rows, t * LANES:(t + 1) * LANES]
            for head in heads:
                stack.append(jnp.where(head, qt, jnp.zeros_like(qt)))
        qm = jnp.concatenate(stack, axis=0)
        sc = lax.dot_general(kw, qm, (((1,), (1,)), ((), ())), preferred_element_type=F32)
        sc = sc + bias_s[jnp.minimum(i, 1)]
        m = jnp.maximum(jnp.max(sc, axis=0, keepdims=True), sink)
        p_s[i] = jnp.exp2(sc - m).astype(BF16)
        m_s[i] = jnp.broadcast_to(m, (SUBLANES, group * QBLK))
        return carry

    lax.fori_loop(0, seq // QBLK, scores, 0, unroll=BLOCK_UNROLL)

    def values(i, carry):
        rows = pl.ds(pl.multiple_of(i * QBLK, QBLK), QBLK)
        m = m_s[i][0:1, :]
        lhs = jnp.concatenate(
            [jnp.concatenate([vt_s[i], vt_s[i + 1]], axis=1), ones_rows], axis=0)
        ot = jnp.dot(lhs, p_s[i], preferred_element_type=F32)
        inv = 1.0 / (ot[HEAD_DIM:HEAD_DIM + 1, :] + jnp.exp2(sink - m))
        on = ot[0:HEAD_DIM, :] * inv
        for t in range(group // 2):
            pair_t = jnp.concatenate(
                [on[:, 2 * t * QBLK:(2 * t + 1) * QBLK], on[:, (2 * t + 1) * QBLK:(2 * t + 2) * QBLK]],
                axis=0)
            o_ref[0, rows, t * LANES:(t + 1) * LANES] = pair_t.T.astype(o_ref.dtype)
        return carry

    lax.fori_loop(0, seq // QBLK, values, 0, unroll=16)


def _swa(proj, tables, q_gain, k_gain, sinks):
    b, s, _ = proj.shape
    group = SWA_Q_HEADS // SWA_KV_HEADS
    q_w = group * HEAD_DIM
    k_t0 = SWA_Q_HEADS * HEAD_DIM // LANES
    v_t0 = k_t0 + SWA_KV_HEADS * HEAD_DIM // LANES
    tab = pl.BlockSpec((1, s, LANES), lambda bi, g: (bi, 0, 0), pipeline_mode=pl.Buffered(1))
    two = lambda g: jnp.concatenate([g, g]).reshape(1, LANES)
    sink_rows = jnp.broadcast_to(sinks.reshape(SWA_KV_HEADS, group, 1), (SWA_KV_HEADS, group, LANES))
    return pl.pallas_call(
        functools.partial(_swa_kernel, seq=s),
        grid=(b, SWA_KV_HEADS),
        in_specs=[
            pl.BlockSpec((1, s, q_w), lambda bi, g: (bi, 0, g)),
            pl.BlockSpec((1, s, LANES), lambda bi, g: (bi, 0, k_t0 + g // 2)),
            pl.BlockSpec((1, s, LANES), lambda bi, g: (bi, 0, v_t0 + g // 2)),
            tab, tab,
            _resident((1, LANES)),
            _resident((1, LANES)),
            pl.BlockSpec((1, group, LANES), lambda bi, g: (g, 0, 0)),
        ],
        out_specs=pl.BlockSpec((1, s, q_w), lambda bi, g: (bi, 0, g)),
        out_shape=jax.ShapeDtypeStruct((b, s, SWA_Q_HEADS * HEAD_DIM), BF16),
        scratch_shapes=[
            pltpu.VMEM((s, q_w), BF16),
            pltpu.VMEM((s + QBLK, LANES), BF16),
            pltpu.VMEM((s // QBLK + 1, HEAD_DIM, QBLK), BF16),
            pltpu.VMEM((2, 2 * QBLK, group * QBLK), F32),
            pltpu.VMEM((s // QBLK, 2 * QBLK, group * QBLK), BF16),
            pltpu.VMEM((s // QBLK, SUBLANES, group * QBLK), F32),
            pltpu.VMEM((s, q_w), F32),
            pltpu.VMEM((s, LANES), F32),
        ],
        compiler_params=_cparams(2),
        name="swa_attention",
    )(proj, proj, proj, *tables, two(q_gain), two(k_gain), sink_rows)


def kernel(x, positions, norm_mix, norm_mlp, mlp_w_up, mlp_w_down, hyb_w_in, hyb_w_out, ret_gn_gain, dil_q_gain, dil_k_gain, swa_w_qkv, swa_b_qkv, swa_w_out, swa_q_gain, swa_k_gain, swa_sinks):
    b, s, d = x.shape
    depth = norm_mix.shape[0]
    ret_tab = _rope_table(positions, RET_QK_DIM // 2, RET_THETA, RET_QK_DIM)
    rope_tab = _rope_table(positions, ROPE_DIMS // 2, ROPE_THETA, ROPE_DIMS)
    log_gamma = jnp.log1p(-jnp.exp2(-5.0 - jnp.arange(RET_HEADS, dtype=F32)))
    ret_w = RET_HEADS * RET_V_DIM
    dil_col0 = 2 * RET_HEADS * RET_QK_DIM + 2 * ret_w

    x2d = x.reshape(b * s, d)
    for layer in range(depth):
        i = layer // 2
        if layer % 2 == 0:
            zero_bias = jnp.zeros((hyb_w_in.shape[2],), F32)
            proj = _norm_proj(x2d, norm_mix[layer], hyb_w_in, i, zero_bias).reshape(b, s, -1)
            ra = _retention(proj, ret_tab, log_gamma, ret_gn_gain[i])
            da = _dilated(proj, rope_tab, dil_q_gain[i], dil_k_gain[i], dil_col0)
            mixed = [ra.reshape(b * s, -1), da.reshape(b * s, -1)]
            w_out = hyb_w_out
        else:
            proj = _norm_proj(x2d, norm_mix[layer], swa_w_qkv, i, swa_b_qkv[i]).reshape(b, s, -1)
            att = _swa(proj, rope_tab, swa_q_gain[i], swa_k_gain[i], swa_sinks[i])
            mixed = [att.reshape(b * s, -1)]
            w_out = swa_w_out
        x2d = _out_mlp(mixed, x2d, w_out, i, norm_mlp[layer], mlp_w_up, mlp_w_down, layer)
    return x2d.reshape(b, s, d)
```

```python
import functools

import jax
import jax.numpy as jnp
import numpy as np
from jax import lax
from jax.experimental import pallas as pl
from jax.experimental.pallas import tpu as pltpu

F32 = jnp.float32
BF16 = jnp.bfloat16

D_MODEL = 1024
D_FF = 4 * D_MODEL
HEAD_DIM = 64
EPS = 1e-6
RET_HEADS = 4
RET_QK_DIM = 64
RET_V_DIM = 128
RET_CHUNK = 128
RET_THETA = 10000.0
DIL_HEADS = 8
DIL_PATTERNS = ((128, 1), (512, 4), (2048, 16))
SWA_Q_HEADS = 16
SWA_KV_HEADS = 4
SWA_WINDOW = 128
ROPE_THETA = 500000.0
ROPE_DIMS = HEAD_DIM // 4

LANES = 128
SUBLANES = 8
QBLK = 128
VMEM_LIMIT_BYTES = 56 * 1024 * 1024
NEG_BIG = -1e30
LOG2_E = 1.4426950408889634
Q_SCALE = HEAD_DIM ** -0.5 * LOG2_E

ROW_TILE = 512
MLP_ROW_TILE = 512
FF_CHUNK = 1024
PRO_ROWS = 128
BLOCK_UNROLL = 16
DEN_ROWS = 16
COARSE_R, FINE_R = DIL_PATTERNS[1][1], DIL_PATTERNS[2][1]


def _cparams(n_axes):
    return pltpu.CompilerParams(
        dimension_semantics=("arbitrary",) * n_axes,
        vmem_limit_bytes=VMEM_LIMIT_BYTES,
    )


def _resident(shape):
    nd = len(shape)
    return pl.BlockSpec(shape, lambda *_: (0,) * nd, pipeline_mode=pl.Buffered(1))


def _resident_layer(stack_shape, layer):
    return pl.BlockSpec((None,) + tuple(stack_shape[1:]), lambda *_: (layer, 0, 0),
                        pipeline_mode=pl.Buffered(1))


def _split3(x):
    hi = x.astype(BF16)
    r1 = x - hi.astype(F32)
    mid = r1.astype(BF16)
    lo = (r1 - mid.astype(F32)).astype(BF16)
    return hi, mid, lo


def _trig_kernel(p_ref, spread_ref, inv_ref, ec_ref, es_ref, base_ref, c_ref, s_ref, *, n_pos):
    dense = sum(jnp.dot(part, spread_ref[...], preferred_element_type=F32)
                for part in _split3(p_ref[...]))
    ang = dense * inv_ref[...]
    cos_parts = _split3(jnp.cos(ang))
    sin_parts = _split3(jnp.sin(ang))
    rows = p_ref.shape[0]
    for c in range(n_pos):
        dst = pl.ds(c, rows, stride=n_pos)
        c_ref[dst, :] = base_ref[...] + sum(
            jnp.dot(part, ec_ref[c], preferred_element_type=F32) for part in cos_parts)
        s_ref[dst, :] = sum(
            jnp.dot(part, es_ref[c], preferred_element_type=F32) for part in sin_parts)


def _selectors(half):
    n_pos = LANES // half
    ec = np.zeros((n_pos, LANES, LANES), np.float32)
    es = np.zeros((n_pos, LANES, LANES), np.float32)
    base = np.zeros((1, LANES), np.float32)
    for j in range(LANES):
        d = j % HEAD_DIM
        if d >= 2 * half:
            base[0, j] = 1.0
            continue
        f = d % half
        for c in range(n_pos):
            ec[c, c * half + f, j] = 1.0
            es[c, c * half + f, j] = -1.0 if d < half else 1.0
    return n_pos, jnp.asarray(ec, BF16), jnp.asarray(es, BF16), jnp.asarray(base)


def _rope_table(positions, half, theta, n_rot):
    b, s = positions.shape
    posf = positions.astype(F32)
    inv = jnp.power(jnp.float32(theta), -jnp.arange(half, dtype=F32) * (2.0 / n_rot))
    n_pos, ec, es, base = _selectors(half)
    rows = s // n_pos
    p = posf.reshape(b, rows, n_pos)
    spread = jnp.asarray(np.repeat(np.eye(n_pos, dtype=np.float32), half, axis=1), BF16)
    inv_row = jnp.tile(inv, n_pos)[None, :]
    tile = min(rows, 2048 // n_pos)
    assert rows % tile == 0
    dense = pl.BlockSpec((None, tile, n_pos), lambda bi, i: (bi, i, 0))
    wide = pl.BlockSpec((None, tile * n_pos, LANES), lambda bi, i: (bi, i, 0))
    c_tab, s_tab = pl.pallas_call(
        functools.partial(_trig_kernel, n_pos=n_pos),
        grid=(b, rows // tile),
        in_specs=[dense, _resident(spread.shape), _resident(inv_row.shape), _resident(ec.shape),
                  _resident(es.shape), _resident(base.shape)],
        out_specs=[wide, wide],
        out_shape=[jax.ShapeDtypeStruct((b, s, LANES), F32)] * 2,
        compiler_params=_cparams(2),
        name="trig_tables",
    )(p, spread, inv_row, ec, es, base)
    return c_tab, s_tab


def _swap_matrix(half):
    src = lax.broadcasted_iota(jnp.int32, (LANES, LANES), 0)
    dst = lax.broadcasted_iota(jnp.int32, (LANES, LANES), 1)
    d = dst & (HEAD_DIM - 1)
    want = jnp.where(d < half, dst + half, jnp.where(d < 2 * half, dst - half, -1))
    return jnp.where(src == want, 1.0, 0.0).astype(BF16)


def _rotate(x, c, s, swap):
    swapped = jnp.dot(x.astype(BF16), swap, preferred_element_type=F32)
    return x * c + swapped * s


def _lo_mask(shape):
    return lax.broadcasted_iota(jnp.int32, shape, len(shape) - 1) < HEAD_DIM


def _head_ones():
    r = lax.broadcasted_iota(jnp.int32, (LANES, LANES), 0) < HEAD_DIM
    c = lax.broadcasted_iota(jnp.int32, (LANES, LANES), 1) < HEAD_DIM
    return jnp.where(r == c, 1.0, 0.0).astype(BF16)


def _head_rinv(x, head_ones):
    ss = jnp.dot((x * x).astype(BF16), head_ones, preferred_element_type=F32)
    return lax.rsqrt(ss * (1.0 / HEAD_DIM) + EPS)


def _norm_proj_kernel(x_ref, g_ref, w_ref, b_ref, o_ref, *, n_chunk):
    x = x_ref[...]
    ms = jnp.mean(x * x, axis=-1, keepdims=True)
    h = (x * lax.rsqrt(ms + EPS) * g_ref[...]).astype(BF16)
    n = o_ref.shape[-1]
    for c in range(0, n, n_chunk):
        acc = jnp.dot(h, w_ref[:, c:c + n_chunk].astype(BF16), preferred_element_type=F32)
        o_ref[:, c:c + n_chunk] = (acc + b_ref[:, c:c + n_chunk]).astype(o_ref.dtype)


def _norm_proj(x2d, gain, w_stack, layer, bias):
    m, d = x2d.shape
    n = w_stack.shape[2]
    return pl.pallas_call(
        functools.partial(_norm_proj_kernel, n_chunk=512),
        grid=(m // ROW_TILE,),
        in_specs=[
            pl.BlockSpec((ROW_TILE, d), lambda i: (i, 0)),
            _resident((1, d)),
            _resident_layer(w_stack.shape, layer),
            _resident((1, n)),
        ],
        out_specs=pl.BlockSpec((ROW_TILE, n), lambda i: (i, 0)),
        out_shape=jax.ShapeDtypeStruct((m, n), BF16),
        compiler_params=_cparams(1),
        name="norm_proj",
    )(x2d, gain.reshape(1, d), w_stack, bias.reshape(1, n))


def _out_mlp_kernel(*refs, n_mix):
    a_refs = refs[:n_mix]
    x_ref, wo_ref, g_ref, wup_ref, wdn_ref, o_ref = refs[n_mix:]
    mixed = a_refs[0][...] if n_mix == 1 else jnp.concatenate([a[...] for a in a_refs], axis=-1)
    x1 = x_ref[...] + jnp.dot(mixed, wo_ref[...].astype(BF16), preferred_element_type=F32)
    ms = jnp.mean(x1 * x1, axis=-1, keepdims=True)
    h = (x1 * lax.rsqrt(ms + EPS) * g_ref[...]).astype(BF16)
    y = x1
    for c in range(0, D_FF, FF_CHUNK):
        u = jnp.dot(h, wup_ref[:, c:c + FF_CHUNK].astype(BF16), preferred_element_type=F32)
        u = jnp.square(jnp.maximum(u, 0.0)).astype(BF16)
        y = y + jnp.dot(u, wdn_ref[c:c + FF_CHUNK, :].astype(BF16), preferred_element_type=F32)
    o_ref[...] = y


def _out_mlp(mixed, x2d, w_out, out_layer, gain, w_up, w_down, mlp_layer):
    m, d = x2d.shape
    n_mix = len(mixed)
    in_specs = [pl.BlockSpec((MLP_ROW_TILE, a.shape[1]), lambda i: (i, 0)) for a in mixed]
    in_specs.append(pl.BlockSpec((MLP_ROW_TILE, d), lambda i: (i, 0)))
    in_specs += [_resident_layer(w_out.shape, out_layer), _resident((1, d)),
                 _resident_layer(w_up.shape, mlp_layer), _resident_layer(w_down.shape, mlp_layer)]
    return pl.pallas_call(
        functools.partial(_out_mlp_kernel, n_mix=n_mix),
        grid=(m // MLP_ROW_TILE,),
        in_specs=in_specs,
        out_specs=pl.BlockSpec((MLP_ROW_TILE, d), lambda i: (i, 0)),
        out_shape=jax.ShapeDtypeStruct((m, d), F32),
        compiler_params=_cparams(1),
        name="out_mlp",
    )(*mixed, x2d, w_out, gain.reshape(1, d), w_up, w_down)


def _retention_kernel(q_ref, k_ref, v_ref, g_ref, c_ref, s_ref, lg_ref, gn_ref,
                      o_ref, q_s, k_s, kt_s, o_s, kv_s, *, seq):
    n_chunks = seq // RET_CHUNK
    swap = _swap_matrix(RET_QK_DIM // 2)
    cs = RET_CHUNK

    def prologue(n, carry):
        r0 = pl.multiple_of(n * cs, cs)
        rows = pl.ds(r0, cs)
        c, sn = c_ref[0, rows, :], s_ref[0, rows, :]
        q = _rotate(q_ref[0, rows, :].astype(F32), c, sn, swap)
        k = _rotate(k_ref[0, rows, :].astype(F32), c, sn, swap) * (RET_QK_DIM ** -0.5)
        q_s[n] = q.astype(BF16)
        k_s[n] = k.astype(BF16)
        kt_s[n] = k.T
        return carry

    lax.fori_loop(0, n_chunks, prologue, 0, unroll=16)

    lo = _lo_mask((cs, LANES))
    row = lax.broadcasted_iota(jnp.int32, (cs, cs), 0).astype(F32)
    col = lax.broadcasted_iota(jnp.int32, (cs, cs), 1).astype(F32)
    diff = row - col
    consts = []
    for hh in range(2):
        lg = lg_ref[0, hh:hh + 1, :]
        decay = jnp.where(diff >= 0, jnp.exp(lg * jnp.maximum(diff, 0.0)), 0.0)
        xi = jnp.exp(lg * (row + 1.0))
        zeta = jnp.exp(lg * (cs - 1.0 - col[0:1, :]))
        cd = jnp.exp(lg * float(cs))
        head = lo if hh == 0 else jnp.logical_not(lo)
        consts.append((decay, xi, zeta, cd, head))

    def inner(n, carry):
        rows = pl.ds(pl.multiple_of(n * cs, cs), cs)
        qc, kc, ktc = q_s[n], k_s[n], kt_s[n]
        for hh in range(2):
            decay, xi, zeta, cd, head = consts[hh]
            cols = slice(hh * RET_V_DIM, (hh + 1) * RET_V_DIM)
            qm = jnp.where(head, qc, jnp.zeros_like(qc))
            vc = v_ref[0, rows, cols]
            sc = lax.dot_general(qm, kc, (((1,), (1,)), ((), ())), preferred_element_type=F32) * decay
            o_s[rows, cols] = jnp.dot(sc.astype(BF16), vc, preferred_element_type=F32)
            kz = (ktc * zeta).astype(BF16)
            kv_s[n, hh] = jnp.dot(kz, vc, preferred_element_type=F32)
        return carry

    lax.fori_loop(0, n_chunks, inner, 0, unroll=16)

    def cross(n, state):
        rows = pl.ds(pl.multiple_of(n * cs, cs), cs)
        qc = q_s[n]
        new_state = []
        for hh in range(2):
            decay, xi, zeta, cd, head = consts[hh]
            cols = slice(hh * RET_V_DIM, (hh + 1) * RET_V_DIM)
            r_prev = state[hh]
            qm = jnp.where(head, qc, jnp.zeros_like(qc))
            o_s[rows, cols] += jnp.dot(qm, r_prev.astype(BF16), preferred_element_type=F32) * xi
            new_state.append(r_prev * cd + kv_s[n, hh])
        return tuple(new_state)

    zero = jnp.zeros((LANES, RET_V_DIM), F32)
    lax.fori_loop(0, n_chunks, cross, (zero, zero), unroll=16)

    def finish(n, carry):
        rows = pl.ds(pl.multiple_of(n * cs, cs), cs)
        for hh in range(2):
            cols = slice(hh * RET_V_DIM, (hh + 1) * RET_V_DIM)
            o = o_s[rows, cols]
            mu = jnp.mean(o, axis=-1, keepdims=True)
            dev = o - mu
            var = jnp.mean(dev * dev, axis=-1, keepdims=True)
            y = dev * lax.rsqrt(var + EPS) * gn_ref[0, hh:hh + 1, :]
            gate = g_ref[0, rows, cols].astype(F32)
            o_ref[0, rows, cols] = (gate * jax.nn.sigmoid(gate) * y).astype(o_ref.dtype)
        return carry

    lax.fori_loop(0, n_chunks, finish, 0, unroll=2)


def _retention(proj, tables, log_gamma, gn_gain):
    b, s, _ = proj.shape
    n_pairs = RET_HEADS // 2
    pair_w = 2 * RET_V_DIM
    qk_tiles = RET_HEADS * RET_QK_DIM // LANES
    v_off = 2 * qk_tiles * LANES // pair_w
    g_off = v_off + RET_HEADS * RET_V_DIM // pair_w
    tab = pl.BlockSpec((1, s, LANES), lambda bi, p: (bi, 0, 0), pipeline_mode=pl.Buffered(1))
    lg = jnp.broadcast_to(log_gamma.reshape(n_pairs, 2, 1), (n_pairs, 2, LANES))
    n_chunks = s // RET_CHUNK
    return pl.pallas_call(
        functools.partial(_retention_kernel, seq=s),
        grid=(b, n_pairs),
        in_specs=[
            pl.BlockSpec((1, s, LANES), lambda bi, p: (bi, 0, p)),
            pl.BlockSpec((1, s, LANES), lambda bi, p: (bi, 0, qk_tiles + p)),
            pl.BlockSpec((1, s, pair_w), lambda bi, p: (bi, 0, v_off + p)),
            pl.BlockSpec((1, s, pair_w), lambda bi, p: (bi, 0, g_off + p)),
            tab, tab,
            pl.BlockSpec((1, 2, LANES), lambda bi, p: (p, 0, 0)),
            pl.BlockSpec((1, 2, RET_V_DIM), lambda bi, p: (p, 0, 0)),
        ],
        out_specs=pl.BlockSpec((1, s, pair_w), lambda bi, p: (bi, 0, p)),
        out_shape=jax.ShapeDtypeStruct((b, s, RET_HEADS * RET_V_DIM), BF16),
        scratch_shapes=[
            pltpu.VMEM((n_chunks, RET_CHUNK, LANES), BF16),
            pltpu.VMEM((n_chunks, RET_CHUNK, LANES), BF16),
            pltpu.VMEM((n_chunks, LANES, RET_CHUNK), F32),
            pltpu.VMEM((s, pair_w), F32),
            pltpu.VMEM((n_chunks, 2, LANES, RET_V_DIM), F32),
        ],
        compiler_params=_cparams(2),
        name="retention",
    )(proj, proj, proj, proj, *tables, lg, gn_gain.reshape(n_pairs, 2, RET_V_DIM))


def _band_bias_t(lo_off, hi_off):
    c = lax.broadcasted_iota(jnp.int32, (2 * QBLK, QBLK), 0)
    a = lax.broadcasted_iota(jnp.int32, (2 * QBLK, QBLK), 1)
    band = (c - a >= lo_off) & (c - a <= hi_off)
    later = jnp.where(band, 0.0, NEG_BIG).astype(F32)
    first = jnp.where(band & (c >= QBLK), 0.0, NEG_BIG).astype(F32)
    return first, later


def _dilated_kernel(q_ref, k_ref, v_ref, c_ref, s_ref, qg_ref, kg_ref, o_ref,
                    qn_s, kn_s, vn_s, q4_s, k4_s, v4_s, qt_s, kc_s, vt_s, bias_s, p_s, m_s,
                    o0_s, o1_s, o2_s, l0_s, l1_s, l2_s, *, seq):
    swap = _swap_matrix(ROPE_DIMS // 2)
    head_ones = _head_ones()
    rq_s, rk_s = o0_s, l0_s

    def norms(n, carry):
        rows = pl.ds(pl.multiple_of(n * PRO_ROWS, PRO_ROWS), PRO_ROWS)
        rq_s[rows, :] = _head_rinv(q_ref[0, rows, :].astype(F32), head_ones)
        rk_s[rows, :] = _head_rinv(k_ref[0, rows, :].astype(F32), head_ones)
        return carry

    lax.fori_loop(0, seq // PRO_ROWS, norms, 0, unroll=16)

    def prologue(n, carry):
        rows = pl.ds(pl.multiple_of(n * PRO_ROWS, PRO_ROWS), PRO_ROWS)
        c, sn = c_ref[0, rows, :], s_ref[0, rows, :]
        q = q_ref[0, rows, :].astype(F32) * rq_s[rows, :] * qg_ref[...]
        qn_s[rows, :] = _rotate(q, c, sn, swap) * Q_SCALE
        k = k_ref[0, rows, :].astype(F32) * rk_s[rows, :] * kg_ref[...]
        kn_s[rows, :] = _rotate(k, c, sn, swap)
        vn_s[rows, :] = v_ref[0, rows, :].astype(F32)
        return carry

    lax.fori_loop(0, seq // PRO_ROWS, prologue, 0, unroll=16)

    first, later = _band_bias_t(0, QBLK)
    bias_s[0] = first.astype(BF16)
    bias_s[1] = later.astype(BF16)
    eye = (lax.broadcasted_iota(jnp.int32, (QBLK, QBLK), 0)
           == lax.broadcasted_iota(jnp.int32, (QBLK, QBLK), 1))
    eye = jnp.where(eye, 1.0, 0.0).astype(BF16)
    eye_pair = jnp.concatenate([eye, eye], axis=1)
    zero_pad = jnp.zeros((QBLK, LANES), BF16)
    top = lax.broadcasted_iota(jnp.int32, (LANES, QBLK), 0) < HEAD_DIM
    ones_rows = jnp.ones((DEN_ROWS, 2 * QBLK), BF16)

    outs = (o0_s, o1_s, o2_s)
    lses = (l0_s, l1_s, l2_s)
    n_flat = seq // QBLK
    for (window, r), on_s, ln_s in zip(DIL_PATTERNS, outs, lses):
        assert window // r == QBLK
        n_blk = seq // r // QBLK
        blk_shift = n_blk.bit_length() - 1
        assert n_blk == 1 << blk_shift

        def pad(j, c2, n_blk=n_blk):
            z = j * (n_blk + 1)
            kc_s[pl.ds(pl.multiple_of(z * QBLK, QBLK), QBLK), :] = zero_pad
            vt_s[z] = zero_pad
            return c2

        lax.fori_loop(0, r, pad, 0)

        def split(idx, r=r, n_blk=n_blk, blk_shift=blk_shift):
            j = lax.shift_right_logical(idx, blk_shift)
            i = idx & (n_blk - 1)
            return j, i

        def gather(idx, c2, r=r, split=split):
            j, i = split(idx)
            if r == FINE_R:
                base = ((j & (COARSE_R - 1)) * (seq // COARSE_R)
                        + lax.shift_right_logical(j, COARSE_R.bit_length() - 1))
                src = pl.ds(base + (QBLK * r // COARSE_R) * i, QBLK, stride=r // COARSE_R)
                q, k, v = q4_s[src, :], k4_s[src, :], v4_s[src, :]
            else:
                src = pl.ds(j + r * QBLK * i, QBLK, stride=r)
                q, k, v = qn_s[src, :], kn_s[src, :], vn_s[src, :]
            if r == COARSE_R:
                dense = pl.ds(pl.multiple_of(idx * QBLK, QBLK), QBLK)
                q4_s[dense, :] = q
                k4_s[dense, :] = k
                v4_s[dense, :] = v
            k0 = pl.multiple_of((idx + j + 1) * QBLK, QBLK)
            qt_s[idx] = q.T.astype(BF16)
            kc_s[pl.ds(k0, QBLK), :] = k.astype(BF16)
            vt_s[idx + j + 1] = v.T.astype(BF16)
            return c2

        lax.fori_loop(0, n_flat, gather, 0, unroll=16)

        def scores(idx, c2, split=split):
            j, i = split(idx)
            k0 = pl.multiple_of((idx + j) * QBLK, QBLK)
            qt = qt_s[idx]
            kw = kc_s[pl.ds(k0, 2 * QBLK), :]
            zero = jnp.zeros_like(qt)
            rhs = jnp.concatenate([jnp.where(top, qt, zero), jnp.where(top, zero, qt)], axis=1)
            lhs = jnp.concatenate([kw, bias_s[jnp.minimum(i, 1)]], axis=1)
            sc = jnp.dot(lhs, jnp.concatenate([rhs, eye_pair], axis=0), preferred_element_type=F32)
            m = jnp.max(sc, axis=0, keepdims=True)
            p_s[idx] = jnp.exp2(sc - m).astype(BF16)
            m_s[idx] = jnp.broadcast_to(m, (SUBLANES, 2 * QBLK))
            return c2

        lax.fori_loop(0, n_flat, scores, 0, unroll=16)

        def block(idx, c2, r=r, split=split, on_s=on_s, ln_s=ln_s):
            j, i = split(idx)
            p = p_s[idx]
            m = m_s[idx][0:1, :]
            lhs = jnp.concatenate(
                [jnp.concatenate([vt_s[idx + j], vt_s[idx + j + 1]], axis=1), ones_rows], axis=0)
            ot = jnp.dot(lhs, p, preferred_element_type=F32)
            den = ot[LANES:LANES + 1, :]
            inv = 1.0 / den
            lse = m + jnp.log2(den)
            o_t = jnp.concatenate(
                [ot[0:HEAD_DIM, 0:QBLK] * inv[:, 0:QBLK],
                 ot[HEAD_DIM:LANES, QBLK:2 * QBLK] * inv[:, QBLK:2 * QBLK]], axis=0)
            l_t = jnp.concatenate(
                [jnp.broadcast_to(lse[:, 0:QBLK], (HEAD_DIM, QBLK)),
                 jnp.broadcast_to(lse[:, QBLK:2 * QBLK], (HEAD_DIM, QBLK))], axis=0)
            dst = pl.ds(j + r * QBLK * i, QBLK, stride=r)
            on_s[dst, :] = o_t.T
            ln_s[dst, :] = l_t.T
            return c2

        lax.fori_loop(0, n_flat, block, 0, unroll=16)

    def combine(n, carry):
        r0 = pl.multiple_of(n * PRO_ROWS, PRO_ROWS)
        rows = pl.ds(r0, PRO_ROWS)
        ls = [l_s[rows, :] for l_s in lses]
        m = jnp.maximum(jnp.maximum(ls[0], ls[1]), ls[2])
        es = [jnp.exp2(l - m) for l in ls]
        num = es[0] * o0_s[rows, :] + es[1] * o1_s[rows, :] + es[2] * o2_s[rows, :]
        o_ref[0, rows, :] = (num / (es[0] + es[1] + es[2])).astype(o_ref.dtype)
        return carry

    lax.fori_loop(0, seq // PRO_ROWS, combine, 0, unroll=8)


def _dilated(proj, tables, q_gain, k_gain, col0):
    b, s, _ = proj.shape
    n_pairs = DIL_HEADS * HEAD_DIM // LANES
    t0 = col0 // LANES
    tab = pl.BlockSpec((1, s, LANES), lambda bi, p: (bi, 0, 0), pipeline_mode=pl.Buffered(1))
    two = lambda g: jnp.concatenate([g, g]).reshape(1, LANES)
    nat = pltpu.VMEM((s, LANES), F32)
    max_r = max(r for _, r in DIL_PATTERNS)
    return pl.pallas_call(
        functools.partial(_dilated_kernel, seq=s),
        grid=(b, n_pairs),
        in_specs=[
            pl.BlockSpec((1, s, LANES), lambda bi, p: (bi, 0, t0 + p)),
            pl.BlockSpec((1, s, LANES), lambda bi, p: (bi, 0, t0 + n_pairs + p)),
            pl.BlockSpec((1, s, LANES), lambda bi, p: (bi, 0, t0 + 2 * n_pairs + p)),
            tab, tab,
            _resident((1, LANES)),
            _resident((1, LANES)),
        ],
        out_specs=pl.BlockSpec((1, s, LANES), lambda bi, p: (bi, 0, p)),
        out_shape=jax.ShapeDtypeStruct((b, s, DIL_HEADS * HEAD_DIM), BF16),
        scratch_shapes=[
            nat, nat, nat, nat, nat, nat,
            pltpu.VMEM((s // QBLK, LANES, QBLK), BF16),
            pltpu.VMEM((s + max_r * QBLK, LANES), BF16),
            pltpu.VMEM((s // QBLK + max_r, LANES, QBLK), BF16),
            pltpu.VMEM((2, 2 * QBLK, QBLK), BF16),
            pltpu.VMEM((s // QBLK, 2 * QBLK, 2 * QBLK), BF16),
            pltpu.VMEM((s // QBLK, SUBLANES, 2 * QBLK), F32),
            nat, nat, nat, nat, nat, nat,
        ],
        compiler_params=_cparams(2),
        name="dilated_attention",
    )(proj, proj, proj, *tables, two(q_gain), two(k_gain))


def _swa_kernel(q_ref, k_ref, v_ref, c_ref, s_ref, qg_ref, kg_ref, sink_ref, o_ref,
                qt_s, kd_s, vt_s, bias_s, p_s, m_s, rq_s, rk_s, *, seq):
    group = SWA_Q_HEADS // SWA_KV_HEADS
    swap = _swap_matrix(ROPE_DIMS // 2)
    head_ones = _head_ones()
    kv_in_hi = (pl.program_id(1) % 2) == 1
    blk_per_step = PRO_ROWS // QBLK
    src = lax.broadcasted_iota(jnp.int32, (LANES, LANES), 0)
    dst = lax.broadcasted_iota(jnp.int32, (LANES, LANES), 1) & (HEAD_DIM - 1)
    dup = jnp.where(src == dst + jnp.where(kv_in_hi, HEAD_DIM, 0), 1.0, 0.0).astype(BF16)

    def norms(n, carry):
        rows = pl.ds(pl.multiple_of(n * PRO_ROWS, PRO_ROWS), PRO_ROWS)
        for t in range(group // 2):
            cols = slice(t * LANES, (t + 1) * LANES)
            rq_s[rows, cols] = _head_rinv(q_ref[0, rows, cols].astype(F32), head_ones)
        rk_s[rows, :] = _head_rinv(k_ref[0, rows, :].astype(F32), head_ones)
        return carry

    lax.fori_loop(0, seq // PRO_ROWS, norms, 0, unroll=16)

    def prologue(n, carry):
        r0 = pl.multiple_of(n * PRO_ROWS, PRO_ROWS)
        rows = pl.ds(r0, PRO_ROWS)
        c, sn = c_ref[0, rows, :], s_ref[0, rows, :]
        for t in range(group // 2):
            cols = slice(t * LANES, (t + 1) * LANES)
            q = q_ref[0, rows, cols].astype(F32) * rq_s[rows, cols] * qg_ref[...]
            q_t = (_rotate(q, c, sn, swap) * Q_SCALE).T
            for u in range(blk_per_step):
                qt_s[n * blk_per_step + u, t] = q_t[:, u * QBLK:(u + 1) * QBLK].astype(BF16)
        k = k_ref[0, rows, :].astype(F32) * rk_s[rows, :] * kg_ref[...]
        k = _rotate(k, c, sn, swap).astype(BF16)
        kd_s[pl.ds(r0 + QBLK, PRO_ROWS), :] = jnp.dot(k, dup, preferred_element_type=F32).astype(BF16)
        v_t = v_ref[0, rows, :].astype(F32).T
        v_t = jnp.where(kv_in_hi, v_t[HEAD_DIM:LANES, :], v_t[0:HEAD_DIM, :]).astype(BF16)
        for u in range(blk_per_step):
            vt_s[n * blk_per_step + u + 1] = v_t[:, u * QBLK:(u + 1) * QBLK]
        return carry

    lax.fori_loop(0, seq // PRO_ROWS, prologue, 0, unroll=16)

    first, later = _band_bias_t(1, QBLK)
    bias_s[0] = first.astype(BF16)
    bias_s[1] = later.astype(BF16)
    eye = (lax.broadcasted_iota(jnp.int32, (QBLK, QBLK), 0)
           == lax.broadcasted_iota(jnp.int32, (QBLK, QBLK), 1))
    eye = jnp.where(eye, 1.0, 0.0).astype(BF16)
    eye_pair = jnp.concatenate([eye, eye], axis=1)
    top = lax.broadcasted_iota(jnp.int32, (LANES, QBLK), 0) < HEAD_DIM
    kd_s[0:QBLK, :] = jnp.zeros((QBLK, LANES), BF16)
    vt_s[0] = jnp.zeros((HEAD_DIM, QBLK), BF16)
    lo = _lo_mask((QBLK, LANES))
    heads = (lo, jnp.logical_not(lo))
    sink = jnp.concatenate([sink_ref[0, hh:hh + 1, :] for hh in range(group)], axis=1) * LOG2_E
    ones_rows = jnp.ones((DEN_ROWS, 2 * QBLK), BF16)

    def scores(i, carry):
        d0 = pl.multiple_of(i * QBLK, QBLK)
        rows = pl.ds(d0, QBLK)
        kw = kd_s[pl.ds(d0, 2 * QBLK), :]
        lhs = jnp.concatenate([kw, bias_s[jnp.minimum(i, 1)]], axis=1)
        for t in range(group // 2):
            qt = qt_s[i, t]
            zero = jnp.zeros_like(qt)
            rhs = jnp.concatenate([jnp.where(top, qt, zero), jnp.where(top, zero, qt)], axis=1)
            cols = slice(2 * t * QBLK, 2 * (t + 1) * QBLK)
            sc = jnp.dot(lhs, jnp.concatenate([rhs, eye_pair], axis=0), preferred_element_type=F32)
            m = jnp.maximum(jnp.max(sc, axis=0, keepdims=True), sink[:, cols])
            p_s[i, :, cols] = jnp.exp2(sc - m).astype(BF16)
            m_s[i, :, cols] = jnp.broadcast_to(m, (SUBLANES, 2 * QBLK))
        return carry

    lax.fori_loop(0, seq // QBLK, scores, 0, unroll=BLOCK_UNROLL)

    def values(i, carry):
        rows = pl.ds(pl.multiple_of(i * QBLK, QBLK), QBLK)
        m = m_s[i][0:1, :]
        lhs = jnp.concatenate(
            [jnp.concatenate([vt_s[i], vt_s[i + 1]], axis=1), ones_rows], axis=0)
        ot = jnp.dot(lhs, p_s[i], preferred_element_type=F32)
        inv = 1.0 / (ot[HEAD_DIM:HEAD_DIM + 1, :] + jnp.exp2(sink - m))
        on = ot[0:HEAD_DIM, :] * inv
        for t in range(group // 2):
            pair_t = jnp.concatenate(
                [on[:, 2 * t * QBLK:(2 * t + 1) * QBLK], on[:, (2 * t + 1) * QBLK:(2 * t + 2) * QBLK]],
                axis=0)
            o_ref[0, rows, t * LANES:(t + 1) * LANES] = pair_t.T.astype(o_ref.dtype)
        return carry

    lax.fori_loop(0, seq // QBLK, values, 0, unroll=16)


def _swa(proj, tables, q_gain, k_gain, sinks):
    b, s, _ = proj.shape
    group = SWA_Q_HEADS // SWA_KV_HEADS
    q_w = group * HEAD_DIM
    k_t0 = SWA_Q_HEADS * HEAD_DIM // LANES
    v_t0 = k_t0 + SWA_KV_HEADS * HEAD_DIM // LANES
    tab = pl.BlockSpec((1, s, LANES), lambda bi, g: (bi, 0, 0), pipeline_mode=pl.Buffered(1))
    two = lambda g: jnp.concatenate([g, g]).reshape(1, LANES)
    sink_rows = jnp.broadcast_to(sinks.reshape(SWA_KV_HEADS, group, 1), (SWA_KV_HEADS, group, LANES))
    return pl.pallas_call(
        functools.partial(_swa_kernel, seq=s),
        grid=(b, SWA_KV_HEADS),
        in_specs=[
            pl.BlockSpec((1, s, q_w), lambda bi, g: (bi, 0, g)),
            pl.BlockSpec((1, s, LANES), lambda bi, g: (bi, 0, k_t0 + g // 2)),
            pl.BlockSpec((1, s, LANES), lambda bi, g: (bi, 0, v_t0 + g // 2)),
            tab, tab,
            _resident((1, LANES)),
            _resident((1, LANES)),
            pl.BlockSpec((1, group, LANES), lambda bi, g: (g, 0, 0)),
        ],
        out_specs=pl.BlockSpec((1, s, q_w), lambda bi, g: (bi, 0, g)),
        out_shape=jax.ShapeDtypeStruct((b, s, SWA_Q_HEADS * HEAD_DIM), BF16),
        scratch_shapes=[
            pltpu.VMEM((s // QBLK, group // 2, LANES, QBLK), BF16),
            pltpu.VMEM((s + QBLK, LANES), BF16),
            pltpu.VMEM((s // QBLK + 1, HEAD_DIM, QBLK), BF16),
            pltpu.VMEM((2, 2 * QBLK, QBLK), BF16),
            pltpu.VMEM((s // QBLK, 2 * QBLK, group * QBLK), BF16),
            pltpu.VMEM((s // QBLK, SUBLANES, group * QBLK), F32),
            pltpu.VMEM((s, q_w), F32),
            pltpu.VMEM((s, LANES), F32),
        ],
        compiler_params=_cparams(2),
        name="swa_attention",
    )(proj, proj, proj, *tables, two(q_gain), two(k_gain), sink_rows)


def kernel(x, positions, norm_mix, norm_mlp, mlp_w_up, mlp_w_down, hyb_w_in, hyb_w_out, ret_gn_gain, dil_q_gain, dil_k_gain, swa_w_qkv, swa_b_qkv, swa_w_out, swa_q_gain, swa_k_gain, swa_sinks):
    b, s, d = x.shape
    depth = norm_mix.shape[0]
    ret_tab = _rope_table(positions, RET_QK_DIM // 2, RET_THETA, RET_QK_DIM)
    rope_tab = _rope_table(positions, ROPE_DIMS // 2, ROPE_THETA, ROPE_DIMS)
    log_gamma = jnp.log1p(-jnp.exp2(-5.0 - jnp.arange(RET_HEADS, dtype=F32)))
    ret_w = RET_HEADS * RET_V_DIM
    dil_col0 = 2 * RET_HEADS * RET_QK_DIM + 2 * ret_w

    x2d = x.reshape(b * s, d)
    for layer in range(depth):
        i = layer // 2
        if layer % 2 == 0:
            zero_bias = jnp.zeros((hyb_w_in.shape[2],), F32)
            proj = _norm_proj(x2d, norm_mix[layer], hyb_w_in, i, zero_bias).reshape(b, s, -1)
            ra = _retention(proj, ret_tab, log_gamma, ret_gn_gain[i])
            da = _dilated(proj, rope_tab, dil_q_gain[i], dil_k_gain[i], dil_col0)
            mixed = [ra.reshape(b * s, -1), da.reshape(b * s, -1)]
            w_out = hyb_w_out
        else:
            proj = _norm_proj(x2d, norm_mix[layer], swa_w_qkv, i, swa_b_qkv[i]).reshape(b, s, -1)
            att = _swa(proj, rope_tab, swa_q_gain[i], swa_k_gain[i], swa_sinks[i])
            mixed = [att.reshape(b * s, -1)]
            w_out = swa_w_out
        x2d = _out_mlp(mixed, x2d, w_out, i, norm_mlp[layer], mlp_w_up, mlp_w_down, layer)
    return x2d.reshape(b, s, d)
```

```python
import functools

import jax
import jax.numpy as jnp
import numpy as np
from jax import lax
from jax.experimental import pallas as pl
from jax.experimental.pallas import tpu as pltpu

F32 = jnp.float32
BF16 = jnp.bfloat16

D_MODEL = 1024
D_FF = 4 * D_MODEL
HEAD_DIM = 64
EPS = 1e-6
RET_HEADS = 4
RET_QK_DIM = 64
RET_V_DIM = 128
RET_CHUNK = 128
RET_THETA = 10000.0
DIL_HEADS = 8
DIL_PATTERNS = ((128, 1), (512, 4), (2048, 16))
SWA_Q_HEADS = 16
SWA_KV_HEADS = 4
SWA_WINDOW = 128
ROPE_THETA = 500000.0
ROPE_DIMS = HEAD_DIM // 4

LANES = 128
SUBLANES = 8
QBLK = 128
VMEM_LIMIT_BYTES = 56 * 1024 * 1024
NEG_BIG = -1e30
LOG2_E = 1.4426950408889634
Q_SCALE = HEAD_DIM ** -0.5 * LOG2_E

ROW_TILE = 512
MLP_ROW_TILE = 512
FF_CHUNK = 1024
PRO_ROWS = 128
BLOCK_UNROLL = 8
DEN_ROWS = 16
COARSE_R, FINE_R = DIL_PATTERNS[1][1], DIL_PATTERNS[2][1]


def _cparams(n_axes):
    return pltpu.CompilerParams(
        dimension_semantics=("arbitrary",) * n_axes,
        vmem_limit_bytes=VMEM_LIMIT_BYTES,
    )


def _resident(shape):
    nd = len(shape)
    return pl.BlockSpec(shape, lambda *_: (0,) * nd, pipeline_mode=pl.Buffered(1))


def _resident_layer(stack_shape, layer):
    return pl.BlockSpec((None,) + tuple(stack_shape[1:]), lambda *_: (layer, 0, 0),
                        pipeline_mode=pl.Buffered(1))


def _split3(x):
    hi = x.astype(BF16)
    r1 = x - hi.astype(F32)
    mid = r1.astype(BF16)
    lo = (r1 - mid.astype(F32)).astype(BF16)
    return hi, mid, lo


def _trig_kernel(p_ref, spread_ref, inv_ref, ec_ref, es_ref, base_ref, c_ref, s_ref, *, n_pos):
    dense = sum(jnp.dot(part, spread_ref[...], preferred_element_type=F32)
                for part in _split3(p_ref[...]))
    ang = dense * inv_ref[...]
    cos_parts = _split3(jnp.cos(ang))
    sin_parts = _split3(jnp.sin(ang))
    rows = p_ref.shape[0]
    for c in range(n_pos):
        dst = pl.ds(c, rows, stride=n_pos)
        c_ref[dst, :] = base_ref[...] + sum(
            jnp.dot(part, ec_ref[c], preferred_element_type=F32) for part in cos_parts)
        s_ref[dst, :] = sum(
            jnp.dot(part, es_ref[c], preferred_element_type=F32) for part in sin_parts)


def _selectors(half):
    n_pos = LANES // half
    ec = np.zeros((n_pos, LANES, LANES), np.float32)
    es = np.zeros((n_pos, LANES, LANES), np.float32)
    base = np.zeros((1, LANES), np.float32)
    for j in range(LANES):
        d = j % HEAD_DIM
        if d >= 2 * half:
            base[0, j] = 1.0
            continue
        f = d % half
        for c in range(n_pos):
            ec[c, c * half + f, j] = 1.0
            es[c, c * half + f, j] = -1.0 if d < half else 1.0
    return n_pos, jnp.asarray(ec, BF16), jnp.asarray(es, BF16), jnp.asarray(base)


def _rope_table(positions, half, theta, n_rot):
    b, s = positions.shape
    posf = positions.astype(F32)
    inv = jnp.power(jnp.float32(theta), -jnp.arange(half, dtype=F32) * (2.0 / n_rot))
    n_pos, ec, es, base = _selectors(half)
    rows = s // n_pos
    p = posf.reshape(b, rows, n_pos)
    spread = jnp.asarray(np.repeat(np.eye(n_pos, dtype=np.float32), half, axis=1), BF16)
    inv_row = jnp.tile(inv, n_pos)[None, :]
    tile = min(rows, 2048 // n_pos)
    assert rows % tile == 0
    dense = pl.BlockSpec((None, tile, n_pos), lambda bi, i: (bi, i, 0))
    wide = pl.BlockSpec((None, tile * n_pos, LANES), lambda bi, i: (bi, i, 0))
    c_tab, s_tab = pl.pallas_call(
        functools.partial(_trig_kernel, n_pos=n_pos),
        grid=(b, rows // tile),
        in_specs=[dense, _resident(spread.shape), _resident(inv_row.shape), _resident(ec.shape),
                  _resident(es.shape), _resident(base.shape)],
        out_specs=[wide, wide],
        out_shape=[jax.ShapeDtypeStruct((b, s, LANES), F32)] * 2,
        compiler_params=_cparams(2),
        name="trig_tables",
    )(p, spread, inv_row, ec, es, base)
    return c_tab, s_tab


def _swap_matrix(half):
    src = lax.broadcasted_iota(jnp.int32, (LANES, LANES), 0)
    dst = lax.broadcasted_iota(jnp.int32, (LANES, LANES), 1)
    d = dst & (HEAD_DIM - 1)
    want = jnp.where(d < half, dst + half, jnp.where(d < 2 * half, dst - half, -1))
    return jnp.where(src == want, 1.0, 0.0).astype(BF16)


def _rotate(x, c, s, swap):
    swapped = jnp.dot(x.astype(BF16), swap, preferred_element_type=F32)
    return x * c + swapped * s


def _lo_mask(shape):
    return lax.broadcasted_iota(jnp.int32, shape, len(shape) - 1) < HEAD_DIM


def _head_ones():
    r = lax.broadcasted_iota(jnp.int32, (LANES, LANES), 0) < HEAD_DIM
    c = lax.broadcasted_iota(jnp.int32, (LANES, LANES), 1) < HEAD_DIM
    return jnp.where(r == c, 1.0, 0.0).astype(BF16)


def _head_rinv(x, head_ones):
    ss = jnp.dot((x * x).astype(BF16), head_ones, preferred_element_type=F32)
    return lax.rsqrt(ss * (1.0 / HEAD_DIM) + EPS)


def _norm_proj_kernel(x_ref, g_ref, w_ref, b_ref, o_ref, *, n_chunk):
    x = x_ref[...]
    ms = jnp.mean(x * x, axis=-1, keepdims=True)
    h = (x * lax.rsqrt(ms + EPS) * g_ref[...]).astype(BF16)
    n = o_ref.shape[-1]
    for c in range(0, n, n_chunk):
        acc = jnp.dot(h, w_ref[:, c:c + n_chunk].astype(BF16), preferred_element_type=F32)
        o_ref[:, c:c + n_chunk] = (acc + b_ref[:, c:c + n_chunk]).astype(o_ref.dtype)


def _norm_proj(x2d, gain, w_stack, layer, bias):
    m, d = x2d.shape
    n = w_stack.shape[2]
    return pl.pallas_call(
        functools.partial(_norm_proj_kernel, n_chunk=512),
        grid=(m // ROW_TILE,),
        in_specs=[
            pl.BlockSpec((ROW_TILE, d), lambda i: (i, 0)),
            _resident((1, d)),
            _resident_layer(w_stack.shape, layer),
            _resident((1, n)),
        ],
        out_specs=pl.BlockSpec((ROW_TILE, n), lambda i: (i, 0)),
        out_shape=jax.ShapeDtypeStruct((m, n), BF16),
        compiler_params=_cparams(1),
        name="norm_proj",
    )(x2d, gain.reshape(1, d), w_stack, bias.reshape(1, n))


def _out_mlp_kernel(*refs, n_mix):
    a_refs = refs[:n_mix]
    x_ref, wo_ref, g_ref, wup_ref, wdn_ref, o_ref = refs[n_mix:]
    mixed = a_refs[0][...] if n_mix == 1 else jnp.concatenate([a[...] for a in a_refs], axis=-1)
    x1 = x_ref[...] + jnp.dot(mixed, wo_ref[...].astype(BF16), preferred_element_type=F32)
    ms = jnp.mean(x1 * x1, axis=-1, keepdims=True)
    h = (x1 * lax.rsqrt(ms + EPS) * g_ref[...]).astype(BF16)
    y = x1
    for c in range(0, D_FF, FF_CHUNK):
        u = jnp.dot(h, wup_ref[:, c:c + FF_CHUNK].astype(BF16), preferred_element_type=F32)
        u = jnp.square(jnp.maximum(u, 0.0)).astype(BF16)
        y = y + jnp.dot(u, wdn_ref[c:c + FF_CHUNK, :].astype(BF16), preferred_element_type=F32)
    o_ref[...] = y


def _out_mlp(mixed, x2d, w_out, out_layer, gain, w_up, w_down, mlp_layer):
    m, d = x2d.shape
    n_mix = len(mixed)
    in_specs = [pl.BlockSpec((MLP_ROW_TILE, a.shape[1]), lambda i: (i, 0)) for a in mixed]
    in_specs.append(pl.BlockSpec((MLP_ROW_TILE, d), lambda i: (i, 0)))
    in_specs += [_resident_layer(w_out.shape, out_layer), _resident((1, d)),
                 _resident_layer(w_up.shape, mlp_layer), _resident_layer(w_down.shape, mlp_layer)]
    return pl.pallas_call(
        functools.partial(_out_mlp_kernel, n_mix=n_mix),
        grid=(m // MLP_ROW_TILE,),
        in_specs=in_specs,
        out_specs=pl.BlockSpec((MLP_ROW_TILE, d), lambda i: (i, 0)),
        out_shape=jax.ShapeDtypeStruct((m, d), F32),
        compiler_params=_cparams(1),
        name="out_mlp",
    )(*mixed, x2d, w_out, gain.reshape(1, d), w_up, w_down)


def _retention_kernel(q_ref, k_ref, v_ref, g_ref, c_ref, s_ref, lg_ref, gn_ref,
                      o_ref, q_s, k_s, kt_s, o_s, kv_s, *, seq):
    n_chunks = seq // RET_CHUNK
    swap = _swap_matrix(RET_QK_DIM // 2)
    cs = RET_CHUNK

    def prologue(n, carry):
        r0 = pl.multiple_of(n * cs, cs)
        rows = pl.ds(r0, cs)
        c, sn = c_ref[0, rows, :], s_ref[0, rows, :]
        q = _rotate(q_ref[0, rows, :].astype(F32), c, sn, swap)
        k = _rotate(k_ref[0, rows, :].astype(F32), c, sn, swap) * (RET_QK_DIM ** -0.5)
        q_s[n] = q.astype(BF16)
        k_s[n] = k.astype(BF16)
        kt_s[n] = k.T
        return carry

    lax.fori_loop(0, n_chunks, prologue, 0, unroll=16)

    lo = _lo_mask((cs, LANES))
    row = lax.broadcasted_iota(jnp.int32, (cs, cs), 0).astype(F32)
    col = lax.broadcasted_iota(jnp.int32, (cs, cs), 1).astype(F32)
    diff = row - col
    consts = []
    for hh in range(2):
        lg = lg_ref[0, hh:hh + 1, :]
        decay = jnp.where(diff >= 0, jnp.exp(lg * jnp.maximum(diff, 0.0)), 0.0)
        xi = jnp.exp(lg * (row + 1.0))
        zeta = jnp.exp(lg * (cs - 1.0 - col[0:1, :]))
        cd = jnp.exp(lg * float(cs))
        head = lo if hh == 0 else jnp.logical_not(lo)
        consts.append((decay, xi, zeta, cd, head))

    def inner(n, carry):
        rows = pl.ds(pl.multiple_of(n * cs, cs), cs)
        qc, kc, ktc = q_s[n], k_s[n], kt_s[n]
        for hh in range(2):
            decay, xi, zeta, cd, head = consts[hh]
            cols = slice(hh * RET_V_DIM, (hh + 1) * RET_V_DIM)
            qm = jnp.where(head, qc, jnp.zeros_like(qc))
            vc = v_ref[0, rows, cols]
            sc = lax.dot_general(qm, kc, (((1,), (1,)), ((), ())), preferred_element_type=F32) * decay
            o_s[rows, cols] = jnp.dot(sc.astype(BF16), vc, preferred_element_type=F32)
            kz = (ktc * zeta).astype(BF16)
            kv_s[n, hh] = jnp.dot(kz, vc, preferred_element_type=F32)
        return carry

    lax.fori_loop(0, n_chunks, inner, 0, unroll=16)

    def cross(n, state):
        rows = pl.ds(pl.multiple_of(n * cs, cs), cs)
        qc = q_s[n]
        new_state = []
        for hh in range(2):
            decay, xi, zeta, cd, head = consts[hh]
            cols = slice(hh * RET_V_DIM, (hh + 1) * RET_V_DIM)
            r_prev = state[hh]
            qm = jnp.where(head, qc, jnp.zeros_like(qc))
            o_s[rows, cols] += jnp.dot(qm, r_prev.astype(BF16), preferred_element_type=F32) * xi
            new_state.append(r_prev * cd + kv_s[n, hh])
        return tuple(new_state)

    zero = jnp.zeros((LANES, RET_V_DIM), F32)
    lax.fori_loop(0, n_chunks, cross, (zero, zero), unroll=16)

    def finish(n, carry):
        rows = pl.ds(pl.multiple_of(n * cs, cs), cs)
        for hh in range(2):
            cols = slice(hh * RET_V_DIM, (hh + 1) * RET_V_DIM)
            o = o_s[rows, cols]
            mu = jnp.mean(o, axis=-1, keepdims=True)
            dev = o - mu
            var = jnp.mean(dev * dev, axis=-1, keepdims=True)
            y = dev * lax.rsqrt(var + EPS) * gn_ref[0, hh:hh + 1, :]
            gate = g_ref[0, rows, cols].astype(F32)
            o_ref[0, rows, cols] = (gate * jax.nn.sigmoid(gate) * y).astype(o_ref.dtype)
        return carry

    lax.fori_loop(0, n_chunks, finish, 0, unroll=2)


def _retention(proj, tables, log_gamma, gn_gain):
    b, s, _ = proj.shape
    n_pairs = RET_HEADS // 2
    pair_w = 2 * RET_V_DIM
    qk_tiles = RET_HEADS * RET_QK_DIM // LANES
    v_off = 2 * qk_tiles * LANES // pair_w
    g_off = v_off + RET_HEADS * RET_V_DIM // pair_w
    tab = pl.BlockSpec((1, s, LANES), lambda bi, p: (bi, 0, 0), pipeline_mode=pl.Buffered(1))
    lg = jnp.broadcast_to(log_gamma.reshape(n_pairs, 2, 1), (n_pairs, 2, LANES))
    n_chunks = s // RET_CHUNK
    return pl.pallas_call(
        functools.partial(_retention_kernel, seq=s),
        grid=(b, n_pairs),
        in_specs=[
            pl.BlockSpec((1, s, LANES), lambda bi, p: (bi, 0, p)),
            pl.BlockSpec((1, s, LANES), lambda bi, p: (bi, 0, qk_tiles + p)),
            pl.BlockSpec((1, s, pair_w), lambda bi, p: (bi, 0, v_off + p)),
            pl.BlockSpec((1, s, pair_w), lambda bi, p: (bi, 0, g_off + p)),
            tab, tab,
            pl.BlockSpec((1, 2, LANES), lambda bi, p: (p, 0, 0)),
            pl.BlockSpec((1, 2, RET_V_DIM), lambda bi, p: (p, 0, 0)),
        ],
        out_specs=pl.BlockSpec((1, s, pair_w), lambda bi, p: (bi, 0, p)),
        out_shape=jax.ShapeDtypeStruct((b, s, RET_HEADS * RET_V_DIM), BF16),
        scratch_shapes=[
            pltpu.VMEM((n_chunks, RET_CHUNK, LANES), BF16),
            pltpu.VMEM((n_chunks, RET_CHUNK, LANES), BF16),
            pltpu.VMEM((n_chunks, LANES, RET_CHUNK), F32),
            pltpu.VMEM((s, pair_w), F32),
            pltpu.VMEM((n_chunks, 2, LANES, RET_V_DIM), F32),
        ],
        compiler_params=_cparams(2),
        name="retention",
    )(proj, proj, proj, proj, *tables, lg, gn_gain.reshape(n_pairs, 2, RET_V_DIM))


def _band_bias_t(lo_off, hi_off):
    c = lax.broadcasted_iota(jnp.int32, (2 * QBLK, QBLK), 0)
    a = lax.broadcasted_iota(jnp.int32, (2 * QBLK, QBLK), 1)
    band = (c - a >= lo_off) & (c - a <= hi_off)
    later = jnp.where(band, 0.0, NEG_BIG).astype(F32)
    first = jnp.where(band & (c >= QBLK), 0.0, NEG_BIG).astype(F32)
    return first, later


def _dilated_kernel(q_ref, k_ref, v_ref, c_ref, s_ref, qg_ref, kg_ref, o_ref,
                    qn_s, kn_s, vn_s, q4_s, k4_s, v4_s, qt_s, kc_s, vt_s, bias_s, p_s, m_s,
                    o0_s, o1_s, o2_s, l0_s, l1_s, l2_s, *, seq):
    swap = _swap_matrix(ROPE_DIMS // 2)
    head_ones = _head_ones()
    rq_s, rk_s = o0_s, l0_s

    def norms(n, carry):
        rows = pl.ds(pl.multiple_of(n * PRO_ROWS, PRO_ROWS), PRO_ROWS)
        rq_s[rows, :] = _head_rinv(q_ref[0, rows, :].astype(F32), head_ones)
        rk_s[rows, :] = _head_rinv(k_ref[0, rows, :].astype(F32), head_ones)
        return carry

    lax.fori_loop(0, seq // PRO_ROWS, norms, 0, unroll=16)

    def prologue(n, carry):
        rows = pl.ds(pl.multiple_of(n * PRO_ROWS, PRO_ROWS), PRO_ROWS)
        c, sn = c_ref[0, rows, :], s_ref[0, rows, :]
        q = q_ref[0, rows, :].astype(F32) * rq_s[rows, :] * qg_ref[...]
        qn_s[rows, :] = _rotate(q, c, sn, swap) * Q_SCALE
        k = k_ref[0, rows, :].astype(F32) * rk_s[rows, :] * kg_ref[...]
        kn_s[rows, :] = _rotate(k, c, sn, swap)
        vn_s[rows, :] = v_ref[0, rows, :].astype(F32)
        return carry

    lax.fori_loop(0, seq // PRO_ROWS, prologue, 0, unroll=16)

    first, later = _band_bias_t(0, QBLK)
    bias_s[0] = first.astype(BF16)
    bias_s[1] = later.astype(BF16)
    eye = (lax.broadcasted_iota(jnp.int32, (QBLK, QBLK), 0)
           == lax.broadcasted_iota(jnp.int32, (QBLK, QBLK), 1))
    eye = jnp.where(eye, 1.0, 0.0).astype(BF16)
    eye_pair = jnp.concatenate([eye, eye], axis=1)
    zero_pad = jnp.zeros((QBLK, LANES), BF16)
    top = lax.broadcasted_iota(jnp.int32, (LANES, QBLK), 0) < HEAD_DIM
    ones_rows = jnp.ones((DEN_ROWS, 2 * QBLK), BF16)

    outs = (o0_s, o1_s, o2_s)
    lses = (l0_s, l1_s, l2_s)
    n_flat = seq // QBLK
    for (window, r), on_s, ln_s in zip(DIL_PATTERNS, outs, lses):
        assert window // r == QBLK
        n_blk = seq // r // QBLK
        blk_shift = n_blk.bit_length() - 1
        assert n_blk == 1 << blk_shift

        def pad(j, c2, n_blk=n_blk):
            z = j * (n_blk + 1)
            kc_s[pl.ds(pl.multiple_of(z * QBLK, QBLK), QBLK), :] = zero_pad
            vt_s[z] = zero_pad
            return c2

        lax.fori_loop(0, r, pad, 0)

        def split(idx, r=r, n_blk=n_blk, blk_shift=blk_shift):
            j = lax.shift_right_logical(idx, blk_shift)
            i = idx & (n_blk - 1)
            return j, i

        def gather(idx, c2, r=r, split=split):
            j, i = split(idx)
            if r == FINE_R:
                base = ((j & (COARSE_R - 1)) * (seq // COARSE_R)
                        + lax.shift_right_logical(j, COARSE_R.bit_length() - 1))
                src = pl.ds(base + (QBLK * r // COARSE_R) * i, QBLK, stride=r // COARSE_R)
                q, k, v = q4_s[src, :], k4_s[src, :], v4_s[src, :]
            else:
                src = pl.ds(j + r * QBLK * i, QBLK, stride=r)
                q, k, v = qn_s[src, :], kn_s[src, :], vn_s[src, :]
            if r == COARSE_R:
                dense = pl.ds(pl.multiple_of(idx * QBLK, QBLK), QBLK)
                q4_s[dense, :] = q
                k4_s[dense, :] = k
                v4_s[dense, :] = v
            k0 = pl.multiple_of((idx + j + 1) * QBLK, QBLK)
            qt_s[idx] = q.T.astype(BF16)
            kc_s[pl.ds(k0, QBLK), :] = k.astype(BF16)
            vt_s[idx + j + 1] = v.T.astype(BF16)
            return c2

        lax.fori_loop(0, n_flat, gather, 0, unroll=16)

        def scores(idx, c2, split=split):
            j, i = split(idx)
            k0 = pl.multiple_of((idx + j) * QBLK, QBLK)
            qt = qt_s[idx]
            kw = kc_s[pl.ds(k0, 2 * QBLK), :]
            zero = jnp.zeros_like(qt)
            rhs = jnp.concatenate([jnp.where(top, qt, zero), jnp.where(top, zero, qt)], axis=1)
            lhs = jnp.concatenate([kw, bias_s[jnp.minimum(i, 1)]], axis=1)
            sc = jnp.dot(lhs, jnp.concatenate([rhs, eye_pair], axis=0), preferred_element_type=F32)
            m = jnp.max(sc, axis=0, keepdims=True)
            p_s[idx] = jnp.exp2(sc - m).astype(BF16)
            m_s[idx] = jnp.broadcast_to(m, (SUBLANES, 2 * QBLK))
            return c2

        lax.fori_loop(0, n_flat, scores, 0, unroll=8)

        def block(idx, c2, r=r, split=split, on_s=on_s, ln_s=ln_s):
            j, i = split(idx)
            p = p_s[idx]
            m = m_s[idx][0:1, :]
            lhs = jnp.concatenate(
                [jnp.concatenate([vt_s[idx + j], vt_s[idx + j + 1]], axis=1), ones_rows], axis=0)
            ot = jnp.dot(lhs, p, preferred_element_type=F32)
            den = ot[LANES:LANES + 1, :]
            inv = 1.0 / den
            lse = m + jnp.log2(den)
            o_t = jnp.concatenate(
                [ot[0:HEAD_DIM, 0:QBLK] * inv[:, 0:QBLK],
                 ot[HEAD_DIM:LANES, QBLK:2 * QBLK] * inv[:, QBLK:2 * QBLK]], axis=0)
            l_t = jnp.concatenate(
                [jnp.broadcast_to(lse[:, 0:QBLK], (HEAD_DIM, QBLK)),
                 jnp.broadcast_to(lse[:, QBLK:2 * QBLK], (HEAD_DIM, QBLK))], axis=0)
            dst = pl.ds(j + r * QBLK * i, QBLK, stride=r)
            on_s[dst, :] = o_t.T
            ln_s[dst, :] = l_t.T
            return c2

        lax.fori_loop(0, n_flat, block, 0, unroll=16)

    def combine(n, carry):
        r0 = pl.multiple_of(n * PRO_ROWS, PRO_ROWS)
        rows = pl.ds(r0, PRO_ROWS)
        ls = [l_s[rows, :] for l_s in lses]
        m = jnp.maximum(jnp.maximum(ls[0], ls[1]), ls[2])
        es = [jnp.exp2(l - m) for l in ls]
        num = es[0] * o0_s[rows, :] + es[1] * o1_s[rows, :] + es[2] * o2_s[rows, :]
        o_ref[0, rows, :] = (num / (es[0] + es[1] + es[2])).astype(o_ref.dtype)
        return carry

    lax.fori_loop(0, seq // PRO_ROWS, combine, 0, unroll=8)


def _dilated(proj, tables, q_gain, k_gain, col0):
    b, s, _ = proj.shape
    n_pairs = DIL_HEADS * HEAD_DIM // LANES
    t0 = col0 // LANES
    tab = pl.BlockSpec((1, s, LANES), lambda bi, p: (bi, 0, 0), pipeline_mode=pl.Buffered(1))
    two = lambda g: jnp.concatenate([g, g]).reshape(1, LANES)
    nat = pltpu.VMEM((s, LANES), F32)
    max_r = max(r for _, r in DIL_PATTERNS)
    return pl.pallas_call(
        functools.partial(_dilated_kernel, seq=s),
        grid=(b, n_pairs),
        in_specs=[
            pl.BlockSpec((1, s, LANES), lambda bi, p: (bi, 0, t0 + p)),
            pl.BlockSpec((1, s, LANES), lambda bi, p: (bi, 0, t0 + n_pairs + p)),
            pl.BlockSpec((1, s, LANES), lambda bi, p: (bi, 0, t0 + 2 * n_pairs + p)),
            tab, tab,
            _resident((1, LANES)),
            _resident((1, LANES)),
        ],
        out_specs=pl.BlockSpec((1, s, LANES), lambda bi, p: (bi, 0, p)),
        out_shape=jax.ShapeDtypeStruct((b, s, DIL_HEADS * HEAD_DIM), BF16),
        scratch_shapes=[
            nat, nat, nat, nat, nat, nat,
            pltpu.VMEM((s // QBLK, LANES, QBLK), BF16),
            pltpu.VMEM((s + max_r * QBLK, LANES), BF16),
            pltpu.VMEM((s // QBLK + max_r, LANES, QBLK), BF16),
            pltpu.VMEM((2, 2 * QBLK, QBLK), BF16),
            pltpu.VMEM((s // QBLK, 2 * QBLK, 2 * QBLK), BF16),
            pltpu.VMEM((s // QBLK, SUBLANES, 2 * QBLK), F32),
            nat, nat, nat, nat, nat, nat,
        ],
        compiler_params=_cparams(2),
        name="dilated_attention",
    )(proj, proj, proj, *tables, two(q_gain), two(k_gain))


def _swa_kernel(q_ref, k_ref, v_ref, c_ref, s_ref, qg_ref, kg_ref, sink_ref, o_ref,
                qt_s, kd_s, vt_s, bias_s, p_s, m_s, rq_s, rk_s, *, seq):
    group = SWA_Q_HEADS // SWA_KV_HEADS
    swap = _swap_matrix(ROPE_DIMS // 2)
    head_ones = _head_ones()
    kv_in_hi = (pl.program_id(1) % 2) == 1
    blk_per_step = PRO_ROWS // QBLK
    src = lax.broadcasted_iota(jnp.int32, (LANES, LANES), 0)
    dst = lax.broadcasted_iota(jnp.int32, (LANES, LANES), 1) & (HEAD_DIM - 1)
    dup = jnp.where(src == dst + jnp.where(kv_in_hi, HEAD_DIM, 0), 1.0, 0.0).astype(BF16)

    def norms(n, carry):
        rows = pl.ds(pl.multiple_of(n * PRO_ROWS, PRO_ROWS), PRO_ROWS)
        for t in range(group // 2):
            cols = slice(t * LANES, (t + 1) * LANES)
            rq_s[rows, cols] = _head_rinv(q_ref[0, rows, cols].astype(F32), head_ones)
        rk_s[rows, :] = _head_rinv(k_ref[0, rows, :].astype(F32), head_ones)
        return carry

    lax.fori_loop(0, seq // PRO_ROWS, norms, 0, unroll=16)

    def prologue(n, carry):
        r0 = pl.multiple_of(n * PRO_ROWS, PRO_ROWS)
        rows = pl.ds(r0, PRO_ROWS)
        c, sn = c_ref[0, rows, :], s_ref[0, rows, :]
        for t in range(group // 2):
            cols = slice(t * LANES, (t + 1) * LANES)
            q = q_ref[0, rows, cols].astype(F32) * rq_s[rows, cols] * qg_ref[...]
            q_t = (_rotate(q, c, sn, swap) * Q_SCALE).T
            for u in range(blk_per_step):
                qt_s[n * blk_per_step + u, t] = q_t[:, u * QBLK:(u + 1) * QBLK].astype(BF16)
        k = k_ref[0, rows, :].astype(F32) * rk_s[rows, :] * kg_ref[...]
        k = _rotate(k, c, sn, swap).astype(BF16)
        kd_s[pl.ds(r0 + QBLK, PRO_ROWS), :] = jnp.dot(k, dup, preferred_element_type=F32).astype(BF16)
        v_t = v_ref[0, rows, :].astype(F32).T
        v_t = jnp.where(kv_in_hi, v_t[HEAD_DIM:LANES, :], v_t[0:HEAD_DIM, :]).astype(BF16)
        for u in range(blk_per_step):
            vt_s[n * blk_per_step + u + 1] = v_t[:, u * QBLK:(u + 1) * QBLK]
        return carry

    lax.fori_loop(0, seq // PRO_ROWS, prologue, 0, unroll=16)

    first, later = _band_bias_t(1, QBLK)
    bias_s[0] = first.astype(BF16)
    bias_s[1] = later.astype(BF16)
    eye = (lax.broadcasted_iota(jnp.int32, (QBLK, QBLK), 0)
           == lax.broadcasted_iota(jnp.int32, (QBLK, QBLK), 1))
    eye = jnp.where(eye, 1.0, 0.0).astype(BF16)
    eye_pair = jnp.concatenate([eye, eye], axis=1)
    top = lax.broadcasted_iota(jnp.int32, (LANES, QBLK), 0) < HEAD_DIM
    kd_s[0:QBLK, :] = jnp.zeros((QBLK, LANES), BF16)
    vt_s[0] = jnp.zeros((HEAD_DIM, QBLK), BF16)
    sink = jnp.concatenate([sink_ref[0, hh:hh + 1, :] for hh in range(group)], axis=1) * LOG2_E
    ones_rows = jnp.ones((DEN_ROWS, 2 * QBLK), BF16)

    def scores(i, carry):
        d0 = pl.multiple_of(i * QBLK, QBLK)
        rows = pl.ds(d0, QBLK)
        kw = kd_s[pl.ds(d0, 2 * QBLK), :]
        lhs = jnp.concatenate([kw, bias_s[jnp.minimum(i, 1)]], axis=1)
        for t in range(group // 2):
            qt = qt_s[i, t]
            zero = jnp.zeros_like(qt)
            rhs = jnp.concatenate([jnp.where(top, qt, zero), jnp.where(top, zero, qt)], axis=1)
            cols = slice(2 * t * QBLK, 2 * (t + 1) * QBLK)
            sc = jnp.dot(lhs, jnp.concatenate([rhs, eye_pair], axis=0), preferred_element_type=F32)
            m = jnp.maximum(jnp.max(sc, axis=0, keepdims=True), sink[:, cols])
            p_s[i, :, cols] = jnp.exp2(sc - m).astype(BF16)
            m_s[i, :, cols] = jnp.broadcast_to(m, (SUBLANES, 2 * QBLK))
        return carry

    lax.fori_loop(0, seq // QBLK, scores, 0, unroll=BLOCK_UNROLL)

    def values(i, carry):
        rows = pl.ds(pl.multiple_of(i * QBLK, QBLK), QBLK)
        m = m_s[i][0:1, :]
        lhs = jnp.concatenate(
            [jnp.concatenate([vt_s[i], vt_s[i + 1]], axis=1), ones_rows], axis=0)
        ot = jnp.dot(lhs, p_s[i], preferred_element_type=F32)
        inv = 1.0 / (ot[HEAD_DIM:HEAD_DIM + 1, :] + jnp.exp2(sink - m))
        on = ot[0:HEAD_DIM, :] * inv
        for t in range(group // 2):
            pair_t = jnp.concatenate(
                [on[:, 2 * t * QBLK:(2 * t + 1) * QBLK], on[:, (2 * t + 1) * QBLK:(2 * t + 2) * QBLK]],
                axis=0)
            o_ref[0, rows, t * LANES:(t + 1) * LANES] = pair_t.T.astype(o_ref.dtype)
        return carry

    lax.fori_loop(0, seq // QBLK, values, 0, unroll=16)


def _swa(proj, tables, q_gain, k_gain, sinks):
    b, s, _ = proj.shape
    group = SWA_Q_HEADS // SWA_KV_HEADS
    q_w = group * HEAD_DIM
    k_t0 = SWA_Q_HEADS * HEAD_DIM // LANES
    v_t0 = k_t0 + SWA_KV_HEADS * HEAD_DIM // LANES
    tab = pl.BlockSpec((1, s, LANES), lambda bi, g: (bi, 0, 0), pipeline_mode=pl.Buffered(1))
    two = lambda g: jnp.concatenate([g, g]).reshape(1, LANES)
    sink_rows = jnp.broadcast_to(sinks.reshape(SWA_KV_HEADS, group, 1), (SWA_KV_HEADS, group, LANES))
    return pl.pallas_call(
        functools.partial(_swa_kernel, seq=s),
        grid=(b, SWA_KV_HEADS),
        in_specs=[
            pl.BlockSpec((1, s, q_w), lambda bi, g: (bi, 0, g)),
            pl.BlockSpec((1, s, LANES), lambda bi, g: (bi, 0, k_t0 + g // 2)),
            pl.BlockSpec((1, s, LANES), lambda bi, g: (bi, 0, v_t0 + g // 2)),
            tab, tab,
            _resident((1, LANES)),
            _resident((1, LANES)),
            pl.BlockSpec((1, group, LANES), lambda bi, g: (g, 0, 0)),
        ],
        out_specs=pl.BlockSpec((1, s, q_w), lambda bi, g: (bi, 0, g)),
        out_shape=jax.ShapeDtypeStruct((b, s, SWA_Q_HEADS * HEAD_DIM), BF16),
        scratch_shapes=[
            pltpu.VMEM((s // QBLK, group // 2, LANES, QBLK), BF16),
            pltpu.VMEM((s + QBLK, LANES), BF16),
            pltpu.VMEM((s // QBLK + 1, HEAD_DIM, QBLK), BF16),
            pltpu.VMEM((2, 2 * QBLK, QBLK), BF16),
            pltpu.VMEM((s // QBLK, 2 * QBLK, group * QBLK), BF16),
            pltpu.VMEM((s // QBLK, SUBLANES, group * QBLK), F32),
            pltpu.VMEM((s, q_w), F32),
            pltpu.VMEM((s, LANES), F32),
        ],
        compiler_params=_cparams(2),
        name="swa_attention",
    )(proj, proj, proj, *tables, two(q_gain), two(k_gain), sink_rows)


def kernel(x, positions, norm_mix, norm_mlp, mlp_w_up, mlp_w_down, hyb_w_in, hyb_w_out, ret_gn_gain, dil_q_gain, dil_k_gain, swa_w_qkv, swa_b_qkv, swa_w_out, swa_q_gain, swa_k_gain, swa_sinks):
    b, s, d = x.shape
    depth = norm_mix.shape[0]
    ret_tab = _rope_table(positions, RET_QK_DIM // 2, RET_THETA, RET_QK_DIM)
    rope_tab = _rope_table(positions, ROPE_DIMS // 2, ROPE_THETA, ROPE_DIMS)
    log_gamma = jnp.log1p(-jnp.exp2(-5.0 - jnp.arange(RET_HEADS, dtype=F32)))
    ret_w = RET_HEADS * RET_V_DIM
    dil_col0 = 2 * RET_HEADS * RET_QK_DIM + 2 * ret_w

    x2d = x.reshape(b * s, d)
    for layer in range(depth):
        i = layer // 2
        if layer % 2 == 0:
            zero_bias = jnp.zeros((hyb_w_in.shape[2],), F32)
            proj = _norm_proj(x2d, norm_mix[layer], hyb_w_in, i, zero_bias).reshape(b, s, -1)
            ra = _retention(proj, ret_tab, log_gamma, ret_gn_gain[i])
            da = _dilated(proj, rope_tab, dil_q_gain[i], dil_k_gain[i], dil_col0)
            mixed = [ra.reshape(b * s, -1), da.reshape(b * s, -1)]
            w_out = hyb_w_out
        else:
            proj = _norm_proj(x2d, norm_mix[layer], swa_w_qkv, i, swa_b_qkv[i]).reshape(b, s, -1)
            att = _swa(proj, rope_tab, swa_q_gain[i], swa_k_gain[i], swa_sinks[i])
            mixed = [att.reshape(b * s, -1)]
            w_out = swa_w_out
        x2d = _out_mlp(mixed, x2d, w_out, i, norm_mlp[layer], mlp_w_up, mlp_w_down, layer)
    return x2d.reshape(b, s, d)
```

```python
import functools

import jax
import jax.numpy as jnp
import numpy as np
from jax import lax
from jax.experimental import pallas as pl
from jax.experimental.pallas import tpu as pltpu

F32 = jnp.float32
BF16 = jnp.bfloat16

D_MODEL = 1024
D_FF = 4 * D_MODEL
HEAD_DIM = 64
EPS = 1e-6
RET_HEADS = 4
RET_QK_DIM = 64
RET_V_DIM = 128
RET_CHUNK = 128
RET_THETA = 10000.0
DIL_HEADS = 8
DIL_PATTERNS = ((128, 1), (512, 4), (2048, 16))
SWA_Q_HEADS = 16
SWA_KV_HEADS = 4
SWA_WINDOW = 128
ROPE_THETA = 500000.0
ROPE_DIMS = HEAD_DIM // 4

LANES = 128
SUBLANES = 8
QBLK = 128
VMEM_LIMIT_BYTES = 56 * 1024 * 1024
NEG_BIG = -1e30
LOG2_E = 1.4426950408889634
Q_SCALE = HEAD_DIM ** -0.5 * LOG2_E

ROW_TILE = 1024
MLP_ROW_TILE = 512
FF_CHUNK = 1024
PRO_ROWS = 128
BLOCK_UNROLL = 16
DEN_ROWS = 16
COARSE_R, FINE_R = DIL_PATTERNS[1][1], DIL_PATTERNS[2][1]


def _cparams(n_axes):
    return pltpu.CompilerParams(
        dimension_semantics=("arbitrary",) * n_axes,
        vmem_limit_bytes=VMEM_LIMIT_BYTES,
    )


def _resident(shape):
    nd = len(shape)
    return pl.BlockSpec(shape, lambda *_: (0,) * nd, pipeline_mode=pl.Buffered(1))


def _resident_layer(stack_shape, layer):
    return pl.BlockSpec((None,) + tuple(stack_shape[1:]), lambda *_: (layer, 0, 0),
                        pipeline_mode=pl.Buffered(1))


def _split3(x):
    hi = x.astype(BF16)
    r1 = x - hi.astype(F32)
    mid = r1.astype(BF16)
    lo = (r1 - mid.astype(F32)).astype(BF16)
    return hi, mid, lo


def _trig_kernel(p_ref, spread_ref, inv_ref, ec_ref, es_ref, base_ref, c_ref, s_ref, *, n_pos):
    dense = sum(jnp.dot(part, spread_ref[...], preferred_element_type=F32)
                for part in _split3(p_ref[...]))
    ang = dense * inv_ref[...]
    cos_parts = _split3(jnp.cos(ang))
    sin_parts = _split3(jnp.sin(ang))
    rows = p_ref.shape[0]
    for c in range(n_pos):
        dst = pl.ds(c, rows, stride=n_pos)
        c_ref[dst, :] = base_ref[...] + sum(
            jnp.dot(part, ec_ref[c], preferred_element_type=F32) for part in cos_parts)
        s_ref[dst, :] = sum(
            jnp.dot(part, es_ref[c], preferred_element_type=F32) for part in sin_parts)


def _selectors(half):
    n_pos = LANES // half
    ec = np.zeros((n_pos, LANES, LANES), np.float32)
    es = np.zeros((n_pos, LANES, LANES), np.float32)
    base = np.zeros((1, LANES), np.float32)
    for j in range(LANES):
        d = j % HEAD_DIM
        if d >= 2 * half:
            base[0, j] = 1.0
            continue
        f = d % half
        for c in range(n_pos):
            ec[c, c * half + f, j] = 1.0
            es[c, c * half + f, j] = -1.0 if d < half else 1.0
    return n_pos, jnp.asarray(ec, BF16), jnp.asarray(es, BF16), jnp.asarray(base)


def _rope_table(positions, half, theta, n_rot):
    b, s = positions.shape
    posf = positions.astype(F32)
    inv = jnp.power(jnp.float32(theta), -jnp.arange(half, dtype=F32) * (2.0 / n_rot))
    n_pos, ec, es, base = _selectors(half)
    rows = s // n_pos
    p = posf.reshape(b, rows, n_pos)
    spread = jnp.asarray(np.repeat(np.eye(n_pos, dtype=np.float32), half, axis=1), BF16)
    inv_row = jnp.tile(inv, n_pos)[None, :]
    tile = min(rows, 2048 // n_pos)
    assert rows % tile == 0
    dense = pl.BlockSpec((None, tile, n_pos), lambda bi, i: (bi, i, 0))
    wide = pl.BlockSpec((None, tile * n_pos, LANES), lambda bi, i: (bi, i, 0))
    c_tab, s_tab = pl.pallas_call(
        functools.partial(_trig_kernel, n_pos=n_pos),
        grid=(b, rows // tile),
        in_specs=[dense, _resident(spread.shape), _resident(inv_row.shape), _resident(ec.shape),
                  _resident(es.shape), _resident(base.shape)],
        out_specs=[wide, wide],
        out_shape=[jax.ShapeDtypeStruct((b, s, LANES), F32)] * 2,
        compiler_params=_cparams(2),
        name="trig_tables",
    )(p, spread, inv_row, ec, es, base)
    return c_tab, s_tab


def _swap_matrix(half):
    src = lax.broadcasted_iota(jnp.int32, (LANES, LANES), 0)
    dst = lax.broadcasted_iota(jnp.int32, (LANES, LANES), 1)
    d = dst & (HEAD_DIM - 1)
    want = jnp.where(d < half, dst + half, jnp.where(d < 2 * half, dst - half, -1))
    return jnp.where(src == want, 1.0, 0.0).astype(BF16)


def _rotate(x, c, s, swap):
    swapped = jnp.dot(x.astype(BF16), swap, preferred_element_type=F32)
    return x * c + swapped * s


def _lo_mask(shape):
    return lax.broadcasted_iota(jnp.int32, shape, len(shape) - 1) < HEAD_DIM


def _head_ones():
    r = lax.broadcasted_iota(jnp.int32, (LANES, LANES), 0) < HEAD_DIM
    c = lax.broadcasted_iota(jnp.int32, (LANES, LANES), 1) < HEAD_DIM
    return jnp.where(r == c, 1.0, 0.0).astype(BF16)


def _head_rinv(x, head_ones):
    ss = jnp.dot((x * x).astype(BF16), head_ones, preferred_element_type=F32)
    return lax.rsqrt(ss * (1.0 / HEAD_DIM) + EPS)


def _norm_proj_kernel(x_ref, g_ref, w_ref, b_ref, o_ref, *, n_chunk):
    x = x_ref[...]
    ms = jnp.mean(x * x, axis=-1, keepdims=True)
    h = (x * lax.rsqrt(ms + EPS) * g_ref[...]).astype(BF16)
    n = o_ref.shape[-1]
    for c in range(0, n, n_chunk):
        acc = jnp.dot(h, w_ref[:, c:c + n_chunk].astype(BF16), preferred_element_type=F32)
        o_ref[:, c:c + n_chunk] = (acc + b_ref[:, c:c + n_chunk]).astype(o_ref.dtype)


def _norm_proj(x2d, gain, w_stack, layer, bias):
    m, d = x2d.shape
    n = w_stack.shape[2]
    return pl.pallas_call(
        functools.partial(_norm_proj_kernel, n_chunk=512),
        grid=(m // ROW_TILE,),
        in_specs=[
            pl.BlockSpec((ROW_TILE, d), lambda i: (i, 0)),
            _resident((1, d)),
            _resident_layer(w_stack.shape, layer),
            _resident((1, n)),
        ],
        out_specs=pl.BlockSpec((ROW_TILE, n), lambda i: (i, 0)),
        out_shape=jax.ShapeDtypeStruct((m, n), BF16),
        compiler_params=_cparams(1),
        name="norm_proj",
    )(x2d, gain.reshape(1, d), w_stack, bias.reshape(1, n))


def _out_mlp_kernel(*refs, n_mix):
    a_refs = refs[:n_mix]
    x_ref, wo_ref, g_ref, wup_ref, wdn_ref, o_ref = refs[n_mix:]
    mixed = a_refs[0][...] if n_mix == 1 else jnp.concatenate([a[...] for a in a_refs], axis=-1)
    x1 = x_ref[...] + jnp.dot(mixed, wo_ref[...].astype(BF16), preferred_element_type=F32)
    ms = jnp.mean(x1 * x1, axis=-1, keepdims=True)
    h = (x1 * lax.rsqrt(ms + EPS) * g_ref[...]).astype(BF16)
    y = x1
    for c in range(0, D_FF, FF_CHUNK):
        u = jnp.dot(h, wup_ref[:, c:c + FF_CHUNK].astype(BF16), preferred_element_type=F32)
        u = jnp.square(jnp.maximum(u, 0.0)).astype(BF16)
        y = y + jnp.dot(u, wdn_ref[c:c + FF_CHUNK, :].astype(BF16), preferred_element_type=F32)
    o_ref[...] = y


def _out_mlp(mixed, x2d, w_out, out_layer, gain, w_up, w_down, mlp_layer):
    m, d = x2d.shape
    n_mix = len(mixed)
    in_specs = [pl.BlockSpec((MLP_ROW_TILE, a.shape[1]), lambda i: (i, 0)) for a in mixed]
    in_specs.append(pl.BlockSpec((MLP_ROW_TILE, d), lambda i: (i, 0)))
    in_specs += [_resident_layer(w_out.shape, out_layer), _resident((1, d)),
                 _resident_layer(w_up.shape, mlp_layer), _resident_layer(w_down.shape, mlp_layer)]
    return pl.pallas_call(
        functools.partial(_out_mlp_kernel, n_mix=n_mix),
        grid=(m // MLP_ROW_TILE,),
        in_specs=in_specs,
        out_specs=pl.BlockSpec((MLP_ROW_TILE, d), lambda i: (i, 0)),
        out_shape=jax.ShapeDtypeStruct((m, d), F32),
        compiler_params=_cparams(1),
        name="out_mlp",
    )(*mixed, x2d, w_out, gain.reshape(1, d), w_up, w_down)


def _retention_kernel(q_ref, k_ref, v_ref, g_ref, c_ref, s_ref, lg_ref, gn_ref,
                      o_ref, q_s, k_s, kt_s, o_s, kv_s, *, seq):
    n_chunks = seq // RET_CHUNK
    swap = _swap_matrix(RET_QK_DIM // 2)
    cs = RET_CHUNK

    def prologue(n, carry):
        r0 = pl.multiple_of(n * cs, cs)
        rows = pl.ds(r0, cs)
        c, sn = c_ref[0, rows, :], s_ref[0, rows, :]
        q = _rotate(q_ref[0, rows, :].astype(F32), c, sn, swap)
        k = _rotate(k_ref[0, rows, :].astype(F32), c, sn, swap) * (RET_QK_DIM ** -0.5)
        q_s[n] = q.astype(BF16)
        k_s[n] = k.astype(BF16)
        kt_s[n] = k.T
        return carry

    lax.fori_loop(0, n_chunks, prologue, 0, unroll=16)

    lo = _lo_mask((cs, LANES))
    row = lax.broadcasted_iota(jnp.int32, (cs, cs), 0).astype(F32)
    col = lax.broadcasted_iota(jnp.int32, (cs, cs), 1).astype(F32)
    diff = row - col
    consts = []
    for hh in range(2):
        lg = lg_ref[0, hh:hh + 1, :]
        decay = jnp.where(diff >= 0, jnp.exp(lg * jnp.maximum(diff, 0.0)), 0.0)
        xi = jnp.exp(lg * (row + 1.0))
        zeta = jnp.exp(lg * (cs - 1.0 - col[0:1, :]))
        cd = jnp.exp(lg * float(cs))
        head = lo if hh == 0 else jnp.logical_not(lo)
        consts.append((decay, xi, zeta, cd, head))

    def inner(n, carry):
        rows = pl.ds(pl.multiple_of(n * cs, cs), cs)
        qc, kc, ktc = q_s[n], k_s[n], kt_s[n]
        for hh in range(2):
            decay, xi, zeta, cd, head = consts[hh]
            cols = slice(hh * RET_V_DIM, (hh + 1) * RET_V_DIM)
            qm = jnp.where(head, qc, jnp.zeros_like(qc))
            vc = v_ref[0, rows, cols]
            sc = lax.dot_general(qm, kc, (((1,), (1,)), ((), ())), preferred_element_type=F32) * decay
            o_s[rows, cols] = jnp.dot(sc.astype(BF16), vc, preferred_element_type=F32)
            kz = (ktc * zeta).astype(BF16)
            kv_s[n, hh] = jnp.dot(kz, vc, preferred_element_type=F32)
        return carry

    lax.fori_loop(0, n_chunks, inner, 0, unroll=16)

    def cross(n, state):
        rows = pl.ds(pl.multiple_of(n * cs, cs), cs)
        qc = q_s[n]
        new_state = []
        for hh in range(2):
            decay, xi, zeta, cd, head = consts[hh]
            cols = slice(hh * RET_V_DIM, (hh + 1) * RET_V_DIM)
            r_prev = state[hh]
            qm = jnp.where(head, qc, jnp.zeros_like(qc))
            o_s[rows, cols] += jnp.dot(qm, r_prev.astype(BF16), preferred_element_type=F32) * xi
            new_state.append(r_prev * cd + kv_s[n, hh])
        return tuple(new_state)

    zero = jnp.zeros((LANES, RET_V_DIM), F32)
    lax.fori_loop(0, n_chunks, cross, (zero, zero), unroll=16)

    def finish(n, carry):
        rows = pl.ds(pl.multiple_of(n * cs, cs), cs)
        for hh in range(2):
            cols = slice(hh * RET_V_DIM, (hh + 1) * RET_V_DIM)
            o = o_s[rows, cols]
            mu = jnp.mean(o, axis=-1, keepdims=True)
            dev = o - mu
            var = jnp.mean(dev * dev, axis=-1, keepdims=True)
            y = dev * lax.rsqrt(var + EPS) * gn_ref[0, hh:hh + 1, :]
            gate = g_ref[0, rows, cols].astype(F32)
            o_ref[0, rows, cols] = (gate * jax.nn.sigmoid(gate) * y).astype(o_ref.dtype)
        return carry

    lax.fori_loop(0, n_chunks, finish, 0, unroll=2)


def _retention(proj, tables, log_gamma, gn_gain):
    b, s, _ = proj.shape
    n_pairs = RET_HEADS // 2
    pair_w = 2 * RET_V_DIM
    qk_tiles = RET_HEADS * RET_QK_DIM // LANES
    v_off = 2 * qk_tiles * LANES // pair_w
    g_off = v_off + RET_HEADS * RET_V_DIM // pair_w
    tab = pl.BlockSpec((1, s, LANES), lambda bi, p: (bi, 0, 0), pipeline_mode=pl.Buffered(1))
    lg = jnp.broadcast_to(log_gamma.reshape(n_pairs, 2, 1), (n_pairs, 2, LANES))
    n_chunks = s // RET_CHUNK
    return pl.pallas_call(
        functools.partial(_retention_kernel, seq=s),
        grid=(b, n_pairs),
        in_specs=[
            pl.BlockSpec((1, s, LANES), lambda bi, p: (bi, 0, p)),
            pl.BlockSpec((1, s, LANES), lambda bi, p: (bi, 0, qk_tiles + p)),
            pl.BlockSpec((1, s, pair_w), lambda bi, p: (bi, 0, v_off + p)),
            pl.BlockSpec((1, s, pair_w), lambda bi, p: (bi, 0, g_off + p)),
            tab, tab,
            pl.BlockSpec((1, 2, LANES), lambda bi, p: (p, 0, 0)),
            pl.BlockSpec((1, 2, RET_V_DIM), lambda bi, p: (p, 0, 0)),
        ],
        out_specs=pl.BlockSpec((1, s, pair_w), lambda bi, p: (bi, 0, p)),
        out_shape=jax.ShapeDtypeStruct((b, s, RET_HEADS * RET_V_DIM), BF16),
        scratch_shapes=[
            pltpu.VMEM((n_chunks, RET_CHUNK, LANES), BF16),
            pltpu.VMEM((n_chunks, RET_CHUNK, LANES), BF16),
            pltpu.VMEM((n_chunks, LANES, RET_CHUNK), F32),
            pltpu.VMEM((s, pair_w), F32),
            pltpu.VMEM((n_chunks, 2, LANES, RET_V_DIM), F32),
        ],
        compiler_params=_cparams(2),
        name="retention",
    )(proj, proj, proj, proj, *tables, lg, gn_gain.reshape(n_pairs, 2, RET_V_DIM))


def _band_bias_t(lo_off, hi_off):
    c = lax.broadcasted_iota(jnp.int32, (2 * QBLK, QBLK), 0)
    a = lax.broadcasted_iota(jnp.int32, (2 * QBLK, QBLK), 1)
    band = (c - a >= lo_off) & (c - a <= hi_off)
    later = jnp.where(band, 0.0, NEG_BIG).astype(F32)
    first = jnp.where(band & (c >= QBLK), 0.0, NEG_BIG).astype(F32)
    return first, later


def _dilated_kernel(q_ref, k_ref, v_ref, c_ref, s_ref, qg_ref, kg_ref, o_ref,
                    qn_s, kn_s, vn_s, q4_s, k4_s, v4_s, qt_s, kc_s, vt_s, bias_s, p_s, m_s,
                    o0_s, o1_s, o2_s, l0_s, l1_s, l2_s, *, seq):
    swap = _swap_matrix(ROPE_DIMS // 2)
    head_ones = _head_ones()
    rq_s, rk_s = o0_s, l0_s

    def norms(n, carry):
        rows = pl.ds(pl.multiple_of(n * PRO_ROWS, PRO_ROWS), PRO_ROWS)
        rq_s[rows, :] = _head_rinv(q_ref[0, rows, :].astype(F32), head_ones)
        rk_s[rows, :] = _head_rinv(k_ref[0, rows, :].astype(F32), head_ones)
        return carry

    lax.fori_loop(0, seq // PRO_ROWS, norms, 0, unroll=16)

    def prologue(n, carry):
        rows = pl.ds(pl.multiple_of(n * PRO_ROWS, PRO_ROWS), PRO_ROWS)
        c, sn = c_ref[0, rows, :], s_ref[0, rows, :]
        q = q_ref[0, rows, :].astype(F32) * rq_s[rows, :] * qg_ref[...]
        qn_s[rows, :] = _rotate(q, c, sn, swap) * Q_SCALE
        k = k_ref[0, rows, :].astype(F32) * rk_s[rows, :] * kg_ref[...]
        kn_s[rows, :] = _rotate(k, c, sn, swap)
        vn_s[rows, :] = v_ref[0, rows, :].astype(F32)
        return carry

    lax.fori_loop(0, seq // PRO_ROWS, prologue, 0, unroll=16)

    first, later = _band_bias_t(0, QBLK)
    bias_s[0] = first.astype(BF16)
    bias_s[1] = later.astype(BF16)
    eye = (lax.broadcasted_iota(jnp.int32, (QBLK, QBLK), 0)
           == lax.broadcasted_iota(jnp.int32, (QBLK, QBLK), 1))
    eye = jnp.where(eye, 1.0, 0.0).astype(BF16)
    eye_pair = jnp.concatenate([eye, eye], axis=1)
    zero_pad = jnp.zeros((QBLK, LANES), BF16)
    top = lax.broadcasted_iota(jnp.int32, (LANES, QBLK), 0) < HEAD_DIM
    ones_rows = jnp.ones((DEN_ROWS, 2 * QBLK), BF16)

    outs = (o0_s, o1_s, o2_s)
    lses = (l0_s, l1_s, l2_s)
    n_flat = seq // QBLK
    for (window, r), on_s, ln_s in zip(DIL_PATTERNS, outs, lses):
        assert window // r == QBLK
        n_blk = seq // r // QBLK
        blk_shift = n_blk.bit_length() - 1
        assert n_blk == 1 << blk_shift

        def pad(j, c2, n_blk=n_blk):
            z = j * (n_blk + 1)
            kc_s[pl.ds(pl.multiple_of(z * QBLK, QBLK), QBLK), :] = zero_pad
            vt_s[z] = zero_pad
            return c2

        lax.fori_loop(0, r, pad, 0)

        def split(idx, r=r, n_blk=n_blk, blk_shift=blk_shift):
            j = lax.shift_right_logical(idx, blk_shift)
            i = idx & (n_blk - 1)
            return j, i

        def gather(idx, c2, r=r, split=split):
            j, i = split(idx)
            if r == FINE_R:
                base = ((j & (COARSE_R - 1)) * (seq // COARSE_R)
                        + lax.shift_right_logical(j, COARSE_R.bit_length() - 1))
                src = pl.ds(base + (QBLK * r // COARSE_R) * i, QBLK, stride=r // COARSE_R)
                q, k, v = q4_s[src, :], k4_s[src, :], v4_s[src, :]
            else:
                src = pl.ds(j + r * QBLK * i, QBLK, stride=r)
                q, k, v = qn_s[src, :], kn_s[src, :], vn_s[src, :]
            if r == COARSE_R:
                dense = pl.ds(pl.multiple_of(idx * QBLK, QBLK), QBLK)
                q4_s[dense, :] = q
                k4_s[dense, :] = k
                v4_s[dense, :] = v
            k0 = pl.multiple_of((idx + j + 1) * QBLK, QBLK)
            qt_s[idx] = q.T.astype(BF16)
            kc_s[pl.ds(k0, QBLK), :] = k.astype(BF16)
            vt_s[idx + j + 1] = v.T.astype(BF16)
            return c2

        lax.fori_loop(0, n_flat, gather, 0, unroll=16)

        def scores(idx, c2, split=split):
            j, i = split(idx)
            k0 = pl.multiple_of((idx + j) * QBLK, QBLK)
            qt = qt_s[idx]
            kw = kc_s[pl.ds(k0, 2 * QBLK), :]
            zero = jnp.zeros_like(qt)
            rhs = jnp.concatenate([jnp.where(top, qt, zero), jnp.where(top, zero, qt)], axis=1)
            lhs = jnp.concatenate([kw, bias_s[jnp.minimum(i, 1)]], axis=1)
            sc = jnp.dot(lhs, jnp.concatenate([rhs, eye_pair], axis=0), preferred_element_type=F32)
            m = jnp.max(sc, axis=0, keepdims=True)
            p_s[idx] = jnp.exp2(sc - m).astype(BF16)
            m_s[idx] = jnp.broadcast_to(m, (SUBLANES, 2 * QBLK))
            return c2

        lax.fori_loop(0, n_flat, scores, 0, unroll=16)

        def block(idx, c2, r=r, split=split, on_s=on_s, ln_s=ln_s):
            j, i = split(idx)
            p = p_s[idx]
            m = m_s[idx][0:1, :]
            lhs = jnp.concatenate(
                [jnp.concatenate([vt_s[idx + j], vt_s[idx + j + 1]], axis=1), ones_rows], axis=0)
            ot = jnp.dot(lhs, p, preferred_element_type=F32)
            den = ot[LANES:LANES + 1, :]
            inv = 1.0 / den
            lse = m + jnp.log2(den)
            o_t = jnp.concatenate(
                [ot[0:HEAD_DIM, 0:QBLK] * inv[:, 0:QBLK],
                 ot[HEAD_DIM:LANES, QBLK:2 * QBLK] * inv[:, QBLK:2 * QBLK]], axis=0)
            l_t = jnp.concatenate(
                [jnp.broadcast_to(lse[:, 0:QBLK], (HEAD_DIM, QBLK)),
                 jnp.broadcast_to(lse[:, QBLK:2 * QBLK], (HEAD_DIM, QBLK))], axis=0)
            dst = pl.ds(j + r * QBLK * i, QBLK, stride=r)
            on_s[dst, :] = o_t.T
            ln_s[dst, :] = l_t.T
            return c2

        lax.fori_loop(0, n_flat, block, 0, unroll=16)

    def combine(n, carry):
        r0 = pl.multiple_of(n * PRO_ROWS, PRO_ROWS)
        rows = pl.ds(r0, PRO_ROWS)
        ls = [l_s[rows, :] for l_s in lses]
        m = jnp.maximum(jnp.maximum(ls[0], ls[1]), ls[2])
        es = [jnp.exp2(l - m) for l in ls]
        num = es[0] * o0_s[rows, :] + es[1] * o1_s[rows, :] + es[2] * o2_s[rows, :]
        o_ref[0, rows, :] = (num / (es[0] + es[1] + es[2])).astype(o_ref.dtype)
        return carry

    lax.fori_loop(0, seq // PRO_ROWS, combine, 0, unroll=8)


def _dilated(proj, tables, q_gain, k_gain, col0):
    b, s, _ = proj.shape
    n_pairs = DIL_HEADS * HEAD_DIM // LANES
    t0 = col0 // LANES
    tab = pl.BlockSpec((1, s, LANES), lambda bi, p: (bi, 0, 0), pipeline_mode=pl.Buffered(1))
    two = lambda g: jnp.concatenate([g, g]).reshape(1, LANES)
    nat = pltpu.VMEM((s, LANES), F32)
    max_r = max(r for _, r in DIL_PATTERNS)
    return pl.pallas_call(
        functools.partial(_dilated_kernel, seq=s),
        grid=(b, n_pairs),
        in_specs=[
            pl.BlockSpec((1, s, LANES), lambda bi, p: (bi, 0, t0 + p)),
            pl.BlockSpec((1, s, LANES), lambda bi, p: (bi, 0, t0 + n_pairs + p)),
            pl.BlockSpec((1, s, LANES), lambda bi, p: (bi, 0, t0 + 2 * n_pairs + p)),
            tab, tab,
            _resident((1, LANES)),
            _resident((1, LANES)),
        ],
        out_specs=pl.BlockSpec((1, s, LANES), lambda bi, p: (bi, 0, p)),
        out_shape=jax.ShapeDtypeStruct((b, s, DIL_HEADS * HEAD_DIM), BF16),
        scratch_shapes=[
            nat, nat, nat, nat, nat, nat,
            pltpu.VMEM((s // QBLK, LANES, QBLK), BF16),
            pltpu.VMEM((s + max_r * QBLK, LANES), BF16),
            pltpu.VMEM((s // QBLK + max_r, LANES, QBLK), BF16),
            pltpu.VMEM((2, 2 * QBLK, QBLK), BF16),
            pltpu.VMEM((s // QBLK, 2 * QBLK, 2 * QBLK), BF16),
            pltpu.VMEM((s // QBLK, SUBLANES, 2 * QBLK), F32),
            nat, nat, nat, nat, nat, nat,
        ],
        compiler_params=_cparams(2),
        name="dilated_attention",
    )(proj, proj, proj, *tables, two(q_gain), two(k_gain))


def _swa_kernel(q_ref, k_ref, v_ref, c_ref, s_ref, qg_ref, kg_ref, sink_ref, o_ref,
                qt_s, kd_s, vt_s, bias_s, p_s, m_s, rq_s, rk_s, *, seq):
    group = SWA_Q_HEADS // SWA_KV_HEADS
    swap = _swap_matrix(ROPE_DIMS // 2)
    head_ones = _head_ones()
    kv_in_hi = (pl.program_id(1) % 2) == 1
    blk_per_step = PRO_ROWS // QBLK
    src = lax.broadcasted_iota(jnp.int32, (LANES, LANES), 0)
    dst = lax.broadcasted_iota(jnp.int32, (LANES, LANES), 1) & (HEAD_DIM - 1)
    dup = jnp.where(src == dst + jnp.where(kv_in_hi, HEAD_DIM, 0), 1.0, 0.0).astype(BF16)

    def norms(n, carry):
        rows = pl.ds(pl.multiple_of(n * PRO_ROWS, PRO_ROWS), PRO_ROWS)
        for t in range(group // 2):
            cols = slice(t * LANES, (t + 1) * LANES)
            rq_s[rows, cols] = _head_rinv(q_ref[0, rows, cols].astype(F32), head_ones)
        rk_s[rows, :] = _head_rinv(k_ref[0, rows, :].astype(F32), head_ones)
        return carry

    lax.fori_loop(0, seq // PRO_ROWS, norms, 0, unroll=16)

    def prologue(n, carry):
        r0 = pl.multiple_of(n * PRO_ROWS, PRO_ROWS)
        rows = pl.ds(r0, PRO_ROWS)
        c, sn = c_ref[0, rows, :], s_ref[0, rows, :]
        for t in range(group // 2):
            cols = slice(t * LANES, (t + 1) * LANES)
            q = q_ref[0, rows, cols].astype(F32) * rq_s[rows, cols] * qg_ref[...]
            q_t = (_rotate(q, c, sn, swap) * Q_SCALE).T
            for u in range(blk_per_step):
                qt_s[n * blk_per_step + u, t] = q_t[:, u * QBLK:(u + 1) * QBLK].astype(BF16)
        k = k_ref[0, rows, :].astype(F32) * rk_s[rows, :] * kg_ref[...]
        k = _rotate(k, c, sn, swap).astype(BF16)
        kd_s[pl.ds(r0 + QBLK, PRO_ROWS), :] = jnp.dot(k, dup, preferred_element_type=F32).astype(BF16)
        v_t = v_ref[0, rows, :].astype(F32).T
        v_t = jnp.where(kv_in_hi, v_t[HEAD_DIM:LANES, :], v_t[0:HEAD_DIM, :]).astype(BF16)
        for u in range(blk_per_step):
            vt_s[n * blk_per_step + u + 1] = v_t[:, u * QBLK:(u + 1) * QBLK]
        return carry

    lax.fori_loop(0, seq // PRO_ROWS, prologue, 0, unroll=16)

    first, later = _band_bias_t(1, QBLK)
    bias_s[0] = first.astype(BF16)
    bias_s[1] = later.astype(BF16)
    eye = (lax.broadcasted_iota(jnp.int32, (QBLK, QBLK), 0)
           == lax.broadcasted_iota(jnp.int32, (QBLK, QBLK), 1))
    eye = jnp.where(eye, 1.0, 0.0).astype(BF16)
    eye_pair = jnp.concatenate([eye, eye], axis=1)
    top = lax.broadcasted_iota(jnp.int32, (LANES, QBLK), 0) < HEAD_DIM
    kd_s[0:QBLK, :] = jnp.zeros((QBLK, LANES), BF16)
    vt_s[0] = jnp.zeros((HEAD_DIM, QBLK), BF16)
    sink = jnp.concatenate([sink_ref[0, hh:hh + 1, :] for hh in range(group)], axis=1) * LOG2_E
    ones_rows = jnp.ones((DEN_ROWS, 2 * QBLK), BF16)

    def scores(i, carry):
        d0 = pl.multiple_of(i * QBLK, QBLK)
        rows = pl.ds(d0, QBLK)
        kw = kd_s[pl.ds(d0, 2 * QBLK), :]
        lhs = jnp.concatenate([kw, bias_s[jnp.minimum(i, 1)]], axis=1)
        for t in range(group // 2):
            qt = qt_s[i, t]
            zero = jnp.zeros_like(qt)
            rhs = jnp.concatenate([jnp.where(top, qt, zero), jnp.where(top, zero, qt)], axis=1)
            cols = slice(2 * t * QBLK, 2 * (t + 1) * QBLK)
            sc = jnp.dot(lhs, jnp.concatenate([rhs, eye_pair], axis=0), preferred_element_type=F32)
            m = jnp.maximum(jnp.max(sc, axis=0, keepdims=True), sink[:, cols])
            p_s[i, :, cols] = jnp.exp2(sc - m).astype(BF16)
            m_s[i, :, cols] = jnp.broadcast_to(m, (SUBLANES, 2 * QBLK))
        return carry

    lax.fori_loop(0, seq // QBLK, scores, 0, unroll=BLOCK_UNROLL)

    def values(i, carry):
        rows = pl.ds(pl.multiple_of(i * QBLK, QBLK), QBLK)
        m = m_s[i][0:1, :]
        lhs = jnp.concatenate(
            [jnp.concatenate([vt_s[i], vt_s[i + 1]], axis=1), ones_rows], axis=0)
        ot = jnp.dot(lhs, p_s[i], preferred_element_type=F32)
        inv = 1.0 / (ot[HEAD_DIM:HEAD_DIM + 1, :] + jnp.exp2(sink - m))
        on = ot[0:HEAD_DIM, :] * inv
        for t in range(group // 2):
            pair_t = jnp.concatenate(
                [on[:, 2 * t * QBLK:(2 * t + 1) * QBLK], on[:, (2 * t + 1) * QBLK:(2 * t + 2) * QBLK]],
                axis=0)
            o_ref[0, rows, t * LANES:(t + 1) * LANES] = pair_t.T.astype(o_ref.dtype)
        return carry

    lax.fori_loop(0, seq // QBLK, values, 0, unroll=16)


def _swa(proj, tables, q_gain, k_gain, sinks):
    b, s, _ = proj.shape
    group = SWA_Q_HEADS // SWA_KV_HEADS
    q_w = group * HEAD_DIM
    k_t0 = SWA_Q_HEADS * HEAD_DIM // LANES
    v_t0 = k_t0 + SWA_KV_HEADS * HEAD_DIM // LANES
    tab = pl.BlockSpec((1, s, LANES), lambda bi, g: (bi, 0, 0), pipeline_mode=pl.Buffered(1))
    two = lambda g: jnp.concatenate([g, g]).reshape(1, LANES)
    sink_rows = jnp.broadcast_to(sinks.reshape(SWA_KV_HEADS, group, 1), (SWA_KV_HEADS, group, LANES))
    return pl.pallas_call(
        functools.partial(_swa_kernel, seq=s),
        grid=(b, SWA_KV_HEADS),
        in_specs=[
            pl.BlockSpec((1, s, q_w), lambda bi, g: (bi, 0, g)),
            pl.BlockSpec((1, s, LANES), lambda bi, g: (bi, 0, k_t0 + g // 2)),
            pl.BlockSpec((1, s, LANES), lambda bi, g: (bi, 0, v_t0 + g // 2)),
            tab, tab,
            _resident((1, LANES)),
            _resident((1, LANES)),
            pl.BlockSpec((1, group, LANES), lambda bi, g: (g, 0, 0)),
        ],
        out_specs=pl.BlockSpec((1, s, q_w), lambda bi, g: (bi, 0, g)),
        out_shape=jax.ShapeDtypeStruct((b, s, SWA_Q_HEADS * HEAD_DIM), BF16),
        scratch_shapes=[
            pltpu.VMEM((s // QBLK, group // 2, LANES, QBLK), BF16),
            pltpu.VMEM((s + QBLK, LANES), BF16),
            pltpu.VMEM((s // QBLK + 1, HEAD_DIM, QBLK), BF16),
            pltpu.VMEM((2, 2 * QBLK, QBLK), BF16),
            pltpu.VMEM((s // QBLK, 2 * QBLK, group * QBLK), BF16),
            pltpu.VMEM((s // QBLK, SUBLANES, group * QBLK), F32),
            pltpu.VMEM((s, q_w), F32),
            pltpu.VMEM((s, LANES), F32),
        ],
        compiler_params=_cparams(2),
        name="swa_attention",
    )(proj, proj, proj, *tables, two(q_gain), two(k_gain), sink_rows)


def kernel(x, positions, norm_mix, norm_mlp, mlp_w_up, mlp_w_down, hyb_w_in, hyb_w_out, ret_gn_gain, dil_q_gain, dil_k_gain, swa_w_qkv, swa_b_qkv, swa_w_out, swa_q_gain, swa_k_gain, swa_sinks):
    b, s, d = x.shape
    depth = norm_mix.shape[0]
    ret_tab = _rope_table(positions, RET_QK_DIM // 2, RET_THETA, RET_QK_DIM)
    rope_tab = _rope_table(positions, ROPE_DIMS // 2, ROPE_THETA, ROPE_DIMS)
    log_gamma = jnp.log1p(-jnp.exp2(-5.0 - jnp.arange(RET_HEADS, dtype=F32)))
    ret_w = RET_HEADS * RET_V_DIM
    dil_col0 = 2 * RET_HEADS * RET_QK_DIM + 2 * ret_w

    x2d = x.reshape(b * s, d)
    for layer in range(depth):
        i = layer // 2
        if layer % 2 == 0:
            zero_bias = jnp.zeros((hyb_w_in.shape[2],), F32)
            proj = _norm_proj(x2d, norm_mix[layer], hyb_w_in, i, zero_bias).reshape(b, s, -1)
            ra = _retention(proj, ret_tab, log_gamma, ret_gn_gain[i])
            da = _dilated(proj, rope_tab, dil_q_gain[i], dil_k_gain[i], dil_col0)
            mixed = [ra.reshape(b * s, -1), da.reshape(b * s, -1)]
            w_out = hyb_w_out
        else:
            proj = _norm_proj(x2d, norm_mix[layer], swa_w_qkv, i, swa_b_qkv[i]).reshape(b, s, -1)
            att = _swa(proj, rope_tab, swa_q_gain[i], swa_k_gain[i], swa_sinks[i])
            mixed = [att.reshape(b * s, -1)]
            w_out = swa_w_out
        x2d = _out_mlp(mixed, x2d, w_out, i, norm_mlp[layer], mlp_w_up, mlp_w_down, layer)
    return x2d.reshape(b, s, d)
```

```python
import functools

import jax
import jax.numpy as jnp
import numpy as np
from jax import lax
from jax.experimental import pallas as pl
from jax.experimental.pallas import tpu as pltpu

F32 = jnp.float32
BF16 = jnp.bfloat16

D_MODEL = 1024
D_FF = 4 * D_MODEL
HEAD_DIM = 64
EPS = 1e-6
RET_HEADS = 4
RET_QK_DIM = 64
RET_V_DIM = 128
RET_CHUNK = 128
RET_THETA = 10000.0
DIL_HEADS = 8
DIL_PATTERNS = ((128, 1), (512, 4), (2048, 16))
SWA_Q_HEADS = 16
SWA_KV_HEADS = 4
SWA_WINDOW = 128
ROPE_THETA = 500000.0
ROPE_DIMS = HEAD_DIM // 4

LANES = 128
SUBLANES = 8
QBLK = 128
VMEM_LIMIT_BYTES = 56 * 1024 * 1024
NEG_BIG = -1e30
LOG2_E = 1.4426950408889634
Q_SCALE = HEAD_DIM ** -0.5 * LOG2_E

ROW_TILE = 1024
MLP_ROW_TILE = 512
FF_CHUNK = 1024
PRO_ROWS = 128
BLOCK_UNROLL = 16
DEN_ROWS = 16
COARSE_R, FINE_R = DIL_PATTERNS[1][1], DIL_PATTERNS[2][1]


def _cparams(n_axes):
    return pltpu.CompilerParams(
        dimension_semantics=("arbitrary",) * n_axes,
        vmem_limit_bytes=VMEM_LIMIT_BYTES,
    )


def _resident(shape):
    nd = len(shape)
    return pl.BlockSpec(shape, lambda *_: (0,) * nd, pipeline_mode=pl.Buffered(1))


def _resident_layer(stack_shape, layer):
    return pl.BlockSpec((None,) + tuple(stack_shape[1:]), lambda *_: (layer, 0, 0),
                        pipeline_mode=pl.Buffered(1))


def _split3(x):
    hi = x.astype(BF16)
    r1 = x - hi.astype(F32)
    mid = r1.astype(BF16)
    lo = (r1 - mid.astype(F32)).astype(BF16)
    return hi, mid, lo


def _trig_kernel(p_ref, spread_ref, inv_ref, ec_ref, es_ref, base_ref, c_ref, s_ref, *, n_pos):
    dense = sum(jnp.dot(part, spread_ref[...], preferred_element_type=F32)
                for part in _split3(p_ref[...]))
    ang = dense * inv_ref[...]
    cos_parts = _split3(jnp.cos(ang))
    sin_parts = _split3(jnp.sin(ang))
    rows = p_ref.shape[0]
    for c in range(n_pos):
        dst = pl.ds(c, rows, stride=n_pos)
        c_ref[dst, :] = base_ref[...] + sum(
            jnp.dot(part, ec_ref[c], preferred_element_type=F32) for part in cos_parts)
        s_ref[dst, :] = sum(
            jnp.dot(part, es_ref[c], preferred_element_type=F32) for part in sin_parts)


def _selectors(half):
    n_pos = LANES // half
    ec = np.zeros((n_pos, LANES, LANES), np.float32)
    es = np.zeros((n_pos, LANES, LANES), np.float32)
    base = np.zeros((1, LANES), np.float32)
    for j in range(LANES):
        d = j % HEAD_DIM
        if d >= 2 * half:
            base[0, j] = 1.0
            continue
        f = d % half
        for c in range(n_pos):
            ec[c, c * half + f, j] = 1.0
            es[c, c * half + f, j] = -1.0 if d < half else 1.0
    return n_pos, jnp.asarray(ec, BF16), jnp.asarray(es, BF16), jnp.asarray(base)


def _rope_table(positions, half, theta, n_rot):
    b, s = positions.shape
    posf = positions.astype(F32)
    inv = jnp.power(jnp.float32(theta), -jnp.arange(half, dtype=F32) * (2.0 / n_rot))
    n_pos, ec, es, base = _selectors(half)
    rows = s // n_pos
    p = posf.reshape(b, rows, n_pos)
    spread = jnp.asarray(np.repeat(np.eye(n_pos, dtype=np.float32), half, axis=1), BF16)
    inv_row = jnp.tile(inv, n_pos)[None, :]
    tile = min(rows, 2048 // n_pos)
    assert rows % tile == 0
    dense = pl.BlockSpec((None, tile, n_pos), lambda bi, i: (bi, i, 0))
    wide = pl.BlockSpec((None, tile * n_pos, LANES), lambda bi, i: (bi, i, 0))
    c_tab, s_tab = pl.pallas_call(
        functools.partial(_trig_kernel, n_pos=n_pos),
        grid=(b, rows // tile),
        in_specs=[dense, _resident(spread.shape), _resident(inv_row.shape), _resident(ec.shape),
                  _resident(es.shape), _resident(base.shape)],
        out_specs=[wide, wide],
        out_shape=[jax.ShapeDtypeStruct((b, s, LANES), F32)] * 2,
        compiler_params=_cparams(2),
        name="trig_tables",
    )(p, spread, inv_row, ec, es, base)
    return c_tab, s_tab


def _swap_matrix(half):
    src = lax.broadcasted_iota(jnp.int32, (LANES, LANES), 0)
    dst = lax.broadcasted_iota(jnp.int32, (LANES, LANES), 1)
    d = dst & (HEAD_DIM - 1)
    want = jnp.where(d < half, dst + half, jnp.where(d < 2 * half, dst - half, -1))
    return jnp.where(src == want, 1.0, 0.0).astype(BF16)


def _rotate(x, c, s, swap):
    swapped = jnp.dot(x.astype(BF16), swap, preferred_element_type=F32)
    return x * c + swapped * s


def _lo_mask(shape):
    return lax.broadcasted_iota(jnp.int32, shape, len(shape) - 1) < HEAD_DIM


def _head_ones():
    r = lax.broadcasted_iota(jnp.int32, (LANES, LANES), 0) < HEAD_DIM
    c = lax.broadcasted_iota(jnp.int32, (LANES, LANES), 1) < HEAD_DIM
    return jnp.where(r == c, 1.0, 0.0).astype(BF16)


def _head_rinv(x, head_ones):
    ss = jnp.dot((x * x).astype(BF16), head_ones, preferred_element_type=F32)
    return lax.rsqrt(ss * (1.0 / HEAD_DIM) + EPS)


def _norm_proj_kernel(x_ref, g_ref, w_ref, b_ref, o_ref, *, n_chunk):
    x = x_ref[...]
    ms = jnp.mean(x * x, axis=-1, keepdims=True)
    h = (x * lax.rsqrt(ms + EPS) * g_ref[...]).astype(BF16)
    n = o_ref.shape[-1]
    for c in range(0, n, n_chunk):
        acc = jnp.dot(h, w_ref[:, c:c + n_chunk].astype(BF16), preferred_element_type=F32)
        o_ref[:, c:c + n_chunk] = (acc + b_ref[:, c:c + n_chunk]).astype(o_ref.dtype)


def _norm_proj(x2d, gain, w_stack, layer, bias):
    m, d = x2d.shape
    n = w_stack.shape[2]
    return pl.pallas_call(
        functools.partial(_norm_proj_kernel, n_chunk=512),
        grid=(m // ROW_TILE,),
        in_specs=[
            pl.BlockSpec((ROW_TILE, d), lambda i: (i, 0)),
            _resident((1, d)),
            _resident_layer(w_stack.shape, layer),
            _resident((1, n)),
        ],
        out_specs=pl.BlockSpec((ROW_TILE, n), lambda i: (i, 0)),
        out_shape=jax.ShapeDtypeStruct((m, n), BF16),
        compiler_params=_cparams(1),
        name="norm_proj",
    )(x2d, gain.reshape(1, d), w_stack, bias.reshape(1, n))


def _out_mlp_kernel(*refs, n_mix):
    a_refs = refs[:n_mix]
    x_ref, wo_ref, g_ref, wup_ref, wdn_ref, o_ref = refs[n_mix:]
    mixed = a_refs[0][...] if n_mix == 1 else jnp.concatenate([a[...] for a in a_refs], axis=-1)
    x1 = x_ref[...] + jnp.dot(mixed, wo_ref[...].astype(BF16), preferred_element_type=F32)
    ms = jnp.mean(x1 * x1, axis=-1, keepdims=True)
    h = (x1 * lax.rsqrt(ms + EPS) * g_ref[...]).astype(BF16)
    y = x1
    for c in range(0, D_FF, FF_CHUNK):
        u = jnp.dot(h, wup_ref[:, c:c + FF_CHUNK].astype(BF16), preferred_element_type=F32)
        u = jnp.square(jnp.maximum(u, 0.0)).astype(BF16)
        y = y + jnp.dot(u, wdn_ref[c:c + FF_CHUNK, :].astype(BF16), preferred_element_type=F32)
    o_ref[...] = y


def _out_mlp(mixed, x2d, w_out, out_layer, gain, w_up, w_down, mlp_layer):
    m, d = x2d.shape
    n_mix = len(mixed)
    in_specs = [pl.BlockSpec((MLP_ROW_TILE, a.shape[1]), lambda i: (i, 0)) for a in mixed]
    in_specs.append(pl.BlockSpec((MLP_ROW_TILE, d), lambda i: (i, 0)))
    in_specs += [_resident_layer(w_out.shape, out_layer), _resident((1, d)),
                 _resident_layer(w_up.shape, mlp_layer), _resident_layer(w_down.shape, mlp_layer)]
    return pl.pallas_call(
        functools.partial(_out_mlp_kernel, n_mix=n_mix),
        grid=(m // MLP_ROW_TILE,),
        in_specs=in_specs,
        out_specs=pl.BlockSpec((MLP_ROW_TILE, d), lambda i: (i, 0)),
        out_shape=jax.ShapeDtypeStruct((m, d), F32),
        compiler_params=_cparams(1),
        name="out_mlp",
    )(*mixed, x2d, w_out, gain.reshape(1, d), w_up, w_down)


def _retention_kernel(q_ref, k_ref, v_ref, g_ref, c_ref, s_ref, lg_ref, gn_ref,
                      o_ref, q_s, k_s, vt_s, o_s, kv_s, *, seq):
    n_chunks = seq // RET_CHUNK
    swap = _swap_matrix(RET_QK_DIM // 2)
    cs = RET_CHUNK

    def prologue(n, carry):
        rows = pl.ds(pl.multiple_of(n * cs, cs), cs)
        c, sn = c_ref[0, rows, :], s_ref[0, rows, :]
        q = _rotate(q_ref[0, rows, :].astype(F32), c, sn, swap)
        k = _rotate(k_ref[0, rows, :].astype(F32), c, sn, swap) * (RET_QK_DIM ** -0.5)
        q_s[n] = q.astype(BF16)
        k_s[n] = k.astype(BF16)
        for hh in range(2):
            cols = slice(hh * RET_V_DIM, (hh + 1) * RET_V_DIM)
            vt_s[n, hh] = v_ref[0, rows, cols].astype(F32).T.astype(BF16)
        return carry

    lax.fori_loop(0, n_chunks, prologue, 0, unroll=16)

    lo = _lo_mask((cs, LANES))
    key = lax.broadcasted_iota(jnp.int32, (cs, cs), 0).astype(F32)
    tok = lax.broadcasted_iota(jnp.int32, (cs, cs), 1).astype(F32)
    ahead = tok - key
    consts = []
    for hh in range(2):
        lg = lg_ref[0, hh:hh + 1, :]
        decay_t = jnp.where(ahead >= 0, jnp.exp(lg * jnp.maximum(ahead, 0.0)), 0.0)
        xi = jnp.exp(lg * (tok[0:1, :] + 1.0))
        zeta = jnp.exp(lg * (cs - 1.0 - tok[0:1, :]))
        cd = jnp.exp(lg * float(cs))
        head = lo if hh == 0 else jnp.logical_not(lo)
        gain = jnp.broadcast_to(gn_ref[0, hh:hh + 1, :], (cs, RET_V_DIM)).T
        consts.append((decay_t, xi, zeta, cd, head, gain))

    def inner(n, carry):
        qc, kc = q_s[n], k_s[n]
        for hh in range(2):
            decay_t, xi, zeta, cd, head, gain = consts[hh]
            qm = jnp.where(head, qc, jnp.zeros_like(qc))
            vt = vt_s[n, hh]
            sc_t = lax.dot_general(kc, qm, (((1,), (1,)), ((), ())),
                                   preferred_element_type=F32) * decay_t
            o_s[n, hh] = jnp.dot(vt, sc_t.astype(BF16), preferred_element_type=F32)
            vz = (vt.astype(F32) * zeta).astype(BF16)
            kv_s[n, hh] = jnp.dot(vz, kc, preferred_element_type=F32)
        return carry

    lax.fori_loop(0, n_chunks, inner, 0, unroll=16)

    def cross(n, state):
        qc = q_s[n]
        new_state = []
        for hh in range(2):
            decay_t, xi, zeta, cd, head, gain = consts[hh]
            r_prev = state[hh]
            qm = jnp.where(head, qc, jnp.zeros_like(qc))
            o_s[n, hh] += lax.dot_general(r_prev.astype(BF16), qm, (((1,), (1,)), ((), ())),
                                          preferred_element_type=F32) * xi
            new_state.append(r_prev * cd + kv_s[n, hh])
        return tuple(new_state)

    zero = jnp.zeros((RET_V_DIM, LANES), F32)
    lax.fori_loop(0, n_chunks, cross, (zero, zero), unroll=16)

    def finish(n, carry):
        rows = pl.ds(pl.multiple_of(n * cs, cs), cs)
        for hh in range(2):
            gain = consts[hh][5]
            cols = slice(hh * RET_V_DIM, (hh + 1) * RET_V_DIM)
            o = o_s[n, hh]
            mu = jnp.mean(o, axis=0, keepdims=True)
            dev = o - mu
            var = jnp.mean(dev * dev, axis=0, keepdims=True)
            y = (dev * lax.rsqrt(var + EPS) * gain).T
            gate = g_ref[0, rows, cols].astype(F32)
            o_ref[0, rows, cols] = (gate * jax.nn.sigmoid(gate) * y).astype(o_ref.dtype)
        return carry

    lax.fori_loop(0, n_chunks, finish, 0, unroll=16)


def _retention(proj, tables, log_gamma, gn_gain):
    b, s, _ = proj.shape
    n_pairs = RET_HEADS // 2
    pair_w = 2 * RET_V_DIM
    qk_tiles = RET_HEADS * RET_QK_DIM // LANES
    v_off = 2 * qk_tiles * LANES // pair_w
    g_off = v_off + RET_HEADS * RET_V_DIM // pair_w
    tab = pl.BlockSpec((1, s, LANES), lambda bi, p: (bi, 0, 0), pipeline_mode=pl.Buffered(1))
    lg = jnp.broadcast_to(log_gamma.reshape(n_pairs, 2, 1), (n_pairs, 2, LANES))
    n_chunks = s // RET_CHUNK
    return pl.pallas_call(
        functools.partial(_retention_kernel, seq=s),
        grid=(b, n_pairs),
        in_specs=[
            pl.BlockSpec((1, s, LANES), lambda bi, p: (bi, 0, p)),
            pl.BlockSpec((1, s, LANES), lambda bi, p: (bi, 0, qk_tiles + p)),
            pl.BlockSpec((1, s, pair_w), lambda bi, p: (bi, 0, v_off + p)),
            pl.BlockSpec((1, s, pair_w), lambda bi, p: (bi, 0, g_off + p)),
            tab, tab,
            pl.BlockSpec((1, 2, LANES), lambda bi, p: (p, 0, 0)),
            pl.BlockSpec((1, 2, RET_V_DIM), lambda bi, p: (p, 0, 0)),
        ],
        out_specs=pl.BlockSpec((1, s, pair_w), lambda bi, p: (bi, 0, p)),
        out_shape=jax.ShapeDtypeStruct((b, s, RET_HEADS * RET_V_DIM), BF16),
        scratch_shapes=[
            pltpu.VMEM((n_chunks, RET_CHUNK, LANES), BF16),
            pltpu.VMEM((n_chunks, RET_CHUNK, LANES), BF16),
            pltpu.VMEM((n_chunks, 2, RET_V_DIM, RET_CHUNK), BF16),
            pltpu.VMEM((n_chunks, 2, RET_V_DIM, RET_CHUNK), F32),
            pltpu.VMEM((n_chunks, 2, RET_V_DIM, LANES), F32),
        ],
        compiler_params=_cparams(2),
        name="retention",
    )(proj, proj, proj, proj, *tables, lg, gn_gain.reshape(n_pairs, 2, RET_V_DIM))


def _band_bias_t(lo_off, hi_off):
    c = lax.broadcasted_iota(jnp.int32, (2 * QBLK, QBLK), 0)
    a = lax.broadcasted_iota(jnp.int32, (2 * QBLK, QBLK), 1)
    band = (c - a >= lo_off) & (c - a <= hi_off)
    later = jnp.where(band, 0.0, NEG_BIG).astype(F32)
    first = jnp.where(band & (c >= QBLK), 0.0, NEG_BIG).astype(F32)
    return first, later


def _dilated_kernel(q_ref, k_ref, v_ref, c_ref, s_ref, qg_ref, kg_ref, o_ref,
                    qn_s, kn_s, vn_s, q4_s, k4_s, v4_s, qt_s, kc_s, vt_s, bias_s, p_s, m_s,
                    o0_s, o1_s, o2_s, l0_s, l1_s, l2_s, *, seq):
    swap = _swap_matrix(ROPE_DIMS // 2)
    head_ones = _head_ones()
    rq_s, rk_s = o0_s, l0_s

    def norms(n, carry):
        rows = pl.ds(pl.multiple_of(n * PRO_ROWS, PRO_ROWS), PRO_ROWS)
        rq_s[rows, :] = _head_rinv(q_ref[0, rows, :].astype(F32), head_ones)
        rk_s[rows, :] = _head_rinv(k_ref[0, rows, :].astype(F32), head_ones)
        return carry

    lax.fori_loop(0, seq // PRO_ROWS, norms, 0, unroll=16)

    def prologue(n, carry):
        rows = pl.ds(pl.multiple_of(n * PRO_ROWS, PRO_ROWS), PRO_ROWS)
        c, sn = c_ref[0, rows, :], s_ref[0, rows, :]
        q = q_ref[0, rows, :].astype(F32) * rq_s[rows, :] * qg_ref[...]
        qn_s[rows, :] = _rotate(q, c, sn, swap) * Q_SCALE
        k = k_ref[0, rows, :].astype(F32) * rk_s[rows, :] * kg_ref[...]
        kn_s[rows, :] = _rotate(k, c, sn, swap)
        vn_s[rows, :] = v_ref[0, rows, :].astype(F32)
        return carry

    lax.fori_loop(0, seq // PRO_ROWS, prologue, 0, unroll=16)

    first, later = _band_bias_t(0, QBLK)
    bias_s[0] = first.astype(BF16)
    bias_s[1] = later.astype(BF16)
    eye = (lax.broadcasted_iota(jnp.int32, (QBLK, QBLK), 0)
           == lax.broadcasted_iota(jnp.int32, (QBLK, QBLK), 1))
    eye = jnp.where(eye, 1.0, 0.0).astype(BF16)
    eye_pair = jnp.concatenate([eye, eye], axis=1)
    zero_pad = jnp.zeros((QBLK, LANES), BF16)
    top = lax.broadcasted_iota(jnp.int32, (LANES, QBLK), 0) < HEAD_DIM
    ones_rows = jnp.ones((DEN_ROWS, 2 * QBLK), BF16)

    outs = (o0_s, o1_s, o2_s)
    lses = (l0_s, l1_s, l2_s)
    n_flat = seq // QBLK
    for (window, r), on_s, ln_s in zip(DIL_PATTERNS, outs, lses):
        assert window // r == QBLK
        n_blk = seq // r // QBLK
        blk_shift = n_blk.bit_length() - 1
        assert n_blk == 1 << blk_shift

        def pad(j, c2, n_blk=n_blk):
            z = j * (n_blk + 1)
            kc_s[pl.ds(pl.multiple_of(z * QBLK, QBLK), QBLK), :] = zero_pad
            vt_s[z] = zero_pad
            return c2

        lax.fori_loop(0, r, pad, 0)

        def split(idx, r=r, n_blk=n_blk, blk_shift=blk_shift):
            j = lax.shift_right_logical(idx, blk_shift)
            i = idx & (n_blk - 1)
            return j, i

        def gather(idx, c2, r=r, split=split):
            j, i = split(idx)
            if r == FINE_R:
                base = ((j & (COARSE_R - 1)) * (seq // COARSE_R)
                        + lax.shift_right_logical(j, COARSE_R.bit_length() - 1))
                src = pl.ds(base + (QBLK * r // COARSE_R) * i, QBLK, stride=r // COARSE_R)
                q, k, v = q4_s[src, :], k4_s[src, :], v4_s[src, :]
            else:
                src = pl.ds(j + r * QBLK * i, QBLK, stride=r)
                q, k, v = qn_s[src, :], kn_s[src, :], vn_s[src, :]
            if r == COARSE_R:
                dense = pl.ds(pl.multiple_of(idx * QBLK, QBLK), QBLK)
                q4_s[dense, :] = q
                k4_s[dense, :] = k
                v4_s[dense, :] = v
            k0 = pl.multiple_of((idx + j + 1) * QBLK, QBLK)
            qt_s[idx] = q.T.astype(BF16)
            kc_s[pl.ds(k0, QBLK), :] = k.astype(BF16)
            vt_s[idx + j + 1] = v.T.astype(BF16)
            return c2

        lax.fori_loop(0, n_flat, gather, 0, unroll=16)

        def scores(idx, c2, split=split):
            j, i = split(idx)
            k0 = pl.multiple_of((idx + j) * QBLK, QBLK)
            qt = qt_s[idx]
            kw = kc_s[pl.ds(k0, 2 * QBLK), :]
            zero = jnp.zeros_like(qt)
            rhs = jnp.concatenate([jnp.where(top, qt, zero), jnp.where(top, zero, qt)], axis=1)
            lhs = jnp.concatenate([kw, bias_s[jnp.minimum(i, 1)]], axis=1)
            sc = jnp.dot(lhs, jnp.concatenate([rhs, eye_pair], axis=0), preferred_element_type=F32)
            m = jnp.max(sc, axis=0, keepdims=True)
            p_s[idx] = jnp.exp2(sc - m).astype(BF16)
            m_s[idx] = jnp.broadcast_to(m, (SUBLANES, 2 * QBLK))
            return c2

        lax.fori_loop(0, n_flat, scores, 0, unroll=16)

        def block(idx, c2, r=r, split=split, on_s=on_s, ln_s=ln_s):
            j, i = split(idx)
            p = p_s[idx]
            m = m_s[idx][0:1, :]
            lhs = jnp.concatenate(
                [jnp.concatenate([vt_s[idx + j], vt_s[idx + j + 1]], axis=1), ones_rows], axis=0)
            ot = jnp.dot(lhs, p, preferred_element_type=F32)
            den = ot[LANES:LANES + 1, :]
            inv = 1.0 / den
            lse = m + jnp.log2(den)
            o_t = jnp.concatenate(
                [ot[0:HEAD_DIM, 0:QBLK] * inv[:, 0:QBLK],
                 ot[HEAD_DIM:LANES, QBLK:2 * QBLK] * inv[:, QBLK:2 * QBLK]], axis=0)
            l_t = jnp.concatenate(
                [jnp.broadcast_to(lse[:, 0:QBLK], (HEAD_DIM, QBLK)),
                 jnp.broadcast_to(lse[:, QBLK:2 * QBLK], (HEAD_DIM, QBLK))], axis=0)
            dst = pl.ds(j + r * QBLK * i, QBLK, stride=r)
            on_s[dst, :] = o_t.T
            ln_s[dst, :] = l_t.T
            return c2

        lax.fori_loop(0, n_flat, block, 0, unroll=16)

    def combine(n, carry):
        r0 = pl.multiple_of(n * PRO_ROWS, PRO_ROWS)
        rows = pl.ds(r0, PRO_ROWS)
        ls = [l_s[rows, :] for l_s in lses]
        m = jnp.maximum(jnp.maximum(ls[0], ls[1]), ls[2])
        es = [jnp.exp2(l - m) for l in ls]
        num = es[0] * o0_s[rows, :] + es[1] * o1_s[rows, :] + es[2] * o2_s[rows, :]
        o_ref[0, rows, :] = (num / (es[0] + es[1] + es[2])).astype(o_ref.dtype)
        return carry

    lax.fori_loop(0, seq // PRO_ROWS, combine, 0, unroll=8)


def _dilated(proj, tables, q_gain, k_gain, col0):
    b, s, _ = proj.shape
    n_pairs = DIL_HEADS * HEAD_DIM // LANES
    t0 = col0 // LANES
    tab = pl.BlockSpec((1, s, LANES), lambda bi, p: (bi, 0, 0), pipeline_mode=pl.Buffered(1))
    two = lambda g: jnp.concatenate([g, g]).reshape(1, LANES)
    nat = pltpu.VMEM((s, LANES), F32)
    max_r = max(r for _, r in DIL_PATTERNS)
    return pl.pallas_call(
        functools.partial(_dilated_kernel, seq=s),
        grid=(b, n_pairs),
        in_specs=[
            pl.BlockSpec((1, s, LANES), lambda bi, p: (bi, 0, t0 + p)),
            pl.BlockSpec((1, s, LANES), lambda bi, p: (bi, 0, t0 + n_pairs + p)),
            pl.BlockSpec((1, s, LANES), lambda bi, p: (bi, 0, t0 + 2 * n_pairs + p)),
            tab, tab,
            _resident((1, LANES)),
            _resident((1, LANES)),
        ],
        out_specs=pl.BlockSpec((1, s, LANES), lambda bi, p: (bi, 0, p)),
        out_shape=jax.ShapeDtypeStruct((b, s, DIL_HEADS * HEAD_DIM), BF16),
        scratch_shapes=[
            nat, nat, nat, nat, nat, nat,
            pltpu.VMEM((s // QBLK, LANES, QBLK), BF16),
            pltpu.VMEM((s + max_r * QBLK, LANES), BF16),
            pltpu.VMEM((s // QBLK + max_r, LANES, QBLK), BF16),
            pltpu.VMEM((2, 2 * QBLK, QBLK), BF16),
            pltpu.VMEM((s // QBLK, 2 * QBLK, 2 * QBLK), BF16),
            pltpu.VMEM((s // QBLK, SUBLANES, 2 * QBLK), F32),
            nat, nat, nat, nat, nat, nat,
        ],
        compiler_params=_cparams(2),
        name="dilated_attention",
    )(proj, proj, proj, *tables, two(q_gain), two(k_gain))


def _swa_kernel(q_ref, k_ref, v_ref, c_ref, s_ref, qg_ref, kg_ref, sink_ref, o_ref,
                qt_s, kd_s, vt_s, bias_s, p_s, m_s, rq_s, rk_s, *, seq):
    group = SWA_Q_HEADS // SWA_KV_HEADS
    swap = _swap_matrix(ROPE_DIMS // 2)
    head_ones = _head_ones()
    kv_in_hi = (pl.program_id(1) % 2) == 1
    blk_per_step = PRO_ROWS // QBLK
    src = lax.broadcasted_iota(jnp.int32, (LANES, LANES), 0)
    dst = lax.broadcasted_iota(jnp.int32, (LANES, LANES), 1) & (HEAD_DIM - 1)
    dup = jnp.where(src == dst + jnp.where(kv_in_hi, HEAD_DIM, 0), 1.0, 0.0).astype(BF16)

    def norms(n, carry):
        rows = pl.ds(pl.multiple_of(n * PRO_ROWS, PRO_ROWS), PRO_ROWS)
        for t in range(group // 2):
            cols = slice(t * LANES, (t + 1) * LANES)
            rq_s[rows, cols] = _head_rinv(q_ref[0, rows, cols].astype(F32), head_ones)
        rk_s[rows, :] = _head_rinv(k_ref[0, rows, :].astype(F32), head_ones)
        return carry

    lax.fori_loop(0, seq // PRO_ROWS, norms, 0, unroll=16)

    def prologue(n, carry):
        r0 = pl.multiple_of(n * PRO_ROWS, PRO_ROWS)
        rows = pl.ds(r0, PRO_ROWS)
        c, sn = c_ref[0, rows, :], s_ref[0, rows, :]
        for t in range(group // 2):
            cols = slice(t * LANES, (t + 1) * LANES)
            q = q_ref[0, rows, cols].astype(F32) * rq_s[rows, cols] * qg_ref[...]
            q_t = (_rotate(q, c, sn, swap) * Q_SCALE).T
            for u in range(blk_per_step):
                qt_s[n * blk_per_step + u, t] = q_t[:, u * QBLK:(u + 1) * QBLK].astype(BF16)
        k = k_ref[0, rows, :].astype(F32) * rk_s[rows, :] * kg_ref[...]
        k = _rotate(k, c, sn, swap).astype(BF16)
        kd_s[pl.ds(r0 + QBLK, PRO_ROWS), :] = jnp.dot(k, dup, preferred_element_type=F32).astype(BF16)
        v_t = v_ref[0, rows, :].astype(F32).T
        v_t = jnp.where(kv_in_hi, v_t[HEAD_DIM:LANES, :], v_t[0:HEAD_DIM, :]).astype(BF16)
        for u in range(blk_per_step):
            vt_s[n * blk_per_step + u + 1] = v_t[:, u * QBLK:(u + 1) * QBLK]
        return carry

    lax.fori_loop(0, seq // PRO_ROWS, prologue, 0, unroll=16)

    first, later = _band_bias_t(1, QBLK)
    bias_s[0] = first.astype(BF16)
    bias_s[1] = later.astype(BF16)
    eye = (lax.broadcasted_iota(jnp.int32, (QBLK, QBLK), 0)
           == lax.broadcasted_iota(jnp.int32, (QBLK, QBLK), 1))
    eye = jnp.where(eye, 1.0, 0.0).astype(BF16)
    eye_pair = jnp.concatenate([eye, eye], axis=1)
    top = lax.broadcasted_iota(jnp.int32, (LANES, QBLK), 0) < HEAD_DIM
    kd_s[0:QBLK, :] = jnp.zeros((QBLK, LANES), BF16)
    vt_s[0] = jnp.zeros((HEAD_DIM, QBLK), BF16)
    sink = jnp.concatenate([sink_ref[0, hh:hh + 1, :] for hh in range(group)], axis=1) * LOG2_E
    ones_rows = jnp.ones((DEN_ROWS, 2 * QBLK), BF16)

    def scores(i, carry):
        d0 = pl.multiple_of(i * QBLK, QBLK)
        rows = pl.ds(d0, QBLK)
        kw = kd_s[pl.ds(d0, 2 * QBLK), :]
        lhs = jnp.concatenate([kw, bias_s[jnp.minimum(i, 1)]], axis=1)
        for t in range(group // 2):
            qt = qt_s[i, t]
            zero = jnp.zeros_like(qt)
            rhs = jnp.concatenate([jnp.where(top, qt, zero), jnp.where(top, zero, qt)], axis=1)
            cols = slice(2 * t * QBLK, 2 * (t + 1) * QBLK)
            sc = jnp.dot(lhs, jnp.concatenate([rhs, eye_pair], axis=0), preferred_element_type=F32)
            m = jnp.maximum(jnp.max(sc, axis=0, keepdims=True), sink[:, cols])
            p_s[i, :, cols] = jnp.exp2(sc - m).astype(BF16)
            m_s[i, :, cols] = jnp.broadcast_to(m, (SUBLANES, 2 * QBLK))
        return carry

    lax.fori_loop(0, seq // QBLK, scores, 0, unroll=BLOCK_UNROLL)

    def values(i, carry):
        rows = pl.ds(pl.multiple_of(i * QBLK, QBLK), QBLK)
        m = m_s[i][0:1, :]
        lhs = jnp.concatenate(
            [jnp.concatenate([vt_s[i], vt_s[i + 1]], axis=1), ones_rows], axis=0)
        ot = jnp.dot(lhs, p_s[i], preferred_element_type=F32)
        inv = 1.0 / (ot[HEAD_DIM:HEAD_DIM + 1, :] + jnp.exp2(sink - m))
        on = ot[0:HEAD_DIM, :] * inv
        for t in range(group // 2):
            pair_t = jnp.concatenate(
                [on[:, 2 * t * QBLK:(2 * t + 1) * QBLK], on[:, (2 * t + 1) * QBLK:(2 * t + 2) * QBLK]],
                axis=0)
            o_ref[0, rows, t * LANES:(t + 1) * LANES] = pair_t.T.astype(o_ref.dtype)
        return carry

    lax.fori_loop(0, seq // QBLK, values, 0, unroll=16)


def _swa(proj, tables, q_gain, k_gain, sinks):
    b, s, _ = proj.shape
    group = SWA_Q_HEADS // SWA_KV_HEADS
    q_w = group * HEAD_DIM
    k_t0 = SWA_Q_HEADS * HEAD_DIM // LANES
    v_t0 = k_t0 + SWA_KV_HEADS * HEAD_DIM // LANES
    tab = pl.BlockSpec((1, s, LANES), lambda bi, g: (bi, 0, 0), pipeline_mode=pl.Buffered(1))
    two = lambda g: jnp.concatenate([g, g]).reshape(1, LANES)
    sink_rows = jnp.broadcast_to(sinks.reshape(SWA_KV_HEADS, group, 1), (SWA_KV_HEADS, group, LANES))
    return pl.pallas_call(
        functools.partial(_swa_kernel, seq=s),
        grid=(b, SWA_KV_HEADS),
        in_specs=[
            pl.BlockSpec((1, s, q_w), lambda bi, g: (bi, 0, g)),
            pl.BlockSpec((1, s, LANES), lambda bi, g: (bi, 0, k_t0 + g // 2)),
            pl.BlockSpec((1, s, LANES), lambda bi, g: (bi, 0, v_t0 + g // 2)),
            tab, tab,
            _resident((1, LANES)),
            _resident((1, LANES)),
            pl.BlockSpec((1, group, LANES), lambda bi, g: (g, 0, 0)),
        ],
        out_specs=pl.BlockSpec((1, s, q_w), lambda bi, g: (bi, 0, g)),
        out_shape=jax.ShapeDtypeStruct((b, s, SWA_Q_HEADS * HEAD_DIM), BF16),
        scratch_shapes=[
            pltpu.VMEM((s // QBLK, group // 2, LANES, QBLK), BF16),
            pltpu.VMEM((s + QBLK, LANES), BF16),
            pltpu.VMEM((s // QBLK + 1, HEAD_DIM, QBLK), BF16),
            pltpu.VMEM((2, 2 * QBLK, QBLK), BF16),
            pltpu.VMEM((s // QBLK, 2 * QBLK, group * QBLK), BF16),
            pltpu.VMEM((s // QBLK, SUBLANES, group * QBLK), F32),
            pltpu.VMEM((s, q_w), F32),
            pltpu.VMEM((s, LANES), F32),
        ],
        compiler_params=_cparams(2),
        name="swa_attention",
    )(proj, proj, proj, *tables, two(q_gain), two(k_gain), sink_rows)


def kernel(x, positions, norm_mix, norm_mlp, mlp_w_up, mlp_w_down, hyb_w_in, hyb_w_out, ret_gn_gain, dil_q_gain, dil_k_gain, swa_w_qkv, swa_b_qkv, swa_w_out, swa_q_gain, swa_k_gain, swa_sinks):
    b, s, d = x.shape
    depth = norm_mix.shape[0]
    ret_tab = _rope_table(positions, RET_QK_DIM // 2, RET_THETA, RET_QK_DIM)
    rope_tab = _rope_table(positions, ROPE_DIMS // 2, ROPE_THETA, ROPE_DIMS)
    log_gamma = jnp.log1p(-jnp.exp2(-5.0 - jnp.arange(RET_HEADS, dtype=F32)))
    ret_w = RET_HEADS * RET_V_DIM
    dil_col0 = 2 * RET_HEADS * RET_QK_DIM + 2 * ret_w

    x2d = x.reshape(b * s, d)
    for layer in range(depth):
        i = layer // 2
        if layer % 2 == 0:
            zero_bias = jnp.zeros((hyb_w_in.shape[2],), F32)
            proj = _norm_proj(x2d, norm_mix[layer], hyb_w_in, i, zero_bias).reshape(b, s, -1)
            ra = _retention(proj, ret_tab, log_gamma, ret_gn_gain[i])
            da = _dilated(proj, rope_tab, dil_q_gain[i], dil_k_gain[i], dil_col0)
            mixed = [ra.reshape(b * s, -1), da.reshape(b * s, -1)]
            w_out = hyb_w_out
        else:
            proj = _norm_proj(x2d, norm_mix[layer], swa_w_qkv, i, swa_b_qkv[i]).reshape(b, s, -1)
            att = _swa(proj, rope_tab, swa_q_gain[i], swa_k_gain[i], swa_sinks[i])
            mixed = [att.reshape(b * s, -1)]
            w_out = swa_w_out
        x2d = _out_mlp(mixed, x2d, w_out, i, norm_mlp[layer], mlp_w_up, mlp_w_down, layer)
    return x2d.reshape(b, s, d)
```

```python
import functools

import jax
import jax.numpy as jnp
import numpy as np
from jax import lax
from jax.experimental import pallas as pl
from jax.experimental.pallas import tpu as pltpu

F32 = jnp.float32
BF16 = jnp.bfloat16

D_MODEL = 1024
D_FF = 4 * D_MODEL
HEAD_DIM = 64
EPS = 1e-6
RET_HEADS = 4
RET_QK_DIM = 64
RET_V_DIM = 128
RET_CHUNK = 128
RET_THETA = 10000.0
DIL_HEADS = 8
DIL_PATTERNS = ((128, 1), (512, 4), (2048, 16))
SWA_Q_HEADS = 16
SWA_KV_HEADS = 4
SWA_WINDOW = 128
ROPE_THETA = 500000.0
ROPE_DIMS = HEAD_DIM // 4

LANES = 128
SUBLANES = 8
QBLK = 128
VMEM_LIMIT_BYTES = 56 * 1024 * 1024
NEG_BIG = -1e30
LOG2_E = 1.4426950408889634
Q_SCALE = HEAD_DIM ** -0.5 * LOG2_E

ROW_TILE = 1024
MLP_ROW_TILE = 512
FF_CHUNK = 1024
PRO_ROWS = 128
BLOCK_UNROLL = 16
DEN_ROWS = 16
COARSE_R, FINE_R = DIL_PATTERNS[1][1], DIL_PATTERNS[2][1]


def _cparams(n_axes):
    return pltpu.CompilerParams(
        dimension_semantics=("arbitrary",) * n_axes,
        vmem_limit_bytes=VMEM_LIMIT_BYTES,
    )


def _resident(shape):
    nd = len(shape)
    return pl.BlockSpec(shape, lambda *_: (0,) * nd, pipeline_mode=pl.Buffered(1))


def _resident_layer(stack_shape, layer):
    return pl.BlockSpec((None,) + tuple(stack_shape[1:]), lambda *_: (layer, 0, 0),
                        pipeline_mode=pl.Buffered(1))


def _split3(x):
    hi = x.astype(BF16)
    r1 = x - hi.astype(F32)
    mid = r1.astype(BF16)
    lo = (r1 - mid.astype(F32)).astype(BF16)
    return hi, mid, lo


def _trig_kernel(p_ref, spread_ref, inv_ref, ec_ref, es_ref, base_ref, c_ref, s_ref, *, n_pos):
    dense = sum(jnp.dot(part, spread_ref[...], preferred_element_type=F32)
                for part in _split3(p_ref[...]))
    ang = dense * inv_ref[...]
    cos_parts = _split3(jnp.cos(ang))
    sin_parts = _split3(jnp.sin(ang))
    rows = p_ref.shape[0]
    for c in range(n_pos):
        dst = pl.ds(c, rows, stride=n_pos)
        c_ref[dst, :] = base_ref[...] + sum(
            jnp.dot(part, ec_ref[c], preferred_element_type=F32) for part in cos_parts)
        s_ref[dst, :] = sum(
            jnp.dot(part, es_ref[c], preferred_element_type=F32) for part in sin_parts)


def _selectors(half):
    n_pos = LANES // half
    ec = np.zeros((n_pos, LANES, LANES), np.float32)
    es = np.zeros((n_pos, LANES, LANES), np.float32)
    base = np.zeros((1, LANES), np.float32)
    for j in range(LANES):
        d = j % HEAD_DIM
        if d >= 2 * half:
            base[0, j] = 1.0
            continue
        f = d % half
        for c in range(n_pos):
            ec[c, c * half + f, j] = 1.0
            es[c, c * half + f, j] = -1.0 if d < half else 1.0
    return n_pos, jnp.asarray(ec, BF16), jnp.asarray(es, BF16), jnp.asarray(base)


def _rope_table(positions, half, theta, n_rot):
    b, s = positions.shape
    posf = positions.astype(F32)
    inv = jnp.power(jnp.float32(theta), -jnp.arange(half, dtype=F32) * (2.0 / n_rot))
    n_pos, ec, es, base = _selectors(half)
    rows = s // n_pos
    p = posf.reshape(b, rows, n_pos)
    spread = jnp.asarray(np.repeat(np.eye(n_pos, dtype=np.float32), half, axis=1), BF16)
    inv_row = jnp.tile(inv, n_pos)[None, :]
    tile = min(rows, 2048 // n_pos)
    assert rows % tile == 0
    dense = pl.BlockSpec((None, tile, n_pos), lambda bi, i: (bi, i, 0))
    wide = pl.BlockSpec((None, tile * n_pos, LANES), lambda bi, i: (bi, i, 0))
    c_tab, s_tab = pl.pallas_call(
        functools.partial(_trig_kernel, n_pos=n_pos),
        grid=(b, rows // tile),
        in_specs=[dense, _resident(spread.shape), _resident(inv_row.shape), _resident(ec.shape),
                  _resident(es.shape), _resident(base.shape)],
        out_specs=[wide, wide],
        out_shape=[jax.ShapeDtypeStruct((b, s, LANES), F32)] * 2,
        compiler_params=_cparams(2),
        name="trig_tables",
    )(p, spread, inv_row, ec, es, base)
    return c_tab, s_tab


def _swap_matrix(half):
    src = lax.broadcasted_iota(jnp.int32, (LANES, LANES), 0)
    dst = lax.broadcasted_iota(jnp.int32, (LANES, LANES), 1)
    d = dst & (HEAD_DIM - 1)
    want = jnp.where(d < half, dst + half, jnp.where(d < 2 * half, dst - half, -1))
    return jnp.where(src == want, 1.0, 0.0).astype(BF16)


def _rotate(x, c, s, swap):
    swapped = jnp.dot(x.astype(BF16), swap, preferred_element_type=F32)
    return x * c + swapped * s


def _lo_mask(shape):
    return lax.broadcasted_iota(jnp.int32, shape, len(shape) - 1) < HEAD_DIM


def _head_ones():
    r = lax.broadcasted_iota(jnp.int32, (LANES, LANES), 0) < HEAD_DIM
    c = lax.broadcasted_iota(jnp.int32, (LANES, LANES), 1) < HEAD_DIM
    return jnp.where(r == c, 1.0, 0.0).astype(BF16)


def _head_rinv(x, head_ones):
    ss = jnp.dot((x * x).astype(BF16), head_ones, preferred_element_type=F32)
    return lax.rsqrt(ss * (1.0 / HEAD_DIM) + EPS)


def _norm_proj_kernel(x_ref, g_ref, w_ref, b_ref, o_ref, *, n_chunk):
    x = x_ref[...]
    ms = jnp.mean(x * x, axis=-1, keepdims=True)
    h = (x * lax.rsqrt(ms + EPS) * g_ref[...]).astype(BF16)
    n = o_ref.shape[-1]
    for c in range(0, n, n_chunk):
        acc = jnp.dot(h, w_ref[:, c:c + n_chunk].astype(BF16), preferred_element_type=F32)
        o_ref[:, c:c + n_chunk] = (acc + b_ref[:, c:c + n_chunk]).astype(o_ref.dtype)


def _norm_proj(x2d, gain, w_stack, layer, bias):
    m, d = x2d.shape
    n = w_stack.shape[2]
    return pl.pallas_call(
        functools.partial(_norm_proj_kernel, n_chunk=512),
        grid=(m // ROW_TILE,),
        in_specs=[
            pl.BlockSpec((ROW_TILE, d), lambda i: (i, 0)),
            _resident((1, d)),
            _resident_layer(w_stack.shape, layer),
            _resident((1, n)),
        ],
        out_specs=pl.BlockSpec((ROW_TILE, n), lambda i: (i, 0)),
        out_shape=jax.ShapeDtypeStruct((m, n), BF16),
        compiler_params=_cparams(1),
        name="norm_proj",
    )(x2d, gain.reshape(1, d), w_stack, bias.reshape(1, n))


def _out_mlp_kernel(*refs, n_mix):
    a_refs = refs[:n_mix]
    x_ref, wo_ref, g_ref, wup_ref, wdn_ref, o_ref = refs[n_mix:]
    mixed = a_refs[0][...] if n_mix == 1 else jnp.concatenate([a[...] for a in a_refs], axis=-1)
    x1 = x_ref[...] + jnp.dot(mixed, wo_ref[...].astype(BF16), preferred_element_type=F32)
    ms = jnp.mean(x1 * x1, axis=-1, keepdims=True)
    h = (x1 * lax.rsqrt(ms + EPS) * g_ref[...]).astype(BF16)
    y = x1
    for c in range(0, D_FF, FF_CHUNK):
        u = jnp.dot(h, wup_ref[:, c:c + FF_CHUNK].astype(BF16), preferred_element_type=F32)
        u = jnp.square(jnp.maximum(u, 0.0)).astype(BF16)
        y = y + jnp.dot(u, wdn_ref[c:c + FF_CHUNK, :].astype(BF16), preferred_element_type=F32)
    o_ref[...] = y


def _out_mlp(mixed, x2d, w_out, out_layer, gain, w_up, w_down, mlp_layer):
    m, d = x2d.shape
    n_mix = len(mixed)
    in_specs = [pl.BlockSpec((MLP_ROW_TILE, a.shape[1]), lambda i: (i, 0)) for a in mixed]
    in_specs.append(pl.BlockSpec((MLP_ROW_TILE, d), lambda i: (i, 0)))
    in_specs += [_resident_layer(w_out.shape, out_layer), _resident((1, d)),
                 _resident_layer(w_up.shape, mlp_layer), _resident_layer(w_down.shape, mlp_layer)]
    return pl.pallas_call(
        functools.partial(_out_mlp_kernel, n_mix=n_mix),
        grid=(m // MLP_ROW_TILE,),
        in_specs=in_specs,
        out_specs=pl.BlockSpec((MLP_ROW_TILE, d), lambda i: (i, 0)),
        out_shape=jax.ShapeDtypeStruct((m, d), F32),
        compiler_params=_cparams(1),
        name="out_mlp",
    )(*mixed, x2d, w_out, gain.reshape(1, d), w_up, w_down)


def _retention_kernel(q_ref, k_ref, v_ref, g_ref, c_ref, s_ref, lg_ref, gn_ref,
                      o_ref, q_s, k_s, vt_s, o_s, kv_s, *, seq):
    n_chunks = seq // RET_CHUNK
    swap = _swap_matrix(RET_QK_DIM // 2)
    cs = RET_CHUNK

    def prologue(n, carry):
        rows = pl.ds(pl.multiple_of(n * cs, cs), cs)
        c, sn = c_ref[0, rows, :], s_ref[0, rows, :]
        q = _rotate(q_ref[0, rows, :].astype(F32), c, sn, swap)
        k = _rotate(k_ref[0, rows, :].astype(F32), c, sn, swap) * (RET_QK_DIM ** -0.5)
        q_s[n] = q.astype(BF16)
        k_s[n] = k.astype(BF16)
        for hh in range(2):
            cols = slice(hh * RET_V_DIM, (hh + 1) * RET_V_DIM)
            vt_s[n, hh] = v_ref[0, rows, cols].astype(F32).T.astype(BF16)
        return carry

    lax.fori_loop(0, n_chunks, prologue, 0, unroll=32)

    lo = _lo_mask((cs, LANES))
    key = lax.broadcasted_iota(jnp.int32, (cs, cs), 0).astype(F32)
    tok = lax.broadcasted_iota(jnp.int32, (cs, cs), 1).astype(F32)
    ahead = tok - key
    consts = []
    for hh in range(2):
        lg = lg_ref[0, hh:hh + 1, :]
        decay_t = jnp.where(ahead >= 0, jnp.exp(lg * jnp.maximum(ahead, 0.0)), 0.0)
        xi = jnp.exp(lg * (tok[0:1, :] + 1.0))
        zeta = jnp.exp(lg * (cs - 1.0 - tok[0:1, :]))
        cd = jnp.exp(lg * float(cs))
        head = lo if hh == 0 else jnp.logical_not(lo)
        gain = jnp.broadcast_to(gn_ref[0, hh:hh + 1, :], (cs, RET_V_DIM)).T
        consts.append((decay_t, xi, zeta, cd, head, gain))

    def inner(n, carry):
        qc, kc = q_s[n], k_s[n]
        for hh in range(2):
            decay_t, xi, zeta, cd, head, gain = consts[hh]
            qm = jnp.where(head, qc, jnp.zeros_like(qc))
            vt = vt_s[n, hh]
            sc_t = lax.dot_general(kc, qm, (((1,), (1,)), ((), ())),
                                   preferred_element_type=F32) * decay_t
            o_s[n, hh] = jnp.dot(vt, sc_t.astype(BF16), preferred_element_type=F32)
            vz = (vt.astype(F32) * zeta).astype(BF16)
            kv_s[n, hh] = jnp.dot(vz, kc, preferred_element_type=F32)
        return carry

    lax.fori_loop(0, n_chunks, inner, 0, unroll=16)

    def cross(n, state):
        qc = q_s[n]
        new_state = []
        for hh in range(2):
            decay_t, xi, zeta, cd, head, gain = consts[hh]
            r_prev = state[hh]
            qm = jnp.where(head, qc, jnp.zeros_like(qc))
            o_s[n, hh] += lax.dot_general(r_prev.astype(BF16), qm, (((1,), (1,)), ((), ())),
                                          preferred_element_type=F32) * xi
            new_state.append(r_prev * cd + kv_s[n, hh])
        return tuple(new_state)

    zero = jnp.zeros((RET_V_DIM, LANES), F32)
    lax.fori_loop(0, n_chunks, cross, (zero, zero), unroll=32)

    def finish(n, carry):
        rows = pl.ds(pl.multiple_of(n * cs, cs), cs)
        for hh in range(2):
            gain = consts[hh][5]
            cols = slice(hh * RET_V_DIM, (hh + 1) * RET_V_DIM)
            o = o_s[n, hh]
            mu = jnp.mean(o, axis=0, keepdims=True)
            dev = o - mu
            var = jnp.mean(dev * dev, axis=0, keepdims=True)
            y = (dev * lax.rsqrt(var + EPS) * gain).T
            gate = g_ref[0, rows, cols].astype(F32)
            o_ref[0, rows, cols] = (gate * jax.nn.sigmoid(gate) * y).astype(o_ref.dtype)
        return carry

    lax.fori_loop(0, n_chunks, finish, 0, unroll=32)


def _retention(proj, tables, log_gamma, gn_gain):
    b, s, _ = proj.shape
    n_pairs = RET_HEADS // 2
    pair_w = 2 * RET_V_DIM
    qk_tiles = RET_HEADS * RET_QK_DIM // LANES
    v_off = 2 * qk_tiles * LANES // pair_w
    g_off = v_off + RET_HEADS * RET_V_DIM // pair_w
    tab = pl.BlockSpec((1, s, LANES), lambda bi, p: (bi, 0, 0), pipeline_mode=pl.Buffered(1))
    lg = jnp.broadcast_to(log_gamma.reshape(n_pairs, 2, 1), (n_pairs, 2, LANES))
    n_chunks = s // RET_CHUNK
    return pl.pallas_call(
        functools.partial(_retention_kernel, seq=s),
        grid=(b, n_pairs),
        in_specs=[
            pl.BlockSpec((1, s, LANES), lambda bi, p: (bi, 0, p)),
            pl.BlockSpec((1, s, LANES), lambda bi, p: (bi, 0, qk_tiles + p)),
            pl.BlockSpec((1, s, pair_w), lambda bi, p: (bi, 0, v_off + p)),
            pl.BlockSpec((1, s, pair_w), lambda bi, p: (bi, 0, g_off + p)),
            tab, tab,
            pl.BlockSpec((1, 2, LANES), lambda bi, p: (p, 0, 0)),
            pl.BlockSpec((1, 2, RET_V_DIM), lambda bi, p: (p, 0, 0)),
        ],
        out_specs=pl.BlockSpec((1, s, pair_w), lambda bi, p: (bi, 0, p)),
        out_shape=jax.ShapeDtypeStruct((b, s, RET_HEADS * RET_V_DIM), BF16),
        scratch_shapes=[
            pltpu.VMEM((n_chunks, RET_CHUNK, LANES), BF16),
            pltpu.VMEM((n_chunks, RET_CHUNK, LANES), BF16),
            pltpu.VMEM((n_chunks, 2, RET_V_DIM, RET_CHUNK), BF16),
            pltpu.VMEM((n_chunks, 2, RET_V_DIM, RET_CHUNK), F32),
            pltpu.VMEM((n_chunks, 2, RET_V_DIM, LANES), F32),
        ],
        compiler_params=_cparams(2),
        name="retention",
    )(proj, proj, proj, proj, *tables, lg, gn_gain.reshape(n_pairs, 2, RET_V_DIM))


def _band_bias_t(lo_off, hi_off):
    c = lax.broadcasted_iota(jnp.int32, (2 * QBLK, QBLK), 0)
    a = lax.broadcasted_iota(jnp.int32, (2 * QBLK, QBLK), 1)
    band = (c - a >= lo_off) & (c - a <= hi_off)
    later = jnp.where(band, 0.0, NEG_BIG).astype(F32)
    first = jnp.where(band & (c >= QBLK), 0.0, NEG_BIG).astype(F32)
    return first, later


def _dilated_kernel(q_ref, k_ref, v_ref, c_ref, s_ref, qg_ref, kg_ref, o_ref,
                    qn_s, kn_s, vn_s, q4_s, k4_s, v4_s, qt_s, kc_s, vt_s, bias_s, p_s, m_s,
                    o0_s, o1_s, o2_s, l0_s, l1_s, l2_s, *, seq):
    swap = _swap_matrix(ROPE_DIMS // 2)
    head_ones = _head_ones()
    rq_s, rk_s = o0_s, l0_s

    def norms(n, carry):
        rows = pl.ds(pl.multiple_of(n * PRO_ROWS, PRO_ROWS), PRO_ROWS)
        rq_s[rows, :] = _head_rinv(q_ref[0, rows, :].astype(F32), head_ones)
        rk_s[rows, :] = _head_rinv(k_ref[0, rows, :].astype(F32), head_ones)
        return carry

    lax.fori_loop(0, seq // PRO_ROWS, norms, 0, unroll=32)

    def prologue(n, carry):
        rows = pl.ds(pl.multiple_of(n * PRO_ROWS, PRO_ROWS), PRO_ROWS)
        c, sn = c_ref[0, rows, :], s_ref[0, rows, :]
        q = q_ref[0, rows, :].astype(F32) * rq_s[rows, :] * qg_ref[...]
        qn_s[rows, :] = _rotate(q, c, sn, swap) * Q_SCALE
        k = k_ref[0, rows, :].astype(F32) * rk_s[rows, :] * kg_ref[...]
        kn_s[rows, :] = _rotate(k, c, sn, swap)
        vn_s[rows, :] = v_ref[0, rows, :].astype(F32)
        return carry

    lax.fori_loop(0, seq // PRO_ROWS, prologue, 0, unroll=32)

    first, later = _band_bias_t(0, QBLK)
    bias_s[0] = first.astype(BF16)
    bias_s[1] = later.astype(BF16)
    eye = (lax.broadcasted_iota(jnp.int32, (QBLK, QBLK), 0)
           == lax.broadcasted_iota(jnp.int32, (QBLK, QBLK), 1))
    eye = jnp.where(eye, 1.0, 0.0).astype(BF16)
    eye_pair = jnp.concatenate([eye, eye], axis=1)
    zero_pad = jnp.zeros((QBLK, LANES), BF16)
    top = lax.broadcasted_iota(jnp.int32, (LANES, QBLK), 0) < HEAD_DIM
    ones_rows = jnp.ones((DEN_ROWS, 2 * QBLK), BF16)

    outs = (o0_s, o1_s, o2_s)
    lses = (l0_s, l1_s, l2_s)
    n_flat = seq // QBLK
    for (window, r), on_s, ln_s in zip(DIL_PATTERNS, outs, lses):
        assert window // r == QBLK
        n_blk = seq // r // QBLK
        blk_shift = n_blk.bit_length() - 1
        assert n_blk == 1 << blk_shift

        def pad(j, c2, n_blk=n_blk):
            z = j * (n_blk + 1)
            kc_s[pl.ds(pl.multiple_of(z * QBLK, QBLK), QBLK), :] = zero_pad
            vt_s[z] = zero_pad
            return c2

        lax.fori_loop(0, r, pad, 0)

        def split(idx, r=r, n_blk=n_blk, blk_shift=blk_shift):
            j = lax.shift_right_logical(idx, blk_shift)
            i = idx & (n_blk - 1)
            return j, i

        def gather(idx, c2, r=r, split=split):
            j, i = split(idx)
            if r == FINE_R:
                base = ((j & (COARSE_R - 1)) * (seq // COARSE_R)
                        + lax.shift_right_logical(j, COARSE_R.bit_length() - 1))
                src = pl.ds(base + (QBLK * r // COARSE_R) * i, QBLK, stride=r // COARSE_R)
                q, k, v = q4_s[src, :], k4_s[src, :], v4_s[src, :]
            else:
                src = pl.ds(j + r * QBLK * i, QBLK, stride=r)
                q, k, v = qn_s[src, :], kn_s[src, :], vn_s[src, :]
            if r == COARSE_R:
                dense = pl.ds(pl.multiple_of(idx * QBLK, QBLK), QBLK)
                q4_s[dense, :] = q
                k4_s[dense, :] = k
                v4_s[dense, :] = v
            k0 = pl.multiple_of((idx + j + 1) * QBLK, QBLK)
            qt_s[idx] = q.T.astype(BF16)
            kc_s[pl.ds(k0, QBLK), :] = k.astype(BF16)
            vt_s[idx + j + 1] = v.T.astype(BF16)
            return c2

        lax.fori_loop(0, n_flat, gather, 0, unroll=32)

        def scores(idx, c2, split=split):
            j, i = split(idx)
            k0 = pl.multiple_of((idx + j) * QBLK, QBLK)
            qt = qt_s[idx]
            kw = kc_s[pl.ds(k0, 2 * QBLK), :]
            zero = jnp.zeros_like(qt)
            rhs = jnp.concatenate([jnp.where(top, qt, zero), jnp.where(top, zero, qt)], axis=1)
            lhs = jnp.concatenate([kw, bias_s[jnp.minimum(i, 1)]], axis=1)
            sc = jnp.dot(lhs, jnp.concatenate([rhs, eye_pair], axis=0), preferred_element_type=F32)
            m = jnp.max(sc, axis=0, keepdims=True)
            p_s[idx] = jnp.exp2(sc - m).astype(BF16)
            m_s[idx] = jnp.broadcast_to(m, (SUBLANES, 2 * QBLK))
            return c2

        lax.fori_loop(0, n_flat, scores, 0, unroll=16)

        def block(idx, c2, r=r, split=split, on_s=on_s, ln_s=ln_s):
            j, i = split(idx)
            p = p_s[idx]
            m = m_s[idx][0:1, :]
            lhs = jnp.concatenate(
                [jnp.concatenate([vt_s[idx + j], vt_s[idx + j + 1]], axis=1), ones_rows], axis=0)
            ot = jnp.dot(lhs, p, preferred_element_type=F32)
            den = ot[LANES:LANES + 1, :]
            inv = 1.0 / den
            lse = m + jnp.log2(den)
            o_t = jnp.concatenate(
                [ot[0:HEAD_DIM, 0:QBLK] * inv[:, 0:QBLK],
                 ot[HEAD_DIM:LANES, QBLK:2 * QBLK] * inv[:, QBLK:2 * QBLK]], axis=0)
            l_t = jnp.concatenate(
                [jnp.broadcast_to(lse[:, 0:QBLK], (HEAD_DIM, QBLK)),
                 jnp.broadcast_to(lse[:, QBLK:2 * QBLK], (HEAD_DIM, QBLK))], axis=0)
            dst = pl.ds(j + r * QBLK * i, QBLK, stride=r)
            on_s[dst, :] = o_t.T
            ln_s[dst, :] = l_t.T
            return c2

        lax.fori_loop(0, n_flat, block, 0, unroll=32)

    def combine(n, carry):
        r0 = pl.multiple_of(n * PRO_ROWS, PRO_ROWS)
        rows = pl.ds(r0, PRO_ROWS)
        ls = [l_s[rows, :] for l_s in lses]
        m = jnp.maximum(jnp.maximum(ls[0], ls[1]), ls[2])
        es = [jnp.exp2(l - m) for l in ls]
        num = es[0] * o0_s[rows, :] + es[1] * o1_s[rows, :] + es[2] * o2_s[rows, :]
        o_ref[0, rows, :] = (num / (es[0] + es[1] + es[2])).astype(o_ref.dtype)
        return carry

    lax.fori_loop(0, seq // PRO_ROWS, combine, 0, unroll=8)


def _dilated(proj, tables, q_gain, k_gain, col0):
    b, s, _ = proj.shape
    n_pairs = DIL_HEADS * HEAD_DIM // LANES
    t0 = col0 // LANES
    tab = pl.BlockSpec((1, s, LANES), lambda bi, p: (bi, 0, 0), pipeline_mode=pl.Buffered(1))
    two = lambda g: jnp.concatenate([g, g]).reshape(1, LANES)
    nat = pltpu.VMEM((s, LANES), F32)
    max_r = max(r for _, r in DIL_PATTERNS)
    return pl.pallas_call(
        functools.partial(_dilated_kernel, seq=s),
        grid=(b, n_pairs),
        in_specs=[
            pl.BlockSpec((1, s, LANES), lambda bi, p: (bi, 0, t0 + p)),
            pl.BlockSpec((1, s, LANES), lambda bi, p: (bi, 0, t0 + n_pairs + p)),
            pl.BlockSpec((1, s, LANES), lambda bi, p: (bi, 0, t0 + 2 * n_pairs + p)),
            tab, tab,
            _resident((1, LANES)),
            _resident((1, LANES)),
        ],
        out_specs=pl.BlockSpec((1, s, LANES), lambda bi, p: (bi, 0, p)),
        out_shape=jax.ShapeDtypeStruct((b, s, DIL_HEADS * HEAD_DIM), BF16),
        scratch_shapes=[
            nat, nat, nat, nat, nat, nat,
            pltpu.VMEM((s // QBLK, LANES, QBLK), BF16),
            pltpu.VMEM((s + max_r * QBLK, LANES), BF16),
            pltpu.VMEM((s // QBLK + max_r, LANES, QBLK), BF16),
            pltpu.VMEM((2, 2 * QBLK, QBLK), BF16),
            pltpu.VMEM((s // QBLK, 2 * QBLK, 2 * QBLK), BF16),
            pltpu.VMEM((s // QBLK, SUBLANES, 2 * QBLK), F32),
            nat, nat, nat, nat, nat, nat,
        ],
        compiler_params=_cparams(2),
        name="dilated_attention",
    )(proj, proj, proj, *tables, two(q_gain), two(k_gain))


def _swa_kernel(q_ref, k_ref, v_ref, c_ref, s_ref, qg_ref, kg_ref, sink_ref, o_ref,
                qt_s, kd_s, vt_s, bias_s, p_s, m_s, rq_s, rk_s, *, seq):
    group = SWA_Q_HEADS // SWA_KV_HEADS
    swap = _swap_matrix(ROPE_DIMS // 2)
    head_ones = _head_ones()
    kv_in_hi = (pl.program_id(1) % 2) == 1
    blk_per_step = PRO_ROWS // QBLK
    src = lax.broadcasted_iota(jnp.int32, (LANES, LANES), 0)
    dst = lax.broadcasted_iota(jnp.int32, (LANES, LANES), 1) & (HEAD_DIM - 1)
    dup = jnp.where(src == dst + jnp.where(kv_in_hi, HEAD_DIM, 0), 1.0, 0.0).astype(BF16)

    def norms(n, carry):
        rows = pl.ds(pl.multiple_of(n * PRO_ROWS, PRO_ROWS), PRO_ROWS)
        for t in range(group // 2):
            cols = slice(t * LANES, (t + 1) * LANES)
            rq_s[rows, cols] = _head_rinv(q_ref[0, rows, cols].astype(F32), head_ones)
        rk_s[rows, :] = _head_rinv(k_ref[0, rows, :].astype(F32), head_ones)
        return carry

    lax.fori_loop(0, seq // PRO_ROWS, norms, 0, unroll=32)

    def prologue(n, carry):
        r0 = pl.multiple_of(n * PRO_ROWS, PRO_ROWS)
        rows = pl.ds(r0, PRO_ROWS)
        c, sn = c_ref[0, rows, :], s_ref[0, rows, :]
        for t in range(group // 2):
            cols = slice(t * LANES, (t + 1) * LANES)
            q = q_ref[0, rows, cols].astype(F32) * rq_s[rows, cols] * qg_ref[...]
            q_t = (_rotate(q, c, sn, swap) * Q_SCALE).T
            for u in range(blk_per_step):
                qt_s[n * blk_per_step + u, t] = q_t[:, u * QBLK:(u + 1) * QBLK].astype(BF16)
        k = k_ref[0, rows, :].astype(F32) * rk_s[rows, :] * kg_ref[...]
        k = _rotate(k, c, sn, swap).astype(BF16)
        kd_s[pl.ds(r0 + QBLK, PRO_ROWS), :] = jnp.dot(k, dup, preferred_element_type=F32).astype(BF16)
        v_t = v_ref[0, rows, :].astype(F32).T
        v_t = jnp.where(kv_in_hi, v_t[HEAD_DIM:LANES, :], v_t[0:HEAD_DIM, :]).astype(BF16)
        for u in range(blk_per_step):
            vt_s[n * blk_per_step + u + 1] = v_t[:, u * QBLK:(u + 1) * QBLK]
        return carry

    lax.fori_loop(0, seq // PRO_ROWS, prologue, 0, unroll=32)

    first, later = _band_bias_t(1, QBLK)
    bias_s[0] = first.astype(BF16)
    bias_s[1] = later.astype(BF16)
    eye = (lax.broadcasted_iota(jnp.int32, (QBLK, QBLK), 0)
           == lax.broadcasted_iota(jnp.int32, (QBLK, QBLK), 1))
    eye = jnp.where(eye, 1.0, 0.0).astype(BF16)
    eye_pair = jnp.concatenate([eye, eye], axis=1)
    top = lax.broadcasted_iota(jnp.int32, (LANES, QBLK), 0) < HEAD_DIM
    kd_s[0:QBLK, :] = jnp.zeros((QBLK, LANES), BF16)
    vt_s[0] = jnp.zeros((HEAD_DIM, QBLK), BF16)
    sink = jnp.concatenate([sink_ref[0, hh:hh + 1, :] for hh in range(group)], axis=1) * LOG2_E
    ones_rows = jnp.ones((DEN_ROWS, 2 * QBLK), BF16)

    def scores(i, carry):
        d0 = pl.multiple_of(i * QBLK, QBLK)
        rows = pl.ds(d0, QBLK)
        kw = kd_s[pl.ds(d0, 2 * QBLK), :]
        lhs = jnp.concatenate([kw, bias_s[jnp.minimum(i, 1)]], axis=1)
        for t in range(group // 2):
            qt = qt_s[i, t]
            zero = jnp.zeros_like(qt)
            rhs = jnp.concatenate([jnp.where(top, qt, zero), jnp.where(top, zero, qt)], axis=1)
            cols = slice(2 * t * QBLK, 2 * (t + 1) * QBLK)
            sc = jnp.dot(lhs, jnp.concatenate([rhs, eye_pair], axis=0), preferred_element_type=F32)
            m = jnp.maximum(jnp.max(sc, axis=0, keepdims=True), sink[:, cols])
            p_s[i, :, cols] = jnp.exp2(sc - m).astype(BF16)
            m_s[i, :, cols] = jnp.broadcast_to(m, (SUBLANES, 2 * QBLK))
        return carry

    lax.fori_loop(0, seq // QBLK, scores, 0, unroll=BLOCK_UNROLL)

    def values(i, carry):
        rows = pl.ds(pl.multiple_of(i * QBLK, QBLK), QBLK)
        m = m_s[i][0:1, :]
        lhs = jnp.concatenate(
            [jnp.concatenate([vt_s[i], vt_s[i + 1]], axis=1), ones_rows], axis=0)
        ot = jnp.dot(lhs, p_s[i], preferred_element_type=F32)
        inv = 1.0 / (ot[HEAD_DIM:HEAD_DIM + 1, :] + jnp.exp2(sink - m))
        on = ot[0:HEAD_DIM, :] * inv
        for t in range(group // 2):
            pair_t = jnp.concatenate(
                [on[:, 2 * t * QBLK:(2 * t + 1) * QBLK], on[:, (2 * t + 1) * QBLK:(2 * t + 2) * QBLK]],
                axis=0)
            o_ref[0, rows, t * LANES:(t + 1) * LANES] = pair_t.T.astype(o_ref.dtype)
        return carry

    lax.fori_loop(0, seq // QBLK, values, 0, unroll=32)


def _swa(proj, tables, q_gain, k_gain, sinks):
    b, s, _ = proj.shape
    group = SWA_Q_HEADS // SWA_KV_HEADS
    q_w = group * HEAD_DIM
    k_t0 = SWA_Q_HEADS * HEAD_DIM // LANES
    v_t0 = k_t0 + SWA_KV_HEADS * HEAD_DIM // LANES
    tab = pl.BlockSpec((1, s, LANES), lambda bi, g: (bi, 0, 0), pipeline_mode=pl.Buffered(1))
    two = lambda g: jnp.concatenate([g, g]).reshape(1, LANES)
    sink_rows = jnp.broadcast_to(sinks.reshape(SWA_KV_HEADS, group, 1), (SWA_KV_HEADS, group, LANES))
    return pl.pallas_call(
        functools.partial(_swa_kernel, seq=s),
        grid=(b, SWA_KV_HEADS),
        in_specs=[
            pl.BlockSpec((1, s, q_w), lambda bi, g: (bi, 0, g)),
            pl.BlockSpec((1, s, LANES), lambda bi, g: (bi, 0, k_t0 + g // 2)),
            pl.BlockSpec((1, s, LANES), lambda bi, g: (bi, 0, v_t0 + g // 2)),
            tab, tab,
            _resident((1, LANES)),
            _resident((1, LANES)),
            pl.BlockSpec((1, group, LANES), lambda bi, g: (g, 0, 0)),
        ],
        out_specs=pl.BlockSpec((1, s, q_w), lambda bi, g: (bi, 0, g)),
        out_shape=jax.ShapeDtypeStruct((b, s, SWA_Q_HEADS * HEAD_DIM), BF16),
        scratch_shapes=[
            pltpu.VMEM((s // QBLK, group // 2, LANES, QBLK), BF16),
            pltpu.VMEM((s + QBLK, LANES), BF16),
            pltpu.VMEM((s // QBLK + 1, HEAD_DIM, QBLK), BF16),
            pltpu.VMEM((2, 2 * QBLK, QBLK), BF16),
            pltpu.VMEM((s // QBLK, 2 * QBLK, group * QBLK), BF16),
            pltpu.VMEM((s // QBLK, SUBLANES, group * QBLK), F32),
            pltpu.VMEM((s, q_w), F32),
            pltpu.VMEM((s, LANES), F32),
        ],
        compiler_params=_cparams(2),
        name="swa_attention",
    )(proj, proj, proj, *tables, two(q_gain), two(k_gain), sink_rows)


def kernel(x, positions, norm_mix, norm_mlp, mlp_w_up, mlp_w_down, hyb_w_in, hyb_w_out, ret_gn_gain, dil_q_gain, dil_k_gain, swa_w_qkv, swa_b_qkv, swa_w_out, swa_q_gain, swa_k_gain, swa_sinks):
    b, s, d = x.shape
    depth = norm_mix.shape[0]
    ret_tab = _rope_table(positions, RET_QK_DIM // 2, RET_THETA, RET_QK_DIM)
    rope_tab = _rope_table(positions, ROPE_DIMS // 2, ROPE_THETA, ROPE_DIMS)
    log_gamma = jnp.log1p(-jnp.exp2(-5.0 - jnp.arange(RET_HEADS, dtype=F32)))
    ret_w = RET_HEADS * RET_V_DIM
    dil_col0 = 2 * RET_HEADS * RET_QK_DIM + 2 * ret_w

    x2d = x.reshape(b * s, d)
    for layer in range(depth):
        i = layer // 2
        if layer % 2 == 0:
            zero_bias = jnp.zeros((hyb_w_in.shape[2],), F32)
            proj = _norm_proj(x2d, norm_mix[layer], hyb_w_in, i, zero_bias).reshape(b, s, -1)
            ra = _retention(proj, ret_tab, log_gamma, ret_gn_gain[i])
            da = _dilated(proj, rope_tab, dil_q_gain[i], dil_k_gain[i], dil_col0)
            mixed = [ra.reshape(b * s, -1), da.reshape(b * s, -1)]
            w_out = hyb_w_out
        else:
            proj = _norm_proj(x2d, norm_mix[layer], swa_w_qkv, i, swa_b_qkv[i]).reshape(b, s, -1)
            att = _swa(proj, rope_tab, swa_q_gain[i], swa_k_gain[i], swa_sinks[i])
            mixed = [att.reshape(b * s, -1)]
            w_out = swa_w_out
        x2d = _out_mlp(mixed, x2d, w_out, i, norm_mlp[layer], mlp_w_up, mlp_w_down, layer)
    return x2d.reshape(b, s, d)
```

```python
import functools

import jax
import jax.numpy as jnp
import numpy as np
from jax import lax
from jax.experimental import pallas as pl
from jax.experimental.pallas import tpu as pltpu

F32 = jnp.float32
BF16 = jnp.bfloat16

D_MODEL = 1024
D_FF = 4 * D_MODEL
HEAD_DIM = 64
EPS = 1e-6
RET_HEADS = 4
RET_QK_DIM = 64
RET_V_DIM = 128
RET_CHUNK = 128
RET_THETA = 10000.0
DIL_HEADS = 8
DIL_PATTERNS = ((128, 1), (512, 4), (2048, 16))
SWA_Q_HEADS = 16
SWA_KV_HEADS = 4
SWA_WINDOW = 128
ROPE_THETA = 500000.0
ROPE_DIMS = HEAD_DIM // 4

LANES = 128
SUBLANES = 8
QBLK = 128
VMEM_LIMIT_BYTES = 56 * 1024 * 1024
NEG_BIG = -1e30
LOG2_E = 1.4426950408889634
Q_SCALE = HEAD_DIM ** -0.5 * LOG2_E

ROW_TILE = 1024
MLP_ROW_TILE = 512
FF_CHUNK = 1024
PRO_ROWS = 128
BLOCK_UNROLL = 32
DEN_ROWS = 16
COARSE_R, FINE_R = DIL_PATTERNS[1][1], DIL_PATTERNS[2][1]


def _cparams(n_axes):
    return pltpu.CompilerParams(
        dimension_semantics=("arbitrary",) * n_axes,
        vmem_limit_bytes=VMEM_LIMIT_BYTES,
    )


def _resident(shape):
    nd = len(shape)
    return pl.BlockSpec(shape, lambda *_: (0,) * nd, pipeline_mode=pl.Buffered(1))


def _resident_layer(stack_shape, layer):
    return pl.BlockSpec((None,) + tuple(stack_shape[1:]), lambda *_: (layer, 0, 0),
                        pipeline_mode=pl.Buffered(1))


def _split3(x):
    hi = x.astype(BF16)
    r1 = x - hi.astype(F32)
    mid = r1.astype(BF16)
    lo = (r1 - mid.astype(F32)).astype(BF16)
    return hi, mid, lo


def _trig_kernel(p_ref, spread_ref, inv_ref, ec_ref, es_ref, base_ref, c_ref, s_ref, *, n_pos):
    dense = sum(jnp.dot(part, spread_ref[...], preferred_element_type=F32)
                for part in _split3(p_ref[...]))
    ang = dense * inv_ref[...]
    cos_parts = _split3(jnp.cos(ang))
    sin_parts = _split3(jnp.sin(ang))
    rows = p_ref.shape[0]
    for c in range(n_pos):
        dst = pl.ds(c, rows, stride=n_pos)
        c_ref[dst, :] = base_ref[...] + sum(
            jnp.dot(part, ec_ref[c], preferred_element_type=F32) for part in cos_parts)
        s_ref[dst, :] = sum(
            jnp.dot(part, es_ref[c], preferred_element_type=F32) for part in sin_parts)


def _selectors(half):
    n_pos = LANES // half
    ec = np.zeros((n_pos, LANES, LANES), np.float32)
    es = np.zeros((n_pos, LANES, LANES), np.float32)
    base = np.zeros((1, LANES), np.float32)
    for j in range(LANES):
        d = j % HEAD_DIM
        if d >= 2 * half:
            base[0, j] = 1.0
            continue
        f = d % half
        for c in range(n_pos):
            ec[c, c * half + f, j] = 1.0
            es[c, c * half + f, j] = -1.0 if d < half else 1.0
    return n_pos, jnp.asarray(ec, BF16), jnp.asarray(es, BF16), jnp.asarray(base)


def _rope_table(positions, half, theta, n_rot):
    b, s = positions.shape
    posf = positions.astype(F32)
    inv = jnp.power(jnp.float32(theta), -jnp.arange(half, dtype=F32) * (2.0 / n_rot))
    n_pos, ec, es, base = _selectors(half)
    rows = s // n_pos
    p = posf.reshape(b, rows, n_pos)
    spread = jnp.asarray(np.repeat(np.eye(n_pos, dtype=np.float32), half, axis=1), BF16)
    inv_row = jnp.tile(inv, n_pos)[None, :]
    tile = min(rows, 2048 // n_pos)
    assert rows % tile == 0
    dense = pl.BlockSpec((None, tile, n_pos), lambda bi, i: (bi, i, 0))
    wide = pl.BlockSpec((None, tile * n_pos, LANES), lambda bi, i: (bi, i, 0))
    c_tab, s_tab = pl.pallas_call(
        functools.partial(_trig_kernel, n_pos=n_pos),
        grid=(b, rows // tile),
        in_specs=[dense, _resident(spread.shape), _resident(inv_row.shape), _resident(ec.shape),
                  _resident(es.shape), _resident(base.shape)],
        out_specs=[wide, wide],
        out_shape=[jax.ShapeDtypeStruct((b, s, LANES), F32)] * 2,
        compiler_params=_cparams(2),
        name="trig_tables",
    )(p, spread, inv_row, ec, es, base)
    return c_tab, s_tab


def _swap_matrix(half):
    src = lax.broadcasted_iota(jnp.int32, (LANES, LANES), 0)
    dst = lax.broadcasted_iota(jnp.int32, (LANES, LANES), 1)
    d = dst & (HEAD_DIM - 1)
    want = jnp.where(d < half, dst + half, jnp.where(d < 2 * half, dst - half, -1))
    return jnp.where(src == want, 1.0, 0.0).astype(BF16)


def _rotate(x, c, s, swap):
    swapped = jnp.dot(x.astype(BF16), swap, preferred_element_type=F32)
    return x * c + swapped * s


def _lo_mask(shape):
    return lax.broadcasted_iota(jnp.int32, shape, len(shape) - 1) < HEAD_DIM


def _head_ones():
    r = lax.broadcasted_iota(jnp.int32, (LANES, LANES), 0) < HEAD_DIM
    c = lax.broadcasted_iota(jnp.int32, (LANES, LANES), 1) < HEAD_DIM
    return jnp.where(r == c, 1.0, 0.0).astype(BF16)


def _head_rinv(x, head_ones):
    ss = jnp.dot((x * x).astype(BF16), head_ones, preferred_element_type=F32)
    return lax.rsqrt(ss * (1.0 / HEAD_DIM) + EPS)


def _norm_proj_kernel(x_ref, g_ref, w_ref, b_ref, o_ref, *, n_chunk):
    x = x_ref[...]
    ms = jnp.mean(x * x, axis=-1, keepdims=True)
    h = (x * lax.rsqrt(ms + EPS) * g_ref[...]).astype(BF16)
    n = o_ref.shape[-1]
    for c in range(0, n, n_chunk):
        acc = jnp.dot(h, w_ref[:, c:c + n_chunk].astype(BF16), preferred_element_type=F32)
        o_ref[:, c:c + n_chunk] = (acc + b_ref[:, c:c + n_chunk]).astype(o_ref.dtype)


def _norm_proj(x2d, gain, w_stack, layer, bias):
    m, d = x2d.shape
    n = w_stack.shape[2]
    return pl.pallas_call(
        functools.partial(_norm_proj_kernel, n_chunk=512),
        grid=(m // ROW_TILE,),
        in_specs=[
            pl.BlockSpec((ROW_TILE, d), lambda i: (i, 0)),
            _resident((1, d)),
            _resident_layer(w_stack.shape, layer),
            _resident((1, n)),
        ],
        out_specs=pl.BlockSpec((ROW_TILE, n), lambda i: (i, 0)),
        out_shape=jax.ShapeDtypeStruct((m, n), BF16),
        compiler_params=_cparams(1),
        name="norm_proj",
    )(x2d, gain.reshape(1, d), w_stack, bias.reshape(1, n))


def _out_mlp_kernel(*refs, n_mix):
    a_refs = refs[:n_mix]
    x_ref, wo_ref, g_ref, wup_ref, wdn_ref, o_ref = refs[n_mix:]
    mixed = a_refs[0][...] if n_mix == 1 else jnp.concatenate([a[...] for a in a_refs], axis=-1)
    x1 = x_ref[...] + jnp.dot(mixed, wo_ref[...].astype(BF16), preferred_element_type=F32)
    ms = jnp.mean(x1 * x1, axis=-1, keepdims=True)
    h = (x1 * lax.rsqrt(ms + EPS) * g_ref[...]).astype(BF16)
    y = x1
    for c in range(0, D_FF, FF_CHUNK):
        u = jnp.dot(h, wup_ref[:, c:c + FF_CHUNK].astype(BF16), preferred_element_type=F32)
        u = jnp.square(jnp.maximum(u, 0.0)).astype(BF16)
        y = y + jnp.dot(u, wdn_ref[c:c + FF_CHUNK, :].astype(BF16), preferred_element_type=F32)
    o_ref[...] = y


def _out_mlp(mixed, x2d, w_out, out_layer, gain, w_up, w_down, mlp_layer):
    m, d = x2d.shape
    n_mix = len(mixed)
    in_specs = [pl.BlockSpec((MLP_ROW_TILE, a.shape[1]), lambda i: (i, 0)) for a in mixed]
    in_specs.append(pl.BlockSpec((MLP_ROW_TILE, d), lambda i: (i, 0)))
    in_specs += [_resident_layer(w_out.shape, out_layer), _resident((1, d)),
                 _resident_layer(w_up.shape, mlp_layer), _resident_layer(w_down.shape, mlp_layer)]
    return pl.pallas_call(
        functools.partial(_out_mlp_kernel, n_mix=n_mix),
        grid=(m // MLP_ROW_TILE,),
        in_specs=in_specs,
        out_specs=pl.BlockSpec((MLP_ROW_TILE, d), lambda i: (i, 0)),
        out_shape=jax.ShapeDtypeStruct((m, d), F32),
        compiler_params=_cparams(1),
        name="out_mlp",
    )(*mixed, x2d, w_out, gain.reshape(1, d), w_up, w_down)


def _retention_kernel(q_ref, k_ref, v_ref, g_ref, c_ref, s_ref, lg_ref, gn_ref,
                      o_ref, q_s, k_s, vt_s, o_s, kv_s, *, seq):
    n_chunks = seq // RET_CHUNK
    swap = _swap_matrix(RET_QK_DIM // 2)
    cs = RET_CHUNK

    def prologue(n, carry):
        rows = pl.ds(pl.multiple_of(n * cs, cs), cs)
        c, sn = c_ref[0, rows, :], s_ref[0, rows, :]
        q = _rotate(q_ref[0, rows, :].astype(F32), c, sn, swap)
        k = _rotate(k_ref[0, rows, :].astype(F32), c, sn, swap) * (RET_QK_DIM ** -0.5)
        q_s[n] = q.astype(BF16)
        k_s[n] = k.astype(BF16)
        for hh in range(2):
            cols = slice(hh * RET_V_DIM, (hh + 1) * RET_V_DIM)
            vt_s[n, hh] = v_ref[0, rows, cols].astype(F32).T.astype(BF16)
        return carry

    lax.fori_loop(0, n_chunks, prologue, 0, unroll=32)

    lo = _lo_mask((cs, LANES))
    key = lax.broadcasted_iota(jnp.int32, (cs, cs), 0).astype(F32)
    tok = lax.broadcasted_iota(jnp.int32, (cs, cs), 1).astype(F32)
    ahead = tok - key
    consts = []
    for hh in range(2):
        lg = lg_ref[0, hh:hh + 1, :]
        decay_t = jnp.where(ahead >= 0, jnp.exp(lg * jnp.maximum(ahead, 0.0)), 0.0)
        xi = jnp.exp(lg * (tok[0:1, :] + 1.0))
        zeta = jnp.exp(lg * (cs - 1.0 - tok[0:1, :]))
        cd = jnp.exp(lg * float(cs))
        head = lo if hh == 0 else jnp.logical_not(lo)
        gain = jnp.broadcast_to(gn_ref[0, hh:hh + 1, :], (cs, RET_V_DIM)).T
        consts.append((decay_t, xi, zeta, cd, head, gain))

    def inner(n, carry):
        qc, kc = q_s[n], k_s[n]
        for hh in range(2):
            decay_t, xi, zeta, cd, head, gain = consts[hh]
            qm = jnp.where(head, qc, jnp.zeros_like(qc))
            vt = vt_s[n, hh]
            sc_t = lax.dot_general(kc, qm, (((1,), (1,)), ((), ())),
                                   preferred_element_type=F32) * decay_t
            o_s[n, hh] = jnp.dot(vt, sc_t.astype(BF16), preferred_element_type=F32)
            vz = (vt.astype(F32) * zeta).astype(BF16)
            kv_s[n, hh] = jnp.dot(vz, kc, preferred_element_type=F32)
        return carry

    lax.fori_loop(0, n_chunks, inner, 0, unroll=16)

    def cross(n, state):
        qc = q_s[n]
        new_state = []
        for hh in range(2):
            decay_t, xi, zeta, cd, head, gain = consts[hh]
            r_prev = state[hh]
            qm = jnp.where(head, qc, jnp.zeros_like(qc))
            o_s[n, hh] += lax.dot_general(r_prev.astype(BF16), qm, (((1,), (1,)), ((), ())),
                                          preferred_element_type=F32) * xi
            new_state.append(r_prev * cd + kv_s[n, hh])
        return tuple(new_state)

    zero = jnp.zeros((RET_V_DIM, LANES), F32)
    lax.fori_loop(0, n_chunks, cross, (zero, zero), unroll=32)

    def finish(n, carry):
        rows = pl.ds(pl.multiple_of(n * cs, cs), cs)
        for hh in range(2):
            gain = consts[hh][5]
            cols = slice(hh * RET_V_DIM, (hh + 1) * RET_V_DIM)
            o = o_s[n, hh]
            mu = jnp.mean(o, axis=0, keepdims=True)
            dev = o - mu
            var = jnp.mean(dev * dev, axis=0, keepdims=True)
            y = (dev * lax.rsqrt(var + EPS) * gain).T
            gate = g_ref[0, rows, cols].astype(F32)
            o_ref[0, rows, cols] = (gate * jax.nn.sigmoid(gate) * y).astype(o_ref.dtype)
        return carry

    lax.fori_loop(0, n_chunks, finish, 0, unroll=32)


def _retention(proj, tables, log_gamma, gn_gain):
    b, s, _ = proj.shape
    n_pairs = RET_HEADS // 2
    pair_w = 2 * RET_V_DIM
    qk_tiles = RET_HEADS * RET_QK_DIM // LANES
    v_off = 2 * qk_tiles * LANES // pair_w
    g_off = v_off + RET_HEADS * RET_V_DIM // pair_w
    tab = pl.BlockSpec((1, s, LANES), lambda bi, p: (bi, 0, 0), pipeline_mode=pl.Buffered(1))
    lg = jnp.broadcast_to(log_gamma.reshape(n_pairs, 2, 1), (n_pairs, 2, LANES))
    n_chunks = s // RET_CHUNK
    return pl.pallas_call(
        functools.partial(_retention_kernel, seq=s),
        grid=(b, n_pairs),
        in_specs=[
            pl.BlockSpec((1, s, LANES), lambda bi, p: (bi, 0, p)),
            pl.BlockSpec((1, s, LANES), lambda bi, p: (bi, 0, qk_tiles + p)),
            pl.BlockSpec((1, s, pair_w), lambda bi, p: (bi, 0, v_off + p)),
            pl.BlockSpec((1, s, pair_w), lambda bi, p: (bi, 0, g_off + p)),
            tab, tab,
            pl.BlockSpec((1, 2, LANES), lambda bi, p: (p, 0, 0)),
            pl.BlockSpec((1, 2, RET_V_DIM), lambda bi, p: (p, 0, 0)),
        ],
        out_specs=pl.BlockSpec((1, s, pair_w), lambda bi, p: (bi, 0, p)),
        out_shape=jax.ShapeDtypeStruct((b, s, RET_HEADS * RET_V_DIM), BF16),
        scratch_shapes=[
            pltpu.VMEM((n_chunks, RET_CHUNK, LANES), BF16),
            pltpu.VMEM((n_chunks, RET_CHUNK, LANES), BF16),
            pltpu.VMEM((n_chunks, 2, RET_V_DIM, RET_CHUNK), BF16),
            pltpu.VMEM((n_chunks, 2, RET_V_DIM, RET_CHUNK), F32),
            pltpu.VMEM((n_chunks, 2, RET_V_DIM, LANES), F32),
        ],
        compiler_params=_cparams(2),
        name="retention",
    )(proj, proj, proj, proj, *tables, lg, gn_gain.reshape(n_pairs, 2, RET_V_DIM))


def _band_bias_t(lo_off, hi_off):
    c = lax.broadcasted_iota(jnp.int32, (2 * QBLK, QBLK), 0)
    a = lax.broadcasted_iota(jnp.int32, (2 * QBLK, QBLK), 1)
    band = (c - a >= lo_off) & (c - a <= hi_off)
    later = jnp.where(band, 0.0, NEG_BIG).astype(F32)
    first = jnp.where(band & (c >= QBLK), 0.0, NEG_BIG).astype(F32)
    return first, later


def _dilated_kernel(q_ref, k_ref, v_ref, c_ref, s_ref, qg_ref, kg_ref, o_ref,
                    qn_s, kn_s, vn_s, q4_s, k4_s, v4_s, qt_s, kc_s, vt_s, bias_s, p_s, m_s,
                    o0_s, o1_s, o2_s, l0_s, l1_s, l2_s, *, seq):
    swap = _swap_matrix(ROPE_DIMS // 2)
    head_ones = _head_ones()
    rq_s, rk_s = o0_s, l0_s

    def norms(n, carry):
        rows = pl.ds(pl.multiple_of(n * PRO_ROWS, PRO_ROWS), PRO_ROWS)
        rq_s[rows, :] = _head_rinv(q_ref[0, rows, :].astype(F32), head_ones)
        rk_s[rows, :] = _head_rinv(k_ref[0, rows, :].astype(F32), head_ones)
        return carry

    lax.fori_loop(0, seq // PRO_ROWS, norms, 0, unroll=32)

    def prologue(n, carry):
        rows = pl.ds(pl.multiple_of(n * PRO_ROWS, PRO_ROWS), PRO_ROWS)
        c, sn = c_ref[0, rows, :], s_ref[0, rows, :]
        q = q_ref[0, rows, :].astype(F32) * rq_s[rows, :] * qg_ref[...]
        qn_s[rows, :] = _rotate(q, c, sn, swap) * Q_SCALE
        k = k_ref[0, rows, :].astype(F32) * rk_s[rows, :] * kg_ref[...]
        kn_s[rows, :] = _rotate(k, c, sn, swap)
        vn_s[rows, :] = v_ref[0, rows, :].astype(F32)
        return carry

    lax.fori_loop(0, seq // PRO_ROWS, prologue, 0, unroll=32)

    first, later = _band_bias_t(0, QBLK)
    bias_s[0] = first.astype(BF16)
    bias_s[1] = later.astype(BF16)
    eye = (lax.broadcasted_iota(jnp.int32, (QBLK, QBLK), 0)
           == lax.broadcasted_iota(jnp.int32, (QBLK, QBLK), 1))
    eye = jnp.where(eye, 1.0, 0.0).astype(BF16)
    eye_pair = jnp.concatenate([eye, eye], axis=1)
    zero_pad = jnp.zeros((QBLK, LANES), BF16)
    top = lax.broadcasted_iota(jnp.int32, (LANES, QBLK), 0) < HEAD_DIM
    ones_rows = jnp.ones((DEN_ROWS, 2 * QBLK), BF16)

    outs = (o0_s, o1_s, o2_s)
    lses = (l0_s, l1_s, l2_s)
    n_flat = seq // QBLK
    for (window, r), on_s, ln_s in zip(DIL_PATTERNS, outs, lses):
        assert window // r == QBLK
        n_blk = seq // r // QBLK
        blk_shift = n_blk.bit_length() - 1
        assert n_blk == 1 << blk_shift

        def pad(j, c2, n_blk=n_blk):
            z = j * (n_blk + 1)
            kc_s[pl.ds(pl.multiple_of(z * QBLK, QBLK), QBLK), :] = zero_pad
            vt_s[z] = zero_pad
            return c2

        lax.fori_loop(0, r, pad, 0)

        def split(idx, r=r, n_blk=n_blk, blk_shift=blk_shift):
            j = lax.shift_right_logical(idx, blk_shift)
            i = idx & (n_blk - 1)
            return j, i

        def gather(idx, c2, r=r, split=split):
            j, i = split(idx)
            if r == FINE_R:
                base = ((j & (COARSE_R - 1)) * (seq // COARSE_R)
                        + lax.shift_right_logical(j, COARSE_R.bit_length() - 1))
                src = pl.ds(base + (QBLK * r // COARSE_R) * i, QBLK, stride=r // COARSE_R)
                q, k, v = q4_s[src, :], k4_s[src, :], v4_s[src, :]
            else:
                src = pl.ds(j + r * QBLK * i, QBLK, stride=r)
                q, k, v = qn_s[src, :], kn_s[src, :], vn_s[src, :]
            if r == COARSE_R:
                dense = pl.ds(pl.multiple_of(idx * QBLK, QBLK), QBLK)
                q4_s[dense, :] = q
                k4_s[dense, :] = k
                v4_s[dense, :] = v
            k0 = pl.multiple_of((idx + j + 1) * QBLK, QBLK)
            qt_s[idx] = q.T.astype(BF16)
            kc_s[pl.ds(k0, QBLK), :] = k.astype(BF16)
            vt_s[idx + j + 1] = v.T.astype(BF16)
            return c2

        lax.fori_loop(0, n_flat, gather, 0, unroll=32)

        def scores(idx, c2, split=split):
            j, i = split(idx)
            k0 = pl.multiple_of((idx + j) * QBLK, QBLK)
            qt = qt_s[idx]
            kw = kc_s[pl.ds(k0, 2 * QBLK), :]
            zero = jnp.zeros_like(qt)
            rhs = jnp.concatenate([jnp.where(top, qt, zero), jnp.where(top, zero, qt)], axis=1)
            lhs = jnp.concatenate([kw, bias_s[jnp.minimum(i, 1)]], axis=1)
            sc = jnp.dot(lhs, jnp.concatenate([rhs, eye_pair], axis=0), preferred_element_type=F32)
            m = jnp.max(sc, axis=0, keepdims=True)
            p_s[idx] = jnp.exp2(sc - m).astype(BF16)
            m_s[idx] = jnp.broadcast_to(m, (SUBLANES, 2 * QBLK))
            return c2

        lax.fori_loop(0, n_flat, scores, 0, unroll=32)

        def block(idx, c2, r=r, split=split, on_s=on_s, ln_s=ln_s):
            j, i = split(idx)
            p = p_s[idx]
            m = m_s[idx][0:1, :]
            lhs = jnp.concatenate(
                [jnp.concatenate([vt_s[idx + j], vt_s[idx + j + 1]], axis=1), ones_rows], axis=0)
            ot = jnp.dot(lhs, p, preferred_element_type=F32)
            den = ot[LANES:LANES + 1, :]
            inv = 1.0 / den
            lse = m + jnp.log2(den)
            o_t = jnp.concatenate(
                [ot[0:HEAD_DIM, 0:QBLK] * inv[:, 0:QBLK],
                 ot[HEAD_DIM:LANES, QBLK:2 * QBLK] * inv[:, QBLK:2 * QBLK]], axis=0)
            l_t = jnp.concatenate(
                [jnp.broadcast_to(lse[:, 0:QBLK], (HEAD_DIM, QBLK)),
                 jnp.broadcast_to(lse[:, QBLK:2 * QBLK], (HEAD_DIM, QBLK))], axis=0)
            dst = pl.ds(j + r * QBLK * i, QBLK, stride=r)
            on_s[dst, :] = o_t.T
            ln_s[dst, :] = l_t.T
            return c2

        lax.fori_loop(0, n_flat, block, 0, unroll=32)

    def combine(n, carry):
        r0 = pl.multiple_of(n * PRO_ROWS, PRO_ROWS)
        rows = pl.ds(r0, PRO_ROWS)
        ls = [l_s[rows, :] for l_s in lses]
        m = jnp.maximum(jnp.maximum(ls[0], ls[1]), ls[2])
        es = [jnp.exp2(l - m) for l in ls]
        num = es[0] * o0_s[rows, :] + es[1] * o1_s[rows, :] + es[2] * o2_s[rows, :]
        o_ref[0, rows, :] = (num / (es[0] + es[1] + es[2])).astype(o_ref.dtype)
        return carry

    lax.fori_loop(0, seq // PRO_ROWS, combine, 0, unroll=32)


def _dilated(proj, tables, q_gain, k_gain, col0):
    b, s, _ = proj.shape
    n_pairs = DIL_HEADS * HEAD_DIM // LANES
    t0 = col0 // LANES
    tab = pl.BlockSpec((1, s, LANES), lambda bi, p: (bi, 0, 0), pipeline_mode=pl.Buffered(1))
    two = lambda g: jnp.concatenate([g, g]).reshape(1, LANES)
    nat = pltpu.VMEM((s, LANES), F32)
    max_r = max(r for _, r in DIL_PATTERNS)
    return pl.pallas_call(
        functools.partial(_dilated_kernel, seq=s),
        grid=(b, n_pairs),
        in_specs=[
            pl.BlockSpec((1, s, LANES), lambda bi, p: (bi, 0, t0 + p)),
            pl.BlockSpec((1, s, LANES), lambda bi, p: (bi, 0, t0 + n_pairs + p)),
            pl.BlockSpec((1, s, LANES), lambda bi, p: (bi, 0, t0 + 2 * n_pairs + p)),
            tab, tab,
            _resident((1, LANES)),
            _resident((1, LANES)),
        ],
        out_specs=pl.BlockSpec((1, s, LANES), lambda bi, p: (bi, 0, p)),
        out_shape=jax.ShapeDtypeStruct((b, s, DIL_HEADS * HEAD_DIM), BF16),
        scratch_shapes=[
            nat, nat, nat, nat, nat, nat,
            pltpu.VMEM((s // QBLK, LANES, QBLK), BF16),
            pltpu.VMEM((s + max_r * QBLK, LANES), BF16),
            pltpu.VMEM((s // QBLK + max_r, LANES, QBLK), BF16),
            pltpu.VMEM((2, 2 * QBLK, QBLK), BF16),
            pltpu.VMEM((s // QBLK, 2 * QBLK, 2 * QBLK), BF16),
            pltpu.VMEM((s // QBLK, SUBLANES, 2 * QBLK), F32),
            nat, nat, nat, nat, nat, nat,
        ],
        compiler_params=_cparams(2),
        name="dilated_attention",
    )(proj, proj, proj, *tables, two(q_gain), two(k_gain))


def _swa_kernel(q_ref, k_ref, v_ref, c_ref, s_ref, qg_ref, kg_ref, sink_ref, o_ref,
                qt_s, kd_s, vt_s, bias_s, p_s, m_s, rq_s, rk_s, *, seq):
    group = SWA_Q_HEADS // SWA_KV_HEADS
    swap = _swap_matrix(ROPE_DIMS // 2)
    head_ones = _head_ones()
    kv_in_hi = (pl.program_id(1) % 2) == 1
    blk_per_step = PRO_ROWS // QBLK
    src = lax.broadcasted_iota(jnp.int32, (LANES, LANES), 0)
    dst = lax.broadcasted_iota(jnp.int32, (LANES, LANES), 1) & (HEAD_DIM - 1)
    dup = jnp.where(src == dst + jnp.where(kv_in_hi, HEAD_DIM, 0), 1.0, 0.0).astype(BF16)

    def norms(n, carry):
        rows = pl.ds(pl.multiple_of(n * PRO_ROWS, PRO_ROWS), PRO_ROWS)
        for t in range(group // 2):
            cols = slice(t * LANES, (t + 1) * LANES)
            rq_s[rows, cols] = _head_rinv(q_ref[0, rows, cols].astype(F32), head_ones)
        rk_s[rows, :] = _head_rinv(k_ref[0, rows, :].astype(F32), head_ones)
        return carry

    lax.fori_loop(0, seq // PRO_ROWS, norms, 0, unroll=32)

    def prologue(n, carry):
        r0 = pl.multiple_of(n * PRO_ROWS, PRO_ROWS)
        rows = pl.ds(r0, PRO_ROWS)
        c, sn = c_ref[0, rows, :], s_ref[0, rows, :]
        for t in range(group // 2):
            cols = slice(t * LANES, (t + 1) * LANES)
            q = q_ref[0, rows, cols].astype(F32) * rq_s[rows, cols] * qg_ref[...]
            q_t = (_rotate(q, c, sn, swap) * Q_SCALE).T
            for u in range(blk_per_step):
                qt_s[n * blk_per_step + u, t] = q_t[:, u * QBLK:(u + 1) * QBLK].astype(BF16)
        k = k_ref[0, rows, :].astype(F32) * rk_s[rows, :] * kg_ref[...]
        k = _rotate(k, c, sn, swap).astype(BF16)
        kd_s[pl.ds(r0 + QBLK, PRO_ROWS), :] = jnp.dot(k, dup, preferred_element_type=F32).astype(BF16)
        v_t = v_ref[0, rows, :].astype(F32).T
        v_t = jnp.where(kv_in_hi, v_t[HEAD_DIM:LANES, :], v_t[0:HEAD_DIM, :]).astype(BF16)
        for u in range(blk_per_step):
            vt_s[n * blk_per_step + u + 1] = v_t[:, u * QBLK:(u + 1) * QBLK]
        return carry

    lax.fori_loop(0, seq // PRO_ROWS, prologue, 0, unroll=32)

    first, later = _band_bias_t(1, QBLK)
    bias_s[0] = first.astype(BF16)
    bias_s[1] = later.astype(BF16)
    eye = (lax.broadcasted_iota(jnp.int32, (QBLK, QBLK), 0)
           == lax.broadcasted_iota(jnp.int32, (QBLK, QBLK), 1))
    eye = jnp.where(eye, 1.0, 0.0).astype(BF16)
    eye_pair = jnp.concatenate([eye, eye], axis=1)
    top = lax.broadcasted_iota(jnp.int32, (LANES, QBLK), 0) < HEAD_DIM
    kd_s[0:QBLK, :] = jnp.zeros((QBLK, LANES), BF16)
    vt_s[0] = jnp.zeros((HEAD_DIM, QBLK), BF16)
    sink = jnp.concatenate([sink_ref[0, hh:hh + 1, :] for hh in range(group)], axis=1) * LOG2_E
    ones_rows = jnp.ones((DEN_ROWS, 2 * QBLK), BF16)

    def scores(i, carry):
        d0 = pl.multiple_of(i * QBLK, QBLK)
        rows = pl.ds(d0, QBLK)
        kw = kd_s[pl.ds(d0, 2 * QBLK), :]
        lhs = jnp.concatenate([kw, bias_s[jnp.minimum(i, 1)]], axis=1)
        for t in range(group // 2):
            qt = qt_s[i, t]
            zero = jnp.zeros_like(qt)
            rhs = jnp.concatenate([jnp.where(top, qt, zero), jnp.where(top, zero, qt)], axis=1)
            cols = slice(2 * t * QBLK, 2 * (t + 1) * QBLK)
            sc = jnp.dot(lhs, jnp.concatenate([rhs, eye_pair], axis=0), preferred_element_type=F32)
            m = jnp.maximum(jnp.max(sc, axis=0, keepdims=True), sink[:, cols])
            p_s[i, :, cols] = jnp.exp2(sc - m).astype(BF16)
            m_s[i, :, cols] = jnp.broadcast_to(m, (SUBLANES, 2 * QBLK))
        return carry

    lax.fori_loop(0, seq // QBLK, scores, 0, unroll=BLOCK_UNROLL)

    def values(i, carry):
        rows = pl.ds(pl.multiple_of(i * QBLK, QBLK), QBLK)
        m = m_s[i][0:1, :]
        lhs = jnp.concatenate(
            [jnp.concatenate([vt_s[i], vt_s[i + 1]], axis=1), ones_rows], axis=0)
        ot = jnp.dot(lhs, p_s[i], preferred_element_type=F32)
        inv = 1.0 / (ot[HEAD_DIM:HEAD_DIM + 1, :] + jnp.exp2(sink - m))
        on = ot[0:HEAD_DIM, :] * inv
        for t in range(group // 2):
            pair_t = jnp.concatenate(
                [on[:, 2 * t * QBLK:(2 * t + 1) * QBLK], on[:, (2 * t + 1) * QBLK:(2 * t + 2) * QBLK]],
                axis=0)
            o_ref[0, rows, t * LANES:(t + 1) * LANES] = pair_t.T.astype(o_ref.dtype)
        return carry

    lax.fori_loop(0, seq // QBLK, values, 0, unroll=32)


def _swa(proj, tables, q_gain, k_gain, sinks):
    b, s, _ = proj.shape
    group = SWA_Q_HEADS // SWA_KV_HEADS
    q_w = group * HEAD_DIM
    k_t0 = SWA_Q_HEADS * HEAD_DIM // LANES
    v_t0 = k_t0 + SWA_KV_HEADS * HEAD_DIM // LANES
    tab = pl.BlockSpec((1, s, LANES), lambda bi, g: (bi, 0, 0), pipeline_mode=pl.Buffered(1))
    two = lambda g: jnp.concatenate([g, g]).reshape(1, LANES)
    sink_rows = jnp.broadcast_to(sinks.reshape(SWA_KV_HEADS, group, 1), (SWA_KV_HEADS, group, LANES))
    return pl.pallas_call(
        functools.partial(_swa_kernel, seq=s),
        grid=(b, SWA_KV_HEADS),
        in_specs=[
            pl.BlockSpec((1, s, q_w), lambda bi, g: (bi, 0, g)),
            pl.BlockSpec((1, s, LANES), lambda bi, g: (bi, 0, k_t0 + g // 2)),
            pl.BlockSpec((1, s, LANES), lambda bi, g: (bi, 0, v_t0 + g // 2)),
            tab, tab,
            _resident((1, LANES)),
            _resident((1, LANES)),
            pl.BlockSpec((1, group, LANES), lambda bi, g: (g, 0, 0)),
        ],
        out_specs=pl.BlockSpec((1, s, q_w), lambda bi, g: (bi, 0, g)),
        out_shape=jax.ShapeDtypeStruct((b, s, SWA_Q_HEADS * HEAD_DIM), BF16),
        scratch_shapes=[
            pltpu.VMEM((s // QBLK, group // 2, LANES, QBLK), BF16),
            pltpu.VMEM((s + QBLK, LANES), BF16),
            pltpu.VMEM((s // QBLK + 1, HEAD_DIM, QBLK), BF16),
            pltpu.VMEM((2, 2 * QBLK, QBLK), BF16),
            pltpu.VMEM((s // QBLK, 2 * QBLK, group * QBLK), BF16),
            pltpu.VMEM((s // QBLK, SUBLANES, group * QBLK), F32),
            pltpu.VMEM((s, q_w), F32),
            pltpu.VMEM((s, LANES), F32),
        ],
        compiler_params=_cparams(2),
        name="swa_attention",
    )(proj, proj, proj, *tables, two(q_gain), two(k_gain), sink_rows)


def kernel(x, positions, norm_mix, norm_mlp, mlp_w_up, mlp_w_down, hyb_w_in, hyb_w_out, ret_gn_gain, dil_q_gain, dil_k_gain, swa_w_qkv, swa_b_qkv, swa_w_out, swa_q_gain, swa_k_gain, swa_sinks):
    b, s, d = x.shape
    depth = norm_mix.shape[0]
    ret_tab = _rope_table(positions, RET_QK_DIM // 2, RET_THETA, RET_QK_DIM)
    rope_tab = _rope_table(positions, ROPE_DIMS // 2, ROPE_THETA, ROPE_DIMS)
    log_gamma = jnp.log1p(-jnp.exp2(-5.0 - jnp.arange(RET_HEADS, dtype=F32)))
    ret_w = RET_HEADS * RET_V_DIM
    dil_col0 = 2 * RET_HEADS * RET_QK_DIM + 2 * ret_w

    x2d = x.reshape(b * s, d)
    for layer in range(depth):
        i = layer // 2
        if layer % 2 == 0:
            zero_bias = jnp.zeros((hyb_w_in.shape[2],), F32)
            proj = _norm_proj(x2d, norm_mix[layer], hyb_w_in, i, zero_bias).reshape(b, s, -1)
            ra = _retention(proj, ret_tab, log_gamma, ret_gn_gain[i])
            da = _dilated(proj, rope_tab, dil_q_gain[i], dil_k_gain[i], dil_col0)
            mixed = [ra.reshape(b * s, -1), da.reshape(b * s, -1)]
            w_out = hyb_w_out
        else:
            proj = _norm_proj(x2d, norm_mix[layer], swa_w_qkv, i, swa_b_qkv[i]).reshape(b, s, -1)
            att = _swa(proj, rope_tab, swa_q_gain[i], swa_k_gain[i], swa_sinks[i])
            mixed = [att.reshape(b * s, -1)]
            w_out = swa_w_out
        x2d = _out_mlp(mixed, x2d, w_out, i, norm_mlp[layer], mlp_w_up, mlp_w_down, layer)
    return x2d.reshape(b, s, d)
```

```python
import functools

import jax
import jax.numpy as jnp
import numpy as np
from jax import lax
from jax.experimental import pallas as pl
from jax.experimental.pallas import tpu as pltpu

F32 = jnp.float32
BF16 = jnp.bfloat16

D_MODEL = 1024
D_FF = 4 * D_MODEL
HEAD_DIM = 64
EPS = 1e-6
RET_HEADS = 4
RET_QK_DIM = 64
RET_V_DIM = 128
RET_CHUNK = 128
RET_THETA = 10000.0
DIL_HEADS = 8
DIL_PATTERNS = ((128, 1), (512, 4), (2048, 16))
SWA_Q_HEADS = 16
SWA_KV_HEADS = 4
SWA_WINDOW = 128
ROPE_THETA = 500000.0
ROPE_DIMS = HEAD_DIM // 4

LANES = 128
SUBLANES = 8
QBLK = 128
VMEM_LIMIT_BYTES = 56 * 1024 * 1024
NEG_BIG = -1e30
LOG2_E = 1.4426950408889634
Q_SCALE = HEAD_DIM ** -0.5 * LOG2_E

ROW_TILE = 1024
MLP_ROW_TILE = 512
FF_CHUNK = 1024
PRO_ROWS = 128
BLOCK_UNROLL = 32
DEN_ROWS = 16
COARSE_R, FINE_R = DIL_PATTERNS[1][1], DIL_PATTERNS[2][1]


def _cparams(n_axes):
    return pltpu.CompilerParams(
        dimension_semantics=("arbitrary",) * n_axes,
        vmem_limit_bytes=VMEM_LIMIT_BYTES,
    )


def _resident(shape):
    nd = len(shape)
    return pl.BlockSpec(shape, lambda *_: (0,) * nd, pipeline_mode=pl.Buffered(1))


def _resident_layer(stack_shape, layer):
    return pl.BlockSpec((None,) + tuple(stack_shape[1:]), lambda *_: (layer, 0, 0),
                        pipeline_mode=pl.Buffered(1))


def _split3(x):
    hi = x.astype(BF16)
    r1 = x - hi.astype(F32)
    mid = r1.astype(BF16)
    lo = (r1 - mid.astype(F32)).astype(BF16)
    return hi, mid, lo


def _trig_kernel(p_ref, spread_ref, inv_ref, ec_ref, es_ref, base_ref, c_ref, s_ref, *, n_pos):
    dense = sum(jnp.dot(part, spread_ref[...], preferred_element_type=F32)
                for part in _split3(p_ref[...]))
    ang = dense * inv_ref[...]
    cos_parts = _split3(jnp.cos(ang))
    sin_parts = _split3(jnp.sin(ang))
    rows = p_ref.shape[0]
    for c in range(n_pos):
        dst = pl.ds(c, rows, stride=n_pos)
        c_ref[dst, :] = base_ref[...] + sum(
            jnp.dot(part, ec_ref[c], preferred_element_type=F32) for part in cos_parts)
        s_ref[dst, :] = sum(
            jnp.dot(part, es_ref[c], preferred_element_type=F32) for part in sin_parts)


def _selectors(half):
    n_pos = LANES // half
    ec = np.zeros((n_pos, LANES, LANES), np.float32)
    es = np.zeros((n_pos, LANES, LANES), np.float32)
    base = np.zeros((1, LANES), np.float32)
    for j in range(LANES):
        d = j % HEAD_DIM
        if d >= 2 * half:
            base[0, j] = 1.0
            continue
        f = d % half
        for c in range(n_pos):
            ec[c, c * half + f, j] = 1.0
            es[c, c * half + f, j] = -1.0 if d < half else 1.0
    return n_pos, jnp.asarray(ec, BF16), jnp.asarray(es, BF16), jnp.asarray(base)


def _rope_table(positions, half, theta, n_rot):
    b, s = positions.shape
    posf = positions.astype(F32)
    inv = jnp.power(jnp.float32(theta), -jnp.arange(half, dtype=F32) * (2.0 / n_rot))
    n_pos, ec, es, base = _selectors(half)
    rows = s // n_pos
    p = posf.reshape(b, rows, n_pos)
    spread = jnp.asarray(np.repeat(np.eye(n_pos, dtype=np.float32), half, axis=1), BF16)
    inv_row = jnp.tile(inv, n_pos)[None, :]
    tile = min(rows, 2048 // n_pos)
    assert rows % tile == 0
    dense = pl.BlockSpec((None, tile, n_pos), lambda bi, i: (bi, i, 0))
    wide = pl.BlockSpec((None, tile * n_pos, LANES), lambda bi, i: (bi, i, 0))
    c_tab, s_tab = pl.pallas_call(
        functools.partial(_trig_kernel, n_pos=n_pos),
        grid=(b, rows // tile),
        in_specs=[dense, _resident(spread.shape), _resident(inv_row.shape), _resident(ec.shape),
                  _resident(es.shape), _resident(base.shape)],
        out_specs=[wide, wide],
        out_shape=[jax.ShapeDtypeStruct((b, s, LANES), F32)] * 2,
        compiler_params=_cparams(2),
        name="trig_tables",
    )(p, spread, inv_row, ec, es, base)
    return c_tab, s_tab


def _swap_matrix(half):
    src = lax.broadcasted_iota(jnp.int32, (LANES, LANES), 0)
    dst = lax.broadcasted_iota(jnp.int32, (LANES, LANES), 1)
    d = dst & (HEAD_DIM - 1)
    want = jnp.where(d < half, dst + half, jnp.where(d < 2 * half, dst - half, -1))
    return jnp.where(src == want, 1.0, 0.0).astype(BF16)


def _rotate(x, c, s, swap):
    swapped = jnp.dot(x.astype(BF16), swap, preferred_element_type=F32)
    return x * c + swapped * s


def _lo_mask(shape):
    return lax.broadcasted_iota(jnp.int32, shape, len(shape) - 1) < HEAD_DIM


def _head_ones():
    r = lax.broadcasted_iota(jnp.int32, (LANES, LANES), 0) < HEAD_DIM
    c = lax.broadcasted_iota(jnp.int32, (LANES, LANES), 1) < HEAD_DIM
    return jnp.where(r == c, 1.0, 0.0).astype(BF16)


def _head_rinv(x, head_ones):
    ss = jnp.dot((x * x).astype(BF16), head_ones, preferred_element_type=F32)
    return lax.rsqrt(ss * (1.0 / HEAD_DIM) + EPS)


def _norm_proj_kernel(x_ref, g_ref, w_ref, b_ref, o_ref, *, n_chunk):
    x = x_ref[...]
    ms = jnp.mean(x * x, axis=-1, keepdims=True)
    h = (x * lax.rsqrt(ms + EPS) * g_ref[...]).astype(BF16)
    n = o_ref.shape[-1]
    for c in range(0, n, n_chunk):
        acc = jnp.dot(h, w_ref[:, c:c + n_chunk].astype(BF16), preferred_element_type=F32)
        o_ref[:, c:c + n_chunk] = (acc + b_ref[:, c:c + n_chunk]).astype(o_ref.dtype)


def _norm_proj(x2d, gain, w_stack, layer, bias):
    m, d = x2d.shape
    n = w_stack.shape[2]
    return pl.pallas_call(
        functools.partial(_norm_proj_kernel, n_chunk=512),
        grid=(m // ROW_TILE,),
        in_specs=[
            pl.BlockSpec((ROW_TILE, d), lambda i: (i, 0)),
            _resident((1, d)),
            _resident_layer(w_stack.shape, layer),
            _resident((1, n)),
        ],
        out_specs=pl.BlockSpec((ROW_TILE, n), lambda i: (i, 0)),
        out_shape=jax.ShapeDtypeStruct((m, n), BF16),
        compiler_params=_cparams(1),
        name="norm_proj",
    )(x2d, gain.reshape(1, d), w_stack, bias.reshape(1, n))


def _out_mlp_kernel(*refs, n_mix):
    a_refs = refs[:n_mix]
    x_ref, wo_ref, g_ref, wup_ref, wdn_ref, o_ref = refs[n_mix:]
    mixed = a_refs[0][...] if n_mix == 1 else jnp.concatenate([a[...] for a in a_refs], axis=-1)
    x1 = x_ref[...] + jnp.dot(mixed, wo_ref[...].astype(BF16), preferred_element_type=F32)
    ms = jnp.mean(x1 * x1, axis=-1, keepdims=True)
    h = (x1 * lax.rsqrt(ms + EPS) * g_ref[...]).astype(BF16)
    y = x1
    for c in range(0, D_FF, FF_CHUNK):
        u = jnp.dot(h, wup_ref[:, c:c + FF_CHUNK].astype(BF16), preferred_element_type=F32)
        u = jnp.square(jnp.maximum(u, 0.0)).astype(BF16)
        y = y + jnp.dot(u, wdn_ref[c:c + FF_CHUNK, :].astype(BF16), preferred_element_type=F32)
    o_ref[...] = y


def _out_mlp(mixed, x2d, w_out, out_layer, gain, w_up, w_down, mlp_layer):
    m, d = x2d.shape
    n_mix = len(mixed)
    in_specs = [pl.BlockSpec((MLP_ROW_TILE, a.shape[1]), lambda i: (i, 0)) for a in mixed]
    in_specs.append(pl.BlockSpec((MLP_ROW_TILE, d), lambda i: (i, 0)))
    in_specs += [_resident_layer(w_out.shape, out_layer), _resident((1, d)),
                 _resident_layer(w_up.shape, mlp_layer), _resident_layer(w_down.shape, mlp_layer)]
    return pl.pallas_call(
        functools.partial(_out_mlp_kernel, n_mix=n_mix),
        grid=(m // MLP_ROW_TILE,),
        in_specs=in_specs,
        out_specs=pl.BlockSpec((MLP_ROW_TILE, d), lambda i: (i, 0)),
        out_shape=jax.ShapeDtypeStruct((m, d), F32),
        compiler_params=_cparams(1),
        name="out_mlp",
    )(*mixed, x2d, w_out, gain.reshape(1, d), w_up, w_down)


def _retention_kernel(q_ref, k_ref, v_ref, g_ref, c_ref, s_ref, lg_ref, gn_ref,
                      o_ref, q_s, k_s, vt_s, o_s, kv_s, *, seq):
    n_chunks = seq // RET_CHUNK
    swap = _swap_matrix(RET_QK_DIM // 2)
    cs = RET_CHUNK

    def prologue(n, carry):
        rows = pl.ds(pl.multiple_of(n * cs, cs), cs)
        c, sn = c_ref[0, rows, :], s_ref[0, rows, :]
        q = _rotate(q_ref[0, rows, :].astype(F32), c, sn, swap)
        k = _rotate(k_ref[0, rows, :].astype(F32), c, sn, swap) * (RET_QK_DIM ** -0.5)
        q_s[n] = q.astype(BF16)
        k_s[n] = k.astype(BF16)
        for hh in range(2):
            cols = slice(hh * RET_V_DIM, (hh + 1) * RET_V_DIM)
            vt_s[n, hh] = v_ref[0, rows, cols].astype(F32).T.astype(BF16)
        return carry

    lax.fori_loop(0, n_chunks, prologue, 0, unroll=32)

    lo = _lo_mask((cs, LANES))
    key = lax.broadcasted_iota(jnp.int32, (cs, cs), 0).astype(F32)
    tok = lax.broadcasted_iota(jnp.int32, (cs, cs), 1).astype(F32)
    ahead = tok - key
    consts = []
    for hh in range(2):
        lg = lg_ref[0, hh:hh + 1, :]
        decay_t = jnp.where(ahead >= 0, jnp.exp(lg * jnp.maximum(ahead, 0.0)), 0.0)
        xi = jnp.exp(lg * (tok[0:1, :] + 1.0))
        zeta = jnp.exp(lg * (cs - 1.0 - tok[0:1, :]))
        cd = jnp.exp(lg * float(cs))
        head = lo if hh == 0 else jnp.logical_not(lo)
        gain = jnp.broadcast_to(gn_ref[0, hh:hh + 1, :], (cs, RET_V_DIM)).T
        consts.append((decay_t, xi, zeta, cd, head, gain))

    def inner(n, carry):
        qc, kc = q_s[n], k_s[n]
        for hh in range(2):
            decay_t, xi, zeta, cd, head, gain = consts[hh]
            qm = jnp.where(head, qc, jnp.zeros_like(qc))
            vt = vt_s[n, hh]
            sc_t = lax.dot_general(kc, qm, (((1,), (1,)), ((), ())),
                                   preferred_element_type=F32) * decay_t
            o_s[n, hh] = jnp.dot(vt, sc_t.astype(BF16), preferred_element_type=F32)
            vz = (vt.astype(F32) * zeta).astype(BF16)
            kv_s[n, hh] = jnp.dot(vz, kc, preferred_element_type=F32)
        return carry

    lax.fori_loop(0, n_chunks, inner, 0, unroll=32)

    def cross(n, state):
        qc = q_s[n]
        new_state = []
        for hh in range(2):
            decay_t, xi, zeta, cd, head, gain = consts[hh]
            r_prev = state[hh]
            qm = jnp.where(head, qc, jnp.zeros_like(qc))
            o_s[n, hh] += lax.dot_general(r_prev.astype(BF16), qm, (((1,), (1,)), ((), ())),
                                          preferred_element_type=F32) * xi
            new_state.append(r_prev * cd + kv_s[n, hh])
        return tuple(new_state)

    zero = jnp.zeros((RET_V_DIM, LANES), F32)
    lax.fori_loop(0, n_chunks, cross, (zero, zero), unroll=32)

    def finish(n, carry):
        rows = pl.ds(pl.multiple_of(n * cs, cs), cs)
        for hh in range(2):
            gain = consts[hh][5]
            cols = slice(hh * RET_V_DIM, (hh + 1) * RET_V_DIM)
            o = o_s[n, hh]
            mu = jnp.mean(o, axis=0, keepdims=True)
            dev = o - mu
            var = jnp.mean(dev * dev, axis=0, keepdims=True)
            y = (dev * lax.rsqrt(var + EPS) * gain).T
            gate = g_ref[0, rows, cols].astype(F32)
            o_ref[0, rows, cols] = (gate * jax.nn.sigmoid(gate) * y).astype(o_ref.dtype)
        return carry

    lax.fori_loop(0, n_chunks, finish, 0, unroll=32)


def _retention(proj, tables, log_gamma, gn_gain):
    b, s, _ = proj.shape
    n_pairs = RET_HEADS // 2
    pair_w = 2 * RET_V_DIM
    qk_tiles = RET_HEADS * RET_QK_DIM // LANES
    v_off = 2 * qk_tiles * LANES // pair_w
    g_off = v_off + RET_HEADS * RET_V_DIM // pair_w
    tab = pl.BlockSpec((1, s, LANES), lambda bi, p: (bi, 0, 0), pipeline_mode=pl.Buffered(1))
    lg = jnp.broadcast_to(log_gamma.reshape(n_pairs, 2, 1), (n_pairs, 2, LANES))
    n_chunks = s // RET_CHUNK
    return pl.pallas_call(
        functools.partial(_retention_kernel, seq=s),
        grid=(b, n_pairs),
        in_specs=[
            pl.BlockSpec((1, s, LANES), lambda bi, p: (bi, 0, p)),
            pl.BlockSpec((1, s, LANES), lambda bi, p: (bi, 0, qk_tiles + p)),
            pl.BlockSpec((1, s, pair_w), lambda bi, p: (bi, 0, v_off + p)),
            pl.BlockSpec((1, s, pair_w), lambda bi, p: (bi, 0, g_off + p)),
            tab, tab,
            pl.BlockSpec((1, 2, LANES), lambda bi, p: (p, 0, 0)),
            pl.BlockSpec((1, 2, RET_V_DIM), lambda bi, p: (p, 0, 0)),
        ],
        out_specs=pl.BlockSpec((1, s, pair_w), lambda bi, p: (bi, 0, p)),
        out_shape=jax.ShapeDtypeStruct((b, s, RET_HEADS * RET_V_DIM), BF16),
        scratch_shapes=[
            pltpu.VMEM((n_chunks, RET_CHUNK, LANES), BF16),
            pltpu.VMEM((n_chunks, RET_CHUNK, LANES), BF16),
            pltpu.VMEM((n_chunks, 2, RET_V_DIM, RET_CHUNK), BF16),
            pltpu.VMEM((n_chunks, 2, RET_V_DIM, RET_CHUNK), F32),
            pltpu.VMEM((n_chunks, 2, RET_V_DIM, LANES), F32),
        ],
        compiler_params=_cparams(2),
        name="retention",
    )(proj, proj, proj, proj, *tables, lg, gn_gain.reshape(n_pairs, 2, RET_V_DIM))


def _band_bias_t(lo_off, hi_off):
    c = lax.broadcasted_iota(jnp.int32, (2 * QBLK, QBLK), 0)
    a = lax.broadcasted_iota(jnp.int32, (2 * QBLK, QBLK), 1)
    band = (c - a >= lo_off) & (c - a <= hi_off)
    later = jnp.where(band, 0.0, NEG_BIG).astype(F32)
    first = jnp.where(band & (c >= QBLK), 0.0, NEG_BIG).astype(F32)
    return first, later


def _dilated_kernel(q_ref, k_ref, v_ref, c_ref, s_ref, qg_ref, kg_ref, o_ref,
                    qn_s, kn_s, vn_s, q4_s, k4_s, v4_s, qt_s, kc_s, vt_s, bias_s, p_s, m_s,
                    o0_s, o1_s, o2_s, l0_s, l1_s, l2_s, *, seq):
    swap = _swap_matrix(ROPE_DIMS // 2)
    head_ones = _head_ones()
    rq_s, rk_s = o0_s, l0_s

    def norms(n, carry):
        rows = pl.ds(pl.multiple_of(n * PRO_ROWS, PRO_ROWS), PRO_ROWS)
        rq_s[rows, :] = _head_rinv(q_ref[0, rows, :].astype(F32), head_ones)
        rk_s[rows, :] = _head_rinv(k_ref[0, rows, :].astype(F32), head_ones)
        return carry

    lax.fori_loop(0, seq // PRO_ROWS, norms, 0, unroll=32)

    def prologue(n, carry):
        rows = pl.ds(pl.multiple_of(n * PRO_ROWS, PRO_ROWS), PRO_ROWS)
        c, sn = c_ref[0, rows, :], s_ref[0, rows, :]
        q = q_ref[0, rows, :].astype(F32) * rq_s[rows, :] * qg_ref[...]
        qn_s[rows, :] = _rotate(q, c, sn, swap) * Q_SCALE
        k = k_ref[0, rows, :].astype(F32) * rk_s[rows, :] * kg_ref[...]
        kn_s[rows, :] = _rotate(k, c, sn, swap)
        vn_s[rows, :] = v_ref[0, rows, :].astype(F32)
        return carry

    lax.fori_loop(0, seq // PRO_ROWS, prologue, 0, unroll=32)

    first, later = _band_bias_t(0, QBLK)
    bias_s[0] = first.astype(BF16)
    bias_s[1] = later.astype(BF16)
    eye = (lax.broadcasted_iota(jnp.int32, (QBLK, QBLK), 0)
           == lax.broadcasted_iota(jnp.int32, (QBLK, QBLK), 1))
    eye = jnp.where(eye, 1.0, 0.0).astype(BF16)
    eye_pair = jnp.concatenate([eye, eye], axis=1)
    zero_pad = jnp.zeros((QBLK, LANES), BF16)
    top = lax.broadcasted_iota(jnp.int32, (LANES, QBLK), 0) < HEAD_DIM
    ones_rows = jnp.ones((DEN_ROWS, 2 * QBLK), BF16)

    outs = (o0_s, o1_s, o2_s)
    lses = (l0_s, l1_s, l2_s)
    n_flat = seq // QBLK
    for (window, r), on_s, ln_s in zip(DIL_PATTERNS, outs, lses):
        assert window // r == QBLK
        n_blk = seq // r // QBLK
        blk_shift = n_blk.bit_length() - 1
        assert n_blk == 1 << blk_shift

        def pad(j, c2, n_blk=n_blk):
            z = j * (n_blk + 1)
            kc_s[pl.ds(pl.multiple_of(z * QBLK, QBLK), QBLK), :] = zero_pad
            vt_s[z] = zero_pad
            return c2

        lax.fori_loop(0, r, pad, 0, unroll=r)

        def split(idx, r=r, n_blk=n_blk, blk_shift=blk_shift):
            j = lax.shift_right_logical(idx, blk_shift)
            i = idx & (n_blk - 1)
            return j, i

        def gather(idx, c2, r=r, split=split):
            j, i = split(idx)
            if r == FINE_R:
                base = ((j & (COARSE_R - 1)) * (seq // COARSE_R)
                        + lax.shift_right_logical(j, COARSE_R.bit_length() - 1))
                src = pl.ds(base + (QBLK * r // COARSE_R) * i, QBLK, stride=r // COARSE_R)
                q, k, v = q4_s[src, :], k4_s[src, :], v4_s[src, :]
            else:
                src = pl.ds(j + r * QBLK * i, QBLK, stride=r)
                q, k, v = qn_s[src, :], kn_s[src, :], vn_s[src, :]
            if r == COARSE_R:
                dense = pl.ds(pl.multiple_of(idx * QBLK, QBLK), QBLK)
                q4_s[dense, :] = q
                k4_s[dense, :] = k
                v4_s[dense, :] = v
            k0 = pl.multiple_of((idx + j + 1) * QBLK, QBLK)
            qt_s[idx] = q.T.astype(BF16)
            kc_s[pl.ds(k0, QBLK), :] = k.astype(BF16)
            vt_s[idx + j + 1] = v.T.astype(BF16)
            return c2

        lax.fori_loop(0, n_flat, gather, 0, unroll=32)

        def scores(idx, c2, split=split):
            j, i = split(idx)
            k0 = pl.multiple_of((idx + j) * QBLK, QBLK)
            qt = qt_s[idx]
            kw = kc_s[pl.ds(k0, 2 * QBLK), :]
            zero = jnp.zeros_like(qt)
            rhs = jnp.concatenate([jnp.where(top, qt, zero), jnp.where(top, zero, qt)], axis=1)
            lhs = jnp.concatenate([kw, bias_s[jnp.minimum(i, 1)]], axis=1)
            sc = jnp.dot(lhs, jnp.concatenate([rhs, eye_pair], axis=0), preferred_element_type=F32)
            m = jnp.max(sc, axis=0, keepdims=True)
            p_s[idx] = jnp.exp2(sc - m).astype(BF16)
            m_s[idx] = jnp.broadcast_to(m, (SUBLANES, 2 * QBLK))
            return c2

        lax.fori_loop(0, n_flat, scores, 0, unroll=32)

        def block(idx, c2, r=r, split=split, on_s=on_s, ln_s=ln_s):
            j, i = split(idx)
            p = p_s[idx]
            m = m_s[idx][0:1, :]
            lhs = jnp.concatenate(
                [jnp.concatenate([vt_s[idx + j], vt_s[idx + j + 1]], axis=1), ones_rows], axis=0)
            ot = jnp.dot(lhs, p, preferred_element_type=F32)
            den = ot[LANES:LANES + 1, :]
            inv = 1.0 / den
            lse = m + jnp.log2(den)
            o_t = jnp.concatenate(
                [ot[0:HEAD_DIM, 0:QBLK] * inv[:, 0:QBLK],
                 ot[HEAD_DIM:LANES, QBLK:2 * QBLK] * inv[:, QBLK:2 * QBLK]], axis=0)
            l_t = jnp.concatenate(
                [jnp.broadcast_to(lse[:, 0:QBLK], (HEAD_DIM, QBLK)),
                 jnp.broadcast_to(lse[:, QBLK:2 * QBLK], (HEAD_DIM, QBLK))], axis=0)
            dst = pl.ds(j + r * QBLK * i, QBLK, stride=r)
            on_s[dst, :] = o_t.T
            ln_s[dst, :] = l_t.T
            return c2

        lax.fori_loop(0, n_flat, block, 0, unroll=32)

    def combine(n, carry):
        r0 = pl.multiple_of(n * PRO_ROWS, PRO_ROWS)
        rows = pl.ds(r0, PRO_ROWS)
        ls = [l_s[rows, :] for l_s in lses]
        m = jnp.maximum(jnp.maximum(ls[0], ls[1]), ls[2])
        es = [jnp.exp2(l - m) for l in ls]
        num = es[0] * o0_s[rows, :] + es[1] * o1_s[rows, :] + es[2] * o2_s[rows, :]
        o_ref[0, rows, :] = (num / (es[0] + es[1] + es[2])).astype(o_ref.dtype)
        return carry

    lax.fori_loop(0, seq // PRO_ROWS, combine, 0, unroll=32)


def _dilated(proj, tables, q_gain, k_gain, col0):
    b, s, _ = proj.shape
    n_pairs = DIL_HEADS * HEAD_DIM // LANES
    t0 = col0 // LANES
    tab = pl.BlockSpec((1, s, LANES), lambda bi, p: (bi, 0, 0), pipeline_mode=pl.Buffered(1))
    two = lambda g: jnp.concatenate([g, g]).reshape(1, LANES)
    nat = pltpu.VMEM((s, LANES), F32)
    max_r = max(r for _, r in DIL_PATTERNS)
    return pl.pallas_call(
        functools.partial(_dilated_kernel, seq=s),
        grid=(b, n_pairs),
        in_specs=[
            pl.BlockSpec((1, s, LANES), lambda bi, p: (bi, 0, t0 + p)),
            pl.BlockSpec((1, s, LANES), lambda bi, p: (bi, 0, t0 + n_pairs + p)),
            pl.BlockSpec((1, s, LANES), lambda bi, p: (bi, 0, t0 + 2 * n_pairs + p)),
            tab, tab,
            _resident((1, LANES)),
            _resident((1, LANES)),
        ],
        out_specs=pl.BlockSpec((1, s, LANES), lambda bi, p: (bi, 0, p)),
        out_shape=jax.ShapeDtypeStruct((b, s, DIL_HEADS * HEAD_DIM), BF16),
        scratch_shapes=[
            nat, nat, nat, nat, nat, nat,
            pltpu.VMEM((s // QBLK, LANES, QBLK), BF16),
            pltpu.VMEM((s + max_r * QBLK, LANES), BF16),
            pltpu.VMEM((s // QBLK + max_r, LANES, QBLK), BF16),
            pltpu.VMEM((2, 2 * QBLK, QBLK), BF16),
            pltpu.VMEM((s // QBLK, 2 * QBLK, 2 * QBLK), BF16),
            pltpu.VMEM((s // QBLK, SUBLANES, 2 * QBLK), F32),
            nat, nat, nat, nat, nat, nat,
        ],
        compiler_params=_cparams(2),
        name="dilated_attention",
    )(proj, proj, proj, *tables, two(q_gain), two(k_gain))


def _swa_kernel(q_ref, k_ref, v_ref, c_ref, s_ref, qg_ref, kg_ref, sink_ref, o_ref,
                qt_s, kd_s, vt_s, bias_s, p_s, m_s, rq_s, rk_s, *, seq):
    group = SWA_Q_HEADS // SWA_KV_HEADS
    swap = _swap_matrix(ROPE_DIMS // 2)
    head_ones = _head_ones()
    kv_in_hi = (pl.program_id(1) % 2) == 1
    blk_per_step = PRO_ROWS // QBLK
    src = lax.broadcasted_iota(jnp.int32, (LANES, LANES), 0)
    dst = lax.broadcasted_iota(jnp.int32, (LANES, LANES), 1) & (HEAD_DIM - 1)
    dup = jnp.where(src == dst + jnp.where(kv_in_hi, HEAD_DIM, 0), 1.0, 0.0).astype(BF16)

    def norms(n, carry):
        rows = pl.ds(pl.multiple_of(n * PRO_ROWS, PRO_ROWS), PRO_ROWS)
        for t in range(group // 2):
            cols = slice(t * LANES, (t + 1) * LANES)
            rq_s[rows, cols] = _head_rinv(q_ref[0, rows, cols].astype(F32), head_ones)
        rk_s[rows, :] = _head_rinv(k_ref[0, rows, :].astype(F32), head_ones)
        return carry

    lax.fori_loop(0, seq // PRO_ROWS, norms, 0, unroll=32)

    def prologue(n, carry):
        r0 = pl.multiple_of(n * PRO_ROWS, PRO_ROWS)
        rows = pl.ds(r0, PRO_ROWS)
        c, sn = c_ref[0, rows, :], s_ref[0, rows, :]
        for t in range(group // 2):
            cols = slice(t * LANES, (t + 1) * LANES)
            q = q_ref[0, rows, cols].astype(F32) * rq_s[rows, cols] * qg_ref[...]
            q_t = (_rotate(q, c, sn, swap) * Q_SCALE).T
            for u in range(blk_per_step):
                qt_s[n * blk_per_step + u, t] = q_t[:, u * QBLK:(u + 1) * QBLK].astype(BF16)
        k = k_ref[0, rows, :].astype(F32) * rk_s[rows, :] * kg_ref[...]
        k = _rotate(k, c, sn, swap).astype(BF16)
        kd_s[pl.ds(r0 + QBLK, PRO_ROWS), :] = jnp.dot(k, dup, preferred_element_type=F32).astype(BF16)
        v_t = v_ref[0, rows, :].astype(F32).T
        v_t = jnp.where(kv_in_hi, v_t[HEAD_DIM:LANES, :], v_t[0:HEAD_DIM, :]).astype(BF16)
        for u in range(blk_per_step):
            vt_s[n * blk_per_step + u + 1] = v_t[:, u * QBLK:(u + 1) * QBLK]
        return carry

    lax.fori_loop(0, seq // PRO_ROWS, prologue, 0, unroll=32)

    first, later = _band_bias_t(1, QBLK)
    bias_s[0] = first.astype(BF16)
    bias_s[1] = later.astype(BF16)
    eye = (lax.broadcasted_iota(jnp.int32, (QBLK, QBLK), 0)
           == lax.broadcasted_iota(jnp.int32, (QBLK, QBLK), 1))
    eye = jnp.where(eye, 1.0, 0.0).astype(BF16)
    eye_pair = jnp.concatenate([eye, eye], axis=1)
    top = lax.broadcasted_iota(jnp.int32, (LANES, QBLK), 0) < HEAD_DIM
    kd_s[0:QBLK, :] = jnp.zeros((QBLK, LANES), BF16)
    vt_s[0] = jnp.zeros((HEAD_DIM, QBLK), BF16)
    sink = jnp.concatenate([sink_ref[0, hh:hh + 1, :] for hh in range(group)], axis=1) * LOG2_E
    ones_rows = jnp.ones((DEN_ROWS, 2 * QBLK), BF16)

    def scores(i, carry):
        d0 = pl.multiple_of(i * QBLK, QBLK)
        rows = pl.ds(d0, QBLK)
        kw = kd_s[pl.ds(d0, 2 * QBLK), :]
        lhs = jnp.concatenate([kw, bias_s[jnp.minimum(i, 1)]], axis=1)
        for t in range(group // 2):
            qt = qt_s[i, t]
            zero = jnp.zeros_like(qt)
            rhs = jnp.concatenate([jnp.where(top, qt, zero), jnp.where(top, zero, qt)], axis=1)
            cols = slice(2 * t * QBLK, 2 * (t + 1) * QBLK)
            sc = jnp.dot(lhs, jnp.concatenate([rhs, eye_pair], axis=0), preferred_element_type=F32)
            m = jnp.maximum(jnp.max(sc, axis=0, keepdims=True), sink[:, cols])
            p_s[i, :, cols] = jnp.exp2(sc - m).astype(BF16)
            m_s[i, :, cols] = jnp.broadcast_to(m, (SUBLANES, 2 * QBLK))
        return carry

    lax.fori_loop(0, seq // QBLK, scores, 0, unroll=BLOCK_UNROLL)

    def values(i, carry):
        rows = pl.ds(pl.multiple_of(i * QBLK, QBLK), QBLK)
        m = m_s[i][0:1, :]
        lhs = jnp.concatenate(
            [jnp.concatenate([vt_s[i], vt_s[i + 1]], axis=1), ones_rows], axis=0)
        ot = jnp.dot(lhs, p_s[i], preferred_element_type=F32)
        inv = 1.0 / (ot[HEAD_DIM:HEAD_DIM + 1, :] + jnp.exp2(sink - m))
        on = ot[0:HEAD_DIM, :] * inv
        for t in range(group // 2):
            pair_t = jnp.concatenate(
                [on[:, 2 * t * QBLK:(2 * t + 1) * QBLK], on[:, (2 * t + 1) * QBLK:(2 * t + 2) * QBLK]],
                axis=0)
            o_ref[0, rows, t * LANES:(t + 1) * LANES] = pair_t.T.astype(o_ref.dtype)
        return carry

    lax.fori_loop(0, seq // QBLK, values, 0, unroll=32)


def _swa(proj, tables, q_gain, k_gain, sinks):
    b, s, _ = proj.shape
    group = SWA_Q_HEADS // SWA_KV_HEADS
    q_w = group * HEAD_DIM
    k_t0 = SWA_Q_HEADS * HEAD_DIM // LANES
    v_t0 = k_t0 + SWA_KV_HEADS * HEAD_DIM // LANES
    tab = pl.BlockSpec((1, s, LANES), lambda bi, g: (bi, 0, 0), pipeline_mode=pl.Buffered(1))
    two = lambda g: jnp.concatenate([g, g]).reshape(1, LANES)
    sink_rows = jnp.broadcast_to(sinks.reshape(SWA_KV_HEADS, group, 1), (SWA_KV_HEADS, group, LANES))
    return pl.pallas_call(
        functools.partial(_swa_kernel, seq=s),
        grid=(b, SWA_KV_HEADS),
        in_specs=[
            pl.BlockSpec((1, s, q_w), lambda bi, g: (bi, 0, g)),
            pl.BlockSpec((1, s, LANES), lambda bi, g: (bi, 0, k_t0 + g // 2)),
            pl.BlockSpec((1, s, LANES), lambda bi, g: (bi, 0, v_t0 + g // 2)),
            tab, tab,
            _resident((1, LANES)),
            _resident((1, LANES)),
            pl.BlockSpec((1, group, LANES), lambda bi, g: (g, 0, 0)),
        ],
        out_specs=pl.BlockSpec((1, s, q_w), lambda bi, g: (bi, 0, g)),
        out_shape=jax.ShapeDtypeStruct((b, s, SWA_Q_HEADS * HEAD_DIM), BF16),
        scratch_shapes=[
            pltpu.VMEM((s // QBLK, group // 2, LANES, QBLK), BF16),
            pltpu.VMEM((s + QBLK, LANES), BF16),
            pltpu.VMEM((s // QBLK + 1, HEAD_DIM, QBLK), BF16),
            pltpu.VMEM((2, 2 * QBLK, QBLK), BF16),
            pltpu.VMEM((s // QBLK, 2 * QBLK, group * QBLK), BF16),
            pltpu.VMEM((s // QBLK, SUBLANES, group * QBLK), F32),
            pltpu.VMEM((s, q_w), F32),
            pltpu.VMEM((s, LANES), F32),
        ],
        compiler_params=_cparams(2),
        name="swa_attention",
    )(proj, proj, proj, *tables, two(q_gain), two(k_gain), sink_rows)


def kernel(x, positions, norm_mix, norm_mlp, mlp_w_up, mlp_w_down, hyb_w_in, hyb_w_out, ret_gn_gain, dil_q_gain, dil_k_gain, swa_w_qkv, swa_b_qkv, swa_w_out, swa_q_gain, swa_k_gain, swa_sinks):
    b, s, d = x.shape
    depth = norm_mix.shape[0]
    ret_tab = _rope_table(positions, RET_QK_DIM // 2, RET_THETA, RET_QK_DIM)
    rope_tab = _rope_table(positions, ROPE_DIMS // 2, ROPE_THETA, ROPE_DIMS)
    log_gamma = jnp.log1p(-jnp.exp2(-5.0 - jnp.arange(RET_HEADS, dtype=F32)))
    ret_w = RET_HEADS * RET_V_DIM
    dil_col0 = 2 * RET_HEADS * RET_QK_DIM + 2 * ret_w

    x2d = x.reshape(b * s, d)
    for layer in range(depth):
        i = layer // 2
        if layer % 2 == 0:
            zero_bias = jnp.zeros((hyb_w_in.shape[2],), F32)
            proj = _norm_proj(x2d, norm_mix[layer], hyb_w_in, i, zero_bias).reshape(b, s, -1)
            ra = _retention(proj, ret_tab, log_gamma, ret_gn_gain[i])
            da = _dilated(proj, rope_tab, dil_q_gain[i], dil_k_gain[i], dil_col0)
            mixed = [ra.reshape(b * s, -1), da.reshape(b * s, -1)]
            w_out = hyb_w_out
        else:
            proj = _norm_proj(x2d, norm_mix[layer], swa_w_qkv, i, swa_b_qkv[i]).reshape(b, s, -1)
            att = _swa(proj, rope_tab, swa_q_gain[i], swa_k_gain[i], swa_sinks[i])
            mixed = [att.reshape(b * s, -1)]
            w_out = swa_w_out
        x2d = _out_mlp(mixed, x2d, w_out, i, norm_mlp[layer], mlp_w_up, mlp_w_down, layer)
    return x2d.reshape(b, s, d)
```

```python
import functools

import jax
import jax.numpy as jnp
import numpy as np
from jax import lax
from jax.experimental import pallas as pl
from jax.experimental.pallas import tpu as pltpu

F32 = jnp.float32
BF16 = jnp.bfloat16

D_MODEL = 1024
D_FF = 4 * D_MODEL
HEAD_DIM = 64
EPS = 1e-6
RET_HEADS = 4
RET_QK_DIM = 64
RET_V_DIM = 128
RET_CHUNK = 128
RET_THETA = 10000.0
DIL_HEADS = 8
DIL_PATTERNS = ((128, 1), (512, 4), (2048, 16))
SWA_Q_HEADS = 16
SWA_KV_HEADS = 4
SWA_WINDOW = 128
ROPE_THETA = 500000.0
ROPE_DIMS = HEAD_DIM // 4

LANES = 128
SUBLANES = 8
QBLK = 128
VMEM_LIMIT_BYTES = 56 * 1024 * 1024
NEG_BIG = -1e30
LOG2_E = 1.4426950408889634
Q_SCALE = HEAD_DIM ** -0.5 * LOG2_E

ROW_TILE = 1024
MLP_ROW_TILE = 512
FF_CHUNK = 1024
PRO_ROWS = 128
BLOCK_UNROLL = 32
DEN_ROWS = 16
COARSE_R, FINE_R = DIL_PATTERNS[1][1], DIL_PATTERNS[2][1]


def _cparams(n_axes):
    return pltpu.CompilerParams(
        dimension_semantics=("arbitrary",) * n_axes,
        vmem_limit_bytes=VMEM_LIMIT_BYTES,
    )


def _resident(shape):
    nd = len(shape)
    return pl.BlockSpec(shape, lambda *_: (0,) * nd, pipeline_mode=pl.Buffered(1))


def _resident_layer(stack_shape, layer):
    return pl.BlockSpec((None,) + tuple(stack_shape[1:]), lambda *_: (layer, 0, 0),
                        pipeline_mode=pl.Buffered(1))


def _split3(x):
    hi = x.astype(BF16)
    r1 = x - hi.astype(F32)
    mid = r1.astype(BF16)
    lo = (r1 - mid.astype(F32)).astype(BF16)
    return hi, mid, lo


def _trig_kernel(p_ref, spread_ref, inv_ref, ec_ref, es_ref, base_ref, c_ref, s_ref, *, n_pos):
    dense = sum(jnp.dot(part, spread_ref[...], preferred_element_type=F32)
                for part in _split3(p_ref[...]))
    ang = dense * inv_ref[...]
    cos_parts = _split3(jnp.cos(ang))
    sin_parts = _split3(jnp.sin(ang))
    rows = p_ref.shape[0]
    for c in range(n_pos):
        dst = pl.ds(c, rows, stride=n_pos)
        c_ref[dst, :] = base_ref[...] + sum(
            jnp.dot(part, ec_ref[c], preferred_element_type=F32) for part in cos_parts)
        s_ref[dst, :] = sum(
            jnp.dot(part, es_ref[c], preferred_element_type=F32) for part in sin_parts)


def _selectors(half):
    n_pos = LANES // half
    ec = np.zeros((n_pos, LANES, LANES), np.float32)
    es = np.zeros((n_pos, LANES, LANES), np.float32)
    base = np.zeros((1, LANES), np.float32)
    for j in range(LANES):
        d = j % HEAD_DIM
        if d >= 2 * half:
            base[0, j] = 1.0
            continue
        f = d % half
        for c in range(n_pos):
            ec[c, c * half + f, j] = 1.0
            es[c, c * half + f, j] = -1.0 if d < half else 1.0
    return n_pos, jnp.asarray(ec, BF16), jnp.asarray(es, BF16), jnp.asarray(base)


def _rope_table(positions, half, theta, n_rot):
    b, s = positions.shape
    posf = positions.astype(F32)
    inv = jnp.power(jnp.float32(theta), -jnp.arange(half, dtype=F32) * (2.0 / n_rot))
    n_pos, ec, es, base = _selectors(half)
    rows = s // n_pos
    p = posf.reshape(b, rows, n_pos)
    spread = jnp.asarray(np.repeat(np.eye(n_pos, dtype=np.float32), half, axis=1), BF16)
    inv_row = jnp.tile(inv, n_pos)[None, :]
    tile = min(rows, 2048 // n_pos)
    assert rows % tile == 0
    dense = pl.BlockSpec((None, tile, n_pos), lambda bi, i: (bi, i, 0))
    wide = pl.BlockSpec((None, tile * n_pos, LANES), lambda bi, i: (bi, i, 0))
    c_tab, s_tab = pl.pallas_call(
        functools.partial(_trig_kernel, n_pos=n_pos),
        grid=(b, rows // tile),
        in_specs=[dense, _resident(spread.shape), _resident(inv_row.shape), _resident(ec.shape),
                  _resident(es.shape), _resident(base.shape)],
        out_specs=[wide, wide],
        out_shape=[jax.ShapeDtypeStruct((b, s, LANES), F32)] * 2,
        compiler_params=_cparams(2),
        name="trig_tables",
    )(p, spread, inv_row, ec, es, base)
    return c_tab, s_tab


def _swap_matrix(half):
    src = lax.broadcasted_iota(jnp.int32, (LANES, LANES), 0)
    dst = lax.broadcasted_iota(jnp.int32, (LANES, LANES), 1)
    d = dst & (HEAD_DIM - 1)
    want = jnp.where(d < half, dst + half, jnp.where(d < 2 * half, dst - half, -1))
    return jnp.where(src == want, 1.0, 0.0).astype(BF16)


def _rotate(x, c, s, swap):
    swapped = jnp.dot(x.astype(BF16), swap, preferred_element_type=F32)
    return x * c + swapped * s


def _lo_mask(shape):
    return lax.broadcasted_iota(jnp.int32, shape, len(shape) - 1) < HEAD_DIM


def _head_ones():
    r = lax.broadcasted_iota(jnp.int32, (LANES, LANES), 0) < HEAD_DIM
    c = lax.broadcasted_iota(jnp.int32, (LANES, LANES), 1) < HEAD_DIM
    return jnp.where(r == c, 1.0, 0.0).astype(BF16)


def _head_rinv(x, head_ones):
    ss = jnp.dot((x * x).astype(BF16), head_ones, preferred_element_type=F32)
    return lax.rsqrt(ss * (1.0 / HEAD_DIM) + EPS)


def _norm_proj_kernel(x_ref, g_ref, w_ref, b_ref, o_ref, *, n_chunk):
    x = x_ref[...]
    ms = jnp.mean(x * x, axis=-1, keepdims=True)
    h = (x * lax.rsqrt(ms + EPS) * g_ref[...]).astype(BF16)
    n = o_ref.shape[-1]
    for c in range(0, n, n_chunk):
        acc = jnp.dot(h, w_ref[:, c:c + n_chunk].astype(BF16), preferred_element_type=F32)
        o_ref[:, c:c + n_chunk] = (acc + b_ref[:, c:c + n_chunk]).astype(o_ref.dtype)


def _norm_proj(x2d, gain, w_stack, layer, bias):
    m, d = x2d.shape
    n = w_stack.shape[2]
    return pl.pallas_call(
        functools.partial(_norm_proj_kernel, n_chunk=512),
        grid=(m // ROW_TILE,),
        in_specs=[
            pl.BlockSpec((ROW_TILE, d), lambda i: (i, 0)),
            _resident((1, d)),
            _resident_layer(w_stack.shape, layer),
            _resident((1, n)),
        ],
        out_specs=pl.BlockSpec((ROW_TILE, n), lambda i: (i, 0)),
        out_shape=jax.ShapeDtypeStruct((m, n), BF16),
        compiler_params=_cparams(1),
        name="norm_proj",
    )(x2d, gain.reshape(1, d), w_stack, bias.reshape(1, n))


def _out_mlp_kernel(*refs, n_mix):
    a_refs = refs[:n_mix]
    x_ref, wo_ref, g_ref, wup_ref, wdn_ref, o_ref = refs[n_mix:]
    mixed = a_refs[0][...] if n_mix == 1 else jnp.concatenate([a[...] for a in a_refs], axis=-1)
    x1 = x_ref[...] + jnp.dot(mixed, wo_ref[...].astype(BF16), preferred_element_type=F32)
    ms = jnp.mean(x1 * x1, axis=-1, keepdims=True)
    h = (x1 * lax.rsqrt(ms + EPS) * g_ref[...]).astype(BF16)
    y = x1
    for c in range(0, D_FF, FF_CHUNK):
        u = jnp.dot(h, wup_ref[:, c:c + FF_CHUNK].astype(BF16), preferred_element_type=F32)
        u = jnp.square(jnp.maximum(u, 0.0)).astype(BF16)
        y = y + jnp.dot(u, wdn_ref[c:c + FF_CHUNK, :].astype(BF16), preferred_element_type=F32)
    o_ref[...] = y


def _out_mlp(mixed, x2d, w_out, out_layer, gain, w_up, w_down, mlp_layer):
    m, d = x2d.shape
    n_mix = len(mixed)
    in_specs = [pl.BlockSpec((MLP_ROW_TILE, a.shape[1]), lambda i: (i, 0)) for a in mixed]
    in_specs.append(pl.BlockSpec((MLP_ROW_TILE, d), lambda i: (i, 0)))
    in_specs += [_resident_layer(w_out.shape, out_layer), _resident((1, d)),
                 _resident_layer(w_up.shape, mlp_layer), _resident_layer(w_down.shape, mlp_layer)]
    return pl.pallas_call(
        functools.partial(_out_mlp_kernel, n_mix=n_mix),
        grid=(m // MLP_ROW_TILE,),
        in_specs=in_specs,
        out_specs=pl.BlockSpec((MLP_ROW_TILE, d), lambda i: (i, 0)),
        out_shape=jax.ShapeDtypeStruct((m, d), F32),
        compiler_params=_cparams(1),
        name="out_mlp",
    )(*mixed, x2d, w_out, gain.reshape(1, d), w_up, w_down)


def _retention_kernel(q_ref, k_ref, v_ref, g_ref, c_ref, s_ref, lg_ref, gn_ref,
                      o_ref, q_s, k_s, vt_s, o_s, kv_s, *, seq):
    n_chunks = seq // RET_CHUNK
    swap = _swap_matrix(RET_QK_DIM // 2)
    cs = RET_CHUNK

    def prologue(n, carry):
        rows = pl.ds(pl.multiple_of(n * cs, cs), cs)
        c, sn = c_ref[0, rows, :], s_ref[0, rows, :]
        q = _rotate(q_ref[0, rows, :].astype(F32), c, sn, swap)
        k = _rotate(k_ref[0, rows, :].astype(F32), c, sn, swap) * (RET_QK_DIM ** -0.5)
        q_s[n] = q.astype(BF16)
        k_s[n] = k.astype(BF16)
        for hh in range(2):
            cols = slice(hh * RET_V_DIM, (hh + 1) * RET_V_DIM)
            vt_s[n, hh] = v_ref[0, rows, cols].astype(F32).T.astype(BF16)
        return carry

    lax.fori_loop(0, n_chunks, prologue, 0, unroll=32)

    lo = _lo_mask((cs, LANES))
    key = lax.broadcasted_iota(jnp.int32, (cs, cs), 0).astype(F32)
    tok = lax.broadcasted_iota(jnp.int32, (cs, cs), 1).astype(F32)
    ahead = tok - key
    consts = []
    for hh in range(2):
        lg = lg_ref[0, hh:hh + 1, :]
        decay_t = jnp.where(ahead >= 0, jnp.exp(lg * jnp.maximum(ahead, 0.0)), 0.0)
        xi = jnp.exp(lg * (tok[0:1, :] + 1.0))
        zeta = jnp.exp(lg * (cs - 1.0 - tok[0:1, :]))
        cd = jnp.exp(lg * float(cs))
        head = lo if hh == 0 else jnp.logical_not(lo)
        gain = jnp.broadcast_to(gn_ref[0, hh:hh + 1, :], (cs, RET_V_DIM)).T
        consts.append((decay_t, xi, zeta, cd, head, gain))

    def inner(n, carry):
        qc, kc = q_s[n], k_s[n]
        for hh in range(2):
            decay_t, xi, zeta, cd, head, gain = consts[hh]
            qm = jnp.where(head, qc, jnp.zeros_like(qc))
            vt = vt_s[n, hh]
            sc_t = lax.dot_general(kc, qm, (((1,), (1,)), ((), ())),
                                   preferred_element_type=F32) * decay_t
            o_s[n, hh] = jnp.dot(vt, sc_t.astype(BF16), preferred_element_type=F32)
            vz = (vt.astype(F32) * zeta).astype(BF16)
            kv_s[n, hh] = jnp.dot(vz, kc, preferred_element_type=F32)
        return carry

    lax.fori_loop(0, n_chunks, inner, 0, unroll=32)

    def cross(n, state):
        qc = q_s[n]
        new_state = []
        for hh in range(2):
            decay_t, xi, zeta, cd, head, gain = consts[hh]
            r_prev = state[hh]
            qm = jnp.where(head, qc, jnp.zeros_like(qc))
            o_s[n, hh] += lax.dot_general(r_prev.astype(BF16), qm, (((1,), (1,)), ((), ())),
                                          preferred_element_type=F32) * xi
            new_state.append(r_prev * cd + kv_s[n, hh])
        return tuple(new_state)

    zero = jnp.zeros((RET_V_DIM, LANES), F32)
    lax.fori_loop(0, n_chunks, cross, (zero, zero), unroll=32)

    def finish(n, carry):
        rows = pl.ds(pl.multiple_of(n * cs, cs), cs)
        for hh in range(2):
            gain = consts[hh][5]
            cols = slice(hh * RET_V_DIM, (hh + 1) * RET_V_DIM)
            o = o_s[n, hh]
            mu = jnp.mean(o, axis=0, keepdims=True)
            dev = o - mu
            var = jnp.mean(dev * dev, axis=0, keepdims=True)
            y = (dev * lax.rsqrt(var + EPS) * gain).T
            gate = g_ref[0, rows, cols].astype(F32)
            o_ref[0, rows, cols] = (gate * jax.nn.sigmoid(gate) * y).astype(o_ref.dtype)
        return carry

    lax.fori_loop(0, n_chunks, finish, 0, unroll=32)


def _retention(proj, tables, log_gamma, gn_gain):
    b, s, _ = proj.shape
    n_pairs = RET_HEADS // 2
    pair_w = 2 * RET_V_DIM
    qk_tiles = RET_HEADS * RET_QK_DIM // LANES
    v_off = 2 * qk_tiles * LANES // pair_w
    g_off = v_off + RET_HEADS * RET_V_DIM // pair_w
    tab = pl.BlockSpec((1, s, LANES), lambda bi, p: (bi, 0, 0))
    lg = jnp.broadcast_to(log_gamma.reshape(n_pairs, 2, 1), (n_pairs, 2, LANES))
    n_chunks = s // RET_CHUNK
    return pl.pallas_call(
        functools.partial(_retention_kernel, seq=s),
        grid=(b, n_pairs),
        in_specs=[
            pl.BlockSpec((1, s, LANES), lambda bi, p: (bi, 0, p)),
            pl.BlockSpec((1, s, LANES), lambda bi, p: (bi, 0, qk_tiles + p)),
            pl.BlockSpec((1, s, pair_w), lambda bi, p: (bi, 0, v_off + p)),
            pl.BlockSpec((1, s, pair_w), lambda bi, p: (bi, 0, g_off + p)),
            tab, tab,
            pl.BlockSpec((1, 2, LANES), lambda bi, p: (p, 0, 0)),
            pl.BlockSpec((1, 2, RET_V_DIM), lambda bi, p: (p, 0, 0)),
        ],
        out_specs=pl.BlockSpec((1, s, pair_w), lambda bi, p: (bi, 0, p)),
        out_shape=jax.ShapeDtypeStruct((b, s, RET_HEADS * RET_V_DIM), BF16),
        scratch_shapes=[
            pltpu.VMEM((n_chunks, RET_CHUNK, LANES), BF16),
            pltpu.VMEM((n_chunks, RET_CHUNK, LANES), BF16),
            pltpu.VMEM((n_chunks, 2, RET_V_DIM, RET_CHUNK), BF16),
            pltpu.VMEM((n_chunks, 2, RET_V_DIM, RET_CHUNK), F32),
            pltpu.VMEM((n_chunks, 2, RET_V_DIM, LANES), F32),
        ],
        compiler_params=_cparams(2),
        name="retention",
    )(proj, proj, proj, proj, *tables, lg, gn_gain.reshape(n_pairs, 2, RET_V_DIM))


def _band_bias_t(lo_off, hi_off):
    c = lax.broadcasted_iota(jnp.int32, (2 * QBLK, QBLK), 0)
    a = lax.broadcasted_iota(jnp.int32, (2 * QBLK, QBLK), 1)
    band = (c - a >= lo_off) & (c - a <= hi_off)
    later = jnp.where(band, 0.0, NEG_BIG).astype(F32)
    first = jnp.where(band & (c >= QBLK), 0.0, NEG_BIG).astype(F32)
    return first, later


def _dilated_kernel(q_ref, k_ref, v_ref, c_ref, s_ref, qg_ref, kg_ref, o_ref,
                    qn_s, kn_s, vn_s, q4_s, k4_s, v4_s, qt_s, kc_s, vt_s, bias_s, p_s, m_s,
                    o0_s, o1_s, o2_s, l0_s, l1_s, l2_s, *, seq):
    swap = _swap_matrix(ROPE_DIMS // 2)
    head_ones = _head_ones()
    rq_s, rk_s = o0_s, l0_s

    def norms(n, carry):
        rows = pl.ds(pl.multiple_of(n * PRO_ROWS, PRO_ROWS), PRO_ROWS)
        rq_s[rows, :] = _head_rinv(q_ref[0, rows, :].astype(F32), head_ones)
        rk_s[rows, :] = _head_rinv(k_ref[0, rows, :].astype(F32), head_ones)
        return carry

    lax.fori_loop(0, seq // PRO_ROWS, norms, 0, unroll=32)

    def prologue(n, carry):
        rows = pl.ds(pl.multiple_of(n * PRO_ROWS, PRO_ROWS), PRO_ROWS)
        c, sn = c_ref[0, rows, :], s_ref[0, rows, :]
        q = q_ref[0, rows, :].astype(F32) * rq_s[rows, :] * qg_ref[...]
        qn_s[rows, :] = _rotate(q, c, sn, swap) * Q_SCALE
        k = k_ref[0, rows, :].astype(F32) * rk_s[rows, :] * kg_ref[...]
        kn_s[rows, :] = _rotate(k, c, sn, swap)
        vn_s[rows, :] = v_ref[0, rows, :].astype(F32)
        return carry

    lax.fori_loop(0, seq // PRO_ROWS, prologue, 0, unroll=32)

    first, later = _band_bias_t(0, QBLK)
    bias_s[0] = first.astype(BF16)
    bias_s[1] = later.astype(BF16)
    eye = (lax.broadcasted_iota(jnp.int32, (QBLK, QBLK), 0)
           == lax.broadcasted_iota(jnp.int32, (QBLK, QBLK), 1))
    eye = jnp.where(eye, 1.0, 0.0).astype(BF16)
    eye_pair = jnp.concatenate([eye, eye], axis=1)
    zero_pad = jnp.zeros((QBLK, LANES), BF16)
    top = lax.broadcasted_iota(jnp.int32, (LANES, QBLK), 0) < HEAD_DIM
    ones_rows = jnp.ones((DEN_ROWS, 2 * QBLK), BF16)

    outs = (o0_s, o1_s, o2_s)
    lses = (l0_s, l1_s, l2_s)
    n_flat = seq // QBLK
    for (window, r), on_s, ln_s in zip(DIL_PATTERNS, outs, lses):
        assert window // r == QBLK
        n_blk = seq // r // QBLK
        blk_shift = n_blk.bit_length() - 1
        assert n_blk == 1 << blk_shift

        def pad(j, c2, n_blk=n_blk):
            z = j * (n_blk + 1)
            kc_s[pl.ds(pl.multiple_of(z * QBLK, QBLK), QBLK), :] = zero_pad
            vt_s[z] = zero_pad
            return c2

        lax.fori_loop(0, r, pad, 0, unroll=r)

        def split(idx, r=r, n_blk=n_blk, blk_shift=blk_shift):
            j = lax.shift_right_logical(idx, blk_shift)
            i = idx & (n_blk - 1)
            return j, i

        def gather(idx, c2, r=r, split=split):
            j, i = split(idx)
            if r == FINE_R:
                base = ((j & (COARSE_R - 1)) * (seq // COARSE_R)
                        + lax.shift_right_logical(j, COARSE_R.bit_length() - 1))
                src = pl.ds(base + (QBLK * r // COARSE_R) * i, QBLK, stride=r // COARSE_R)
                q, k, v = q4_s[src, :], k4_s[src, :], v4_s[src, :]
            else:
                src = pl.ds(j + r * QBLK * i, QBLK, stride=r)
                q, k, v = qn_s[src, :], kn_s[src, :], vn_s[src, :]
            if r == COARSE_R:
                dense = pl.ds(pl.multiple_of(idx * QBLK, QBLK), QBLK)
                q4_s[dense, :] = q
                k4_s[dense, :] = k
                v4_s[dense, :] = v
            k0 = pl.multiple_of((idx + j + 1) * QBLK, QBLK)
            qt_s[idx] = q.T.astype(BF16)
            kc_s[pl.ds(k0, QBLK), :] = k.astype(BF16)
            vt_s[idx + j + 1] = v.T.astype(BF16)
            return c2

        lax.fori_loop(0, n_flat, gather, 0, unroll=32)

        def scores(idx, c2, split=split):
            j, i = split(idx)
            k0 = pl.multiple_of((idx + j) * QBLK, QBLK)
            qt = qt_s[idx]
            kw = kc_s[pl.ds(k0, 2 * QBLK), :]
            zero = jnp.zeros_like(qt)
            rhs = jnp.concatenate([jnp.where(top, qt, zero), jnp.where(top, zero, qt)], axis=1)
            lhs = jnp.concatenate([kw, bias_s[jnp.minimum(i, 1)]], axis=1)
            sc = jnp.dot(lhs, jnp.concatenate([rhs, eye_pair], axis=0), preferred_element_type=F32)
            m = jnp.max(sc, axis=0, keepdims=True)
            p_s[idx] = jnp.exp2(sc - m).astype(BF16)
            m_s[idx] = jnp.broadcast_to(m, (SUBLANES, 2 * QBLK))
            return c2

        lax.fori_loop(0, n_flat, scores, 0, unroll=32)

        def block(idx, c2, r=r, split=split, on_s=on_s, ln_s=ln_s):
            j, i = split(idx)
            p = p_s[idx]
            m = m_s[idx][0:1, :]
            lhs = jnp.concatenate(
                [jnp.concatenate([vt_s[idx + j], vt_s[idx + j + 1]], axis=1), ones_rows], axis=0)
            ot = jnp.dot(lhs, p, preferred_element_type=F32)
            den = ot[LANES:LANES + 1, :]
            inv = 1.0 / den
            lse = m + jnp.log2(den)
            o_t = jnp.concatenate(
                [ot[0:HEAD_DIM, 0:QBLK] * inv[:, 0:QBLK],
                 ot[HEAD_DIM:LANES, QBLK:2 * QBLK] * inv[:, QBLK:2 * QBLK]], axis=0)
            l_t = jnp.concatenate(
                [jnp.broadcast_to(lse[:, 0:QBLK], (HEAD_DIM, QBLK)),
                 jnp.broadcast_to(lse[:, QBLK:2 * QBLK], (HEAD_DIM, QBLK))], axis=0)
            dst = pl.ds(j + r * QBLK * i, QBLK, stride=r)
            on_s[dst, :] = o_t.T
            ln_s[dst, :] = l_t.T
            return c2

        lax.fori_loop(0, n_flat, block, 0, unroll=32)

    def combine(n, carry):
        r0 = pl.multiple_of(n * PRO_ROWS, PRO_ROWS)
        rows = pl.ds(r0, PRO_ROWS)
        ls = [l_s[rows, :] for l_s in lses]
        m = jnp.maximum(jnp.maximum(ls[0], ls[1]), ls[2])
        es = [jnp.exp2(l - m) for l in ls]
        num = es[0] * o0_s[rows, :] + es[1] * o1_s[rows, :] + es[2] * o2_s[rows, :]
        o_ref[0, rows, :] = (num / (es[0] + es[1] + es[2])).astype(o_ref.dtype)
        return carry

    lax.fori_loop(0, seq // PRO_ROWS, combine, 0, unroll=32)


def _dilated(proj, tables, q_gain, k_gain, col0):
    b, s, _ = proj.shape
    n_pairs = DIL_HEADS * HEAD_DIM // LANES
    t0 = col0 // LANES
    tab = pl.BlockSpec((1, s, LANES), lambda bi, p: (bi, 0, 0))
    two = lambda g: jnp.concatenate([g, g]).reshape(1, LANES)
    nat = pltpu.VMEM((s, LANES), F32)
    max_r = max(r for _, r in DIL_PATTERNS)
    return pl.pallas_call(
        functools.partial(_dilated_kernel, seq=s),
        grid=(b, n_pairs),
        in_specs=[
            pl.BlockSpec((1, s, LANES), lambda bi, p: (bi, 0, t0 + p)),
            pl.BlockSpec((1, s, LANES), lambda bi, p: (bi, 0, t0 + n_pairs + p)),
            pl.BlockSpec((1, s, LANES), lambda bi, p: (bi, 0, t0 + 2 * n_pairs + p)),
            tab, tab,
            _resident((1, LANES)),
            _resident((1, LANES)),
        ],
        out_specs=pl.BlockSpec((1, s, LANES), lambda bi, p: (bi, 0, p)),
        out_shape=jax.ShapeDtypeStruct((b, s, DIL_HEADS * HEAD_DIM), BF16),
        scratch_shapes=[
            nat, nat, nat, nat, nat, nat,
            pltpu.VMEM((s // QBLK, LANES, QBLK), BF16),
            pltpu.VMEM((s + max_r * QBLK, LANES), BF16),
            pltpu.VMEM((s // QBLK + max_r, LANES, QBLK), BF16),
            pltpu.VMEM((2, 2 * QBLK, QBLK), BF16),
            pltpu.VMEM((s // QBLK, 2 * QBLK, 2 * QBLK), BF16),
            pltpu.VMEM((s // QBLK, SUBLANES, 2 * QBLK), F32),
            nat, nat, nat, nat, nat, nat,
        ],
        compiler_params=_cparams(2),
        name="dilated_attention",
    )(proj, proj, proj, *tables, two(q_gain), two(k_gain))


def _swa_kernel(q_ref, k_ref, v_ref, c_ref, s_ref, qg_ref, kg_ref, sink_ref, o_ref,
                qt_s, kd_s, vt_s, bias_s, p_s, m_s, rq_s, rk_s, *, seq):
    group = SWA_Q_HEADS // SWA_KV_HEADS
    swap = _swap_matrix(ROPE_DIMS // 2)
    head_ones = _head_ones()
    kv_in_hi = (pl.program_id(1) % 2) == 1
    blk_per_step = PRO_ROWS // QBLK
    src = lax.broadcasted_iota(jnp.int32, (LANES, LANES), 0)
    dst = lax.broadcasted_iota(jnp.int32, (LANES, LANES), 1) & (HEAD_DIM - 1)
    dup = jnp.where(src == dst + jnp.where(kv_in_hi, HEAD_DIM, 0), 1.0, 0.0).astype(BF16)

    def norms(n, carry):
        rows = pl.ds(pl.multiple_of(n * PRO_ROWS, PRO_ROWS), PRO_ROWS)
        for t in range(group // 2):
            cols = slice(t * LANES, (t + 1) * LANES)
            rq_s[rows, cols] = _head_rinv(q_ref[0, rows, cols].astype(F32), head_ones)
        rk_s[rows, :] = _head_rinv(k_ref[0, rows, :].astype(F32), head_ones)
        return carry

    lax.fori_loop(0, seq // PRO_ROWS, norms, 0, unroll=32)

    def prologue(n, carry):
        r0 = pl.multiple_of(n * PRO_ROWS, PRO_ROWS)
        rows = pl.ds(r0, PRO_ROWS)
        c, sn = c_ref[0, rows, :], s_ref[0, rows, :]
        for t in range(group // 2):
            cols = slice(t * LANES, (t + 1) * LANES)
            q = q_ref[0, rows, cols].astype(F32) * rq_s[rows, cols] * qg_ref[...]
            q_t = (_rotate(q, c, sn, swap) * Q_SCALE).T
            for u in range(blk_per_step):
                qt_s[n * blk_per_step + u, t] = q_t[:, u * QBLK:(u + 1) * QBLK].astype(BF16)
        k = k_ref[0, rows, :].astype(F32) * rk_s[rows, :] * kg_ref[...]
        k = _rotate(k, c, sn, swap).astype(BF16)
        kd_s[pl.ds(r0 + QBLK, PRO_ROWS), :] = jnp.dot(k, dup, preferred_element_type=F32).astype(BF16)
        v_t = v_ref[0, rows, :].astype(F32).T
        v_t = jnp.where(kv_in_hi, v_t[HEAD_DIM:LANES, :], v_t[0:HEAD_DIM, :]).astype(BF16)
        for u in range(blk_per_step):
            vt_s[n * blk_per_step + u + 1] = v_t[:, u * QBLK:(u + 1) * QBLK]
        return carry

    lax.fori_loop(0, seq // PRO_ROWS, prologue, 0, unroll=32)

    first, later = _band_bias_t(1, QBLK)
    bias_s[0] = first.astype(BF16)
    bias_s[1] = later.astype(BF16)
    eye = (lax.broadcasted_iota(jnp.int32, (QBLK, QBLK), 0)
           == lax.broadcasted_iota(jnp.int32, (QBLK, QBLK), 1))
    eye = jnp.where(eye, 1.0, 0.0).astype(BF16)
    eye_pair = jnp.concatenate([eye, eye], axis=1)
    top = lax.broadcasted_iota(jnp.int32, (LANES, QBLK), 0) < HEAD_DIM
    kd_s[0:QBLK, :] = jnp.zeros((QBLK, LANES), BF16)
    vt_s[0] = jnp.zeros((HEAD_DIM, QBLK), BF16)
    sink = jnp.concatenate([sink_ref[0, hh:hh + 1, :] for hh in range(group)], axis=1) * LOG2_E
    ones_rows = jnp.ones((DEN_ROWS, 2 * QBLK), BF16)

    def scores(i, carry):
        d0 = pl.multiple_of(i * QBLK, QBLK)
        rows = pl.ds(d0, QBLK)
        kw = kd_s[pl.ds(d0, 2 * QBLK), :]
        lhs = jnp.concatenate([kw, bias_s[jnp.minimum(i, 1)]], axis=1)
        for t in range(group // 2):
            qt = qt_s[i, t]
            zero = jnp.zeros_like(qt)
            rhs = jnp.concatenate([jnp.where(top, qt, zero), jnp.where(top, zero, qt)], axis=1)
            cols = slice(2 * t * QBLK, 2 * (t + 1) * QBLK)
            sc = jnp.dot(lhs, jnp.concatenate([rhs, eye_pair], axis=0), preferred_element_type=F32)
            m = jnp.maximum(jnp.max(sc, axis=0, keepdims=True), sink[:, cols])
            p_s[i, :, cols] = jnp.exp2(sc - m).astype(BF16)
            m_s[i, :, cols] = jnp.broadcast_to(m, (SUBLANES, 2 * QBLK))
        return carry

    lax.fori_loop(0, seq // QBLK, scores, 0, unroll=BLOCK_UNROLL)

    def values(i, carry):
        rows = pl.ds(pl.multiple_of(i * QBLK, QBLK), QBLK)
        m = m_s[i][0:1, :]
        lhs = jnp.concatenate(
            [jnp.concatenate([vt_s[i], vt_s[i + 1]], axis=1), ones_rows], axis=0)
        ot = jnp.dot(lhs, p_s[i], preferred_element_type=F32)
        inv = 1.0 / (ot[HEAD_DIM:HEAD_DIM + 1, :] + jnp.exp2(sink - m))
        on = ot[0:HEAD_DIM, :] * inv
        for t in range(group // 2):
            pair_t = jnp.concatenate(
                [on[:, 2 * t * QBLK:(2 * t + 1) * QBLK], on[:, (2 * t + 1) * QBLK:(2 * t + 2) * QBLK]],
                axis=0)
            o_ref[0, rows, t * LANES:(t + 1) * LANES] = pair_t.T.astype(o_ref.dtype)
        return carry

    lax.fori_loop(0, seq // QBLK, values, 0, unroll=32)


def _swa(proj, tables, q_gain, k_gain, sinks):
    b, s, _ = proj.shape
    group = SWA_Q_HEADS // SWA_KV_HEADS
    q_w = group * HEAD_DIM
    k_t0 = SWA_Q_HEADS * HEAD_DIM // LANES
    v_t0 = k_t0 + SWA_KV_HEADS * HEAD_DIM // LANES
    tab = pl.BlockSpec((1, s, LANES), lambda bi, g: (bi, 0, 0))
    two = lambda g: jnp.concatenate([g, g]).reshape(1, LANES)
    sink_rows = jnp.broadcast_to(sinks.reshape(SWA_KV_HEADS, group, 1), (SWA_KV_HEADS, group, LANES))
    return pl.pallas_call(
        functools.partial(_swa_kernel, seq=s),
        grid=(b, SWA_KV_HEADS),
        in_specs=[
            pl.BlockSpec((1, s, q_w), lambda bi, g: (bi, 0, g)),
            pl.BlockSpec((1, s, LANES), lambda bi, g: (bi, 0, k_t0 + g // 2)),
            pl.BlockSpec((1, s, LANES), lambda bi, g: (bi, 0, v_t0 + g // 2)),
            tab, tab,
            _resident((1, LANES)),
            _resident((1, LANES)),
            pl.BlockSpec((1, group, LANES), lambda bi, g: (g, 0, 0)),
        ],
        out_specs=pl.BlockSpec((1, s, q_w), lambda bi, g: (bi, 0, g)),
        out_shape=jax.ShapeDtypeStruct((b, s, SWA_Q_HEADS * HEAD_DIM), BF16),
        scratch_shapes=[
            pltpu.VMEM((s // QBLK, group // 2, LANES, QBLK), BF16),
            pltpu.VMEM((s + QBLK, LANES), BF16),
            pltpu.VMEM((s // QBLK + 1, HEAD_DIM, QBLK), BF16),
            pltpu.VMEM((2, 2 * QBLK, QBLK), BF16),
            pltpu.VMEM((s // QBLK, 2 * QBLK, group * QBLK), BF16),
            pltpu.VMEM((s // QBLK, SUBLANES, group * QBLK), F32),
            pltpu.VMEM((s, q_w), F32),
            pltpu.VMEM((s, LANES), F32),
        ],
        compiler_params=_cparams(2),
        name="swa_attention",
    )(proj, proj, proj, *tables, two(q_gain), two(k_gain), sink_rows)


def kernel(x, positions, norm_mix, norm_mlp, mlp_w_up, mlp_w_down, hyb_w_in, hyb_w_out, ret_gn_gain, dil_q_gain, dil_k_gain, swa_w_qkv, swa_b_qkv, swa_w_out, swa_q_gain, swa_k_gain, swa_sinks):
    b, s, d = x.shape
    depth = norm_mix.shape[0]
    ret_tab = _rope_table(positions, RET_QK_DIM // 2, RET_THETA, RET_QK_DIM)
    rope_tab = _rope_table(positions, ROPE_DIMS // 2, ROPE_THETA, ROPE_DIMS)
    log_gamma = jnp.log1p(-jnp.exp2(-5.0 - jnp.arange(RET_HEADS, dtype=F32)))
    ret_w = RET_HEADS * RET_V_DIM
    dil_col0 = 2 * RET_HEADS * RET_QK_DIM + 2 * ret_w

    x2d = x.reshape(b * s, d)
    for layer in range(depth):
        i = layer // 2
        if layer % 2 == 0:
            zero_bias = jnp.zeros((hyb_w_in.shape[2],), F32)
            proj = _norm_proj(x2d, norm_mix[layer], hyb_w_in, i, zero_bias).reshape(b, s, -1)
            ra = _retention(proj, ret_tab, log_gamma, ret_gn_gain[i])
            da = _dilated(proj, rope_tab, dil_q_gain[i], dil_k_gain[i], dil_col0)
            mixed = [ra.reshape(b * s, -1), da.reshape(b * s, -1)]
            w_out = hyb_w_out
        else:
            proj = _norm_proj(x2d, norm_mix[layer], swa_w_qkv, i, swa_b_qkv[i]).reshape(b, s, -1)
            att = _swa(proj, rope_tab, swa_q_gain[i], swa_k_gain[i], swa_sinks[i])
            mixed = [att.reshape(b * s, -1)]
            w_out = swa_w_out
        x2d = _out_mlp(mixed, x2d, w_out, i, norm_mlp[layer], mlp_w_up, mlp_w_down, layer)
    return x2d.reshape(b, s, d)
```

```python
import functools

import jax
import jax.numpy as jnp
import numpy as np
from jax import lax
from jax.experimental import pallas as pl
from jax.experimental.pallas import tpu as pltpu

F32 = jnp.float32
BF16 = jnp.bfloat16

D_MODEL = 1024
D_FF = 4 * D_MODEL
HEAD_DIM = 64
EPS = 1e-6
RET_HEADS = 4
RET_QK_DIM = 64
RET_V_DIM = 128
RET_CHUNK = 128
RET_THETA = 10000.0
DIL_HEADS = 8
DIL_PATTERNS = ((128, 1), (512, 4), (2048, 16))
SWA_Q_HEADS = 16
SWA_KV_HEADS = 4
SWA_WINDOW = 128
ROPE_THETA = 500000.0
ROPE_DIMS = HEAD_DIM // 4

LANES = 128
SUBLANES = 8
QBLK = 128
VMEM_LIMIT_BYTES = 56 * 1024 * 1024
NEG_BIG = -1e30
LOG2_E = 1.4426950408889634
Q_SCALE = HEAD_DIM ** -0.5 * LOG2_E

ROW_TILE = 1024
MLP_ROW_TILE = 512
FF_CHUNK = 1024
PRO_ROWS = 128
BLOCK_UNROLL = 32
DEN_ROWS = 16
COARSE_R, FINE_R = DIL_PATTERNS[1][1], DIL_PATTERNS[2][1]


def _cparams(n_axes):
    return pltpu.CompilerParams(
        dimension_semantics=("arbitrary",) * n_axes,
        vmem_limit_bytes=VMEM_LIMIT_BYTES,
    )


def _resident(shape):
    nd = len(shape)
    return pl.BlockSpec(shape, lambda *_: (0,) * nd, pipeline_mode=pl.Buffered(1))


def _resident_layer(stack_shape, layer):
    return pl.BlockSpec((None,) + tuple(stack_shape[1:]), lambda *_: (layer, 0, 0),
                        pipeline_mode=pl.Buffered(1))


def _split3(x):
    hi = x.astype(BF16)
    r1 = x - hi.astype(F32)
    mid = r1.astype(BF16)
    lo = (r1 - mid.astype(F32)).astype(BF16)
    return hi, mid, lo


def _trig_kernel(p_ref, spread_ref, inv_ref, ec_ref, es_ref, base_ref, c_ref, s_ref, *, n_pos):
    dense = sum(jnp.dot(part, spread_ref[...], preferred_element_type=F32)
                for part in _split3(p_ref[...]))
    ang = dense * inv_ref[...]
    cos_parts = _split3(jnp.cos(ang))
    sin_parts = _split3(jnp.sin(ang))
    rows = p_ref.shape[0]
    for c in range(n_pos):
        dst = pl.ds(c, rows, stride=n_pos)
        c_ref[dst, :] = base_ref[...] + sum(
            jnp.dot(part, ec_ref[c], preferred_element_type=F32) for part in cos_parts)
        s_ref[dst, :] = sum(
            jnp.dot(part, es_ref[c], preferred_element_type=F32) for part in sin_parts)


def _selectors(half):
    n_pos = LANES // half
    ec = np.zeros((n_pos, LANES, LANES), np.float32)
    es = np.zeros((n_pos, LANES, LANES), np.float32)
    base = np.zeros((1, LANES), np.float32)
    for j in range(LANES):
        d = j % HEAD_DIM
        if d >= 2 * half:
            base[0, j] = 1.0
            continue
        f = d % half
        for c in range(n_pos):
            ec[c, c * half + f, j] = 1.0
            es[c, c * half + f, j] = -1.0 if d < half else 1.0
    return n_pos, jnp.asarray(ec, BF16), jnp.asarray(es, BF16), jnp.asarray(base)


def _rope_table(positions, half, theta, n_rot):
    b, s = positions.shape
    posf = positions.astype(F32)
    inv = jnp.power(jnp.float32(theta), -jnp.arange(half, dtype=F32) * (2.0 / n_rot))
    n_pos, ec, es, base = _selectors(half)
    rows = s // n_pos
    p = posf.reshape(b, rows, n_pos)
    spread = jnp.asarray(np.repeat(np.eye(n_pos, dtype=np.float32), half, axis=1), BF16)
    inv_row = jnp.tile(inv, n_pos)[None, :]
    tile = min(rows, 2048 // n_pos)
    assert rows % tile == 0
    dense = pl.BlockSpec((None, tile, n_pos), lambda bi, i: (bi, i, 0))
    wide = pl.BlockSpec((None, tile * n_pos, LANES), lambda bi, i: (bi, i, 0))
    c_tab, s_tab = pl.pallas_call(
        functools.partial(_trig_kernel, n_pos=n_pos),
        grid=(b, rows // tile),
        in_specs=[dense, _resident(spread.shape), _resident(inv_row.shape), _resident(ec.shape),
                  _resident(es.shape), _resident(base.shape)],
        out_specs=[wide, wide],
        out_shape=[jax.ShapeDtypeStruct((b, s, LANES), F32)] * 2,
        compiler_params=_cparams(2),
        name="trig_tables",
    )(p, spread, inv_row, ec, es, base)
    return c_tab, s_tab


def _swap_matrix(half):
    src = lax.broadcasted_iota(jnp.int32, (LANES, LANES), 0)
    dst = lax.broadcasted_iota(jnp.int32, (LANES, LANES), 1)
    d = dst & (HEAD_DIM - 1)
    want = jnp.where(d < half, dst + half, jnp.where(d < 2 * half, dst - half, -1))
    return jnp.where(src == want, 1.0, 0.0).astype(BF16)


def _rotate(x, c, s, swap):
    swapped = jnp.dot(x.astype(BF16), swap, preferred_element_type=F32)
    return x * c + swapped * s


def _lo_mask(shape):
    return lax.broadcasted_iota(jnp.int32, shape, len(shape) - 1) < HEAD_DIM


def _head_ones():
    r = lax.broadcasted_iota(jnp.int32, (LANES, LANES), 0) < HEAD_DIM
    c = lax.broadcasted_iota(jnp.int32, (LANES, LANES), 1) < HEAD_DIM
    return jnp.where(r == c, 1.0, 0.0).astype(BF16)


def _head_rinv(x, head_ones):
    ss = jnp.dot((x * x).astype(BF16), head_ones, preferred_element_type=F32)
    return lax.rsqrt(ss * (1.0 / HEAD_DIM) + EPS)


def _norm_proj_kernel(x_ref, g_ref, w_ref, b_ref, o_ref, *, n_chunk):
    x = x_ref[...]
    ms = jnp.mean(x * x, axis=-1, keepdims=True)
    h = (x * lax.rsqrt(ms + EPS) * g_ref[...]).astype(BF16)
    n = o_ref.shape[-1]
    for c in range(0, n, n_chunk):
        acc = jnp.dot(h, w_ref[:, c:c + n_chunk].astype(BF16), preferred_element_type=F32)
        o_ref[:, c:c + n_chunk] = (acc + b_ref[:, c:c + n_chunk]).astype(o_ref.dtype)


def _norm_proj(x2d, gain, w_stack, layer, bias):
    m, d = x2d.shape
    n = w_stack.shape[2]
    return pl.pallas_call(
        functools.partial(_norm_proj_kernel, n_chunk=512),
        grid=(m // ROW_TILE,),
        in_specs=[
            pl.BlockSpec((ROW_TILE, d), lambda i: (i, 0)),
            _resident((1, d)),
            _resident_layer(w_stack.shape, layer),
            _resident((1, n)),
        ],
        out_specs=pl.BlockSpec((ROW_TILE, n), lambda i: (i, 0)),
        out_shape=jax.ShapeDtypeStruct((m, n), BF16),
        compiler_params=_cparams(1),
        name="norm_proj",
    )(x2d, gain.reshape(1, d), w_stack, bias.reshape(1, n))


def _out_mlp_kernel(*refs, n_mix):
    a_refs = refs[:n_mix]
    x_ref, wo_ref, g_ref, wup_ref, wdn_ref, o_ref = refs[n_mix:]
    mixed = a_refs[0][...] if n_mix == 1 else jnp.concatenate([a[...] for a in a_refs], axis=-1)
    x1 = x_ref[...] + jnp.dot(mixed, wo_ref[...].astype(BF16), preferred_element_type=F32)
    ms = jnp.mean(x1 * x1, axis=-1, keepdims=True)
    h = (x1 * lax.rsqrt(ms + EPS) * g_ref[...]).astype(BF16)
    y = x1
    for c in range(0, D_FF, FF_CHUNK):
        u = jnp.dot(h, wup_ref[:, c:c + FF_CHUNK].astype(BF16), preferred_element_type=F32)
        u = jnp.square(jnp.maximum(u, 0.0)).astype(BF16)
        y = y + jnp.dot(u, wdn_ref[c:c + FF_CHUNK, :].astype(BF16), preferred_element_type=F32)
    o_ref[...] = y


def _out_mlp(mixed, x2d, w_out, out_layer, gain, w_up, w_down, mlp_layer):
    m, d = x2d.shape
    n_mix = len(mixed)
    in_specs = [pl.BlockSpec((MLP_ROW_TILE, a.shape[1]), lambda i: (i, 0)) for a in mixed]
    in_specs.append(pl.BlockSpec((MLP_ROW_TILE, d), lambda i: (i, 0)))
    in_specs += [_resident_layer(w_out.shape, out_layer), _resident((1, d)),
                 _resident_layer(w_up.shape, mlp_layer), _resident_layer(w_down.shape, mlp_layer)]
    return pl.pallas_call(
        functools.partial(_out_mlp_kernel, n_mix=n_mix),
        grid=(m // MLP_ROW_TILE,),
        in_specs=in_specs,
        out_specs=pl.BlockSpec((MLP_ROW_TILE, d), lambda i: (i, 0)),
        out_shape=jax.ShapeDtypeStruct((m, d), F32),
        compiler_params=_cparams(1),
        name="out_mlp",
    )(*mixed, x2d, w_out, gain.reshape(1, d), w_up, w_down)


def _retention_kernel(q_ref, k_ref, v_ref, g_ref, c_ref, s_ref, lg_ref, gn_ref,
                      o_ref, q_s, k_s, vt_s, o_s, kv_s, *, seq):
    n_chunks = seq // RET_CHUNK
    swap = _swap_matrix(RET_QK_DIM // 2)
    cs = RET_CHUNK

    def prologue(n, carry):
        rows = pl.ds(pl.multiple_of(n * cs, cs), cs)
        c, sn = c_ref[0, rows, :], s_ref[0, rows, :]
        q = _rotate(q_ref[0, rows, :].astype(F32), c, sn, swap)
        k = _rotate(k_ref[0, rows, :].astype(F32), c, sn, swap) * (RET_QK_DIM ** -0.5)
        q_s[n] = q.astype(BF16)
        k_s[n] = k.astype(BF16)
        for hh in range(2):
            cols = slice(hh * RET_V_DIM, (hh + 1) * RET_V_DIM)
            vt_s[n, hh] = v_ref[0, rows, cols].astype(F32).T.astype(BF16)
        return carry

    lax.fori_loop(0, n_chunks, prologue, 0, unroll=32)

    lo = _lo_mask((cs, LANES))
    key = lax.broadcasted_iota(jnp.int32, (cs, cs), 0).astype(F32)
    tok = lax.broadcasted_iota(jnp.int32, (cs, cs), 1).astype(F32)
    ahead = tok - key
    consts = []
    for hh in range(2):
        lg = lg_ref[0, hh:hh + 1, :]
        decay_t = jnp.where(ahead >= 0, jnp.exp(lg * jnp.maximum(ahead, 0.0)), 0.0)
        xi = jnp.exp(lg * (tok[0:1, :] + 1.0))
        zeta = jnp.exp(lg * (cs - 1.0 - tok[0:1, :]))
        cd = jnp.exp(lg * float(cs))
        head = lo if hh == 0 else jnp.logical_not(lo)
        gain = jnp.broadcast_to(gn_ref[0, hh:hh + 1, :], (cs, RET_V_DIM)).T
        consts.append((decay_t, xi, zeta, cd, head, gain))

    def inner(n, carry):
        qc, kc = q_s[n], k_s[n]
        for hh in range(2):
            decay_t, xi, zeta, cd, head, gain = consts[hh]
            qm = jnp.where(head, qc, jnp.zeros_like(qc))
            vt = vt_s[n, hh]
            sc_t = lax.dot_general(kc, qm, (((1,), (1,)), ((), ())),
                                   preferred_element_type=F32) * decay_t
            o_s[n, hh] = jnp.dot(vt, sc_t.astype(BF16), preferred_element_type=F32)
            vz = (vt.astype(F32) * zeta).astype(BF16)
            kv_s[n, hh] = jnp.dot(vz, kc, preferred_element_type=F32)
        return carry

    lax.fori_loop(0, n_chunks, inner, 0, unroll=32)

    def cross(n, state):
        qc = q_s[n]
        new_state = []
        for hh in range(2):
            decay_t, xi, zeta, cd, head, gain = consts[hh]
            r_prev = state[hh]
            qm = jnp.where(head, qc, jnp.zeros_like(qc))
            o_s[n, hh] += lax.dot_general(r_prev.astype(BF16), qm, (((1,), (1,)), ((), ())),
                                          preferred_element_type=F32) * xi
            new_state.append(r_prev * cd + kv_s[n, hh])
        return tuple(new_state)

    zero = jnp.zeros((RET_V_DIM, LANES), F32)
    lax.fori_loop(0, n_chunks, cross, (zero, zero), unroll=32)

    def finish(n, carry):
        rows = pl.ds(pl.multiple_of(n * cs, cs), cs)
        for hh in range(2):
            gain = consts[hh][5]
            cols = slice(hh * RET_V_DIM, (hh + 1) * RET_V_DIM)
            o = o_s[n, hh]
            mu = jnp.mean(o, axis=0, keepdims=True)
            dev = o - mu
            var = jnp.mean(dev * dev, axis=0, keepdims=True)
            y = (dev * lax.rsqrt(var + EPS) * gain).T
            gate = g_ref[0, rows, cols].astype(F32)
            o_ref[0, rows, cols] = (gate * jax.nn.sigmoid(gate) * y).astype(o_ref.dtype)
        return carry

    lax.fori_loop(0, n_chunks, finish, 0, unroll=32)


def _retention(proj, tables, log_gamma, gn_gain):
    b, s, _ = proj.shape
    n_pairs = RET_HEADS // 2
    pair_w = 2 * RET_V_DIM
    qk_tiles = RET_HEADS * RET_QK_DIM // LANES
    v_off = 2 * qk_tiles * LANES // pair_w
    g_off = v_off + RET_HEADS * RET_V_DIM // pair_w
    tab = pl.BlockSpec((1, s, LANES), lambda bi, p: (bi, 0, 0))
    lg = jnp.broadcast_to(log_gamma.reshape(n_pairs, 2, 1), (n_pairs, 2, LANES))
    n_chunks = s // RET_CHUNK
    return pl.pallas_call(
        functools.partial(_retention_kernel, seq=s),
        grid=(b, n_pairs),
        in_specs=[
            pl.BlockSpec((1, s, LANES), lambda bi, p: (bi, 0, p)),
            pl.BlockSpec((1, s, LANES), lambda bi, p: (bi, 0, qk_tiles + p)),
            pl.BlockSpec((1, s, pair_w), lambda bi, p: (bi, 0, v_off + p)),
            pl.BlockSpec((1, s, pair_w), lambda bi, p: (bi, 0, g_off + p)),
            tab, tab,
            pl.BlockSpec((1, 2, LANES), lambda bi, p: (p, 0, 0)),
            pl.BlockSpec((1, 2, RET_V_DIM), lambda bi, p: (p, 0, 0)),
        ],
        out_specs=pl.BlockSpec((1, s, pair_w), lambda bi, p: (bi, 0, p)),
        out_shape=jax.ShapeDtypeStruct((b, s, RET_HEADS * RET_V_DIM), BF16),
        scratch_shapes=[
            pltpu.VMEM((n_chunks, RET_CHUNK, LANES), BF16),
            pltpu.VMEM((n_chunks, RET_CHUNK, LANES), BF16),
            pltpu.VMEM((n_chunks, 2, RET_V_DIM, RET_CHUNK), BF16),
            pltpu.VMEM((n_chunks, 2, RET_V_DIM, RET_CHUNK), F32),
            pltpu.VMEM((n_chunks, 2, RET_V_DIM, LANES), F32),
        ],
        compiler_params=_cparams(2),
        name="retention",
    )(proj, proj, proj, proj, *tables, lg, gn_gain.reshape(n_pairs, 2, RET_V_DIM))


def _band_bias_t(lo_off, hi_off):
    c = lax.broadcasted_iota(jnp.int32, (2 * QBLK, QBLK), 0)
    a = lax.broadcasted_iota(jnp.int32, (2 * QBLK, QBLK), 1)
    band = (c - a >= lo_off) & (c - a <= hi_off)
    later = jnp.where(band, 0.0, NEG_BIG).astype(F32)
    first = jnp.where(band & (c >= QBLK), 0.0, NEG_BIG).astype(F32)
    return first, later


def _dilated_kernel(q_ref, k_ref, v_ref, c_ref, s_ref, qg_ref, kg_ref, o_ref,
                    qn_s, kn_s, vn_s, q4_s, k4_s, v4_s, qt_s, kc_s, vt_s, bias_s, p_s, m_s,
                    o0_s, o1_s, o2_s, l0_s, l1_s, l2_s, *, seq):
    swap = _swap_matrix(ROPE_DIMS // 2)
    head_ones = _head_ones()
    rq_s, rk_s = o0_s, l0_s

    def norms(n, carry):
        rows = pl.ds(pl.multiple_of(n * PRO_ROWS, PRO_ROWS), PRO_ROWS)
        rq_s[rows, :] = _head_rinv(q_ref[0, rows, :].astype(F32), head_ones)
        rk_s[rows, :] = _head_rinv(k_ref[0, rows, :].astype(F32), head_ones)
        return carry

    lax.fori_loop(0, seq // PRO_ROWS, norms, 0, unroll=32)

    def prologue(n, carry):
        rows = pl.ds(pl.multiple_of(n * PRO_ROWS, PRO_ROWS), PRO_ROWS)
        c, sn = c_ref[0, rows, :], s_ref[0, rows, :]
        q = q_ref[0, rows, :].astype(F32) * rq_s[rows, :] * qg_ref[...]
        q = _rotate(q, c, sn, swap) * Q_SCALE
        k = k_ref[0, rows, :].astype(F32) * rk_s[rows, :] * kg_ref[...]
        k = _rotate(k, c, sn, swap)
        v = v_ref[0, rows, :].astype(F32)
        qn_s[rows, :] = q
        kn_s[rows, :] = k
        vn_s[rows, :] = v
        qt_s[n] = q.T.astype(BF16)
        kc_s[pl.ds(pl.multiple_of((n + 1) * QBLK, QBLK), QBLK), :] = k.astype(BF16)
        vt_s[n + 1] = v.T.astype(BF16)
        return carry

    lax.fori_loop(0, seq // PRO_ROWS, prologue, 0, unroll=32)

    first, later = _band_bias_t(0, QBLK)
    bias_s[0] = first.astype(BF16)
    bias_s[1] = later.astype(BF16)
    eye = (lax.broadcasted_iota(jnp.int32, (QBLK, QBLK), 0)
           == lax.broadcasted_iota(jnp.int32, (QBLK, QBLK), 1))
    eye = jnp.where(eye, 1.0, 0.0).astype(BF16)
    eye_pair = jnp.concatenate([eye, eye], axis=1)
    zero_pad = jnp.zeros((QBLK, LANES), BF16)
    top = lax.broadcasted_iota(jnp.int32, (LANES, QBLK), 0) < HEAD_DIM
    ones_rows = jnp.ones((DEN_ROWS, 2 * QBLK), BF16)

    outs = (o0_s, o1_s, o2_s)
    lses = (l0_s, l1_s, l2_s)
    n_flat = seq // QBLK
    for (window, r), on_s, ln_s in zip(DIL_PATTERNS, outs, lses):
        assert window // r == QBLK and PRO_ROWS == QBLK and DIL_PATTERNS[0][1] == 1
        n_blk = seq // r // QBLK
        blk_shift = n_blk.bit_length() - 1
        assert n_blk == 1 << blk_shift

        def pad(j, c2, n_blk=n_blk):
            z = j * (n_blk + 1)
            kc_s[pl.ds(pl.multiple_of(z * QBLK, QBLK), QBLK), :] = zero_pad
            vt_s[z] = zero_pad
            return c2

        lax.fori_loop(0, r, pad, 0, unroll=r)

        def split(idx, r=r, n_blk=n_blk, blk_shift=blk_shift):
            j = lax.shift_right_logical(idx, blk_shift)
            i = idx & (n_blk - 1)
            return j, i

        def gather(idx, c2, r=r, split=split):
            j, i = split(idx)
            if r == FINE_R:
                base = ((j & (COARSE_R - 1)) * (seq // COARSE_R)
                        + lax.shift_right_logical(j, COARSE_R.bit_length() - 1))
                src = pl.ds(base + (QBLK * r // COARSE_R) * i, QBLK, stride=r // COARSE_R)
                q, k, v = q4_s[src, :], k4_s[src, :], v4_s[src, :]
            else:
                src = pl.ds(j + r * QBLK * i, QBLK, stride=r)
                q, k, v = qn_s[src, :], kn_s[src, :], vn_s[src, :]
            if r == COARSE_R:
                dense = pl.ds(pl.multiple_of(idx * QBLK, QBLK), QBLK)
                q4_s[dense, :] = q
                k4_s[dense, :] = k
                v4_s[dense, :] = v
            k0 = pl.multiple_of((idx + j + 1) * QBLK, QBLK)
            qt_s[idx] = q.T.astype(BF16)
            kc_s[pl.ds(k0, QBLK), :] = k.astype(BF16)
            vt_s[idx + j + 1] = v.T.astype(BF16)
            return c2

        if r > 1:
            lax.fori_loop(0, n_flat, gather, 0, unroll=32)

        def scores(idx, c2, split=split):
            j, i = split(idx)
            k0 = pl.multiple_of((idx + j) * QBLK, QBLK)
            qt = qt_s[idx]
            kw = kc_s[pl.ds(k0, 2 * QBLK), :]
            zero = jnp.zeros_like(qt)
            rhs = jnp.concatenate([jnp.where(top, qt, zero), jnp.where(top, zero, qt)], axis=1)
            lhs = jnp.concatenate([kw, bias_s[jnp.minimum(i, 1)]], axis=1)
            sc = jnp.dot(lhs, jnp.concatenate([rhs, eye_pair], axis=0), preferred_element_type=F32)
            m = jnp.max(sc, axis=0, keepdims=True)
            p_s[idx] = jnp.exp2(sc - m).astype(BF16)
            m_s[idx] = jnp.broadcast_to(m, (SUBLANES, 2 * QBLK))
            return c2

        lax.fori_loop(0, n_flat, scores, 0, unroll=32)

        def block(idx, c2, r=r, split=split, on_s=on_s, ln_s=ln_s):
            j, i = split(idx)
            p = p_s[idx]
            m = m_s[idx][0:1, :]
            lhs = jnp.concatenate(
                [jnp.concatenate([vt_s[idx + j], vt_s[idx + j + 1]], axis=1), ones_rows], axis=0)
            ot = jnp.dot(lhs, p, preferred_element_type=F32)
            den = ot[LANES:LANES + 1, :]
            inv = 1.0 / den
            lse = m + jnp.log2(den)
            o_t = jnp.concatenate(
                [ot[0:HEAD_DIM, 0:QBLK] * inv[:, 0:QBLK],
                 ot[HEAD_DIM:LANES, QBLK:2 * QBLK] * inv[:, QBLK:2 * QBLK]], axis=0)
            l_t = jnp.concatenate(
                [jnp.broadcast_to(lse[:, 0:QBLK], (HEAD_DIM, QBLK)),
                 jnp.broadcast_to(lse[:, QBLK:2 * QBLK], (HEAD_DIM, QBLK))], axis=0)
            dst = pl.ds(j + r * QBLK * i, QBLK, stride=r)
            on_s[dst, :] = o_t.T
            ln_s[dst, :] = l_t.T
            return c2

        lax.fori_loop(0, n_flat, block, 0, unroll=32)

    def combine(n, carry):
        r0 = pl.multiple_of(n * PRO_ROWS, PRO_ROWS)
        rows = pl.ds(r0, PRO_ROWS)
        ls = [l_s[rows, :] for l_s in lses]
        m = jnp.maximum(jnp.maximum(ls[0], ls[1]), ls[2])
        es = [jnp.exp2(l - m) for l in ls]
        num = es[0] * o0_s[rows, :] + es[1] * o1_s[rows, :] + es[2] * o2_s[rows, :]
        o_ref[0, rows, :] = (num / (es[0] + es[1] + es[2])).astype(o_ref.dtype)
        return carry

    lax.fori_loop(0, seq // PRO_ROWS, combine, 0, unroll=32)


def _dilated(proj, tables, q_gain, k_gain, col0):
    b, s, _ = proj.shape
    n_pairs = DIL_HEADS * HEAD_DIM // LANES
    t0 = col0 // LANES
    tab = pl.BlockSpec((1, s, LANES), lambda bi, p: (bi, 0, 0))
    two = lambda g: jnp.concatenate([g, g]).reshape(1, LANES)
    nat = pltpu.VMEM((s, LANES), F32)
    max_r = max(r for _, r in DIL_PATTERNS)
    return pl.pallas_call(
        functools.partial(_dilated_kernel, seq=s),
        grid=(b, n_pairs),
        in_specs=[
            pl.BlockSpec((1, s, LANES), lambda bi, p: (bi, 0, t0 + p)),
            pl.BlockSpec((1, s, LANES), lambda bi, p: (bi, 0, t0 + n_pairs + p)),
            pl.BlockSpec((1, s, LANES), lambda bi, p: (bi, 0, t0 + 2 * n_pairs + p)),
            tab, tab,
            _resident((1, LANES)),
            _resident((1, LANES)),
        ],
        out_specs=pl.BlockSpec((1, s, LANES), lambda bi, p: (bi, 0, p)),
        out_shape=jax.ShapeDtypeStruct((b, s, DIL_HEADS * HEAD_DIM), BF16),
        scratch_shapes=[
            nat, nat, nat, nat, nat, nat,
            pltpu.VMEM((s // QBLK, LANES, QBLK), BF16),
            pltpu.VMEM((s + max_r * QBLK, LANES), BF16),
            pltpu.VMEM((s // QBLK + max_r, LANES, QBLK), BF16),
            pltpu.VMEM((2, 2 * QBLK, QBLK), BF16),
            pltpu.VMEM((s // QBLK, 2 * QBLK, 2 * QBLK), BF16),
            pltpu.VMEM((s // QBLK, SUBLANES, 2 * QBLK), F32),
            nat, nat, nat, nat, nat, nat,
        ],
        compiler_params=_cparams(2),
        name="dilated_attention",
    )(proj, proj, proj, *tables, two(q_gain), two(k_gain))


def _swa_kernel(q_ref, k_ref, v_ref, c_ref, s_ref, qg_ref, kg_ref, sink_ref, o_ref,
                qt_s, kd_s, vt_s, bias_s, p_s, m_s, rq_s, rk_s, *, seq):
    group = SWA_Q_HEADS // SWA_KV_HEADS
    swap = _swap_matrix(ROPE_DIMS // 2)
    head_ones = _head_ones()
    kv_in_hi = (pl.program_id(1) % 2) == 1
    blk_per_step = PRO_ROWS // QBLK
    src = lax.broadcasted_iota(jnp.int32, (LANES, LANES), 0)
    dst = lax.broadcasted_iota(jnp.int32, (LANES, LANES), 1) & (HEAD_DIM - 1)
    dup = jnp.where(src == dst + jnp.where(kv_in_hi, HEAD_DIM, 0), 1.0, 0.0).astype(BF16)

    def norms(n, carry):
        rows = pl.ds(pl.multiple_of(n * PRO_ROWS, PRO_ROWS), PRO_ROWS)
        for t in range(group // 2):
            cols = slice(t * LANES, (t + 1) * LANES)
            rq_s[rows, cols] = _head_rinv(q_ref[0, rows, cols].astype(F32), head_ones)
        rk_s[rows, :] = _head_rinv(k_ref[0, rows, :].astype(F32), head_ones)
        return carry

    lax.fori_loop(0, seq // PRO_ROWS, norms, 0, unroll=32)

    def prologue(n, carry):
        r0 = pl.multiple_of(n * PRO_ROWS, PRO_ROWS)
        rows = pl.ds(r0, PRO_ROWS)
        c, sn = c_ref[0, rows, :], s_ref[0, rows, :]
        for t in range(group // 2):
            cols = slice(t * LANES, (t + 1) * LANES)
            q = q_ref[0, rows, cols].astype(F32) * rq_s[rows, cols] * qg_ref[...]
            q_t = (_rotate(q, c, sn, swap) * Q_SCALE).T
            for u in range(blk_per_step):
                qt_s[n * blk_per_step + u, t] = q_t[:, u * QBLK:(u + 1) * QBLK].astype(BF16)
        k = k_ref[0, rows, :].astype(F32) * rk_s[rows, :] * kg_ref[...]
        k = _rotate(k, c, sn, swap).astype(BF16)
        kd_s[pl.ds(r0 + QBLK, PRO_ROWS), :] = jnp.dot(k, dup, preferred_element_type=F32).astype(BF16)
        v_t = v_ref[0, rows, :].astype(F32).T
        v_t = jnp.where(kv_in_hi, v_t[HEAD_DIM:LANES, :], v_t[0:HEAD_DIM, :]).astype(BF16)
        for u in range(blk_per_step):
            vt_s[n * blk_per_step + u + 1] = v_t[:, u * QBLK:(u + 1) * QBLK]
        return carry

    lax.fori_loop(0, seq // PRO_ROWS, prologue, 0, unroll=32)

    first, later = _band_bias_t(1, QBLK)
    bias_s[0] = first.astype(BF16)
    bias_s[1] = later.astype(BF16)
    eye = (lax.broadcasted_iota(jnp.int32, (QBLK, QBLK), 0)
           == lax.broadcasted_iota(jnp.int32, (QBLK, QBLK), 1))
    eye = jnp.where(eye, 1.0, 0.0).astype(BF16)
    eye_pair = jnp.concatenate([eye, eye], axis=1)
    top = lax.broadcasted_iota(jnp.int32, (LANES, QBLK), 0) < HEAD_DIM
    kd_s[0:QBLK, :] = jnp.zeros((QBLK, LANES), BF16)
    vt_s[0] = jnp.zeros((HEAD_DIM, QBLK), BF16)
    sink = jnp.concatenate([sink_ref[0, hh:hh + 1, :] for hh in range(group)], axis=1) * LOG2_E
    ones_rows = jnp.ones((DEN_ROWS, 2 * QBLK), BF16)

    def scores(i, carry):
        d0 = pl.multiple_of(i * QBLK, QBLK)
        rows = pl.ds(d0, QBLK)
        kw = kd_s[pl.ds(d0, 2 * QBLK), :]
        lhs = jnp.concatenate([kw, bias_s[jnp.minimum(i, 1)]], axis=1)
        for t in range(group // 2):
            qt = qt_s[i, t]
            zero = jnp.zeros_like(qt)
            rhs = jnp.concatenate([jnp.where(top, qt, zero), jnp.where(top, zero, qt)], axis=1)
            cols = slice(2 * t * QBLK, 2 * (t + 1) * QBLK)
            sc = jnp.dot(lhs, jnp.concatenate([rhs, eye_pair], axis=0), preferred_element_type=F32)
            m = jnp.maximum(jnp.max(sc, axis=0, keepdims=True), sink[:, cols])
            p_s[i, :, cols] = jnp.exp2(sc - m).astype(BF16)
            m_s[i, :, cols] = jnp.broadcast_to(m, (SUBLANES, 2 * QBLK))
        return carry

    lax.fori_loop(0, seq // QBLK, scores, 0, unroll=BLOCK_UNROLL)

    def values(i, carry):
        rows = pl.ds(pl.multiple_of(i * QBLK, QBLK), QBLK)
        m = m_s[i][0:1, :]
        lhs = jnp.concatenate(
            [jnp.concatenate([vt_s[i], vt_s[i + 1]], axis=1), ones_rows], axis=0)
        ot = jnp.dot(lhs, p_s[i], preferred_element_type=F32)
        inv = 1.0 / (ot[HEAD_DIM:HEAD_DIM + 1, :] + jnp.exp2(sink - m))
        on = ot[0:HEAD_DIM, :] * inv
        for t in range(group // 2):
            pair_t = jnp.concatenate(
                [on[:, 2 * t * QBLK:(2 * t + 1) * QBLK], on[:, (2 * t + 1) * QBLK:(2 * t + 2) * QBLK]],
                axis=0)
            o_ref[0, rows, t * LANES:(t + 1) * LANES] = pair_t.T.astype(o_ref.dtype)
        return carry

    lax.fori_loop(0, seq // QBLK, values, 0, unroll=32)


def _swa(proj, tables, q_gain, k_gain, sinks):
    b, s, _ = proj.shape
    group = SWA_Q_HEADS // SWA_KV_HEADS
    q_w = group * HEAD_DIM
    k_t0 = SWA_Q_HEADS * HEAD_DIM // LANES
    v_t0 = k_t0 + SWA_KV_HEADS * HEAD_DIM // LANES
    tab = pl.BlockSpec((1, s, LANES), lambda bi, g: (bi, 0, 0))
    two = lambda g: jnp.concatenate([g, g]).reshape(1, LANES)
    sink_rows = jnp.broadcast_to(sinks.reshape(SWA_KV_HEADS, group, 1), (SWA_KV_HEADS, group, LANES))
    return pl.pallas_call(
        functools.partial(_swa_kernel, seq=s),
        grid=(b, SWA_KV_HEADS),
        in_specs=[
            pl.BlockSpec((1, s, q_w), lambda bi, g: (bi, 0, g)),
            pl.BlockSpec((1, s, LANES), lambda bi, g: (bi, 0, k_t0 + g // 2)),
            pl.BlockSpec((1, s, LANES), lambda bi, g: (bi, 0, v_t0 + g // 2)),
            tab, tab,
            _resident((1, LANES)),
            _resident((1, LANES)),
            pl.BlockSpec((1, group, LANES), lambda bi, g: (g, 0, 0)),
        ],
        out_specs=pl.BlockSpec((1, s, q_w), lambda bi, g: (bi, 0, g)),
        out_shape=jax.ShapeDtypeStruct((b, s, SWA_Q_HEADS * HEAD_DIM), BF16),
        scratch_shapes=[
            pltpu.VMEM((s // QBLK, group // 2, LANES, QBLK), BF16),
            pltpu.VMEM((s + QBLK, LANES), BF16),
            pltpu.VMEM((s // QBLK + 1, HEAD_DIM, QBLK), BF16),
            pltpu.VMEM((2, 2 * QBLK, QBLK), BF16),
            pltpu.VMEM((s // QBLK, 2 * QBLK, group * QBLK), BF16),
            pltpu.VMEM((s // QBLK, SUBLANES, group * QBLK), F32),
            pltpu.VMEM((s, q_w), F32),
            pltpu.VMEM((s, LANES), F32),
        ],
        compiler_params=_cparams(2),
        name="swa_attention",
    )(proj, proj, proj, *tables, two(q_gain), two(k_gain), sink_rows)


def kernel(x, positions, norm_mix, norm_mlp, mlp_w_up, mlp_w_down, hyb_w_in, hyb_w_out, ret_gn_gain, dil_q_gain, dil_k_gain, swa_w_qkv, swa_b_qkv, swa_w_out, swa_q_gain, swa_k_gain, swa_sinks):
    b, s, d = x.shape
    depth = norm_mix.shape[0]
    ret_tab = _rope_table(positions, RET_QK_DIM // 2, RET_THETA, RET_QK_DIM)
    rope_tab = _rope_table(positions, ROPE_DIMS // 2, ROPE_THETA, ROPE_DIMS)
    log_gamma = jnp.log1p(-jnp.exp2(-5.0 - jnp.arange(RET_HEADS, dtype=F32)))
    ret_w = RET_HEADS * RET_V_DIM
    dil_col0 = 2 * RET_HEADS * RET_QK_DIM + 2 * ret_w

    x2d = x.reshape(b * s, d)
    for layer in range(depth):
        i = layer // 2
        if layer % 2 == 0:
            zero_bias = jnp.zeros((hyb_w_in.shape[2],), F32)
            proj = _norm_proj(x2d, norm_mix[layer], hyb_w_in, i, zero_bias).reshape(b, s, -1)
            ra = _retention(proj, ret_tab, log_gamma, ret_gn_gain[i])
            da = _dilated(proj, rope_tab, dil_q_gain[i], dil_k_gain[i], dil_col0)
            mixed = [ra.reshape(b * s, -1), da.reshape(b * s, -1)]
            w_out = hyb_w_out
        else:
            proj = _norm_proj(x2d, norm_mix[layer], swa_w_qkv, i, swa_b_qkv[i]).reshape(b, s, -1)
            att = _swa(proj, rope_tab, swa_q_gain[i], swa_k_gain[i], swa_sinks[i])
            mixed = [att.reshape(b * s, -1)]
            w_out = swa_w_out
        x2d = _out_mlp(mixed, x2d, w_out, i, norm_mlp[layer], mlp_w_up, mlp_w_down, layer)
    return x2d.reshape(b, s, d)
```

```python
import functools

import jax
import jax.numpy as jnp
import numpy as np
from jax import lax
from jax.experimental import pallas as pl
from jax.experimental.pallas import tpu as pltpu

F32 = jnp.float32
BF16 = jnp.bfloat16

D_MODEL = 1024
D_FF = 4 * D_MODEL
HEAD_DIM = 64
EPS = 1e-6
RET_HEADS = 4
RET_QK_DIM = 64
RET_V_DIM = 128
RET_CHUNK = 128
RET_THETA = 10000.0
DIL_HEADS = 8
DIL_PATTERNS = ((128, 1), (512, 4), (2048, 16))
SWA_Q_HEADS = 16
SWA_KV_HEADS = 4
SWA_WINDOW = 128
ROPE_THETA = 500000.0
ROPE_DIMS = HEAD_DIM // 4

LANES = 128
SUBLANES = 8
QBLK = 128
VMEM_LIMIT_BYTES = 56 * 1024 * 1024
NEG_BIG = -1e30
LOG2_E = 1.4426950408889634
Q_SCALE = HEAD_DIM ** -0.5 * LOG2_E

ROW_TILE = 1024
MLP_ROW_TILE = 512
FF_CHUNK = 1024
PRO_ROWS = 128
BLOCK_UNROLL = 32
DEN_ROWS = 16
COARSE_R, FINE_R = DIL_PATTERNS[1][1], DIL_PATTERNS[2][1]


def _cparams(n_axes):
    return pltpu.CompilerParams(
        dimension_semantics=("arbitrary",) * n_axes,
        vmem_limit_bytes=VMEM_LIMIT_BYTES,
    )


def _resident(shape):
    nd = len(shape)
    return pl.BlockSpec(shape, lambda *_: (0,) * nd, pipeline_mode=pl.Buffered(1))


def _resident_layer(stack_shape, layer):
    return pl.BlockSpec((None,) + tuple(stack_shape[1:]), lambda *_: (layer, 0, 0),
                        pipeline_mode=pl.Buffered(1))


def _split3(x):
    hi = x.astype(BF16)
    r1 = x - hi.astype(F32)
    mid = r1.astype(BF16)
    lo = (r1 - mid.astype(F32)).astype(BF16)
    return hi, mid, lo


def _trig_kernel(p_ref, spread_ref, inv_ref, ec_ref, es_ref, base_ref, c_ref, s_ref, *, n_pos):
    dense = sum(jnp.dot(part, spread_ref[...], preferred_element_type=F32)
                for part in _split3(p_ref[...]))
    ang = dense * inv_ref[...]
    cos_parts = _split3(jnp.cos(ang))
    sin_parts = _split3(jnp.sin(ang))
    rows = p_ref.shape[0]
    for c in range(n_pos):
        dst = pl.ds(c, rows, stride=n_pos)
        c_ref[dst, :] = base_ref[...] + sum(
            jnp.dot(part, ec_ref[c], preferred_element_type=F32) for part in cos_parts)
        s_ref[dst, :] = sum(
            jnp.dot(part, es_ref[c], preferred_element_type=F32) for part in sin_parts)


def _selectors(half):
    n_pos = LANES // half
    ec = np.zeros((n_pos, LANES, LANES), np.float32)
    es = np.zeros((n_pos, LANES, LANES), np.float32)
    base = np.zeros((1, LANES), np.float32)
    for j in range(LANES):
        d = j % HEAD_DIM
        if d >= 2 * half:
            base[0, j] = 1.0
            continue
        f = d % half
        for c in range(n_pos):
            ec[c, c * half + f, j] = 1.0
            es[c, c * half + f, j] = -1.0 if d < half else 1.0
    return n_pos, jnp.asarray(ec, BF16), jnp.asarray(es, BF16), jnp.asarray(base)


def _rope_table(positions, half, theta, n_rot):
    b, s = positions.shape
    posf = positions.astype(F32)
    inv = jnp.power(jnp.float32(theta), -jnp.arange(half, dtype=F32) * (2.0 / n_rot))
    n_pos, ec, es, base = _selectors(half)
    rows = s // n_pos
    p = posf.reshape(b, rows, n_pos)
    spread = jnp.asarray(np.repeat(np.eye(n_pos, dtype=np.float32), half, axis=1), BF16)
    inv_row = jnp.tile(inv, n_pos)[None, :]
    tile = min(rows, 2048 // n_pos)
    assert rows % tile == 0
    dense = pl.BlockSpec((None, tile, n_pos), lambda bi, i: (bi, i, 0))
    wide = pl.BlockSpec((None, tile * n_pos, LANES), lambda bi, i: (bi, i, 0))
    c_tab, s_tab = pl.pallas_call(
        functools.partial(_trig_kernel, n_pos=n_pos),
        grid=(b, rows // tile),
        in_specs=[dense, _resident(spread.shape), _resident(inv_row.shape), _resident(ec.shape),
                  _resident(es.shape), _resident(base.shape)],
        out_specs=[wide, wide],
        out_shape=[jax.ShapeDtypeStruct((b, s, LANES), F32)] * 2,
        compiler_params=_cparams(2),
        name="trig_tables",
    )(p, spread, inv_row, ec, es, base)
    return c_tab, s_tab


def _swap_matrix(half):
    src = lax.broadcasted_iota(jnp.int32, (LANES, LANES), 0)
    dst = lax.broadcasted_iota(jnp.int32, (LANES, LANES), 1)
    d = dst & (HEAD_DIM - 1)
    want = jnp.where(d < half, dst + half, jnp.where(d < 2 * half, dst - half, -1))
    return jnp.where(src == want, 1.0, 0.0).astype(BF16)


def _rotate(x, c, s, swap):
    swapped = jnp.dot(x.astype(BF16), swap, preferred_element_type=F32)
    return x * c + swapped * s


def _lo_mask(shape):
    return lax.broadcasted_iota(jnp.int32, shape, len(shape) - 1) < HEAD_DIM


def _head_ones():
    r = lax.broadcasted_iota(jnp.int32, (LANES, LANES), 0) < HEAD_DIM
    c = lax.broadcasted_iota(jnp.int32, (LANES, LANES), 1) < HEAD_DIM
    return jnp.where(r == c, 1.0, 0.0).astype(BF16)


def _head_rinv(x, head_ones):
    ss = jnp.dot((x * x).astype(BF16), head_ones, preferred_element_type=F32)
    return lax.rsqrt(ss * (1.0 / HEAD_DIM) + EPS)


def _norm_proj_kernel(x_ref, g_ref, w_ref, b_ref, o_ref, *, n_chunk):
    x = x_ref[...]
    ms = jnp.mean(x * x, axis=-1, keepdims=True)
    h = (x * lax.rsqrt(ms + EPS) * g_ref[...]).astype(BF16)
    n = o_ref.shape[-1]
    for c in range(0, n, n_chunk):
        acc = jnp.dot(h, w_ref[:, c:c + n_chunk].astype(BF16), preferred_element_type=F32)
        o_ref[:, c:c + n_chunk] = (acc + b_ref[:, c:c + n_chunk]).astype(o_ref.dtype)


def _norm_proj(x2d, gain, w_stack, layer, bias):
    m, d = x2d.shape
    n = w_stack.shape[2]
    return pl.pallas_call(
        functools.partial(_norm_proj_kernel, n_chunk=512),
        grid=(m // ROW_TILE,),
        in_specs=[
            pl.BlockSpec((ROW_TILE, d), lambda i: (i, 0)),
            _resident((1, d)),
            _resident_layer(w_stack.shape, layer),
            _resident((1, n)),
        ],
        out_specs=pl.BlockSpec((ROW_TILE, n), lambda i: (i, 0)),
        out_shape=jax.ShapeDtypeStruct((m, n), BF16),
        compiler_params=_cparams(1),
        name="norm_proj",
    )(x2d, gain.reshape(1, d), w_stack, bias.reshape(1, n))


def _out_mlp_kernel(*refs, n_mix):
    a_refs = refs[:n_mix]
    x_ref, wo_ref, g_ref, wup_ref, wdn_ref, o_ref = refs[n_mix:]
    mixed = a_refs[0][...] if n_mix == 1 else jnp.concatenate([a[...] for a in a_refs], axis=-1)
    x1 = x_ref[...] + jnp.dot(mixed, wo_ref[...].astype(BF16), preferred_element_type=F32)
    ms = jnp.mean(x1 * x1, axis=-1, keepdims=True)
    h = (x1 * lax.rsqrt(ms + EPS) * g_ref[...]).astype(BF16)
    y = x1
    for c in range(0, D_FF, FF_CHUNK):
        u = jnp.dot(h, wup_ref[:, c:c + FF_CHUNK].astype(BF16), preferred_element_type=F32)
        u = jnp.square(jnp.maximum(u, 0.0)).astype(BF16)
        y = y + jnp.dot(u, wdn_ref[c:c + FF_CHUNK, :].astype(BF16), preferred_element_type=F32)
    o_ref[...] = y


def _out_mlp(mixed, x2d, w_out, out_layer, gain, w_up, w_down, mlp_layer):
    m, d = x2d.shape
    n_mix = len(mixed)
    in_specs = [pl.BlockSpec((MLP_ROW_TILE, a.shape[1]), lambda i: (i, 0)) for a in mixed]
    in_specs.append(pl.BlockSpec((MLP_ROW_TILE, d), lambda i: (i, 0)))
    in_specs += [_resident_layer(w_out.shape, out_layer), _resident((1, d)),
                 _resident_layer(w_up.shape, mlp_layer), _resident_layer(w_down.shape, mlp_layer)]
    return pl.pallas_call(
        functools.partial(_out_mlp_kernel, n_mix=n_mix),
        grid=(m // MLP_ROW_TILE,),
        in_specs=in_specs,
        out_specs=pl.BlockSpec((MLP_ROW_TILE, d), lambda i: (i, 0)),
        out_shape=jax.ShapeDtypeStruct((m, d), F32),
        compiler_params=_cparams(1),
        name="out_mlp",
    )(*mixed, x2d, w_out, gain.reshape(1, d), w_up, w_down)


def _retention_kernel(q_ref, k_ref, v_ref, g_ref, c_ref, s_ref, lg_ref, gn_ref,
                      o_ref, q_s, k_s, vt_s, o_s, kv_s, *, seq):
    n_chunks = seq // RET_CHUNK
    swap = _swap_matrix(RET_QK_DIM // 2)
    cs = RET_CHUNK

    def prologue(n, carry):
        rows = pl.ds(pl.multiple_of(n * cs, cs), cs)
        c, sn = c_ref[0, rows, :], s_ref[0, rows, :]
        q = _rotate(q_ref[0, rows, :].astype(F32), c, sn, swap)
        k = _rotate(k_ref[0, rows, :].astype(F32), c, sn, swap) * (RET_QK_DIM ** -0.5)
        q_s[n] = q.astype(BF16)
        k_s[n] = k.astype(BF16)
        for hh in range(2):
            cols = slice(hh * RET_V_DIM, (hh + 1) * RET_V_DIM)
            vt_s[n, hh] = v_ref[0, rows, cols].astype(F32).T.astype(BF16)
        return carry

    lax.fori_loop(0, n_chunks, prologue, 0, unroll=32)

    lo = _lo_mask((cs, LANES))
    key = lax.broadcasted_iota(jnp.int32, (cs, cs), 0).astype(F32)
    tok = lax.broadcasted_iota(jnp.int32, (cs, cs), 1).astype(F32)
    ahead = tok - key
    consts = []
    for hh in range(2):
        lg = lg_ref[0, hh:hh + 1, :]
        decay_t = jnp.where(ahead >= 0, jnp.exp(lg * jnp.maximum(ahead, 0.0)), 0.0)
        xi = jnp.exp(lg * (tok[0:1, :] + 1.0))
        zeta = jnp.exp(lg * (cs - 1.0 - tok[0:1, :]))
        cd = jnp.exp(lg * float(cs))
        head = lo if hh == 0 else jnp.logical_not(lo)
        gain = jnp.broadcast_to(gn_ref[0, hh:hh + 1, :], (cs, RET_V_DIM)).T
        consts.append((decay_t, xi, zeta, cd, head, gain))

    def inner(n, carry):
        qc, kc = q_s[n], k_s[n]
        for hh in range(2):
            decay_t, xi, zeta, cd, head, gain = consts[hh]
            qm = jnp.where(head, qc, jnp.zeros_like(qc))
            vt = vt_s[n, hh]
            sc_t = lax.dot_general(kc, qm, (((1,), (1,)), ((), ())),
                                   preferred_element_type=F32) * decay_t
            o_s[n, hh] = jnp.dot(vt, sc_t.astype(BF16), preferred_element_type=F32)
            vz = (vt.astype(F32) * zeta).astype(BF16)
            kv_s[n, hh] = jnp.dot(vz, kc, preferred_element_type=F32)
        return carry

    lax.fori_loop(0, n_chunks, inner, 0, unroll=32)

    def cross(n, state):
        qc = q_s[n]
        new_state = []
        for hh in range(2):
            decay_t, xi, zeta, cd, head, gain = consts[hh]
            r_prev = state[hh]
            qm = jnp.where(head, qc, jnp.zeros_like(qc))
            o_s[n, hh] += lax.dot_general(r_prev.astype(BF16), qm, (((1,), (1,)), ((), ())),
                                          preferred_element_type=F32) * xi
            new_state.append(r_prev * cd + kv_s[n, hh])
        return tuple(new_state)

    zero = jnp.zeros((RET_V_DIM, LANES), F32)
    lax.fori_loop(0, n_chunks, cross, (zero, zero), unroll=32)

    def finish(n, carry):
        rows = pl.ds(pl.multiple_of(n * cs, cs), cs)
        for hh in range(2):
            gain = consts[hh][5]
            cols = slice(hh * RET_V_DIM, (hh + 1) * RET_V_DIM)
            o = o_s[n, hh]
            mu = jnp.mean(o, axis=0, keepdims=True)
            dev = o - mu
            var = jnp.mean(dev * dev, axis=0, keepdims=True)
            y = (dev * lax.rsqrt(var + EPS) * gain).T
            gate = g_ref[0, rows, cols].astype(F32)
            o_ref[0, rows, cols] = (gate * jax.nn.sigmoid(gate) * y).astype(o_ref.dtype)
        return carry

    lax.fori_loop(0, n_chunks, finish, 0, unroll=32)


def _retention(proj, tables, log_gamma, gn_gain):
    b, s, _ = proj.shape
    n_pairs = RET_HEADS // 2
    pair_w = 2 * RET_V_DIM
    qk_tiles = RET_HEADS * RET_QK_DIM // LANES
    v_off = 2 * qk_tiles * LANES // pair_w
    g_off = v_off + RET_HEADS * RET_V_DIM // pair_w
    tab = pl.BlockSpec((1, s, LANES), lambda bi, p: (bi, 0, 0))
    lg = jnp.broadcast_to(log_gamma.reshape(n_pairs, 2, 1), (n_pairs, 2, LANES))
    n_chunks = s // RET_CHUNK
    return pl.pallas_call(
        functools.partial(_retention_kernel, seq=s),
        grid=(b, n_pairs),
        in_specs=[
            pl.BlockSpec((1, s, LANES), lambda bi, p: (bi, 0, p)),
            pl.BlockSpec((1, s, LANES), lambda bi, p: (bi, 0, qk_tiles + p)),
            pl.BlockSpec((1, s, pair_w), lambda bi, p: (bi, 0, v_off + p)),
            pl.BlockSpec((1, s, pair_w), lambda bi, p: (bi, 0, g_off + p)),
            tab, tab,
            pl.BlockSpec((1, 2, LANES), lambda bi, p: (p, 0, 0)),
            pl.BlockSpec((1, 2, RET_V_DIM), lambda bi, p: (p, 0, 0)),
        ],
        out_specs=pl.BlockSpec((1, s, pair_w), lambda bi, p: (bi, 0, p)),
        out_shape=jax.ShapeDtypeStruct((b, s, RET_HEADS * RET_V_DIM), BF16),
        scratch_shapes=[
            pltpu.VMEM((n_chunks, RET_CHUNK, LANES), BF16),
            pltpu.VMEM((n_chunks, RET_CHUNK, LANES), BF16),
            pltpu.VMEM((n_chunks, 2, RET_V_DIM, RET_CHUNK), BF16),
            pltpu.VMEM((n_chunks, 2, RET_V_DIM, RET_CHUNK), F32),
            pltpu.VMEM((n_chunks, 2, RET_V_DIM, LANES), F32),
        ],
        compiler_params=_cparams(2),
        name="retention",
    )(proj, proj, proj, proj, *tables, lg, gn_gain.reshape(n_pairs, 2, RET_V_DIM))


def _band_bias_t(lo_off, hi_off):
    c = lax.broadcasted_iota(jnp.int32, (2 * QBLK, QBLK), 0)
    a = lax.broadcasted_iota(jnp.int32, (2 * QBLK, QBLK), 1)
    band = (c - a >= lo_off) & (c - a <= hi_off)
    later = jnp.where(band, 0.0, NEG_BIG).astype(F32)
    first = jnp.where(band & (c >= QBLK), 0.0, NEG_BIG).astype(F32)
    return first, later


def _dilated_kernel(q_ref, k_ref, v_ref, c_ref, s_ref, qg_ref, kg_ref, o_ref,
                    qn_s, kn_s, vn_s, q4_s, k4_s, v4_s, qt_s, kc_s, vt_s, bias_s, p_s, m_s,
                    o0_s, o1_s, o2_s, l0_s, l1_s, l2_s, *, seq):
    swap = _swap_matrix(ROPE_DIMS // 2)
    head_ones = _head_ones()
    rq_s, rk_s = o0_s, l0_s
    q_gain = qg_ref[...] * Q_SCALE

    def norms(n, carry):
        rows = pl.ds(pl.multiple_of(n * PRO_ROWS, PRO_ROWS), PRO_ROWS)
        rq_s[rows, :] = _head_rinv(q_ref[0, rows, :].astype(F32), head_ones)
        rk_s[rows, :] = _head_rinv(k_ref[0, rows, :].astype(F32), head_ones)
        return carry

    lax.fori_loop(0, seq // PRO_ROWS, norms, 0, unroll=32)

    def prologue(n, carry):
        rows = pl.ds(pl.multiple_of(n * PRO_ROWS, PRO_ROWS), PRO_ROWS)
        c, sn = c_ref[0, rows, :], s_ref[0, rows, :]
        q = q_ref[0, rows, :].astype(F32) * rq_s[rows, :] * q_gain
        q = _rotate(q, c, sn, swap)
        k = k_ref[0, rows, :].astype(F32) * rk_s[rows, :] * kg_ref[...]
        k = _rotate(k, c, sn, swap)
        v = v_ref[0, rows, :].astype(F32)
        qn_s[rows, :] = q
        kn_s[rows, :] = k
        vn_s[rows, :] = v
        qt_s[n] = q.T.astype(BF16)
        kc_s[pl.ds(pl.multiple_of((n + 1) * QBLK, QBLK), QBLK), :] = k.astype(BF16)
        vt_s[n + 1] = v.T.astype(BF16)
        return carry

    lax.fori_loop(0, seq // PRO_ROWS, prologue, 0, unroll=32)

    first, later = _band_bias_t(0, QBLK)
    bias_s[0] = first.astype(BF16)
    bias_s[1] = later.astype(BF16)
    eye = (lax.broadcasted_iota(jnp.int32, (QBLK, QBLK), 0)
           == lax.broadcasted_iota(jnp.int32, (QBLK, QBLK), 1))
    eye = jnp.where(eye, 1.0, 0.0).astype(BF16)
    eye_pair = jnp.concatenate([eye, eye], axis=1)
    zero_pad = jnp.zeros((QBLK, LANES), BF16)
    top = lax.broadcasted_iota(jnp.int32, (LANES, QBLK), 0) < HEAD_DIM
    ones_rows = jnp.ones((DEN_ROWS, 2 * QBLK), BF16)

    outs = (o0_s, o1_s, o2_s)
    lses = (l0_s, l1_s, l2_s)
    n_flat = seq // QBLK
    for (window, r), on_s, ln_s in zip(DIL_PATTERNS, outs, lses):
        assert window // r == QBLK and PRO_ROWS == QBLK and DIL_PATTERNS[0][1] == 1
        n_blk = seq // r // QBLK
        blk_shift = n_blk.bit_length() - 1
        assert n_blk == 1 << blk_shift

        def pad(j, c2, n_blk=n_blk):
            z = j * (n_blk + 1)
            kc_s[pl.ds(pl.multiple_of(z * QBLK, QBLK), QBLK), :] = zero_pad
            vt_s[z] = zero_pad
            return c2

        lax.fori_loop(0, r, pad, 0, unroll=r)

        def split(idx, r=r, n_blk=n_blk, blk_shift=blk_shift):
            j = lax.shift_right_logical(idx, blk_shift)
            i = idx & (n_blk - 1)
            return j, i

        def gather(idx, c2, r=r, split=split):
            j, i = split(idx)
            if r == FINE_R:
                base = ((j & (COARSE_R - 1)) * (seq // COARSE_R)
                        + lax.shift_right_logical(j, COARSE_R.bit_length() - 1))
                src = pl.ds(base + (QBLK * r // COARSE_R) * i, QBLK, stride=r // COARSE_R)
                q, k, v = q4_s[src, :], k4_s[src, :], v4_s[src, :]
            else:
                src = pl.ds(j + r * QBLK * i, QBLK, stride=r)
                q, k, v = qn_s[src, :], kn_s[src, :], vn_s[src, :]
            if r == COARSE_R:
                dense = pl.ds(pl.multiple_of(idx * QBLK, QBLK), QBLK)
                q4_s[dense, :] = q
                k4_s[dense, :] = k
                v4_s[dense, :] = v
            k0 = pl.multiple_of((idx + j + 1) * QBLK, QBLK)
            qt_s[idx] = q.T.astype(BF16)
            kc_s[pl.ds(k0, QBLK), :] = k.astype(BF16)
            vt_s[idx + j + 1] = v.T.astype(BF16)
            return c2

        if r > 1:
            lax.fori_loop(0, n_flat, gather, 0, unroll=32)

        def scores(idx, c2, split=split):
            j, i = split(idx)
            k0 = pl.multiple_of((idx + j) * QBLK, QBLK)
            qt = qt_s[idx]
            kw = kc_s[pl.ds(k0, 2 * QBLK), :]
            zero = jnp.zeros_like(qt)
            rhs = jnp.concatenate([jnp.where(top, qt, zero), jnp.where(top, zero, qt)], axis=1)
            lhs = jnp.concatenate([kw, bias_s[jnp.minimum(i, 1)]], axis=1)
            sc = jnp.dot(lhs, jnp.concatenate([rhs, eye_pair], axis=0), preferred_element_type=F32)
            m = jnp.max(sc, axis=0, keepdims=True)
            p_s[idx] = jnp.exp2(sc - m).astype(BF16)
            m_s[idx] = jnp.broadcast_to(m, (SUBLANES, 2 * QBLK))
            return c2

        lax.fori_loop(0, n_flat, scores, 0, unroll=32)

        def block(idx, c2, r=r, split=split, on_s=on_s, ln_s=ln_s):
            j, i = split(idx)
            p = p_s[idx]
            m = m_s[idx][0:1, :]
            lhs = jnp.concatenate(
                [jnp.concatenate([vt_s[idx + j], vt_s[idx + j + 1]], axis=1), ones_rows], axis=0)
            ot = jnp.dot(lhs, p, preferred_element_type=F32)
            den = ot[LANES:LANES + 1, :]
            inv = 1.0 / den
            lse = m + jnp.log2(den)
            o_t = jnp.concatenate(
                [ot[0:HEAD_DIM, 0:QBLK] * inv[:, 0:QBLK],
                 ot[HEAD_DIM:LANES, QBLK:2 * QBLK] * inv[:, QBLK:2 * QBLK]], axis=0)
            l_t = jnp.concatenate(
                [jnp.broadcast_to(lse[:, 0:QBLK], (HEAD_DIM, QBLK)),
                 jnp.broadcast_to(lse[:, QBLK:2 * QBLK], (HEAD_DIM, QBLK))], axis=0)
            dst = pl.ds(j + r * QBLK * i, QBLK, stride=r)
            on_s[dst, :] = o_t.T
            ln_s[dst, :] = l_t.T
            return c2

        lax.fori_loop(0, n_flat, block, 0, unroll=32)

    def combine(n, carry):
        r0 = pl.multiple_of(n * PRO_ROWS, PRO_ROWS)
        rows = pl.ds(r0, PRO_ROWS)
        ls = [l_s[rows, :] for l_s in lses]
        m = jnp.maximum(jnp.maximum(ls[0], ls[1]), ls[2])
        es = [jnp.exp2(l - m) for l in ls]
        num = es[0] * o0_s[rows, :] + es[1] * o1_s[rows, :] + es[2] * o2_s[rows, :]
        o_ref[0, rows, :] = (num / (es[0] + es[1] + es[2])).astype(o_ref.dtype)
        return carry

    lax.fori_loop(0, seq // PRO_ROWS, combine, 0, unroll=32)


def _dilated(proj, tables, q_gain, k_gain, col0):
    b, s, _ = proj.shape
    n_pairs = DIL_HEADS * HEAD_DIM // LANES
    t0 = col0 // LANES
    tab = pl.BlockSpec((1, s, LANES), lambda bi, p: (bi, 0, 0))
    two = lambda g: jnp.concatenate([g, g]).reshape(1, LANES)
    nat = pltpu.VMEM((s, LANES), F32)
    max_r = max(r for _, r in DIL_PATTERNS)
    return pl.pallas_call(
        functools.partial(_dilated_kernel, seq=s),
        grid=(b, n_pairs),
        in_specs=[
            pl.BlockSpec((1, s, LANES), lambda bi, p: (bi, 0, t0 + p)),
            pl.BlockSpec((1, s, LANES), lambda bi, p: (bi, 0, t0 + n_pairs + p)),
            pl.BlockSpec((1, s, LANES), lambda bi, p: (bi, 0, t0 + 2 * n_pairs + p)),
            tab, tab,
            _resident((1, LANES)),
            _resident((1, LANES)),
        ],
        out_specs=pl.BlockSpec((1, s, LANES), lambda bi, p: (bi, 0, p)),
        out_shape=jax.ShapeDtypeStruct((b, s, DIL_HEADS * HEAD_DIM), BF16),
        scratch_shapes=[
            nat, nat, nat, nat, nat, nat,
            pltpu.VMEM((s // QBLK, LANES, QBLK), BF16),
            pltpu.VMEM((s + max_r * QBLK, LANES), BF16),
            pltpu.VMEM((s // QBLK + max_r, LANES, QBLK), BF16),
            pltpu.VMEM((2, 2 * QBLK, QBLK), BF16),
            pltpu.VMEM((s // QBLK, 2 * QBLK, 2 * QBLK), BF16),
            pltpu.VMEM((s // QBLK, SUBLANES, 2 * QBLK), F32),
            nat, nat, nat, nat, nat, nat,
        ],
        compiler_params=_cparams(2),
        name="dilated_attention",
    )(proj, proj, proj, *tables, two(q_gain), two(k_gain))


def _swa_kernel(q_ref, k_ref, v_ref, c_ref, s_ref, qg_ref, kg_ref, sink_ref, o_ref,
                qt_s, kd_s, vt_s, bias_s, p_s, m_s, rq_s, rk_s, *, seq):
    group = SWA_Q_HEADS // SWA_KV_HEADS
    swap = _swap_matrix(ROPE_DIMS // 2)
    head_ones = _head_ones()
    kv_in_hi = (pl.program_id(1) % 2) == 1
    q_gain = qg_ref[...] * Q_SCALE
    blk_per_step = PRO_ROWS // QBLK
    src = lax.broadcasted_iota(jnp.int32, (LANES, LANES), 0)
    dst = lax.broadcasted_iota(jnp.int32, (LANES, LANES), 1) & (HEAD_DIM - 1)
    dup = jnp.where(src == dst + jnp.where(kv_in_hi, HEAD_DIM, 0), 1.0, 0.0).astype(BF16)

    def norms(n, carry):
        rows = pl.ds(pl.multiple_of(n * PRO_ROWS, PRO_ROWS), PRO_ROWS)
        for t in range(group // 2):
            cols = slice(t * LANES, (t + 1) * LANES)
            rq_s[rows, cols] = _head_rinv(q_ref[0, rows, cols].astype(F32), head_ones)
        rk_s[rows, :] = _head_rinv(k_ref[0, rows, :].astype(F32), head_ones)
        return carry

    lax.fori_loop(0, seq // PRO_ROWS, norms, 0, unroll=32)

    def prologue(n, carry):
        r0 = pl.multiple_of(n * PRO_ROWS, PRO_ROWS)
        rows = pl.ds(r0, PRO_ROWS)
        c, sn = c_ref[0, rows, :], s_ref[0, rows, :]
        for t in range(group // 2):
            cols = slice(t * LANES, (t + 1) * LANES)
            q = q_ref[0, rows, cols].astype(F32) * rq_s[rows, cols] * q_gain
            q_t = _rotate(q, c, sn, swap).T
            for u in range(blk_per_step):
                qt_s[n * blk_per_step + u, t] = q_t[:, u * QBLK:(u + 1) * QBLK].astype(BF16)
        k = k_ref[0, rows, :].astype(F32) * rk_s[rows, :] * kg_ref[...]
        k = _rotate(k, c, sn, swap).astype(BF16)
        kd_s[pl.ds(r0 + QBLK, PRO_ROWS), :] = jnp.dot(k, dup, preferred_element_type=F32).astype(BF16)
        v_t = v_ref[0, rows, :].astype(F32).T
        v_t = jnp.where(kv_in_hi, v_t[HEAD_DIM:LANES, :], v_t[0:HEAD_DIM, :]).astype(BF16)
        for u in range(blk_per_step):
            vt_s[n * blk_per_step + u + 1] = v_t[:, u * QBLK:(u + 1) * QBLK]
        return carry

    lax.fori_loop(0, seq // PRO_ROWS, prologue, 0, unroll=32)

    first, later = _band_bias_t(1, QBLK)
    bias_s[0] = first.astype(BF16)
    bias_s[1] = later.astype(BF16)
    eye = (lax.broadcasted_iota(jnp.int32, (QBLK, QBLK), 0)
           == lax.broadcasted_iota(jnp.int32, (QBLK, QBLK), 1))
    eye = jnp.where(eye, 1.0, 0.0).astype(BF16)
    eye_pair = jnp.concatenate([eye, eye], axis=1)
    top = lax.broadcasted_iota(jnp.int32, (LANES, QBLK), 0) < HEAD_DIM
    kd_s[0:QBLK, :] = jnp.zeros((QBLK, LANES), BF16)
    vt_s[0] = jnp.zeros((HEAD_DIM, QBLK), BF16)
    sink = jnp.concatenate([sink_ref[0, hh:hh + 1, :] for hh in range(group)], axis=1) * LOG2_E
    ones_rows = jnp.ones((DEN_ROWS, 2 * QBLK), BF16)

    def scores(i, carry):
        d0 = pl.multiple_of(i * QBLK, QBLK)
        rows = pl.ds(d0, QBLK)
        kw = kd_s[pl.ds(d0, 2 * QBLK), :]
        lhs = jnp.concatenate([kw, bias_s[jnp.minimum(i, 1)]], axis=1)
        for t in range(group // 2):
            qt = qt_s[i, t]
            zero = jnp.zeros_like(qt)
            rhs = jnp.concatenate([jnp.where(top, qt, zero), jnp.where(top, zero, qt)], axis=1)
            cols = slice(2 * t * QBLK, 2 * (t + 1) * QBLK)
            sc = jnp.dot(lhs, jnp.concatenate([rhs, eye_pair], axis=0), preferred_element_type=F32)
            m = jnp.maximum(jnp.max(sc, axis=0, keepdims=True), sink[:, cols])
            p_s[i, :, cols] = jnp.exp2(sc - m).astype(BF16)
            m_s[i, :, cols] = jnp.broadcast_to(m, (SUBLANES, 2 * QBLK))
        return carry

    lax.fori_loop(0, seq // QBLK, scores, 0, unroll=BLOCK_UNROLL)

    def values(i, carry):
        rows = pl.ds(pl.multiple_of(i * QBLK, QBLK), QBLK)
        m = m_s[i][0:1, :]
        lhs = jnp.concatenate(
            [jnp.concatenate([vt_s[i], vt_s[i + 1]], axis=1), ones_rows], axis=0)
        ot = jnp.dot(lhs, p_s[i], preferred_element_type=F32)
        inv = 1.0 / (ot[HEAD_DIM:HEAD_DIM + 1, :] + jnp.exp2(sink - m))
        on = ot[0:HEAD_DIM, :] * inv
        for t in range(group // 2):
            pair_t = jnp.concatenate(
                [on[:, 2 * t * QBLK:(2 * t + 1) * QBLK], on[:, (2 * t + 1) * QBLK:(2 * t + 2) * QBLK]],
                axis=0)
            o_ref[0, rows, t * LANES:(t + 1) * LANES] = pair_t.T.astype(o_ref.dtype)
        return carry

    lax.fori_loop(0, seq // QBLK, values, 0, unroll=32)


def _swa(proj, tables, q_gain, k_gain, sinks):
    b, s, _ = proj.shape
    assert SWA_WINDOW == QBLK
    group = SWA_Q_HEADS // SWA_KV_HEADS
    q_w = group * HEAD_DIM
    k_t0 = SWA_Q_HEADS * HEAD_DIM // LANES
    v_t0 = k_t0 + SWA_KV_HEADS * HEAD_DIM // LANES
    tab = pl.BlockSpec((1, s, LANES), lambda bi, g: (bi, 0, 0))
    two = lambda g: jnp.concatenate([g, g]).reshape(1, LANES)
    sink_rows = jnp.broadcast_to(sinks.reshape(SWA_KV_HEADS, group, 1), (SWA_KV_HEADS, group, LANES))
    return pl.pallas_call(
        functools.partial(_swa_kernel, seq=s),
        grid=(b, SWA_KV_HEADS),
        in_specs=[
            pl.BlockSpec((1, s, q_w), lambda bi, g: (bi, 0, g)),
            pl.BlockSpec((1, s, LANES), lambda bi, g: (bi, 0, k_t0 + g // 2)),
            pl.BlockSpec((1, s, LANES), lambda bi, g: (bi, 0, v_t0 + g // 2)),
            tab, tab,
            _resident((1, LANES)),
            _resident((1, LANES)),
            pl.BlockSpec((1, group, LANES), lambda bi, g: (g, 0, 0)),
        ],
        out_specs=pl.BlockSpec((1, s, q_w), lambda bi, g: (bi, 0, g)),
        out_shape=jax.ShapeDtypeStruct((b, s, SWA_Q_HEADS * HEAD_DIM), BF16),
        scratch_shapes=[
            pltpu.VMEM((s // QBLK, group // 2, LANES, QBLK), BF16),
            pltpu.VMEM((s + QBLK, LANES), BF16),
            pltpu.VMEM((s // QBLK + 1, HEAD_DIM, QBLK), BF16),
            pltpu.VMEM((2, 2 * QBLK, QBLK), BF16),
            pltpu.VMEM((s // QBLK, 2 * QBLK, group * QBLK), BF16),
            pltpu.VMEM((s // QBLK, SUBLANES, group * QBLK), F32),
            pltpu.VMEM((s, q_w), F32),
            pltpu.VMEM((s, LANES), F32),
        ],
        compiler_params=_cparams(2),
        name="swa_attention",
    )(proj, proj, proj, *tables, two(q_gain), two(k_gain), sink_rows)


def kernel(x, positions, norm_mix, norm_mlp, mlp_w_up, mlp_w_down, hyb_w_in, hyb_w_out, ret_gn_gain, dil_q_gain, dil_k_gain, swa_w_qkv, swa_b_qkv, swa_w_out, swa_q_gain, swa_k_gain, swa_sinks):
    b, s, d = x.shape
    depth = norm_mix.shape[0]
    ret_tab = _rope_table(positions, RET_QK_DIM // 2, RET_THETA, RET_QK_DIM)
    rope_tab = _rope_table(positions, ROPE_DIMS // 2, ROPE_THETA, ROPE_DIMS)
    log_gamma = jnp.log1p(-jnp.exp2(-5.0 - jnp.arange(RET_HEADS, dtype=F32)))
    ret_w = RET_HEADS * RET_V_DIM
    dil_col0 = 2 * RET_HEADS * RET_QK_DIM + 2 * ret_w

    x2d = x.reshape(b * s, d)
    for layer in range(depth):
        i = layer // 2
        if layer % 2 == 0:
            zero_bias = jnp.zeros((hyb_w_in.shape[2],), F32)
            proj = _norm_proj(x2d, norm_mix[layer], hyb_w_in, i, zero_bias).reshape(b, s, -1)
            ra = _retention(proj, ret_tab, log_gamma, ret_gn_gain[i])
            da = _dilated(proj, rope_tab, dil_q_gain[i], dil_k_gain[i], dil_col0)
            mixed = [ra.reshape(b * s, -1), da.reshape(b * s, -1)]
            w_out = hyb_w_out
        else:
            proj = _norm_proj(x2d, norm_mix[layer], swa_w_qkv, i, swa_b_qkv[i]).reshape(b, s, -1)
            att = _swa(proj, rope_tab, swa_q_gain[i], swa_k_gain[i], swa_sinks[i])
            mixed = [att.reshape(b * s, -1)]
            w_out = swa_w_out
        x2d = _out_mlp(mixed, x2d, w_out, i, norm_mlp[layer], mlp_w_up, mlp_w_down, layer)
    return x2d.reshape(b, s, d)
```

```python
import functools

import jax
import jax.numpy as jnp
import numpy as np
from jax import lax
from jax.experimental import pallas as pl
from jax.experimental.pallas import tpu as pltpu

F32 = jnp.float32
BF16 = jnp.bfloat16

D_MODEL = 1024
D_FF = 4 * D_MODEL
HEAD_DIM = 64
EPS = 1e-6
RET_HEADS = 4
RET_QK_DIM = 64
RET_V_DIM = 128
RET_CHUNK = 128
RET_THETA = 10000.0
DIL_HEADS = 8
DIL_PATTERNS = ((128, 1), (512, 4), (2048, 16))
SWA_Q_HEADS = 16
SWA_KV_HEADS = 4
SWA_WINDOW = 128
ROPE_THETA = 500000.0
ROPE_DIMS = HEAD_DIM // 4

LANES = 128
SUBLANES = 8
QBLK = 128
VMEM_LIMIT_BYTES = 56 * 1024 * 1024
NEG_BIG = -1e30
LOG2_E = 1.4426950408889634
Q_SCALE = HEAD_DIM ** -0.5 * LOG2_E

ROW_TILE = 1024
MLP_ROW_TILE = 512
FF_CHUNK = 2048
PROJ_CHUNK = 1536
PRO_ROWS = 128
BLOCK_UNROLL = 32
DEN_ROWS = 16
COARSE_R, FINE_R = DIL_PATTERNS[1][1], DIL_PATTERNS[2][1]


def _cparams(n_axes):
    return pltpu.CompilerParams(
        dimension_semantics=("arbitrary",) * n_axes,
        vmem_limit_bytes=VMEM_LIMIT_BYTES,
    )


def _resident(shape):
    nd = len(shape)
    return pl.BlockSpec(shape, lambda *_: (0,) * nd, pipeline_mode=pl.Buffered(1))


def _resident_layer(stack_shape, layer):
    return pl.BlockSpec((None,) + tuple(stack_shape[1:]), lambda *_: (layer, 0, 0),
                        pipeline_mode=pl.Buffered(1))


def _split3(x):
    hi = x.astype(BF16)
    r1 = x - hi.astype(F32)
    mid = r1.astype(BF16)
    lo = (r1 - mid.astype(F32)).astype(BF16)
    return hi, mid, lo


def _trig_kernel(p_ref, spread_ref, inv_ref, ec_ref, es_ref, base_ref, c_ref, s_ref, *, n_pos):
    dense = sum(jnp.dot(part, spread_ref[...], preferred_element_type=F32)
                for part in _split3(p_ref[...]))
    ang = dense * inv_ref[...]
    cos_parts = _split3(jnp.cos(ang))
    sin_parts = _split3(jnp.sin(ang))
    rows = p_ref.shape[0]
    for c in range(n_pos):
        dst = pl.ds(c, rows, stride=n_pos)
        c_ref[dst, :] = base_ref[...] + sum(
            jnp.dot(part, ec_ref[c], preferred_element_type=F32) for part in cos_parts)
        s_ref[dst, :] = sum(
            jnp.dot(part, es_ref[c], preferred_element_type=F32) for part in sin_parts)


def _selectors(half):
    n_pos = LANES // half
    ec = np.zeros((n_pos, LANES, LANES), np.float32)
    es = np.zeros((n_pos, LANES, LANES), np.float32)
    base = np.zeros((1, LANES), np.float32)
    for j in range(LANES):
        d = j % HEAD_DIM
        if d >= 2 * half:
            base[0, j] = 1.0
            continue
        f = d % half
        for c in range(n_pos):
            ec[c, c * half + f, j] = 1.0
            es[c, c * half + f, j] = -1.0 if d < half else 1.0
    return n_pos, jnp.asarray(ec, BF16), jnp.asarray(es, BF16), jnp.asarray(base)


def _rope_table(positions, half, theta, n_rot):
    b, s = positions.shape
    posf = positions.astype(F32)
    inv = jnp.power(jnp.float32(theta), -jnp.arange(half, dtype=F32) * (2.0 / n_rot))
    n_pos, ec, es, base = _selectors(half)
    rows = s // n_pos
    p = posf.reshape(b, rows, n_pos)
    spread = jnp.asarray(np.repeat(np.eye(n_pos, dtype=np.float32), half, axis=1), BF16)
    inv_row = jnp.tile(inv, n_pos)[None, :]
    tile = min(rows, 2048 // n_pos)
    assert rows % tile == 0
    dense = pl.BlockSpec((None, tile, n_pos), lambda bi, i: (bi, i, 0))
    wide = pl.BlockSpec((None, tile * n_pos, LANES), lambda bi, i: (bi, i, 0))
    c_tab, s_tab = pl.pallas_call(
        functools.partial(_trig_kernel, n_pos=n_pos),
        grid=(b, rows // tile),
        in_specs=[dense, _resident(spread.shape), _resident(inv_row.shape), _resident(ec.shape),
                  _resident(es.shape), _resident(base.shape)],
        out_specs=[wide, wide],
        out_shape=[jax.ShapeDtypeStruct((b, s, LANES), F32)] * 2,
        compiler_params=_cparams(2),
        name="trig_tables",
    )(p, spread, inv_row, ec, es, base)
    return c_tab, s_tab


def _swap_matrix(half):
    src = lax.broadcasted_iota(jnp.int32, (LANES, LANES), 0)
    dst = lax.broadcasted_iota(jnp.int32, (LANES, LANES), 1)
    d = dst & (HEAD_DIM - 1)
    want = jnp.where(d < half, dst + half, jnp.where(d < 2 * half, dst - half, -1))
    return jnp.where(src == want, 1.0, 0.0).astype(BF16)


def _rotate(x, c, s, swap):
    swapped = jnp.dot(x.astype(BF16), swap, preferred_element_type=F32)
    return x * c + swapped * s


def _lo_mask(shape):
    return lax.broadcasted_iota(jnp.int32, shape, len(shape) - 1) < HEAD_DIM


def _head_ones():
    r = lax.broadcasted_iota(jnp.int32, (LANES, LANES), 0) < HEAD_DIM
    c = lax.broadcasted_iota(jnp.int32, (LANES, LANES), 1) < HEAD_DIM
    return jnp.where(r == c, 1.0, 0.0).astype(BF16)


def _head_rinv(x, head_ones):
    ss = jnp.dot((x * x).astype(BF16), head_ones, preferred_element_type=F32)
    return lax.rsqrt(ss * (1.0 / HEAD_DIM) + EPS)


def _norm_proj_kernel(x_ref, g_ref, w_ref, b_ref, o_ref, *, n_chunk):
    x = x_ref[...]
    ms = jnp.mean(x * x, axis=-1, keepdims=True)
    h = (x * lax.rsqrt(ms + EPS) * g_ref[...]).astype(BF16)
    n = o_ref.shape[-1]
    for c in range(0, n, n_chunk):
        acc = jnp.dot(h, w_ref[:, c:c + n_chunk].astype(BF16), preferred_element_type=F32)
        o_ref[:, c:c + n_chunk] = (acc + b_ref[:, c:c + n_chunk]).astype(o_ref.dtype)


def _norm_proj(x2d, gain, w_stack, layer, bias):
    m, d = x2d.shape
    n = w_stack.shape[2]
    return pl.pallas_call(
        functools.partial(_norm_proj_kernel, n_chunk=PROJ_CHUNK),
        grid=(m // ROW_TILE,),
        in_specs=[
            pl.BlockSpec((ROW_TILE, d), lambda i: (i, 0)),
            _resident((1, d)),
            _resident_layer(w_stack.shape, layer),
            _resident((1, n)),
        ],
        out_specs=pl.BlockSpec((ROW_TILE, n), lambda i: (i, 0)),
        out_shape=jax.ShapeDtypeStruct((m, n), BF16),
        compiler_params=_cparams(1),
        name="norm_proj",
    )(x2d, gain.reshape(1, d), w_stack, bias.reshape(1, n))


def _out_mlp_kernel(*refs, n_mix):
    a_refs = refs[:n_mix]
    x_ref, wo_ref, g_ref, wup_ref, wdn_ref, o_ref = refs[n_mix:]
    mixed = a_refs[0][...] if n_mix == 1 else jnp.concatenate([a[...] for a in a_refs], axis=-1)
    x1 = x_ref[...] + jnp.dot(mixed, wo_ref[...].astype(BF16), preferred_element_type=F32)
    ms = jnp.mean(x1 * x1, axis=-1, keepdims=True)
    h = (x1 * lax.rsqrt(ms + EPS) * g_ref[...]).astype(BF16)
    y = x1
    for c in range(0, D_FF, FF_CHUNK):
        u = jnp.dot(h, wup_ref[:, c:c + FF_CHUNK].astype(BF16), preferred_element_type=F32)
        u = jnp.square(jnp.maximum(u, 0.0)).astype(BF16)
        y = y + jnp.dot(u, wdn_ref[c:c + FF_CHUNK, :].astype(BF16), preferred_element_type=F32)
    o_ref[...] = y


def _out_mlp(mixed, x2d, w_out, out_layer, gain, w_up, w_down, mlp_layer):
    m, d = x2d.shape
    n_mix = len(mixed)
    in_specs = [pl.BlockSpec((MLP_ROW_TILE, a.shape[1]), lambda i: (i, 0)) for a in mixed]
    in_specs.append(pl.BlockSpec((MLP_ROW_TILE, d), lambda i: (i, 0)))
    in_specs += [_resident_layer(w_out.shape, out_layer), _resident((1, d)),
                 _resident_layer(w_up.shape, mlp_layer), _resident_layer(w_down.shape, mlp_layer)]
    return pl.pallas_call(
        functools.partial(_out_mlp_kernel, n_mix=n_mix),
        grid=(m // MLP_ROW_TILE,),
        in_specs=in_specs,
        out_specs=pl.BlockSpec((MLP_ROW_TILE, d), lambda i: (i, 0)),
        out_shape=jax.ShapeDtypeStruct((m, d), F32),
        compiler_params=_cparams(1),
        name="out_mlp",
    )(*mixed, x2d, w_out, gain.reshape(1, d), w_up, w_down)


def _retention_kernel(q_ref, k_ref, v_ref, g_ref, c_ref, s_ref, lg_ref, gn_ref,
                      o_ref, q_s, k_s, vt_s, o_s, kv_s, *, seq):
    n_chunks = seq // RET_CHUNK
    swap = _swap_matrix(RET_QK_DIM // 2)
    cs = RET_CHUNK

    def prologue(n, carry):
        rows = pl.ds(pl.multiple_of(n * cs, cs), cs)
        c, sn = c_ref[0, rows, :], s_ref[0, rows, :]
        q = _rotate(q_ref[0, rows, :].astype(F32), c, sn, swap)
        k = _rotate(k_ref[0, rows, :].astype(F32), c, sn, swap) * (RET_QK_DIM ** -0.5)
        q_s[n] = q.astype(BF16)
        k_s[n] = k.astype(BF16)
        for hh in range(2):
            cols = slice(hh * RET_V_DIM, (hh + 1) * RET_V_DIM)
            vt_s[n, hh] = v_ref[0, rows, cols].astype(F32).T.astype(BF16)
        return carry

    lax.fori_loop(0, n_chunks, prologue, 0, unroll=32)

    lo = _lo_mask((cs, LANES))
    key = lax.broadcasted_iota(jnp.int32, (cs, cs), 0).astype(F32)
    tok = lax.broadcasted_iota(jnp.int32, (cs, cs), 1).astype(F32)
    ahead = tok - key
    consts = []
    for hh in range(2):
        lg = lg_ref[0, hh:hh + 1, :]
        decay_t = jnp.where(ahead >= 0, jnp.exp(lg * jnp.maximum(ahead, 0.0)), 0.0)
        xi = jnp.exp(lg * (tok[0:1, :] + 1.0))
        zeta = jnp.exp(lg * (cs - 1.0 - tok[0:1, :]))
        cd = jnp.exp(lg * float(cs))
        head = lo if hh == 0 else jnp.logical_not(lo)
        gain = jnp.broadcast_to(gn_ref[0, hh:hh + 1, :], (cs, RET_V_DIM)).T
        consts.append((decay_t, xi, zeta, cd, head, gain))

    def inner(n, carry):
        qc, kc = q_s[n], k_s[n]
        for hh in range(2):
            decay_t, xi, zeta, cd, head, gain = consts[hh]
            qm = jnp.where(head, qc, jnp.zeros_like(qc))
            vt = vt_s[n, hh]
            sc_t = lax.dot_general(kc, qm, (((1,), (1,)), ((), ())),
                                   preferred_element_type=F32) * decay_t
            o_s[n, hh] = jnp.dot(vt, sc_t.astype(BF16), preferred_element_type=F32)
            vz = (vt.astype(F32) * zeta).astype(BF16)
            kv_s[n, hh] = jnp.dot(vz, kc, preferred_element_type=F32)
        return carry

    lax.fori_loop(0, n_chunks, inner, 0, unroll=32)

    def cross(n, state):
        qc = q_s[n]
        new_state = []
        for hh in range(2):
            decay_t, xi, zeta, cd, head, gain = consts[hh]
            r_prev = state[hh]
            qm = jnp.where(head, qc, jnp.zeros_like(qc))
            o_s[n, hh] += lax.dot_general(r_prev.astype(BF16), qm, (((1,), (1,)), ((), ())),
                                          preferred_element_type=F32) * xi
            new_state.append(r_prev * cd + kv_s[n, hh])
        return tuple(new_state)

    zero = jnp.zeros((RET_V_DIM, LANES), F32)
    lax.fori_loop(0, n_chunks, cross, (zero, zero), unroll=32)

    def finish(n, carry):
        rows = pl.ds(pl.multiple_of(n * cs, cs), cs)
        for hh in range(2):
            gain = consts[hh][5]
            cols = slice(hh * RET_V_DIM, (hh + 1) * RET_V_DIM)
            o = o_s[n, hh]
            mu = jnp.mean(o, axis=0, keepdims=True)
            dev = o - mu
            var = jnp.mean(dev * dev, axis=0, keepdims=True)
            y = (dev * lax.rsqrt(var + EPS) * gain).T
            gate = g_ref[0, rows, cols].astype(F32)
            o_ref[0, rows, cols] = (gate * jax.nn.sigmoid(gate) * y).astype(o_ref.dtype)
        return carry

    lax.fori_loop(0, n_chunks, finish, 0, unroll=32)


def _retention(proj, tables, log_gamma, gn_gain):
    b, s, _ = proj.shape
    n_pairs = RET_HEADS // 2
    pair_w = 2 * RET_V_DIM
    qk_tiles = RET_HEADS * RET_QK_DIM // LANES
    v_off = 2 * qk_tiles * LANES // pair_w
    g_off = v_off + RET_HEADS * RET_V_DIM // pair_w
    tab = pl.BlockSpec((1, s, LANES), lambda bi, p: (bi, 0, 0))
    lg = jnp.broadcast_to(log_gamma.reshape(n_pairs, 2, 1), (n_pairs, 2, LANES))
    n_chunks = s // RET_CHUNK
    return pl.pallas_call(
        functools.partial(_retention_kernel, seq=s),
        grid=(b, n_pairs),
        in_specs=[
            pl.BlockSpec((1, s, LANES), lambda bi, p: (bi, 0, p)),
            pl.BlockSpec((1, s, LANES), lambda bi, p: (bi, 0, qk_tiles + p)),
            pl.BlockSpec((1, s, pair_w), lambda bi, p: (bi, 0, v_off + p)),
            pl.BlockSpec((1, s, pair_w), lambda bi, p: (bi, 0, g_off + p)),
            tab, tab,
            pl.BlockSpec((1, 2, LANES), lambda bi, p: (p, 0, 0)),
            pl.BlockSpec((1, 2, RET_V_DIM), lambda bi, p: (p, 0, 0)),
        ],
        out_specs=pl.BlockSpec((1, s, pair_w), lambda bi, p: (bi, 0, p)),
        out_shape=jax.ShapeDtypeStruct((b, s, RET_HEADS * RET_V_DIM), BF16),
        scratch_shapes=[
            pltpu.VMEM((n_chunks, RET_CHUNK, LANES), BF16),
            pltpu.VMEM((n_chunks, RET_CHUNK, LANES), BF16),
            pltpu.VMEM((n_chunks, 2, RET_V_DIM, RET_CHUNK), BF16),
            pltpu.VMEM((n_chunks, 2, RET_V_DIM, RET_CHUNK), F32),
            pltpu.VMEM((n_chunks, 2, RET_V_DIM, LANES), F32),
        ],
        compiler_params=_cparams(2),
        name="retention",
    )(proj, proj, proj, proj, *tables, lg, gn_gain.reshape(n_pairs, 2, RET_V_DIM))


def _band_bias_t(lo_off, hi_off):
    c = lax.broadcasted_iota(jnp.int32, (2 * QBLK, QBLK), 0)
    a = lax.broadcasted_iota(jnp.int32, (2 * QBLK, QBLK), 1)
    band = (c - a >= lo_off) & (c - a <= hi_off)
    later = jnp.where(band, 0.0, NEG_BIG).astype(F32)
    first = jnp.where(band & (c >= QBLK), 0.0, NEG_BIG).astype(F32)
    return first, later


def _dilated_kernel(q_ref, k_ref, v_ref, c_ref, s_ref, qg_ref, kg_ref, o_ref,
                    qn_s, kn_s, vn_s, q4_s, k4_s, v4_s, qt_s, kc_s, vt_s, bias_s, p_s, m_s,
                    o0_s, o1_s, o2_s, l0_s, l1_s, l2_s, *, seq):
    swap = _swap_matrix(ROPE_DIMS // 2)
    head_ones = _head_ones()
    q_gain = qg_ref[...] * Q_SCALE

    def prologue(n, carry):
        rows = pl.ds(pl.multiple_of(n * PRO_ROWS, PRO_ROWS), PRO_ROWS)
        c, sn = c_ref[0, rows, :], s_ref[0, rows, :]
        q = q_ref[0, rows, :].astype(F32)
        q = _rotate(q * _head_rinv(q, head_ones) * q_gain, c, sn, swap)
        k = k_ref[0, rows, :].astype(F32)
        k = _rotate(k * _head_rinv(k, head_ones) * kg_ref[...], c, sn, swap)
        v = v_ref[0, rows, :].astype(F32)
        qn_s[rows, :] = q
        kn_s[rows, :] = k
        vn_s[rows, :] = v
        qt_s[n] = q.T.astype(BF16)
        kc_s[pl.ds(pl.multiple_of((n + 1) * QBLK, QBLK), QBLK), :] = k.astype(BF16)
        vt_s[n + 1] = v.T.astype(BF16)
        return carry

    lax.fori_loop(0, seq // PRO_ROWS, prologue, 0, unroll=32)

    first, later = _band_bias_t(0, QBLK)
    bias_s[0] = first.astype(BF16)
    bias_s[1] = later.astype(BF16)
    eye = (lax.broadcasted_iota(jnp.int32, (QBLK, QBLK), 0)
           == lax.broadcasted_iota(jnp.int32, (QBLK, QBLK), 1))
    eye = jnp.where(eye, 1.0, 0.0).astype(BF16)
    eye_pair = jnp.concatenate([eye, eye], axis=1)
    zero_pad = jnp.zeros((QBLK, LANES), BF16)
    top = lax.broadcasted_iota(jnp.int32, (LANES, QBLK), 0) < HEAD_DIM
    ones_rows = jnp.ones((DEN_ROWS, 2 * QBLK), BF16)

    outs = (o0_s, o1_s, o2_s)
    lses = (l0_s, l1_s, l2_s)
    n_flat = seq // QBLK
    for (window, r), on_s, ln_s in zip(DIL_PATTERNS, outs, lses):
        assert window // r == QBLK and PRO_ROWS == QBLK and DIL_PATTERNS[0][1] == 1
        n_blk = seq // r // QBLK
        blk_shift = n_blk.bit_length() - 1
        assert n_blk == 1 << blk_shift

        def pad(j, c2, n_blk=n_blk):
            z = j * (n_blk + 1)
            kc_s[pl.ds(pl.multiple_of(z * QBLK, QBLK), QBLK), :] = zero_pad
            vt_s[z] = zero_pad
            return c2

        lax.fori_loop(0, r, pad, 0, unroll=r)

        def split(idx, r=r, n_blk=n_blk, blk_shift=blk_shift):
            j = lax.shift_right_logical(idx, blk_shift)
            i = idx & (n_blk - 1)
            return j, i

        def gather(idx, c2, r=r, split=split):
            j, i = split(idx)
            if r == FINE_R:
                base = ((j & (COARSE_R - 1)) * (seq // COARSE_R)
                        + lax.shift_right_logical(j, COARSE_R.bit_length() - 1))
                src = pl.ds(base + (QBLK * r // COARSE_R) * i, QBLK, stride=r // COARSE_R)
                q, k, v = q4_s[src, :], k4_s[src, :], v4_s[src, :]
            else:
                src = pl.ds(j + r * QBLK * i, QBLK, stride=r)
                q, k, v = qn_s[src, :], kn_s[src, :], vn_s[src, :]
            if r == COARSE_R:
                dense = pl.ds(pl.multiple_of(idx * QBLK, QBLK), QBLK)
                q4_s[dense, :] = q
                k4_s[dense, :] = k
                v4_s[dense, :] = v
            k0 = pl.multiple_of((idx + j + 1) * QBLK, QBLK)
            qt_s[idx] = q.T.astype(BF16)
            kc_s[pl.ds(k0, QBLK), :] = k.astype(BF16)
            vt_s[idx + j + 1] = v.T.astype(BF16)
            return c2

        if r > 1:
            lax.fori_loop(0, n_flat, gather, 0, unroll=32)

        def scores(idx, c2, split=split):
            j, i = split(idx)
            k0 = pl.multiple_of((idx + j) * QBLK, QBLK)
            qt = qt_s[idx]
            kw = kc_s[pl.ds(k0, 2 * QBLK), :]
            zero = jnp.zeros_like(qt)
            rhs = jnp.concatenate([jnp.where(top, qt, zero), jnp.where(top, zero, qt)], axis=1)
            lhs = jnp.concatenate([kw, bias_s[jnp.minimum(i, 1)]], axis=1)
            sc = jnp.dot(lhs, jnp.concatenate([rhs, eye_pair], axis=0), preferred_element_type=F32)
            m = jnp.max(sc, axis=0, keepdims=True)
            p_s[idx] = jnp.exp2(sc - m).astype(BF16)
            m_s[idx] = jnp.broadcast_to(m, (SUBLANES, 2 * QBLK))
            return c2

        lax.fori_loop(0, n_flat, scores, 0, unroll=32)

        def block(idx, c2, r=r, split=split, on_s=on_s, ln_s=ln_s):
            j, i = split(idx)
            p = p_s[idx]
            m = m_s[idx][0:1, :]
            lhs = jnp.concatenate(
                [jnp.concatenate([vt_s[idx + j], vt_s[idx + j + 1]], axis=1), ones_rows], axis=0)
            ot = jnp.dot(lhs, p, preferred_element_type=F32)
            den = ot[LANES:LANES + 1, :]
            inv = 1.0 / den
            lse = m + jnp.log2(den)
            o_t = jnp.concatenate(
                [ot[0:HEAD_DIM, 0:QBLK] * inv[:, 0:QBLK],
                 ot[HEAD_DIM:LANES, QBLK:2 * QBLK] * inv[:, QBLK:2 * QBLK]], axis=0)
            l_t = jnp.concatenate(
                [jnp.broadcast_to(lse[:, 0:QBLK], (HEAD_DIM, QBLK)),
                 jnp.broadcast_to(lse[:, QBLK:2 * QBLK], (HEAD_DIM, QBLK))], axis=0)
            dst = pl.ds(j + r * QBLK * i, QBLK, stride=r)
            on_s[dst, :] = o_t.T
            ln_s[dst, :] = l_t.T
            return c2

        lax.fori_loop(0, n_flat, block, 0, unroll=32)

    def combine(n, carry):
        r0 = pl.multiple_of(n * PRO_ROWS, PRO_ROWS)
        rows = pl.ds(r0, PRO_ROWS)
        ls = [l_s[rows, :] for l_s in lses]
        m = jnp.maximum(jnp.maximum(ls[0], ls[1]), ls[2])
        es = [jnp.exp2(l - m) for l in ls]
        num = es[0] * o0_s[rows, :] + es[1] * o1_s[rows, :] + es[2] * o2_s[rows, :]
        o_ref[0, rows, :] = (num / (es[0] + es[1] + es[2])).astype(o_ref.dtype)
        return carry

    lax.fori_loop(0, seq // PRO_ROWS, combine, 0, unroll=32)


def _dilated(proj, tables, q_gain, k_gain, col0):
    b, s, _ = proj.shape
    n_pairs = DIL_HEADS * HEAD_DIM // LANES
    t0 = col0 // LANES
    tab = pl.BlockSpec((1, s, LANES), lambda bi, p: (bi, 0, 0))
    two = lambda g: jnp.concatenate([g, g]).reshape(1, LANES)
    nat = pltpu.VMEM((s, LANES), F32)
    max_r = max(r for _, r in DIL_PATTERNS)
    return pl.pallas_call(
        functools.partial(_dilated_kernel, seq=s),
        grid=(b, n_pairs),
        in_specs=[
            pl.BlockSpec((1, s, LANES), lambda bi, p: (bi, 0, t0 + p)),
            pl.BlockSpec((1, s, LANES), lambda bi, p: (bi, 0, t0 + n_pairs + p)),
            pl.BlockSpec((1, s, LANES), lambda bi, p: (bi, 0, t0 + 2 * n_pairs + p)),
            tab, tab,
            _resident((1, LANES)),
            _resident((1, LANES)),
        ],
        out_specs=pl.BlockSpec((1, s, LANES), lambda bi, p: (bi, 0, p)),
        out_shape=jax.ShapeDtypeStruct((b, s, DIL_HEADS * HEAD_DIM), BF16),
        scratch_shapes=[
            nat, nat, nat, nat, nat, nat,
            pltpu.VMEM((s // QBLK, LANES, QBLK), BF16),
            pltpu.VMEM((s + max_r * QBLK, LANES), BF16),
            pltpu.VMEM((s // QBLK + max_r, LANES, QBLK), BF16),
            pltpu.VMEM((2, 2 * QBLK, QBLK), BF16),
            pltpu.VMEM((s // QBLK, 2 * QBLK, 2 * QBLK), BF16),
            pltpu.VMEM((s // QBLK, SUBLANES, 2 * QBLK), F32),
            nat, nat, nat, nat, nat, nat,
        ],
        compiler_params=_cparams(2),
        name="dilated_attention",
    )(proj, proj, proj, *tables, two(q_gain), two(k_gain))


def _swa_kernel(q_ref, k_ref, v_ref, c_ref, s_ref, qg_ref, kg_ref, sink_ref, o_ref,
                qt_s, kd_s, vt_s, bias_s, p_s, m_s, *, seq):
    group = SWA_Q_HEADS // SWA_KV_HEADS
    swap = _swap_matrix(ROPE_DIMS // 2)
    head_ones = _head_ones()
    kv_in_hi = (pl.program_id(1) % 2) == 1
    q_gain = qg_ref[...] * Q_SCALE
    blk_per_step = PRO_ROWS // QBLK
    src = lax.broadcasted_iota(jnp.int32, (LANES, LANES), 0)
    dst = lax.broadcasted_iota(jnp.int32, (LANES, LANES), 1) & (HEAD_DIM - 1)
    dup = jnp.where(src == dst + jnp.where(kv_in_hi, HEAD_DIM, 0), 1.0, 0.0).astype(BF16)

    def prologue(n, carry):
        r0 = pl.multiple_of(n * PRO_ROWS, PRO_ROWS)
        rows = pl.ds(r0, PRO_ROWS)
        c, sn = c_ref[0, rows, :], s_ref[0, rows, :]
        for t in range(group // 2):
            cols = slice(t * LANES, (t + 1) * LANES)
            q = q_ref[0, rows, cols].astype(F32)
            q_t = _rotate(q * _head_rinv(q, head_ones) * q_gain, c, sn, swap).T
            for u in range(blk_per_step):
                qt_s[n * blk_per_step + u, t] = q_t[:, u * QBLK:(u + 1) * QBLK].astype(BF16)
        k = k_ref[0, rows, :].astype(F32)
        k = _rotate(k * _head_rinv(k, head_ones) * kg_ref[...], c, sn, swap).astype(BF16)
        kd_s[pl.ds(r0 + QBLK, PRO_ROWS), :] = jnp.dot(k, dup, preferred_element_type=F32).astype(BF16)
        v_t = v_ref[0, rows, :].astype(F32).T
        v_t = jnp.where(kv_in_hi, v_t[HEAD_DIM:LANES, :], v_t[0:HEAD_DIM, :]).astype(BF16)
        for u in range(blk_per_step):
            vt_s[n * blk_per_step + u + 1] = v_t[:, u * QBLK:(u + 1) * QBLK]
        return carry

    lax.fori_loop(0, seq // PRO_ROWS, prologue, 0, unroll=32)

    first, later = _band_bias_t(1, QBLK)
    bias_s[0] = first.astype(BF16)
    bias_s[1] = later.astype(BF16)
    eye = (lax.broadcasted_iota(jnp.int32, (QBLK, QBLK), 0)
           == lax.broadcasted_iota(jnp.int32, (QBLK, QBLK), 1))
    eye = jnp.where(eye, 1.0, 0.0).astype(BF16)
    eye_pair = jnp.concatenate([eye, eye], axis=1)
    top = lax.broadcasted_iota(jnp.int32, (LANES, QBLK), 0) < HEAD_DIM
    kd_s[0:QBLK, :] = jnp.zeros((QBLK, LANES), BF16)
    vt_s[0] = jnp.zeros((HEAD_DIM, QBLK), BF16)
    sink = jnp.concatenate([sink_ref[0, hh:hh + 1, :] for hh in range(group)], axis=1) * LOG2_E
    ones_rows = jnp.ones((DEN_ROWS, 2 * QBLK), BF16)

    def scores(i, carry):
        d0 = pl.multiple_of(i * QBLK, QBLK)
        rows = pl.ds(d0, QBLK)
        kw = kd_s[pl.ds(d0, 2 * QBLK), :]
        lhs = jnp.concatenate([kw, bias_s[jnp.minimum(i, 1)]], axis=1)
        for t in range(group // 2):
            qt = qt_s[i, t]
            zero = jnp.zeros_like(qt)
            rhs = jnp.concatenate([jnp.where(top, qt, zero), jnp.where(top, zero, qt)], axis=1)
            cols = slice(2 * t * QBLK, 2 * (t + 1) * QBLK)
            sc = jnp.dot(lhs, jnp.concatenate([rhs, eye_pair], axis=0), preferred_element_type=F32)
            m = jnp.maximum(jnp.max(sc, axis=0, keepdims=True), sink[:, cols])
            p_s[i, :, cols] = jnp.exp2(sc - m).astype(BF16)
            m_s[i, :, cols] = jnp.broadcast_to(m, (SUBLANES, 2 * QBLK))
        return carry

    lax.fori_loop(0, seq // QBLK, scores, 0, unroll=BLOCK_UNROLL)

    def values(i, carry):
        rows = pl.ds(pl.multiple_of(i * QBLK, QBLK), QBLK)
        m = m_s[i][0:1, :]
        lhs = jnp.concatenate(
            [jnp.concatenate([vt_s[i], vt_s[i + 1]], axis=1), ones_rows], axis=0)
        ot = jnp.dot(lhs, p_s[i], preferred_element_type=F32)
        inv = 1.0 / (ot[HEAD_DIM:HEAD_DIM + 1, :] + jnp.exp2(sink - m))
        on = ot[0:HEAD_DIM, :] * inv
        for t in range(group // 2):
            pair_t = jnp.concatenate(
                [on[:, 2 * t * QBLK:(2 * t + 1) * QBLK], on[:, (2 * t + 1) * QBLK:(2 * t + 2) * QBLK]],
                axis=0)
            o_ref[0, rows, t * LANES:(t + 1) * LANES] = pair_t.T.astype(o_ref.dtype)
        return carry

    lax.fori_loop(0, seq // QBLK, values, 0, unroll=32)


def _swa(proj, tables, q_gain, k_gain, sinks):
    b, s, _ = proj.shape
    assert SWA_WINDOW == QBLK
    group = SWA_Q_HEADS // SWA_KV_HEADS
    q_w = group * HEAD_DIM
    k_t0 = SWA_Q_HEADS * HEAD_DIM // LANES
    v_t0 = k_t0 + SWA_KV_HEADS * HEAD_DIM // LANES
    tab = pl.BlockSpec((1, s, LANES), lambda bi, g: (bi, 0, 0))
    two = lambda g: jnp.concatenate([g, g]).reshape(1, LANES)
    sink_rows = jnp.broadcast_to(sinks.reshape(SWA_KV_HEADS, group, 1), (SWA_KV_HEADS, group, LANES))
    return pl.pallas_call(
        functools.partial(_swa_kernel, seq=s),
        grid=(b, SWA_KV_HEADS),
        in_specs=[
            pl.BlockSpec((1, s, q_w), lambda bi, g: (bi, 0, g)),
            pl.BlockSpec((1, s, LANES), lambda bi, g: (bi, 0, k_t0 + g // 2)),
            pl.BlockSpec((1, s, LANES), lambda bi, g: (bi, 0, v_t0 + g // 2)),
            tab, tab,
            _resident((1, LANES)),
            _resident((1, LANES)),
            pl.BlockSpec((1, group, LANES), lambda bi, g: (g, 0, 0)),
        ],
        out_specs=pl.BlockSpec((1, s, q_w), lambda bi, g: (bi, 0, g)),
        out_shape=jax.ShapeDtypeStruct((b, s, SWA_Q_HEADS * HEAD_DIM), BF16),
        scratch_shapes=[
            pltpu.VMEM((s // QBLK, group // 2, LANES, QBLK), BF16),
            pltpu.VMEM((s + QBLK, LANES), BF16),
            pltpu.VMEM((s // QBLK + 1, HEAD_DIM, QBLK), BF16),
            pltpu.VMEM((2, 2 * QBLK, QBLK), BF16),
            pltpu.VMEM((s // QBLK, 2 * QBLK, group * QBLK), BF16),
            pltpu.VMEM((s // QBLK, SUBLANES, group * QBLK), F32),
        ],
        compiler_params=_cparams(2),
        name="swa_attention",
    )(proj, proj, proj, *tables, two(q_gain), two(k_gain), sink_rows)


def kernel(x, positions, norm_mix, norm_mlp, mlp_w_up, mlp_w_down, hyb_w_in, hyb_w_out, ret_gn_gain, dil_q_gain, dil_k_gain, swa_w_qkv, swa_b_qkv, swa_w_out, swa_q_gain, swa_k_gain, swa_sinks):
    b, s, d = x.shape
    depth = norm_mix.shape[0]
    ret_tab = _rope_table(positions, RET_QK_DIM // 2, RET_THETA, RET_QK_DIM)
    rope_tab = _rope_table(positions, ROPE_DIMS // 2, ROPE_THETA, ROPE_DIMS)
    log_gamma = jnp.log1p(-jnp.exp2(-5.0 - jnp.arange(RET_HEADS, dtype=F32)))
    ret_w = RET_HEADS * RET_V_DIM
    dil_col0 = 2 * RET_HEADS * RET_QK_DIM + 2 * ret_w

    x2d = x.reshape(b * s, d)
    for layer in range(depth):
        i = layer // 2
        if layer % 2 == 0:
            zero_bias = jnp.zeros((hyb_w_in.shape[2],), F32)
            proj = _norm_proj(x2d, norm_mix[layer], hyb_w_in, i, zero_bias).reshape(b, s, -1)
            ra = _retention(proj, ret_tab, log_gamma, ret_gn_gain[i])
            da = _dilated(proj, rope_tab, dil_q_gain[i], dil_k_gain[i], dil_col0)
            mixed = [ra.reshape(b * s, -1), da.reshape(b * s, -1)]
            w_out = hyb_w_out
        else:
            proj = _norm_proj(x2d, norm_mix[layer], swa_w_qkv, i, swa_b_qkv[i]).reshape(b, s, -1)
            att = _swa(proj, rope_tab, swa_q_gain[i], swa_k_gain[i], swa_sinks[i])
            mixed = [att.reshape(b * s, -1)]
            w_out = swa_w_out
        x2d = _out_mlp(mixed, x2d, w_out, i, norm_mlp[layer], mlp_w_up, mlp_w_down, layer)
    return x2d.reshape(b, s, d)
```

```python
import functools

import jax
import jax.numpy as jnp
import numpy as np
from jax import lax
from jax.experimental import pallas as pl
from jax.experimental.pallas import tpu as pltpu

F32 = jnp.float32
BF16 = jnp.bfloat16

D_MODEL = 1024
D_FF = 4 * D_MODEL
HEAD_DIM = 64
EPS = 1e-6
RET_HEADS = 4
RET_QK_DIM = 64
RET_V_DIM = 128
RET_CHUNK = 128
RET_THETA = 10000.0
DIL_HEADS = 8
DIL_PATTERNS = ((128, 1), (512, 4), (2048, 16))
SWA_Q_HEADS = 16
SWA_KV_HEADS = 4
SWA_WINDOW = 128
ROPE_THETA = 500000.0
ROPE_DIMS = HEAD_DIM // 4

LANES = 128
SUBLANES = 8
QBLK = 128
VMEM_LIMIT_BYTES = 56 * 1024 * 1024
NEG_BIG = -1e30
LOG2_E = 1.4426950408889634
Q_SCALE = HEAD_DIM ** -0.5 * LOG2_E

ROW_TILE = 1024
MLP_ROW_TILE = 512
FF_CHUNK = 2048
PROJ_CHUNK = 1536
PRO_ROWS = 128
BLOCK_UNROLL = 32
DEN_ROWS = 16
COARSE_R, FINE_R = DIL_PATTERNS[1][1], DIL_PATTERNS[2][1]


def _cparams(n_axes):
    return pltpu.CompilerParams(
        dimension_semantics=("arbitrary",) * n_axes,
        vmem_limit_bytes=VMEM_LIMIT_BYTES,
    )


def _resident(shape):
    nd = len(shape)
    return pl.BlockSpec(shape, lambda *_: (0,) * nd, pipeline_mode=pl.Buffered(1))


def _resident_layer(stack_shape, layer):
    return pl.BlockSpec((None,) + tuple(stack_shape[1:]), lambda *_: (layer, 0, 0),
                        pipeline_mode=pl.Buffered(1))


def _split3(x):
    hi = x.astype(BF16)
    r1 = x - hi.astype(F32)
    mid = r1.astype(BF16)
    lo = (r1 - mid.astype(F32)).astype(BF16)
    return hi, mid, lo


def _trig_kernel(p_ref, spread_ref, inv_ref, ec_ref, es_ref, base_ref, c_ref, s_ref, *, n_pos):
    dense = sum(jnp.dot(part, spread_ref[...], preferred_element_type=F32)
                for part in _split3(p_ref[...]))
    ang = dense * inv_ref[...]
    cos_parts = _split3(jnp.cos(ang))
    sin_parts = _split3(jnp.sin(ang))
    rows = p_ref.shape[0]
    for c in range(n_pos):
        dst = pl.ds(c, rows, stride=n_pos)
        c_ref[dst, :] = base_ref[...] + sum(
            jnp.dot(part, ec_ref[c], preferred_element_type=F32) for part in cos_parts)
        s_ref[dst, :] = sum(
            jnp.dot(part, es_ref[c], preferred_element_type=F32) for part in sin_parts)


def _selectors(half):
    n_pos = LANES // half
    ec = np.zeros((n_pos, LANES, LANES), np.float32)
    es = np.zeros((n_pos, LANES, LANES), np.float32)
    base = np.zeros((1, LANES), np.float32)
    for j in range(LANES):
        d = j % HEAD_DIM
        if d >= 2 * half:
            base[0, j] = 1.0
            continue
        f = d % half
        for c in range(n_pos):
            ec[c, c * half + f, j] = 1.0
            es[c, c * half + f, j] = -1.0 if d < half else 1.0
    return n_pos, jnp.asarray(ec, BF16), jnp.asarray(es, BF16), jnp.asarray(base)


def _rope_table(positions, half, theta, n_rot):
    b, s = positions.shape
    posf = positions.astype(F32)
    inv = jnp.power(jnp.float32(theta), -jnp.arange(half, dtype=F32) * (2.0 / n_rot))
    n_pos, ec, es, base = _selectors(half)
    rows = s // n_pos
    p = posf.reshape(b, rows, n_pos)
    spread = jnp.asarray(np.repeat(np.eye(n_pos, dtype=np.float32), half, axis=1), BF16)
    inv_row = jnp.tile(inv, n_pos)[None, :]
    tile = min(rows, 2048 // n_pos)
    assert rows % tile == 0
    dense = pl.BlockSpec((None, tile, n_pos), lambda bi, i: (bi, i, 0))
    wide = pl.BlockSpec((None, tile * n_pos, LANES), lambda bi, i: (bi, i, 0))
    c_tab, s_tab = pl.pallas_call(
        functools.partial(_trig_kernel, n_pos=n_pos),
        grid=(b, rows // tile),
        in_specs=[dense, _resident(spread.shape), _resident(inv_row.shape), _resident(ec.shape),
                  _resident(es.shape), _resident(base.shape)],
        out_specs=[wide, wide],
        out_shape=[jax.ShapeDtypeStruct((b, s, LANES), F32)] * 2,
        compiler_params=_cparams(2),
        name="trig_tables",
    )(p, spread, inv_row, ec, es, base)
    return c_tab, s_tab


def _swap_matrix(half):
    src = lax.broadcasted_iota(jnp.int32, (LANES, LANES), 0)
    dst = lax.broadcasted_iota(jnp.int32, (LANES, LANES), 1)
    d = dst & (HEAD_DIM - 1)
    want = jnp.where(d < half, dst + half, jnp.where(d < 2 * half, dst - half, -1))
    return jnp.where(src == want, 1.0, 0.0).astype(BF16)


def _rotate(x, c, s, swap):
    swapped = jnp.dot(x.astype(BF16), swap, preferred_element_type=F32)
    return x * c + swapped * s


def _lo_mask(shape):
    return lax.broadcasted_iota(jnp.int32, shape, len(shape) - 1) < HEAD_DIM


def _head_ones():
    r = lax.broadcasted_iota(jnp.int32, (LANES, LANES), 0) < HEAD_DIM
    c = lax.broadcasted_iota(jnp.int32, (LANES, LANES), 1) < HEAD_DIM
    return jnp.where(r == c, 1.0, 0.0).astype(BF16)


def _head_rinv(x, head_ones):
    ss = jnp.dot((x * x).astype(BF16), head_ones, preferred_element_type=F32)
    return lax.rsqrt(ss * (1.0 / HEAD_DIM) + EPS)


def _norm_proj_kernel(x_ref, g_ref, w_ref, b_ref, o_ref, *, n_chunk):
    x = x_ref[...]
    ms = jnp.mean(x * x, axis=-1, keepdims=True)
    h = (x * lax.rsqrt(ms + EPS) * g_ref[...]).astype(BF16)
    n = o_ref.shape[-1]
    for c in range(0, n, n_chunk):
        acc = jnp.dot(h, w_ref[:, c:c + n_chunk].astype(BF16), preferred_element_type=F32)
        o_ref[:, c:c + n_chunk] = (acc + b_ref[:, c:c + n_chunk]).astype(o_ref.dtype)


def _norm_proj(x2d, gain, w_stack, layer, bias):
    m, d = x2d.shape
    n = w_stack.shape[2]
    return pl.pallas_call(
        functools.partial(_norm_proj_kernel, n_chunk=PROJ_CHUNK),
        grid=(m // ROW_TILE,),
        in_specs=[
            pl.BlockSpec((ROW_TILE, d), lambda i: (i, 0)),
            _resident((1, d)),
            _resident_layer(w_stack.shape, layer),
            _resident((1, n)),
        ],
        out_specs=pl.BlockSpec((ROW_TILE, n), lambda i: (i, 0)),
        out_shape=jax.ShapeDtypeStruct((m, n), BF16),
        compiler_params=_cparams(1),
        name="norm_proj",
    )(x2d, gain.reshape(1, d), w_stack, bias.reshape(1, n))


def _out_mlp_kernel(*refs, n_mix):
    a_refs = refs[:n_mix]
    x_ref, wo_ref, g_ref, wup_ref, wdn_ref, o_ref = refs[n_mix:]
    mixed = a_refs[0][...] if n_mix == 1 else jnp.concatenate([a[...] for a in a_refs], axis=-1)
    x1 = x_ref[...] + jnp.dot(mixed, wo_ref[...].astype(BF16), preferred_element_type=F32)
    ms = jnp.mean(x1 * x1, axis=-1, keepdims=True)
    h = (x1 * lax.rsqrt(ms + EPS) * g_ref[...]).astype(BF16)
    y = x1
    for c in range(0, D_FF, FF_CHUNK):
        u = jnp.dot(h, wup_ref[:, c:c + FF_CHUNK].astype(BF16), preferred_element_type=F32)
        u = jnp.square(jnp.maximum(u, 0.0)).astype(BF16)
        y = y + jnp.dot(u, wdn_ref[c:c + FF_CHUNK, :].astype(BF16), preferred_element_type=F32)
    o_ref[...] = y


def _out_mlp(mixed, x2d, w_out, out_layer, gain, w_up, w_down, mlp_layer):
    m, d = x2d.shape
    n_mix = len(mixed)
    in_specs = [pl.BlockSpec((MLP_ROW_TILE, a.shape[1]), lambda i: (i, 0)) for a in mixed]
    in_specs.append(pl.BlockSpec((MLP_ROW_TILE, d), lambda i: (i, 0)))
    in_specs += [_resident_layer(w_out.shape, out_layer), _resident((1, d)),
                 _resident_layer(w_up.shape, mlp_layer), _resident_layer(w_down.shape, mlp_layer)]
    return pl.pallas_call(
        functools.partial(_out_mlp_kernel, n_mix=n_mix),
        grid=(m // MLP_ROW_TILE,),
        in_specs=in_specs,
        out_specs=pl.BlockSpec((MLP_ROW_TILE, d), lambda i: (i, 0)),
        out_shape=jax.ShapeDtypeStruct((m, d), F32),
        compiler_params=_cparams(1),
        name="out_mlp",
    )(*mixed, x2d, w_out, gain.reshape(1, d), w_up, w_down)


def _retention_kernel(q_ref, k_ref, v_ref, g_ref, c_ref, s_ref, lg_ref, gn_ref,
                      o_ref, q_s, k_s, vt_s, o_s, kv_s, *, seq):
    n_chunks = seq // RET_CHUNK
    swap = _swap_matrix(RET_QK_DIM // 2)
    cs = RET_CHUNK

    def prologue(n, carry):
        rows = pl.ds(pl.multiple_of(n * cs, cs), cs)
        c, sn = c_ref[0, rows, :], s_ref[0, rows, :]
        q = _rotate(q_ref[0, rows, :].astype(F32), c, sn, swap)
        k = _rotate(k_ref[0, rows, :].astype(F32), c, sn, swap) * (RET_QK_DIM ** -0.5)
        q_s[n] = q.astype(BF16)
        k_s[n] = k.astype(BF16)
        for hh in range(2):
            cols = slice(hh * RET_V_DIM, (hh + 1) * RET_V_DIM)
            vt_s[n, hh] = v_ref[0, rows, cols].astype(F32).T.astype(BF16)
        return carry

    lax.fori_loop(0, n_chunks, prologue, 0, unroll=32)

    lo = _lo_mask((cs, LANES))
    key = lax.broadcasted_iota(jnp.int32, (cs, cs), 0).astype(F32)
    tok = lax.broadcasted_iota(jnp.int32, (cs, cs), 1).astype(F32)
    ahead = tok - key
    consts = []
    for hh in range(2):
        lg = lg_ref[0, hh:hh + 1, :]
        decay_t = jnp.where(ahead >= 0, jnp.exp(lg * jnp.maximum(ahead, 0.0)), 0.0)
        xi = jnp.exp(lg * (tok[0:1, :] + 1.0))
        zeta = jnp.exp(lg * (cs - 1.0 - tok[0:1, :]))
        cd = jnp.exp(lg * float(cs))
        head = lo if hh == 0 else jnp.logical_not(lo)
        gain = jnp.broadcast_to(gn_ref[0, hh:hh + 1, :], (cs, RET_V_DIM)).T
        consts.append((decay_t, xi, zeta, cd, head, gain))

    def inner(n, carry):
        qc, kc = q_s[n], k_s[n]
        for hh in range(2):
            decay_t, xi, zeta, cd, head, gain = consts[hh]
            qm = jnp.where(head, qc, jnp.zeros_like(qc))
            vt = vt_s[n, hh]
            sc_t = lax.dot_general(kc, qm, (((1,), (1,)), ((), ())),
                                   preferred_element_type=F32) * decay_t
            o_s[n, hh] = jnp.dot(vt, sc_t.astype(BF16), preferred_element_type=F32)
            vz = (vt.astype(F32) * zeta).astype(BF16)
            kv_s[n, hh] = jnp.dot(vz, kc, preferred_element_type=F32)
        return carry

    lax.fori_loop(0, n_chunks, inner, 0, unroll=32)

    def cross(n, state):
        qc = q_s[n]
        new_state = []
        for hh in range(2):
            decay_t, xi, zeta, cd, head, gain = consts[hh]
            r_prev = state[hh]
            qm = jnp.where(head, qc, jnp.zeros_like(qc))
            o_s[n, hh] += lax.dot_general(r_prev.astype(BF16), qm, (((1,), (1,)), ((), ())),
                                          preferred_element_type=F32) * xi
            new_state.append(r_prev * cd + kv_s[n, hh])
        return tuple(new_state)

    zero = jnp.zeros((RET_V_DIM, LANES), F32)
    lax.fori_loop(0, n_chunks, cross, (zero, zero), unroll=32)

    def finish(n, carry):
        rows = pl.ds(pl.multiple_of(n * cs, cs), cs)
        for hh in range(2):
            gain = consts[hh][5]
            cols = slice(hh * RET_V_DIM, (hh + 1) * RET_V_DIM)
            o = o_s[n, hh]
            mu = jnp.mean(o, axis=0, keepdims=True)
            dev = o - mu
            var = jnp.mean(dev * dev, axis=0, keepdims=True)
            y = (dev * lax.rsqrt(var + EPS) * gain).T
            gate = g_ref[0, rows, cols].astype(F32)
            o_ref[0, rows, cols] = (gate * jax.nn.sigmoid(gate) * y).astype(o_ref.dtype)
        return carry

    lax.fori_loop(0, n_chunks, finish, 0, unroll=32)


def _retention(proj, tables, log_gamma, gn_gain):
    b, s, _ = proj.shape
    n_pairs = RET_HEADS // 2
    pair_w = 2 * RET_V_DIM
    qk_tiles = RET_HEADS * RET_QK_DIM // LANES
    v_off = 2 * qk_tiles * LANES // pair_w
    g_off = v_off + RET_HEADS * RET_V_DIM // pair_w
    tab = pl.BlockSpec((1, s, LANES), lambda bi, p: (bi, 0, 0))
    lg = jnp.broadcast_to(log_gamma.reshape(n_pairs, 2, 1), (n_pairs, 2, LANES))
    n_chunks = s // RET_CHUNK
    return pl.pallas_call(
        functools.partial(_retention_kernel, seq=s),
        grid=(b, n_pairs),
        in_specs=[
            pl.BlockSpec((1, s, LANES), lambda bi, p: (bi, 0, p)),
            pl.BlockSpec((1, s, LANES), lambda bi, p: (bi, 0, qk_tiles + p)),
            pl.BlockSpec((1, s, pair_w), lambda bi, p: (bi, 0, v_off + p)),
            pl.BlockSpec((1, s, pair_w), lambda bi, p: (bi, 0, g_off + p)),
            tab, tab,
            pl.BlockSpec((1, 2, LANES), lambda bi, p: (p, 0, 0)),
            pl.BlockSpec((1, 2, RET_V_DIM), lambda bi, p: (p, 0, 0)),
        ],
        out_specs=pl.BlockSpec((1, s, pair_w), lambda bi, p: (bi, 0, p)),
        out_shape=jax.ShapeDtypeStruct((b, s, RET_HEADS * RET_V_DIM), BF16),
        scratch_shapes=[
            pltpu.VMEM((n_chunks, RET_CHUNK, LANES), BF16),
            pltpu.VMEM((n_chunks, RET_CHUNK, LANES), BF16),
            pltpu.VMEM((n_chunks, 2, RET_V_DIM, RET_CHUNK), BF16),
            pltpu.VMEM((n_chunks, 2, RET_V_DIM, RET_CHUNK), F32),
            pltpu.VMEM((n_chunks, 2, RET_V_DIM, LANES), F32),
        ],
        compiler_params=_cparams(2),
        name="retention",
    )(proj, proj, proj, proj, *tables, lg, gn_gain.reshape(n_pairs, 2, RET_V_DIM))


def _band_bias_t(lo_off, hi_off):
    c = lax.broadcasted_iota(jnp.int32, (2 * QBLK, QBLK), 0)
    a = lax.broadcasted_iota(jnp.int32, (2 * QBLK, QBLK), 1)
    band = (c - a >= lo_off) & (c - a <= hi_off)
    later = jnp.where(band, 0.0, NEG_BIG).astype(F32)
    first = jnp.where(band & (c >= QBLK), 0.0, NEG_BIG).astype(F32)
    return first, later


def _dilated_kernel(q_ref, k_ref, v_ref, c_ref, s_ref, qg_ref, kg_ref, o_ref,
                    qn_s, kn_s, vn_s, q4_s, k4_s, v4_s, qt_s, kc_s, vt_s, bias_s, p_s, m_s,
                    o0_s, o1_s, o2_s, l0_s, l1_s, l2_s, *, seq):
    swap = _swap_matrix(ROPE_DIMS // 2)
    head_ones = _head_ones()
    q_gain = qg_ref[...] * Q_SCALE

    def prologue(n, carry):
        rows = pl.ds(pl.multiple_of(n * PRO_ROWS, PRO_ROWS), PRO_ROWS)
        c, sn = c_ref[0, rows, :], s_ref[0, rows, :]
        q = q_ref[0, rows, :].astype(F32)
        q = _rotate(q * _head_rinv(q, head_ones) * q_gain, c, sn, swap)
        k = k_ref[0, rows, :].astype(F32)
        k = _rotate(k * _head_rinv(k, head_ones) * kg_ref[...], c, sn, swap)
        v = v_ref[0, rows, :].astype(F32)
        qn_s[rows, :] = q
        kn_s[rows, :] = k
        vn_s[rows, :] = v
        qt_s[n] = q.T.astype(BF16)
        kc_s[pl.ds(pl.multiple_of((n + 1) * QBLK, QBLK), QBLK), :] = k.astype(BF16)
        vt_s[n + 1] = v.T.astype(BF16)
        return carry

    lax.fori_loop(0, seq // PRO_ROWS, prologue, 0, unroll=32)

    first, later = _band_bias_t(0, QBLK)
    bias_s[0] = first.astype(BF16)
    bias_s[1] = later.astype(BF16)
    eye = (lax.broadcasted_iota(jnp.int32, (QBLK, QBLK), 0)
           == lax.broadcasted_iota(jnp.int32, (QBLK, QBLK), 1))
    eye = jnp.where(eye, 1.0, 0.0).astype(BF16)
    eye_pair = jnp.concatenate([eye, eye], axis=1)
    zero_pad = jnp.zeros((QBLK, LANES), BF16)
    top = lax.broadcasted_iota(jnp.int32, (LANES, QBLK), 0) < HEAD_DIM
    ones_rows = jnp.ones((DEN_ROWS, 2 * QBLK), BF16)

    outs = (o0_s, o1_s, o2_s)
    lses = (l0_s, l1_s, l2_s)
    n_flat = seq // QBLK
    for (window, r), on_s, ln_s in zip(DIL_PATTERNS, outs, lses):
        assert window // r == QBLK and PRO_ROWS == QBLK and DIL_PATTERNS[0][1] == 1
        n_blk = seq // r // QBLK
        blk_shift = n_blk.bit_length() - 1
        assert n_blk == 1 << blk_shift

        def pad(j, c2, n_blk=n_blk):
            z = j * (n_blk + 1)
            kc_s[pl.ds(pl.multiple_of(z * QBLK, QBLK), QBLK), :] = zero_pad
            vt_s[z] = zero_pad
            return c2

        lax.fori_loop(0, r, pad, 0, unroll=r)

        def split(idx, r=r, n_blk=n_blk, blk_shift=blk_shift):
            j = lax.shift_right_logical(idx, blk_shift)
            i = idx & (n_blk - 1)
            return j, i

        def gather(idx, c2, r=r, split=split):
            j, i = split(idx)
            if r == FINE_R:
                base = ((j & (COARSE_R - 1)) * (seq // COARSE_R)
                        + lax.shift_right_logical(j, COARSE_R.bit_length() - 1))
                src = pl.ds(base + (QBLK * r // COARSE_R) * i, QBLK, stride=r // COARSE_R)
                q, k, v = q4_s[src, :], k4_s[src, :], v4_s[src, :]
            else:
                src = pl.ds(j + r * QBLK * i, QBLK, stride=r)
                q, k, v = qn_s[src, :], kn_s[src, :], vn_s[src, :]
            if r == COARSE_R:
                dense = pl.ds(pl.multiple_of(idx * QBLK, QBLK), QBLK)
                q4_s[dense, :] = q
                k4_s[dense, :] = k
                v4_s[dense, :] = v
            k0 = pl.multiple_of((idx + j + 1) * QBLK, QBLK)
            qt_s[idx] = q.T.astype(BF16)
            kc_s[pl.ds(k0, QBLK), :] = k.astype(BF16)
            vt_s[idx + j + 1] = v.T.astype(BF16)
            return c2

        if r > 1:
            lax.fori_loop(0, n_flat, gather, 0, unroll=32)

        def scores(idx, c2, split=split):
            j, i = split(idx)
            k0 = pl.multiple_of((idx + j) * QBLK, QBLK)
            qt = qt_s[idx]
            kw = kc_s[pl.ds(k0, 2 * QBLK), :]
            zero = jnp.zeros_like(qt)
            rhs = jnp.concatenate([jnp.where(top, qt, zero), jnp.where(top, zero, qt)], axis=1)
            lhs = jnp.concatenate([kw, bias_s[jnp.minimum(i, 1)]], axis=1)
            sc = jnp.dot(lhs, jnp.concatenate([rhs, eye_pair], axis=0), preferred_element_type=F32)
            m = jnp.max(sc, axis=0, keepdims=True)
            p_s[idx] = jnp.exp2(sc - m).astype(BF16)
            m_s[idx] = jnp.broadcast_to(m, (SUBLANES, 2 * QBLK))
            return c2

        lax.fori_loop(0, n_flat, scores, 0, unroll=32)

        def block(idx, c2, r=r, split=split, on_s=on_s, ln_s=ln_s):
            j, i = split(idx)
            p = p_s[idx]
            m = m_s[idx][0:1, :]
            lhs = jnp.concatenate(
                [jnp.concatenate([vt_s[idx + j], vt_s[idx + j + 1]], axis=1), ones_rows], axis=0)
            ot = jnp.dot(lhs, p, preferred_element_type=F32)
            den = ot[LANES:LANES + 1, :]
            inv = 1.0 / den
            lse = m + jnp.log2(den)
            o_t = jnp.concatenate(
                [ot[0:HEAD_DIM, 0:QBLK] * inv[:, 0:QBLK],
                 ot[HEAD_DIM:LANES, QBLK:2 * QBLK] * inv[:, QBLK:2 * QBLK]], axis=0)
            l_t = jnp.concatenate(
                [jnp.broadcast_to(lse[:, 0:QBLK], (HEAD_DIM, QBLK)),
                 jnp.broadcast_to(lse[:, QBLK:2 * QBLK], (HEAD_DIM, QBLK))], axis=0)
            dst = pl.ds(j + r * QBLK * i, QBLK, stride=r)
            on_s[dst, :] = o_t.T
            ln_s[dst, :] = l_t.T
            return c2

        lax.fori_loop(0, n_flat, block, 0, unroll=32)

    def combine(n, carry):
        r0 = pl.multiple_of(n * PRO_ROWS, PRO_ROWS)
        rows = pl.ds(r0, PRO_ROWS)
        ls = [l_s[rows, :] for l_s in lses]
        m = jnp.maximum(jnp.maximum(ls[0], ls[1]), ls[2])
        es = [jnp.exp2(l - m) for l in ls]
        num = es[0] * o0_s[rows, :] + es[1] * o1_s[rows, :] + es[2] * o2_s[rows, :]
        o_ref[0, rows, :] = (num / (es[0] + es[1] + es[2])).astype(o_ref.dtype)
        return carry

    lax.fori_loop(0, seq // PRO_ROWS, combine, 0, unroll=32)


def _dilated(proj, tables, q_gain, k_gain, col0):
    b, s, _ = proj.shape
    n_pairs = DIL_HEADS * HEAD_DIM // LANES
    t0 = col0 // LANES
    tab = pl.BlockSpec((1, s, LANES), lambda bi, p: (bi, 0, 0))
    two = lambda g: jnp.concatenate([g, g]).reshape(1, LANES)
    nat = pltpu.VMEM((s, LANES), F32)
    max_r = max(r for _, r in DIL_PATTERNS)
    return pl.pallas_call(
        functools.partial(_dilated_kernel, seq=s),
        grid=(b, n_pairs),
        in_specs=[
            pl.BlockSpec((1, s, LANES), lambda bi, p: (bi, 0, t0 + p)),
            pl.BlockSpec((1, s, LANES), lambda bi, p: (bi, 0, t0 + n_pairs + p)),
            pl.BlockSpec((1, s, LANES), lambda bi, p: (bi, 0, t0 + 2 * n_pairs + p)),
            tab, tab,
            _resident((1, LANES)),
            _resident((1, LANES)),
        ],
        out_specs=pl.BlockSpec((1, s, LANES), lambda bi, p: (bi, 0, p)),
        out_shape=jax.ShapeDtypeStruct((b, s, DIL_HEADS * HEAD_DIM), BF16),
        scratch_shapes=[
            nat, nat, nat, nat, nat, nat,
            pltpu.VMEM((s // QBLK, LANES, QBLK), BF16),
            pltpu.VMEM((s + max_r * QBLK, LANES), BF16),
            pltpu.VMEM((s // QBLK + max_r, LANES, QBLK), BF16),
            pltpu.VMEM((2, 2 * QBLK, QBLK), BF16),
            pltpu.VMEM((s // QBLK, 2 * QBLK, 2 * QBLK), BF16),
            pltpu.VMEM((s // QBLK, SUBLANES, 2 * QBLK), F32),
            nat, nat, nat, nat, nat, nat,
        ],
        compiler_params=_cparams(2),
        name="dilated_attention",
    )(proj, proj, proj, *tables, two(q_gain), two(k_gain))


def _swa_kernel(q_ref, k_ref, v_ref, c_ref, s_ref, qg_ref, kg_ref, sink_ref, o_ref,
                qt_s, kd_s, vt_s, bias_s, p_s, m_s, rq_s, rk_s, *, seq):
    group = SWA_Q_HEADS // SWA_KV_HEADS
    swap = _swap_matrix(ROPE_DIMS // 2)
    head_ones = _head_ones()
    kv_in_hi = (pl.program_id(1) % 2) == 1
    q_gain = qg_ref[...] * Q_SCALE
    blk_per_step = PRO_ROWS // QBLK
    src = lax.broadcasted_iota(jnp.int32, (LANES, LANES), 0)
    dst = lax.broadcasted_iota(jnp.int32, (LANES, LANES), 1) & (HEAD_DIM - 1)
    dup = jnp.where(src == dst + jnp.where(kv_in_hi, HEAD_DIM, 0), 1.0, 0.0).astype(BF16)

    def norms(n, carry):
        rows = pl.ds(pl.multiple_of(n * PRO_ROWS, PRO_ROWS), PRO_ROWS)
        for t in range(group // 2):
            cols = slice(t * LANES, (t + 1) * LANES)
            rq_s[rows, cols] = _head_rinv(q_ref[0, rows, cols].astype(F32), head_ones)
        rk_s[rows, :] = _head_rinv(k_ref[0, rows, :].astype(F32), head_ones)
        return carry

    lax.fori_loop(0, seq // PRO_ROWS, norms, 0, unroll=32)

    def prologue(n, carry):
        r0 = pl.multiple_of(n * PRO_ROWS, PRO_ROWS)
        rows = pl.ds(r0, PRO_ROWS)
        c, sn = c_ref[0, rows, :], s_ref[0, rows, :]
        for t in range(group // 2):
            cols = slice(t * LANES, (t + 1) * LANES)
            q = q_ref[0, rows, cols].astype(F32) * rq_s[rows, cols] * q_gain
            q_t = _rotate(q, c, sn, swap).T
            for u in range(blk_per_step):
                qt_s[n * blk_per_step + u, t] = q_t[:, u * QBLK:(u + 1) * QBLK].astype(BF16)
        k = k_ref[0, rows, :].astype(F32) * rk_s[rows, :] * kg_ref[...]
        k = _rotate(k, c, sn, swap).astype(BF16)
        kd_s[pl.ds(r0 + QBLK, PRO_ROWS), :] = jnp.dot(k, dup, preferred_element_type=F32).astype(BF16)
        v_t = v_ref[0, rows, :].astype(F32).T
        v_t = jnp.where(kv_in_hi, v_t[HEAD_DIM:LANES, :], v_t[0:HEAD_DIM, :]).astype(BF16)
        for u in range(blk_per_step):
            vt_s[n * blk_per_step + u + 1] = v_t[:, u * QBLK:(u + 1) * QBLK]
        return carry

    lax.fori_loop(0, seq // PRO_ROWS, prologue, 0, unroll=32)

    first, later = _band_bias_t(1, QBLK)
    bias_s[0] = first.astype(BF16)
    bias_s[1] = later.astype(BF16)
    eye = (lax.broadcasted_iota(jnp.int32, (QBLK, QBLK), 0)
           == lax.broadcasted_iota(jnp.int32, (QBLK, QBLK), 1))
    eye = jnp.where(eye, 1.0, 0.0).astype(BF16)
    eye_pair = jnp.concatenate([eye, eye], axis=1)
    top = lax.broadcasted_iota(jnp.int32, (LANES, QBLK), 0) < HEAD_DIM
    kd_s[0:QBLK, :] = jnp.zeros((QBLK, LANES), BF16)
    vt_s[0] = jnp.zeros((HEAD_DIM, QBLK), BF16)
    sink = jnp.concatenate([sink_ref[0, hh:hh + 1, :] for hh in range(group)], axis=1) * LOG2_E
    ones_rows = jnp.ones((DEN_ROWS, 2 * QBLK), BF16)

    def scores(i, carry):
        d0 = pl.multiple_of(i * QBLK, QBLK)
        rows = pl.ds(d0, QBLK)
        kw = kd_s[pl.ds(d0, 2 * QBLK), :]
        lhs = jnp.concatenate([kw, bias_s[jnp.minimum(i, 1)]], axis=1)
        for t in range(group // 2):
            qt = qt_s[i, t]
            zero = jnp.zeros_like(qt)
            rhs = jnp.concatenate([jnp.where(top, qt, zero), jnp.where(top, zero, qt)], axis=1)
            cols = slice(2 * t * QBLK, 2 * (t + 1) * QBLK)
            sc = jnp.dot(lhs, jnp.concatenate([rhs, eye_pair], axis=0), preferred_element_type=F32)
            m = jnp.maximum(jnp.max(sc, axis=0, keepdims=True), sink[:, cols])
            p_s[i, :, cols] = jnp.exp2(sc - m).astype(BF16)
            m_s[i, :, cols] = jnp.broadcast_to(m, (SUBLANES, 2 * QBLK))
        return carry

    lax.fori_loop(0, seq // QBLK, scores, 0, unroll=BLOCK_UNROLL)

    def values(i, carry):
        rows = pl.ds(pl.multiple_of(i * QBLK, QBLK), QBLK)
        m = m_s[i][0:1, :]
        lhs = jnp.concatenate(
            [jnp.concatenate([vt_s[i], vt_s[i + 1]], axis=1), ones_rows], axis=0)
        ot = jnp.dot(lhs, p_s[i], preferred_element_type=F32)
        inv = 1.0 / (ot[HEAD_DIM:HEAD_DIM + 1, :] + jnp.exp2(sink - m))
        on = ot[0:HEAD_DIM, :] * inv
        for t in range(group // 2):
            pair_t = jnp.concatenate(
                [on[:, 2 * t * QBLK:(2 * t + 1) * QBLK], on[:, (2 * t + 1) * QBLK:(2 * t + 2) * QBLK]],
                axis=0)
            o_ref[0, rows, t * LANES:(t + 1) * LANES] = pair_t.T.astype(o_ref.dtype)
        return carry

    lax.fori_loop(0, seq // QBLK, values, 0, unroll=32)


def _swa(proj, tables, q_gain, k_gain, sinks):
    b, s, _ = proj.shape
    assert SWA_WINDOW == QBLK
    group = SWA_Q_HEADS // SWA_KV_HEADS
    q_w = group * HEAD_DIM
    k_t0 = SWA_Q_HEADS * HEAD_DIM // LANES
    v_t0 = k_t0 + SWA_KV_HEADS * HEAD_DIM // LANES
    tab = pl.BlockSpec((1, s, LANES), lambda bi, g: (bi, 0, 0))
    two = lambda g: jnp.concatenate([g, g]).reshape(1, LANES)
    sink_rows = jnp.broadcast_to(sinks.reshape(SWA_KV_HEADS, group, 1), (SWA_KV_HEADS, group, LANES))
    return pl.pallas_call(
        functools.partial(_swa_kernel, seq=s),
        grid=(b, SWA_KV_HEADS),
        in_specs=[
            pl.BlockSpec((1, s, q_w), lambda bi, g: (bi, 0, g)),
            pl.BlockSpec((1, s, LANES), lambda bi, g: (bi, 0, k_t0 + g // 2)),
            pl.BlockSpec((1, s, LANES), lambda bi, g: (bi, 0, v_t0 + g // 2)),
            tab, tab,
            _resident((1, LANES)),
            _resident((1, LANES)),
            pl.BlockSpec((1, group, LANES), lambda bi, g: (g, 0, 0)),
        ],
        out_specs=pl.BlockSpec((1, s, q_w), lambda bi, g: (bi, 0, g)),
        out_shape=jax.ShapeDtypeStruct((b, s, SWA_Q_HEADS * HEAD_DIM), BF16),
        scratch_shapes=[
            pltpu.VMEM((s // QBLK, group // 2, LANES, QBLK), BF16),
            pltpu.VMEM((s + QBLK, LANES), BF16),
            pltpu.VMEM((s // QBLK + 1, HEAD_DIM, QBLK), BF16),
            pltpu.VMEM((2, 2 * QBLK, QBLK), BF16),
            pltpu.VMEM((s // QBLK, 2 * QBLK, group * QBLK), BF16),
            pltpu.VMEM((s // QBLK, SUBLANES, group * QBLK), F32),
            pltpu.VMEM((s, q_w), F32),
            pltpu.VMEM((s, LANES), F32),
        ],
        compiler_params=_cparams(2),
        name="swa_attention",
    )(proj, proj, proj, *tables, two(q_gain), two(k_gain), sink_rows)


def kernel(x, positions, norm_mix, norm_mlp, mlp_w_up, mlp_w_down, hyb_w_in, hyb_w_out, ret_gn_gain, dil_q_gain, dil_k_gain, swa_w_qkv, swa_b_qkv, swa_w_out, swa_q_gain, swa_k_gain, swa_sinks):
    b, s, d = x.shape
    depth = norm_mix.shape[0]
    ret_tab = _rope_table(positions, RET_QK_DIM // 2, RET_THETA, RET_QK_DIM)
    rope_tab = _rope_table(positions, ROPE_DIMS // 2, ROPE_THETA, ROPE_DIMS)
    log_gamma = jnp.log1p(-jnp.exp2(-5.0 - jnp.arange(RET_HEADS, dtype=F32)))
    ret_w = RET_HEADS * RET_V_DIM
    dil_col0 = 2 * RET_HEADS * RET_QK_DIM + 2 * ret_w

    x2d = x.reshape(b * s, d)
    for layer in range(depth):
        i = layer // 2
        if layer % 2 == 0:
            zero_bias = jnp.zeros((hyb_w_in.shape[2],), F32)
            proj = _norm_proj(x2d, norm_mix[layer], hyb_w_in, i, zero_bias).reshape(b, s, -1)
            ra = _retention(proj, ret_tab, log_gamma, ret_gn_gain[i])
            da = _dilated(proj, rope_tab, dil_q_gain[i], dil_k_gain[i], dil_col0)
            mixed = [ra.reshape(b * s, -1), da.reshape(b * s, -1)]
            w_out = hyb_w_out
        else:
            proj = _norm_proj(x2d, norm_mix[layer], swa_w_qkv, i, swa_b_qkv[i]).reshape(b, s, -1)
            att = _swa(proj, rope_tab, swa_q_gain[i], swa_k_gain[i], swa_sinks[i])
            mixed = [att.reshape(b * s, -1)]
            w_out = swa_w_out
        x2d = _out_mlp(mixed, x2d, w_out, i, norm_mlp[layer], mlp_w_up, mlp_w_down, layer)
    return x2d.reshape(b, s, d)
```

```python
import functools

import jax
import jax.numpy as jnp
import numpy as np
from jax import lax
from jax.experimental import pallas as pl
from jax.experimental.pallas import tpu as pltpu

F32 = jnp.float32
BF16 = jnp.bfloat16

D_MODEL = 1024
D_FF = 4 * D_MODEL
HEAD_DIM = 64
EPS = 1e-6
RET_HEADS = 4
RET_QK_DIM = 64
RET_V_DIM = 128
RET_CHUNK = 128
RET_THETA = 10000.0
DIL_HEADS = 8
DIL_PATTERNS = ((128, 1), (512, 4), (2048, 16))
SWA_Q_HEADS = 16
SWA_KV_HEADS = 4
SWA_WINDOW = 128
ROPE_THETA = 500000.0
ROPE_DIMS = HEAD_DIM // 4

LANES = 128
SUBLANES = 8
QBLK = 128
VMEM_LIMIT_BYTES = 56 * 1024 * 1024
NEG_BIG = -1e30
LOG2_E = 1.4426950408889634
Q_SCALE = HEAD_DIM ** -0.5 * LOG2_E

ROW_TILE = 1024
MLP_ROW_TILE = 512
FF_CHUNK = 2048
PROJ_CHUNK = 1536
PRO_ROWS = 128
BLOCK_UNROLL = 32
DEN_ROWS = 16
COARSE_R, FINE_R = DIL_PATTERNS[1][1], DIL_PATTERNS[2][1]


def _cparams(n_axes):
    return pltpu.CompilerParams(
        dimension_semantics=("arbitrary",) * n_axes,
        vmem_limit_bytes=VMEM_LIMIT_BYTES,
    )


def _resident(shape):
    nd = len(shape)
    return pl.BlockSpec(shape, lambda *_: (0,) * nd, pipeline_mode=pl.Buffered(1))


def _resident_layer(stack_shape, layer):
    return pl.BlockSpec((None,) + tuple(stack_shape[1:]), lambda *_: (layer, 0, 0),
                        pipeline_mode=pl.Buffered(1))


def _split3(x):
    hi = x.astype(BF16)
    r1 = x - hi.astype(F32)
    mid = r1.astype(BF16)
    lo = (r1 - mid.astype(F32)).astype(BF16)
    return hi, mid, lo


def _trig_kernel(*refs, n_sets):
    for t in range(n_sets):
        _emit_tables(*refs[6 * t:6 * t + 6], *refs[6 * n_sets + 2 * t:6 * n_sets + 2 * t + 2])


def _emit_tables(p_ref, spread_ref, inv_ref, ec_ref, es_ref, base_ref, c_ref, s_ref):
    n_pos = p_ref.shape[1]
    dense = sum(jnp.dot(part, spread_ref[...], preferred_element_type=F32)
                for part in _split3(p_ref[...]))
    ang = dense * inv_ref[...]
    cos_parts = _split3(jnp.cos(ang))
    sin_parts = _split3(jnp.sin(ang))
    rows = p_ref.shape[0]
    for c in range(n_pos):
        dst = pl.ds(c, rows, stride=n_pos)
        c_ref[dst, :] = base_ref[...] + sum(
            jnp.dot(part, ec_ref[c], preferred_element_type=F32) for part in cos_parts)
        s_ref[dst, :] = sum(
            jnp.dot(part, es_ref[c], preferred_element_type=F32) for part in sin_parts)


def _selectors(half):
    n_pos = LANES // half
    ec = np.zeros((n_pos, LANES, LANES), np.float32)
    es = np.zeros((n_pos, LANES, LANES), np.float32)
    base = np.zeros((1, LANES), np.float32)
    for j in range(LANES):
        d = j % HEAD_DIM
        if d >= 2 * half:
            base[0, j] = 1.0
            continue
        f = d % half
        for c in range(n_pos):
            ec[c, c * half + f, j] = 1.0
            es[c, c * half + f, j] = -1.0 if d < half else 1.0
    return n_pos, jnp.asarray(ec, BF16), jnp.asarray(es, BF16), jnp.asarray(base)


def _rope_tables(positions, schemes):
    b, s = positions.shape
    posf = positions.astype(F32)
    operands, in_specs = [], []
    for half, theta, n_rot in schemes:
        inv = jnp.power(jnp.float32(theta), -jnp.arange(half, dtype=F32) * (2.0 / n_rot))
        n_pos, ec, es, base = _selectors(half)
        rows = s // n_pos
        spread = jnp.asarray(np.repeat(np.eye(n_pos, dtype=np.float32), half, axis=1), BF16)
        consts = [spread, jnp.tile(inv, n_pos)[None, :], ec, es, base]
        operands += [posf.reshape(b, rows, n_pos)] + consts
        in_specs += [pl.BlockSpec((None, rows, n_pos), lambda bi: (bi, 0, 0))]
        in_specs += [_resident(c.shape) for c in consts]
    wide = pl.BlockSpec((None, s, LANES), lambda bi: (bi, 0, 0))
    tabs = pl.pallas_call(
        functools.partial(_trig_kernel, n_sets=len(schemes)),
        grid=(b,),
        in_specs=in_specs,
        out_specs=[wide] * (2 * len(schemes)),
        out_shape=[jax.ShapeDtypeStruct((b, s, LANES), F32)] * (2 * len(schemes)),
        compiler_params=_cparams(1),
        name="trig_tables",
    )(*operands)
    return [tuple(tabs[2 * t:2 * t + 2]) for t in range(len(schemes))]


def _swap_matrix(half):
    src = lax.broadcasted_iota(jnp.int32, (LANES, LANES), 0)
    dst = lax.broadcasted_iota(jnp.int32, (LANES, LANES), 1)
    d = dst & (HEAD_DIM - 1)
    want = jnp.where(d < half, dst + half, jnp.where(d < 2 * half, dst - half, -1))
    return jnp.where(src == want, 1.0, 0.0).astype(BF16)


def _rotate(x, c, s, swap):
    swapped = jnp.dot(x.astype(BF16), swap, preferred_element_type=F32)
    return x * c + swapped * s


def _lo_mask(shape):
    return lax.broadcasted_iota(jnp.int32, shape, len(shape) - 1) < HEAD_DIM


def _head_ones():
    r = lax.broadcasted_iota(jnp.int32, (LANES, LANES), 0) < HEAD_DIM
    c = lax.broadcasted_iota(jnp.int32, (LANES, LANES), 1) < HEAD_DIM
    return jnp.where(r == c, 1.0, 0.0).astype(BF16)


def _head_rinv(x, head_ones):
    ss = jnp.dot((x * x).astype(BF16), head_ones, preferred_element_type=F32)
    return lax.rsqrt(ss * (1.0 / HEAD_DIM) + EPS)


def _norm_proj_kernel(x_ref, g_ref, w_ref, b_ref, o_ref, *, n_chunk):
    x = x_ref[...]
    ms = jnp.mean(x * x, axis=-1, keepdims=True)
    h = (x * lax.rsqrt(ms + EPS) * g_ref[...]).astype(BF16)
    n = o_ref.shape[-1]
    for c in range(0, n, n_chunk):
        acc = jnp.dot(h, w_ref[:, c:c + n_chunk].astype(BF16), preferred_element_type=F32)
        o_ref[:, c:c + n_chunk] = (acc + b_ref[:, c:c + n_chunk]).astype(o_ref.dtype)


def _norm_proj(x2d, gain, w_stack, layer, bias):
    m, d = x2d.shape
    n = w_stack.shape[2]
    return pl.pallas_call(
        functools.partial(_norm_proj_kernel, n_chunk=PROJ_CHUNK),
        grid=(m // ROW_TILE,),
        in_specs=[
            pl.BlockSpec((ROW_TILE, d), lambda i: (i, 0)),
            _resident((1, d)),
            _resident_layer(w_stack.shape, layer),
            _resident((1, n)),
        ],
        out_specs=pl.BlockSpec((ROW_TILE, n), lambda i: (i, 0)),
        out_shape=jax.ShapeDtypeStruct((m, n), BF16),
        compiler_params=_cparams(1),
        name="norm_proj",
    )(x2d, gain.reshape(1, d), w_stack, bias.reshape(1, n))


def _out_mlp_kernel(*refs, n_mix):
    a_refs = refs[:n_mix]
    x_ref, wo_ref, g_ref, wup_ref, wdn_ref, o_ref = refs[n_mix:]
    mixed = a_refs[0][...] if n_mix == 1 else jnp.concatenate([a[...] for a in a_refs], axis=-1)
    x1 = x_ref[...] + jnp.dot(mixed, wo_ref[...].astype(BF16), preferred_element_type=F32)
    ms = jnp.mean(x1 * x1, axis=-1, keepdims=True)
    h = (x1 * lax.rsqrt(ms + EPS) * g_ref[...]).astype(BF16)
    y = x1
    for c in range(0, D_FF, FF_CHUNK):
        u = jnp.dot(h, wup_ref[:, c:c + FF_CHUNK].astype(BF16), preferred_element_type=F32)
        u = jnp.square(jnp.maximum(u, 0.0)).astype(BF16)
        y = y + jnp.dot(u, wdn_ref[c:c + FF_CHUNK, :].astype(BF16), preferred_element_type=F32)
    o_ref[...] = y


def _out_mlp(mixed, x2d, w_out, out_layer, gain, w_up, w_down, mlp_layer):
    m, d = x2d.shape
    n_mix = len(mixed)
    in_specs = [pl.BlockSpec((MLP_ROW_TILE, a.shape[1]), lambda i: (i, 0)) for a in mixed]
    in_specs.append(pl.BlockSpec((MLP_ROW_TILE, d), lambda i: (i, 0)))
    in_specs += [_resident_layer(w_out.shape, out_layer), _resident((1, d)),
                 _resident_layer(w_up.shape, mlp_layer), _resident_layer(w_down.shape, mlp_layer)]
    return pl.pallas_call(
        functools.partial(_out_mlp_kernel, n_mix=n_mix),
        grid=(m // MLP_ROW_TILE,),
        in_specs=in_specs,
        out_specs=pl.BlockSpec((MLP_ROW_TILE, d), lambda i: (i, 0)),
        out_shape=jax.ShapeDtypeStruct((m, d), F32),
        compiler_params=_cparams(1),
        name="out_mlp",
    )(*mixed, x2d, w_out, gain.reshape(1, d), w_up, w_down)


def _retention_kernel(q_ref, k_ref, v_ref, g_ref, c_ref, s_ref, lg_ref, gn_ref,
                      o_ref, q_s, k_s, vt_s, o_s, kv_s, *, seq):
    n_chunks = seq // RET_CHUNK
    swap = _swap_matrix(RET_QK_DIM // 2)
    cs = RET_CHUNK

    def prologue(n, carry):
        rows = pl.ds(pl.multiple_of(n * cs, cs), cs)
        c, sn = c_ref[0, rows, :], s_ref[0, rows, :]
        q = _rotate(q_ref[0, rows, :].astype(F32), c, sn, swap)
        k = _rotate(k_ref[0, rows, :].astype(F32), c, sn, swap) * (RET_QK_DIM ** -0.5)
        q_s[n] = q.astype(BF16)
        k_s[n] = k.astype(BF16)
        for hh in range(2):
            cols = slice(hh * RET_V_DIM, (hh + 1) * RET_V_DIM)
            vt_s[n, hh] = v_ref[0, rows, cols].astype(F32).T.astype(BF16)
        return carry

    lax.fori_loop(0, n_chunks, prologue, 0, unroll=32)

    lo = _lo_mask((cs, LANES))
    key = lax.broadcasted_iota(jnp.int32, (cs, cs), 0).astype(F32)
    tok = lax.broadcasted_iota(jnp.int32, (cs, cs), 1).astype(F32)
    ahead = tok - key
    consts = []
    for hh in range(2):
        lg = lg_ref[0, hh:hh + 1, :]
        decay_t = jnp.where(ahead >= 0, jnp.exp(lg * jnp.maximum(ahead, 0.0)), 0.0)
        xi = jnp.exp(lg * (tok[0:1, :] + 1.0))
        zeta = jnp.exp(lg * (cs - 1.0 - tok[0:1, :]))
        cd = jnp.exp(lg * float(cs))
        head = lo if hh == 0 else jnp.logical_not(lo)
        gain = jnp.broadcast_to(gn_ref[0, hh:hh + 1, :], (cs, RET_V_DIM)).T
        consts.append((decay_t, xi, zeta, cd, head, gain))

    def inner(n, carry):
        qc, kc = q_s[n], k_s[n]
        for hh in range(2):
            decay_t, xi, zeta, cd, head, gain = consts[hh]
            qm = jnp.where(head, qc, jnp.zeros_like(qc))
            vt = vt_s[n, hh]
            sc_t = lax.dot_general(kc, qm, (((1,), (1,)), ((), ())),
                                   preferred_element_type=F32) * decay_t
            o_s[n, hh] = jnp.dot(vt, sc_t.astype(BF16), preferred_element_type=F32)
            vz = (vt.astype(F32) * zeta).astype(BF16)
            kv_s[n, hh] = jnp.dot(vz, kc, preferred_element_type=F32)
        return carry

    lax.fori_loop(0, n_chunks, inner, 0, unroll=32)

    def cross(n, state):
        qc = q_s[n]
        new_state = []
        for hh in range(2):
            decay_t, xi, zeta, cd, head, gain = consts[hh]
            r_prev = state[hh]
            qm = jnp.where(head, qc, jnp.zeros_like(qc))
            o_s[n, hh] += lax.dot_general(r_prev.astype(BF16), qm, (((1,), (1,)), ((), ())),
                                          preferred_element_type=F32) * xi
            new_state.append(r_prev * cd + kv_s[n, hh])
        return tuple(new_state)

    zero = jnp.zeros((RET_V_DIM, LANES), F32)
    lax.fori_loop(0, n_chunks, cross, (zero, zero), unroll=32)

    def finish(n, carry):
        rows = pl.ds(pl.multiple_of(n * cs, cs), cs)
        for hh in range(2):
            gain = consts[hh][5]
            cols = slice(hh * RET_V_DIM, (hh + 1) * RET_V_DIM)
            o = o_s[n, hh]
            mu = jnp.mean(o, axis=0, keepdims=True)
            dev = o - mu
            var = jnp.mean(dev * dev, axis=0, keepdims=True)
            y = (dev * lax.rsqrt(var + EPS) * gain).T
            gate = g_ref[0, rows, cols].astype(F32)
            o_ref[0, rows, cols] = (gate * jax.nn.sigmoid(gate) * y).astype(o_ref.dtype)
        return carry

    lax.fori_loop(0, n_chunks, finish, 0, unroll=32)


def _retention(proj, tables, log_gamma, gn_gain):
    b, s, _ = proj.shape
    n_pairs = RET_HEADS // 2
    pair_w = 2 * RET_V_DIM
    qk_tiles = RET_HEADS * RET_QK_DIM // LANES
    v_off = 2 * qk_tiles * LANES // pair_w
    g_off = v_off + RET_HEADS * RET_V_DIM // pair_w
    tab = pl.BlockSpec((1, s, LANES), lambda bi, p: (bi, 0, 0))
    lg = jnp.broadcast_to(log_gamma.reshape(n_pairs, 2, 1), (n_pairs, 2, LANES))
    n_chunks = s // RET_CHUNK
    return pl.pallas_call(
        functools.partial(_retention_kernel, seq=s),
        grid=(b, n_pairs),
        in_specs=[
            pl.BlockSpec((1, s, LANES), lambda bi, p: (bi, 0, p)),
            pl.BlockSpec((1, s, LANES), lambda bi, p: (bi, 0, qk_tiles + p)),
            pl.BlockSpec((1, s, pair_w), lambda bi, p: (bi, 0, v_off + p)),
            pl.BlockSpec((1, s, pair_w), lambda bi, p: (bi, 0, g_off + p)),
            tab, tab,
            pl.BlockSpec((1, 2, LANES), lambda bi, p: (p, 0, 0)),
            pl.BlockSpec((1, 2, RET_V_DIM), lambda bi, p: (p, 0, 0)),
        ],
        out_specs=pl.BlockSpec((1, s, pair_w), lambda bi, p: (bi, 0, p)),
        out_shape=jax.ShapeDtypeStruct((b, s, RET_HEADS * RET_V_DIM), BF16),
        scratch_shapes=[
            pltpu.VMEM((n_chunks, RET_CHUNK, LANES), BF16),
            pltpu.VMEM((n_chunks, RET_CHUNK, LANES), BF16),
            pltpu.VMEM((n_chunks, 2, RET_V_DIM, RET_CHUNK), BF16),
            pltpu.VMEM((n_chunks, 2, RET_V_DIM, RET_CHUNK), F32),
            pltpu.VMEM((n_chunks, 2, RET_V_DIM, LANES), F32),
        ],
        compiler_params=_cparams(2),
        name="retention",
    )(proj, proj, proj, proj, *tables, lg, gn_gain.reshape(n_pairs, 2, RET_V_DIM))


def _band_bias_t(lo_off, hi_off):
    c = lax.broadcasted_iota(jnp.int32, (2 * QBLK, QBLK), 0)
    a = lax.broadcasted_iota(jnp.int32, (2 * QBLK, QBLK), 1)
    band = (c - a >= lo_off) & (c - a <= hi_off)
    later = jnp.where(band, 0.0, NEG_BIG).astype(F32)
    first = jnp.where(band & (c >= QBLK), 0.0, NEG_BIG).astype(F32)
    return first, later


def _dilated_kernel(q_ref, k_ref, v_ref, c_ref, s_ref, qg_ref, kg_ref, o_ref,
                    qn_s, kn_s, vn_s, q4_s, k4_s, v4_s, qt_s, kc_s, vt_s, bias_s, p_s, m_s,
                    o0_s, o1_s, o2_s, l0_s, l1_s, l2_s, *, seq):
    swap = _swap_matrix(ROPE_DIMS // 2)
    head_ones = _head_ones()
    q_gain = qg_ref[...] * Q_SCALE

    def prologue(n, carry):
        rows = pl.ds(pl.multiple_of(n * PRO_ROWS, PRO_ROWS), PRO_ROWS)
        c, sn = c_ref[0, rows, :], s_ref[0, rows, :]
        q = q_ref[0, rows, :].astype(F32)
        q = _rotate(q * _head_rinv(q, head_ones) * q_gain, c, sn, swap)
        k = k_ref[0, rows, :].astype(F32)
        k = _rotate(k * _head_rinv(k, head_ones) * kg_ref[...], c, sn, swap)
        v = v_ref[0, rows, :].astype(F32)
        qn_s[rows, :] = q
        kn_s[rows, :] = k
        vn_s[rows, :] = v
        qt_s[n] = q.T.astype(BF16)
        kc_s[pl.ds(pl.multiple_of((n + 1) * QBLK, QBLK), QBLK), :] = k.astype(BF16)
        vt_s[n + 1] = v.T.astype(BF16)
        return carry

    lax.fori_loop(0, seq // PRO_ROWS, prologue, 0, unroll=32)

    first, later = _band_bias_t(0, QBLK)
    bias_s[0] = first.astype(BF16)
    bias_s[1] = later.astype(BF16)
    eye = (lax.broadcasted_iota(jnp.int32, (QBLK, QBLK), 0)
           == lax.broadcasted_iota(jnp.int32, (QBLK, QBLK), 1))
    eye = jnp.where(eye, 1.0, 0.0).astype(BF16)
    eye_pair = jnp.concatenate([eye, eye], axis=1)
    zero_pad = jnp.zeros((QBLK, LANES), BF16)
    top = lax.broadcasted_iota(jnp.int32, (LANES, QBLK), 0) < HEAD_DIM
    ones_rows = jnp.ones((DEN_ROWS, 2 * QBLK), BF16)

    outs = (o0_s, o1_s, o2_s)
    lses = (l0_s, l1_s, l2_s)
    n_flat = seq // QBLK
    for (window, r), on_s, ln_s in zip(DIL_PATTERNS, outs, lses):
        assert window // r == QBLK and PRO_ROWS == QBLK and DIL_PATTERNS[0][1] == 1
        n_blk = seq // r // QBLK
        blk_shift = n_blk.bit_length() - 1
        assert n_blk == 1 << blk_shift

        def pad(j, c2, n_blk=n_blk):
            z = j * (n_blk + 1)
            kc_s[pl.ds(pl.multiple_of(z * QBLK, QBLK), QBLK), :] = zero_pad
            vt_s[z] = zero_pad
            return c2

        lax.fori_loop(0, r, pad, 0, unroll=r)

        def split(idx, r=r, n_blk=n_blk, blk_shift=blk_shift):
            j = lax.shift_right_logical(idx, blk_shift)
            i = idx & (n_blk - 1)
            return j, i

        def gather(idx, c2, r=r, split=split):
            j, i = split(idx)
            if r == FINE_R:
                base = ((j & (COARSE_R - 1)) * (seq // COARSE_R)
                        + lax.shift_right_logical(j, COARSE_R.bit_length() - 1))
                src = pl.ds(base + (QBLK * r // COARSE_R) * i, QBLK, stride=r // COARSE_R)
                q, k, v = q4_s[src, :], k4_s[src, :], v4_s[src, :]
            else:
                src = pl.ds(j + r * QBLK * i, QBLK, stride=r)
                q, k, v = qn_s[src, :], kn_s[src, :], vn_s[src, :]
            if r == COARSE_R:
                dense = pl.ds(pl.multiple_of(idx * QBLK, QBLK), QBLK)
                q4_s[dense, :] = q
                k4_s[dense, :] = k
                v4_s[dense, :] = v
            k0 = pl.multiple_of((idx + j + 1) * QBLK, QBLK)
            qt_s[idx] = q.T.astype(BF16)
            kc_s[pl.ds(k0, QBLK), :] = k.astype(BF16)
            vt_s[idx + j + 1] = v.T.astype(BF16)
            return c2

        if r > 1:
            lax.fori_loop(0, n_flat, gather, 0, unroll=32)

        def scores(idx, c2, split=split):
            j, i = split(idx)
            k0 = pl.multiple_of((idx + j) * QBLK, QBLK)
            qt = qt_s[idx]
            kw = kc_s[pl.ds(k0, 2 * QBLK), :]
            zero = jnp.zeros_like(qt)
            rhs = jnp.concatenate([jnp.where(top, qt, zero), jnp.where(top, zero, qt)], axis=1)
            lhs = jnp.concatenate([kw, bias_s[jnp.minimum(i, 1)]], axis=1)
            sc = jnp.dot(lhs, jnp.concatenate([rhs, eye_pair], axis=0), preferred_element_type=F32)
            m = jnp.max(sc, axis=0, keepdims=True)
            p_s[idx] = jnp.exp2(sc - m).astype(BF16)
            m_s[idx] = jnp.broadcast_to(m, (SUBLANES, 2 * QBLK))
            return c2

        lax.fori_loop(0, n_flat, scores, 0, unroll=32)

        def block(idx, c2, r=r, split=split, on_s=on_s, ln_s=ln_s):
            j, i = split(idx)
            p = p_s[idx]
            m = m_s[idx][0:1, :]
            lhs = jnp.concatenate(
                [jnp.concatenate([vt_s[idx + j], vt_s[idx + j + 1]], axis=1), ones_rows], axis=0)
            ot = jnp.dot(lhs, p, preferred_element_type=F32)
            den = ot[LANES:LANES + 1, :]
            inv = 1.0 / den
            lse = m + jnp.log2(den)
            o_t = jnp.concatenate(
                [ot[0:HEAD_DIM, 0:QBLK] * inv[:, 0:QBLK],
                 ot[HEAD_DIM:LANES, QBLK:2 * QBLK] * inv[:, QBLK:2 * QBLK]], axis=0)
            l_t = jnp.concatenate(
                [jnp.broadcast_to(lse[:, 0:QBLK], (HEAD_DIM, QBLK)),
                 jnp.broadcast_to(lse[:, QBLK:2 * QBLK], (HEAD_DIM, QBLK))], axis=0)
            dst = pl.ds(j + r * QBLK * i, QBLK, stride=r)
            on_s[dst, :] = o_t.T
            ln_s[dst, :] = l_t.T
            return c2

        lax.fori_loop(0, n_flat, block, 0, unroll=32)

    def combine(n, carry):
        r0 = pl.multiple_of(n * PRO_ROWS, PRO_ROWS)
        rows = pl.ds(r0, PRO_ROWS)
        ls = [l_s[rows, :] for l_s in lses]
        m = jnp.maximum(jnp.maximum(ls[0], ls[1]), ls[2])
        es = [jnp.exp2(l - m) for l in ls]
        num = es[0] * o0_s[rows, :] + es[1] * o1_s[rows, :] + es[2] * o2_s[rows, :]
        o_ref[0, rows, :] = (num / (es[0] + es[1] + es[2])).astype(o_ref.dtype)
        return carry

    lax.fori_loop(0, seq // PRO_ROWS, combine, 0, unroll=32)


def _dilated(proj, tables, q_gain, k_gain, col0):
    b, s, _ = proj.shape
    n_pairs = DIL_HEADS * HEAD_DIM // LANES
    t0 = col0 // LANES
    tab = pl.BlockSpec((1, s, LANES), lambda bi, p: (bi, 0, 0))
    two = lambda g: jnp.concatenate([g, g]).reshape(1, LANES)
    nat = pltpu.VMEM((s, LANES), F32)
    max_r = max(r for _, r in DIL_PATTERNS)
    return pl.pallas_call(
        functools.partial(_dilated_kernel, seq=s),
        grid=(b, n_pairs),
        in_specs=[
            pl.BlockSpec((1, s, LANES), lambda bi, p: (bi, 0, t0 + p)),
            pl.BlockSpec((1, s, LANES), lambda bi, p: (bi, 0, t0 + n_pairs + p)),
            pl.BlockSpec((1, s, LANES), lambda bi, p: (bi, 0, t0 + 2 * n_pairs + p)),
            tab, tab,
            _resident((1, LANES)),
            _resident((1, LANES)),
        ],
        out_specs=pl.BlockSpec((1, s, LANES), lambda bi, p: (bi, 0, p)),
        out_shape=jax.ShapeDtypeStruct((b, s, DIL_HEADS * HEAD_DIM), BF16),
        scratch_shapes=[
            nat, nat, nat, nat, nat, nat,
            pltpu.VMEM((s // QBLK, LANES, QBLK), BF16),
            pltpu.VMEM((s + max_r * QBLK, LANES), BF16),
            pltpu.VMEM((s // QBLK + max_r, LANES, QBLK), BF16),
            pltpu.VMEM((2, 2 * QBLK, QBLK), BF16),
            pltpu.VMEM((s // QBLK, 2 * QBLK, 2 * QBLK), BF16),
            pltpu.VMEM((s // QBLK, SUBLANES, 2 * QBLK), F32),
            nat, nat, nat, nat, nat, nat,
        ],
        compiler_params=_cparams(2),
        name="dilated_attention",
    )(proj, proj, proj, *tables, two(q_gain), two(k_gain))


def _swa_kernel(q_ref, k_ref, v_ref, c_ref, s_ref, qg_ref, kg_ref, sink_ref, o_ref,
                qt_s, kd_s, vt_s, bias_s, p_s, m_s, rq_s, rk_s, *, seq):
    group = SWA_Q_HEADS // SWA_KV_HEADS
    swap = _swap_matrix(ROPE_DIMS // 2)
    head_ones = _head_ones()
    kv_in_hi = (pl.program_id(1) % 2) == 1
    q_gain = qg_ref[...] * Q_SCALE
    blk_per_step = PRO_ROWS // QBLK
    src = lax.broadcasted_iota(jnp.int32, (LANES, LANES), 0)
    dst = lax.broadcasted_iota(jnp.int32, (LANES, LANES), 1) & (HEAD_DIM - 1)
    dup = jnp.where(src == dst + jnp.where(kv_in_hi, HEAD_DIM, 0), 1.0, 0.0).astype(BF16)

    def norms(n, carry):
        rows = pl.ds(pl.multiple_of(n * PRO_ROWS, PRO_ROWS), PRO_ROWS)
        for t in range(group // 2):
            cols = slice(t * LANES, (t + 1) * LANES)
            rq_s[rows, cols] = _head_rinv(q_ref[0, rows, cols].astype(F32), head_ones)
        rk_s[rows, :] = _head_rinv(k_ref[0, rows, :].astype(F32), head_ones)
        return carry

    lax.fori_loop(0, seq // PRO_ROWS, norms, 0, unroll=32)

    def prologue(n, carry):
        r0 = pl.multiple_of(n * PRO_ROWS, PRO_ROWS)
        rows = pl.ds(r0, PRO_ROWS)
        c, sn = c_ref[0, rows, :], s_ref[0, rows, :]
        for t in range(group // 2):
            cols = slice(t * LANES, (t + 1) * LANES)
            q = q_ref[0, rows, cols].astype(F32) * rq_s[rows, cols] * q_gain
            q_t = _rotate(q, c, sn, swap).T
            for u in range(blk_per_step):
                qt_s[n * blk_per_step + u, t] = q_t[:, u * QBLK:(u + 1) * QBLK].astype(BF16)
        k = k_ref[0, rows, :].astype(F32) * rk_s[rows, :] * kg_ref[...]
        k = _rotate(k, c, sn, swap).astype(BF16)
        kd_s[pl.ds(r0 + QBLK, PRO_ROWS), :] = jnp.dot(k, dup, preferred_element_type=F32).astype(BF16)
        v_t = v_ref[0, rows, :].astype(F32).T
        v_t = jnp.where(kv_in_hi, v_t[HEAD_DIM:LANES, :], v_t[0:HEAD_DIM, :]).astype(BF16)
        for u in range(blk_per_step):
            vt_s[n * blk_per_step + u + 1] = v_t[:, u * QBLK:(u + 1) * QBLK]
        return carry

    lax.fori_loop(0, seq // PRO_ROWS, prologue, 0, unroll=32)

    first, later = _band_bias_t(1, QBLK)
    bias_s[0] = first.astype(BF16)
    bias_s[1] = later.astype(BF16)
    eye = (lax.broadcasted_iota(jnp.int32, (QBLK, QBLK), 0)
           == lax.broadcasted_iota(jnp.int32, (QBLK, QBLK), 1))
    eye = jnp.where(eye, 1.0, 0.0).astype(BF16)
    eye_pair = jnp.concatenate([eye, eye], axis=1)
    top = lax.broadcasted_iota(jnp.int32, (LANES, QBLK), 0) < HEAD_DIM
    kd_s[0:QBLK, :] = jnp.zeros((QBLK, LANES), BF16)
    vt_s[0] = jnp.zeros((HEAD_DIM, QBLK), BF16)
    sink = jnp.concatenate([sink_ref[0, hh:hh + 1, :] for hh in range(group)], axis=1) * LOG2_E
    ones_rows = jnp.ones((DEN_ROWS, 2 * QBLK), BF16)

    def scores(i, carry):
        d0 = pl.multiple_of(i * QBLK, QBLK)
        rows = pl.ds(d0, QBLK)
        kw = kd_s[pl.ds(d0, 2 * QBLK), :]
        lhs = jnp.concatenate([kw, bias_s[jnp.minimum(i, 1)]], axis=1)
        for t in range(group // 2):
            qt = qt_s[i, t]
            zero = jnp.zeros_like(qt)
            rhs = jnp.concatenate([jnp.where(top, qt, zero), jnp.where(top, zero, qt)], axis=1)
            cols = slice(2 * t * QBLK, 2 * (t + 1) * QBLK)
            sc = jnp.dot(lhs, jnp.concatenate([rhs, eye_pair], axis=0), preferred_element_type=F32)
            m = jnp.maximum(jnp.max(sc, axis=0, keepdims=True), sink[:, cols])
            p_s[i, :, cols] = jnp.exp2(sc - m).astype(BF16)
            m_s[i, :, cols] = jnp.broadcast_to(m, (SUBLANES, 2 * QBLK))
        return carry

    lax.fori_loop(0, seq // QBLK, scores, 0, unroll=BLOCK_UNROLL)

    def values(i, carry):
        rows = pl.ds(pl.multiple_of(i * QBLK, QBLK), QBLK)
        m = m_s[i][0:1, :]
        lhs = jnp.concatenate(
            [jnp.concatenate([vt_s[i], vt_s[i + 1]], axis=1), ones_rows], axis=0)
        ot = jnp.dot(lhs, p_s[i], preferred_element_type=F32)
        inv = 1.0 / (ot[HEAD_DIM:HEAD_DIM + 1, :] + jnp.exp2(sink - m))
        on = ot[0:HEAD_DIM, :] * inv
        for t in range(group // 2):
            pair_t = jnp.concatenate(
                [on[:, 2 * t * QBLK:(2 * t + 1) * QBLK], on[:, (2 * t + 1) * QBLK:(2 * t + 2) * QBLK]],
                axis=0)
            o_ref[0, rows, t * LANES:(t + 1) * LANES] = pair_t.T.astype(o_ref.dtype)
        return carry

    lax.fori_loop(0, seq // QBLK, values, 0, unroll=32)


def _swa(proj, tables, q_gain, k_gain, sinks):
    b, s, _ = proj.shape
    assert SWA_WINDOW == QBLK
    group = SWA_Q_HEADS // SWA_KV_HEADS
    q_w = group * HEAD_DIM
    k_t0 = SWA_Q_HEADS * HEAD_DIM // LANES
    v_t0 = k_t0 + SWA_KV_HEADS * HEAD_DIM // LANES
    tab = pl.BlockSpec((1, s, LANES), lambda bi, g: (bi, 0, 0))
    two = lambda g: jnp.concatenate([g, g]).reshape(1, LANES)
    sink_rows = jnp.broadcast_to(sinks.reshape(SWA_KV_HEADS, group, 1), (SWA_KV_HEADS, group, LANES))
    return pl.pallas_call(
        functools.partial(_swa_kernel, seq=s),
        grid=(b, SWA_KV_HEADS),
        in_specs=[
            pl.BlockSpec((1, s, q_w), lambda bi, g: (bi, 0, g)),
            pl.BlockSpec((1, s, LANES), lambda bi, g: (bi, 0, k_t0 + g // 2)),
            pl.BlockSpec((1, s, LANES), lambda bi, g: (bi, 0, v_t0 + g // 2)),
            tab, tab,
            _resident((1, LANES)),
            _resident((1, LANES)),
            pl.BlockSpec((1, group, LANES), lambda bi, g: (g, 0, 0)),
        ],
        out_specs=pl.BlockSpec((1, s, q_w), lambda bi, g: (bi, 0, g)),
        out_shape=jax.ShapeDtypeStruct((b, s, SWA_Q_HEADS * HEAD_DIM), BF16),
        scratch_shapes=[
            pltpu.VMEM((s // QBLK, group // 2, LANES, QBLK), BF16),
            pltpu.VMEM((s + QBLK, LANES), BF16),
            pltpu.VMEM((s // QBLK + 1, HEAD_DIM, QBLK), BF16),
            pltpu.VMEM((2, 2 * QBLK, QBLK), BF16),
            pltpu.VMEM((s // QBLK, 2 * QBLK, group * QBLK), BF16),
            pltpu.VMEM((s // QBLK, SUBLANES, group * QBLK), F32),
            pltpu.VMEM((s, q_w), F32),
            pltpu.VMEM((s, LANES), F32),
        ],
        compiler_params=_cparams(2),
        name="swa_attention",
    )(proj, proj, proj, *tables, two(q_gain), two(k_gain), sink_rows)


def kernel(x, positions, norm_mix, norm_mlp, mlp_w_up, mlp_w_down, hyb_w_in, hyb_w_out, ret_gn_gain, dil_q_gain, dil_k_gain, swa_w_qkv, swa_b_qkv, swa_w_out, swa_q_gain, swa_k_gain, swa_sinks):
    b, s, d = x.shape
    depth = norm_mix.shape[0]
    ret_tab, rope_tab = _rope_tables(positions, ((RET_QK_DIM // 2, RET_THETA, RET_QK_DIM),
                                                 (ROPE_DIMS // 2, ROPE_THETA, ROPE_DIMS)))
    log_gamma = jnp.log1p(-jnp.exp2(-5.0 - jnp.arange(RET_HEADS, dtype=F32)))
    ret_w = RET_HEADS * RET_V_DIM
    dil_col0 = 2 * RET_HEADS * RET_QK_DIM + 2 * ret_w

    x2d = x.reshape(b * s, d)
    for layer in range(depth):
        i = layer // 2
        if layer % 2 == 0:
            zero_bias = jnp.zeros((hyb_w_in.shape[2],), F32)
            proj = _norm_proj(x2d, norm_mix[layer], hyb_w_in, i, zero_bias).reshape(b, s, -1)
            ra = _retention(proj, ret_tab, log_gamma, ret_gn_gain[i])
            da = _dilated(proj, rope_tab, dil_q_gain[i], dil_k_gain[i], dil_col0)
            mixed = [ra.reshape(b * s, -1), da.reshape(b * s, -1)]
            w_out = hyb_w_out
        else:
            proj = _norm_proj(x2d, norm_mix[layer], swa_w_qkv, i, swa_b_qkv[i]).reshape(b, s, -1)
            att = _swa(proj, rope_tab, swa_q_gain[i], swa_k_gain[i], swa_sinks[i])
            mixed = [att.reshape(b * s, -1)]
            w_out = swa_w_out
        x2d = _out_mlp(mixed, x2d, w_out, i, norm_mlp[layer], mlp_w_up, mlp_w_down, layer)
    return x2d.reshape(b, s, d)
```
